```python
import math
import jax
import jax.numpy as jnp
from jax import lax
import numpy as np

D_MODEL = 2048
BATCH = 2
SEQ = 4096
DEPTH = 4
DEC_BATCH = 8
DEC_SEQ = 1
PAST_LEN = 16384
PAGE_SIZE = 128

POOL_WIDTH = D_MODEL // 2
POOL_GROUPS = 4
POOL_WINDOWS = (2, 4, 8, 16)
POOL_HIST = 15
RWKV_WIDTH = D_MODEL // 2
RWKV_HEAD = 64
RWKV_HEADS = RWKV_WIDTH // RWKV_HEAD
RWKV_W_RANK = 64
RWKV_A_RANK = 64
RWKV_G_RANK = 128
RWKV_COLS = 3 * RWKV_WIDTH + RWKV_W_RANK + RWKV_A_RANK + RWKV_G_RANK
RWKV_DECAY_SCALE = 0.6065306597126334
GN_EPS = 64e-5
NSA_HEAD = 128
NSA_HEADS = (D_MODEL // 2) // NSA_HEAD
NSA_KV_HEADS = 2
NSA_GROUP = NSA_HEADS // NSA_KV_HEADS
NSA_WIDTH = NSA_HEADS * NSA_HEAD
NSA_BLOCK = 64
NSA_TOPK = 16
NSA_WINDOW = 512
NSA_QBLOCK = 128
NSA_KV_COLS = 6 * NSA_KV_HEADS * NSA_HEAD
NSA_COLS = NSA_WIDTH + NSA_KV_COLS + 3 * NSA_HEADS
N_BRANCH = 3
IN_COLS = POOL_WIDTH + RWKV_COLS + NSA_COLS + N_BRANCH * D_MODEL
D_FF = 256 * ((8 * D_MODEL // 3 + 255) // 256)
CONV_W = 3
ROPE_THETA = 10000.0
RMS_EPS = 1e-6

kernel_name = 'hybrid_pool_rwkv7_nsa_convffn_step'


def rmsnorm(x, g):
    xf = x.astype(jnp.float32)
    y = xf * lax.rsqrt(jnp.mean(xf * xf, axis=-1, keepdims=True) + RMS_EPS)
    return (y * g.astype(jnp.float32)).astype(x.dtype)


def rope(x, pos):
    half = x.shape[-1] // 2
    inv = jnp.exp(-math.log(ROPE_THETA) * jnp.arange(half, dtype=jnp.float32) / half)
    ang = pos.astype(jnp.float32)[:, None] * inv[None, :]
    cos = jnp.cos(ang)[None, :, None, :]
    sin = jnp.sin(ang)[None, :, None, :]
    xf = x.astype(jnp.float32)
    x1, x2 = xf[..., :half], xf[..., half:]
    return jnp.concatenate([x1 * cos - x2 * sin, x2 * cos + x1 * sin], axis=-1).astype(x.dtype)


def pool_mixer(u, hist, pos, w_grp, scale):
    b, t, c = u.shape
    cg = c // POOL_GROUPS
    ue = jnp.concatenate([hist.astype(u.dtype), u], axis=1)
    c0 = jnp.concatenate([jnp.zeros((b, 1, c), jnp.float32), jnp.cumsum(ue.astype(jnp.float32), axis=1)], axis=1)
    end = c0[:, 1 + POOL_HIST:]
    means = []
    for gi, w in enumerate(POOL_WINDOWS):
        sl = slice(gi * cg, (gi + 1) * cg)
        start = c0[:, 1 + POOL_HIST - w:1 + POOL_HIST - w + t, sl]
        cnt = jnp.minimum(w, pos + 1).astype(jnp.float32)[None, :, None]
        means.append((end[..., sl] - start) / cnt)
    d = (jnp.concatenate(means, axis=-1) - u.astype(jnp.float32)).reshape(b, t, POOL_GROUPS, cg)
    y = jnp.einsum('btgc,gcd->btgd', d.astype(u.dtype), w_grp).reshape(b, t, c) * scale
    return y.astype(u.dtype), ue[:, -POOL_HIST:]


def rwkv7_mixer(p, shift_prev, wkv0, mu, w0, w2, a0, a2, g2, k_k, k_a, r_k, ln_w, ln_b):
    b, t, _ = p.shape
    c, hh, n = RWKV_WIDTH, RWKV_HEADS, RWKV_HEAD
    p_prev = jnp.concatenate([shift_prev.astype(p.dtype), p[:, :-1]], axis=1)
    pm = (p + (p_prev - p) * mu).astype(jnp.float32)
    r, k, v = pm[..., :c], pm[..., c:2 * c], pm[..., 2 * c:3 * c]
    o = 3 * c
    w_in = pm[..., o:o + RWKV_W_RANK]
    o = o + RWKV_W_RANK
    a_in = pm[..., o:o + RWKV_A_RANK]
    o = o + RWKV_A_RANK
    g_in = pm[..., o:o + RWKV_G_RANK]
    log_w = -RWKV_DECAY_SCALE * jax.nn.sigmoid(w0 + jnp.tanh(w_in) @ w2)
    a = jax.nn.sigmoid(a0 + a_in @ a2)
    g = jax.nn.sigmoid(g_in) @ g2
    r, k, v, a = (z.reshape(b, t, hh, n) for z in (r, k, v, a))
    w = jnp.exp(log_w.reshape(b, t, hh, n))
    kk = k * k_k
    kk = kk * lax.rsqrt(jnp.maximum(jnp.sum(kk * kk, axis=-1, keepdims=True), 1e-12))
    k = k * (1.0 + (a - 1.0) * k_a)

    def step(s, inp):
        r_t, w_t, k_t, v_t, kk_t, a_t = inp
        sa = jnp.einsum('bhvk,bhk->bhv', s, -kk_t)
        s = s * w_t[:, :, None, :] + sa[..., None] * (kk_t * a_t)[:, :, None, :] + v_t[..., None] * k_t[:, :, None, :]
        return s, jnp.einsum('bhvk,bhk->bhv', s, r_t)

    xs = tuple(jnp.swapaxes(z, 0, 1) for z in (r, w, k, v, kk, a))
    s_fin, y = lax.scan(step, wkv0.astype(jnp.float32), xs)
    y = jnp.swapaxes(y, 0, 1)
    ym = jnp.mean(y, axis=-1, keepdims=True)
    yv = jnp.mean((y - ym) ** 2, axis=-1, keepdims=True)
    yn = ((y - ym) * lax.rsqrt(yv + GN_EPS)).reshape(b, t, c) * ln_w + ln_b
    bonus = (jnp.sum(r * k * r_k, axis=-1, keepdims=True) * v).reshape(b, t, c)
    out = (yn + bonus) * g
    return out.astype(p.dtype), s_fin.astype(wkv0.dtype), p[:, -1:]


def nsa_summaries(kv):
    b, t = kv.shape[:2]
    nb = -(-t // NSA_BLOCK)
    kvp = jnp.pad(kv, ((0, 0), (0, nb * NSA_BLOCK - t), (0, 0), (0, 0), (0, 0)))
    blk = kvp.reshape(b, nb, NSA_BLOCK, 4, NSA_KV_HEADS, NSA_HEAD)
    cmp_mean = jnp.mean(blk[:, :, :, 0:2].astype(jnp.float32), axis=2).astype(kv.dtype)
    sel_k = blk[:, :, :, 2].transpose(0, 3, 1, 2, 4)
    sel_v = blk[:, :, :, 3].transpose(0, 3, 1, 2, 4)
    return cmp_mean[:, :, 0], cmp_mean[:, :, 1], sel_k, sel_v


def nsa_core(q, q_pos, kc, vc, ks_blk, vs_blk, kw, vw, kw_pos):
    b, tq = q.shape[:2]
    nb = kc.shape[1]
    scale = NSA_HEAD ** -0.5
    blk = jnp.arange(nb)
    s = jnp.einsum('bqhgd,bnhd->bqhgn', q, kc).astype(jnp.float32) * scale
    ok = (((blk + 1) * NSA_BLOCK - 1)[None, :] <= q_pos[:, None])[None, :, None, None, :]
    p_c = jax.nn.softmax(jnp.where(ok, s, -1e30), axis=-1) * ok
    o_c = jnp.einsum('bqhgn,bnhd->bqhgd', p_c.astype(vc.dtype), vc)
    imp = jnp.sum(p_c, axis=3)
    cur = (blk[None, :] == (q_pos // NSA_BLOCK)[:, None])[None, :, None, :]
    avail = (blk[None, :] * NSA_BLOCK <= q_pos[:, None])[None, :, None, :]
    imp = jnp.where(cur, NSA_GROUP + 1.0, jnp.where(avail, imp, -1.0))
    n_sel = min(NSA_TOPK, nb)
    _, idx = lax.top_k(imp, n_sel)
    bi = jnp.arange(b)[:, None, None, None]
    hi = jnp.arange(NSA_KV_HEADS)[None, None, :, None]
    k_sel = ks_blk[bi, hi, idx]
    v_sel = vs_blk[bi, hi, idx]
    s = jnp.einsum('bqhgd,bqhkjd->bqhgkj', q, k_sel).astype(jnp.float32) * scale
    pos_s = idx[..., None] * NSA_BLOCK + jnp.arange(NSA_BLOCK)
    oks = (pos_s <= q_pos[None, :, None, None, None])[:, :, :, None]
    s = jnp.where(oks, s, -1e30).reshape(b, tq, NSA_KV_HEADS, NSA_GROUP, n_sel * NSA_BLOCK)
    p_s = jax.nn.softmax(s, axis=-1).reshape(b, tq, NSA_KV_HEADS, NSA_GROUP, n_sel, NSA_BLOCK)
    o_s = jnp.einsum('bqhgkj,bqhkjd->bqhgd', p_s.astype(v_sel.dtype), v_sel)
    s = jnp.einsum('bqhgd,bkhd->bqhgk', q, kw).astype(jnp.float32) * scale
    dp = q_pos[:, None] - kw_pos[None, :]
    okw = ((dp >= 0) & (dp <= NSA_WINDOW) & (kw_pos[None, :] >= 0))[None, :, None, None, :]
    p_w = jax.nn.softmax(jnp.where(okw, s, -1e30), axis=-1)
    o_w = jnp.einsum('bqhgk,bkhd->bqhgd', p_w.astype(vw.dtype), vw)
    return jnp.stack([o_c, o_s, o_w], axis=-2)


def nsa_prompt(q, kv, wkv):
    b, s = q.shape[:2]
    kc, vc, ks_blk, vs_blk = nsa_summaries(kv)
    kw_pad = jnp.pad(wkv, ((0, 0), (NSA_WINDOW, 0), (0, 0), (0, 0), (0, 0)))
    lw = NSA_WINDOW + NSA_QBLOCK

    def one_block(i):
        q0 = i * NSA_QBLOCK
        qb = lax.dynamic_slice_in_dim(q, q0, NSA_QBLOCK, axis=1)
        wb = lax.dynamic_slice_in_dim(kw_pad, q0, lw, axis=1)
        q_pos = q0 + jnp.arange(NSA_QBLOCK, dtype=jnp.int32)
        kw_pos = q0 - NSA_WINDOW + jnp.arange(lw, dtype=jnp.int32)
        return nsa_core(qb, q_pos, kc, vc, ks_blk, vs_blk, wb[:, :, 0], wb[:, :, 1], kw_pos)

    o = lax.map(one_block, jnp.arange(s // NSA_QBLOCK, dtype=jnp.int32))
    return jnp.moveaxis(o, 0, 1).reshape(b, s, NSA_KV_HEADS, NSA_GROUP, 3, NSA_HEAD)


def nsa_sample(q, pos, kv_new, wkv_new, kv_past, win_buf):
    kv = jnp.concatenate([kv_past.astype(kv_new.dtype), kv_new], axis=1)
    kc, vc, ks_blk, vs_blk = nsa_summaries(kv)
    wk = jnp.concatenate([win_buf.astype(wkv_new.dtype), wkv_new], axis=1)
    lb = win_buf.shape[1]
    kw_pos = pos[0] - lb + jnp.arange(wk.shape[1], dtype=jnp.int32)
    o = nsa_core(q, pos, kc, vc, ks_blk, vs_blk, wk[:, :, 0], wk[:, :, 1], kw_pos)
    return o, wk[:, -NSA_WINDOW:]


def nsa_mixer(p, pos, past):
    b, t, _ = p.shape
    q = rope(p[..., :NSA_WIDTH].reshape(b, t, NSA_HEADS, NSA_HEAD), pos)
    q = q.reshape(b, t, NSA_KV_HEADS, NSA_GROUP, NSA_HEAD)
    kvp = p[..., NSA_WIDTH:NSA_WIDTH + NSA_KV_COLS].reshape(b, t, 6, NSA_KV_HEADS, NSA_HEAD)
    keys = rope(kvp[:, :, 0::2].reshape(b, t, 3 * NSA_KV_HEADS, NSA_HEAD), pos).reshape(b, t, 3, NSA_KV_HEADS, NSA_HEAD)
    vals = kvp[:, :, 1::2]
    kv = jnp.stack([keys[:, :, 0], vals[:, :, 0], keys[:, :, 1], vals[:, :, 1]], axis=2)
    wkv = jnp.stack([keys[:, :, 2], vals[:, :, 2]], axis=2)
    gates = jax.nn.sigmoid(p[..., NSA_WIDTH + NSA_KV_COLS:].astype(jnp.float32))
    gates = gates.reshape(b, t, NSA_KV_HEADS, NSA_GROUP, 3)
    if past is None:
        o = nsa_prompt(q, kv, wkv)
        win_new = wkv[:, -NSA_WINDOW:]
    else:
        o, win_new = nsa_sample(q, pos, kv, wkv, past[0], past[1])
    y = jnp.einsum('bthgcd,bthgc->bthgd', o, gates.astype(o.dtype)).reshape(b, t, NSA_WIDTH)
    return y, kv, win_new


def conv_ffn(h, hist, w_up, w_conv, b_conv, w_down):
    t = h.shape[1]
    z = h @ w_up
    ze = jnp.concatenate([hist.astype(z.dtype), z], axis=1)
    zc = ze[:, 0:t] * w_conv[0] + ze[:, 1:1 + t] * w_conv[1] + ze[:, 2:2 + t] * w_conv[2] + b_conv
    a, v = jnp.split(zc, 2, axis=-1)
    return (jax.nn.silu(a) * v) @ w_down, ze[:, -(CONV_W - 1):]


def setup_inputs(seed: int = 0) -> dict:
    key = jax.random.key(seed)
    ks = jax.random.split(key, 40)

    def nrm(i, shape, scale):
        return jax.random.normal(ks[i], shape, jnp.float32) * scale

    n_pages = PAST_LEN // PAGE_SIZE
    n_pool = (DEC_BATCH * n_pages * 5 + 3) // 4
    win = min(NSA_WINDOW, PAST_LEN)
    f2 = 2 * D_FF
    cg = POOL_WIDTH // POOL_GROUPS
    page_table = jax.random.permutation(ks[2], n_pool)[:DEC_BATCH * n_pages].reshape(DEC_BATCH, n_pages).astype(jnp.int32)
    return {
        'x_prompt': nrm(0, (BATCH, SEQ, D_MODEL), 1.0),
        'x_sample': nrm(1, (DEC_BATCH, DEC_SEQ, D_MODEL), 1.0),
        'cache_nsa_kv': nrm(3, (DEPTH, n_pool, PAGE_SIZE, 4, NSA_KV_HEADS, NSA_HEAD), 1.0),
        'page_table': page_table,
        'state_nsa_window': nrm(4, (DEPTH, DEC_BATCH, win, 2, NSA_KV_HEADS, NSA_HEAD), 1.0),
        'state_wkv': nrm(5, (DEPTH, DEC_BATCH, RWKV_HEADS, RWKV_HEAD, RWKV_HEAD), 0.3),
        'state_rwkv_shift': nrm(6, (DEPTH, DEC_BATCH, 1, RWKV_COLS), 1.0),
        'state_pool': nrm(7, (DEPTH, DEC_BATCH, POOL_HIST, POOL_WIDTH), 1.0),
        'state_ffn_conv': nrm(8, (DEPTH, DEC_BATCH, CONV_W - 1, f2), 1.0),
        'norm_mix': 1.0 + nrm(9, (DEPTH, D_MODEL), 0.02),
        'w_in': nrm(10, (DEPTH, D_MODEL, IN_COLS), D_MODEL ** -0.5),
        'pool_w': nrm(11, (DEPTH, POOL_GROUPS, cg, cg), cg ** -0.5),
        'pool_scale': 1.0 + nrm(12, (DEPTH, POOL_WIDTH), 0.02),
        'rwkv_mu': jax.random.uniform(ks[13], (DEPTH, RWKV_COLS), jnp.float32),
        'rwkv_w0': nrm(14, (DEPTH, RWKV_WIDTH), 0.5),
        'rwkv_w2': nrm(15, (DEPTH, RWKV_W_RANK, RWKV_WIDTH), 0.5 * RWKV_W_RANK ** -0.5),
        'rwkv_a0': nrm(16, (DEPTH, RWKV_WIDTH), 0.5),
        'rwkv_a2': nrm(17, (DEPTH, RWKV_A_RANK, RWKV_WIDTH), 0.5 * RWKV_A_RANK ** -0.5),
        'rwkv_g2': nrm(18, (DEPTH, RWKV_G_RANK, RWKV_WIDTH), RWKV_G_RANK ** -0.5),
        'rwkv_k_k': 0.85 + nrm(19, (DEPTH, RWKV_HEADS, RWKV_HEAD), 0.05),
        'rwkv_k_a': 1.0 + nrm(20, (DEPTH, RWKV_HEADS, RWKV_HEAD), 0.05),
        'rwkv_r_k': nrm(21, (DEPTH, RWKV_HEADS, RWKV_HEAD), 0.1),
        'rwkv_ln_w': 1.0 + nrm(22, (DEPTH, RWKV_WIDTH), 0.02),
        'rwkv_ln_b': nrm(23, (DEPTH, RWKV_WIDTH), 0.02),
        'w_branch': nrm(24, (DEPTH, N_BRANCH, D_MODEL // 2, D_MODEL), (D_MODEL // 2) ** -0.5),
        'w_out': nrm(25, (DEPTH, D_MODEL, D_MODEL), 0.5 * D_MODEL ** -0.5),
        'norm_ffn': 1.0 + nrm(26, (DEPTH, D_MODEL), 0.02),
        'ffn_up': nrm(27, (DEPTH, D_MODEL, f2), D_MODEL ** -0.5),
        'ffn_conv': nrm(28, (DEPTH, CONV_W, f2), 0.5),
        'ffn_conv_b': nrm(29, (DEPTH, f2), 0.02),
        'ffn_down': nrm(30, (DEPTH, D_FF, D_MODEL), D_FF ** -0.5),
        'norm_final': 1.0 + nrm(31, (D_MODEL,), 0.02),
    }


def reference(x_prompt, x_sample, cache_nsa_kv, page_table, state_nsa_window, state_wkv, state_rwkv_shift,
              state_pool, state_ffn_conv, norm_mix, w_in, pool_w, pool_scale, rwkv_mu, rwkv_w0, rwkv_w2,
              rwkv_a0, rwkv_a2, rwkv_g2, rwkv_k_k, rwkv_k_a, rwkv_r_k, rwkv_ln_w, rwkv_ln_b, w_branch, w_out,
              norm_ffn, ffn_up, ffn_conv, ffn_conv_b, ffn_down, norm_final):
    def layer(l, x, pos, pool_hist, shift_prev, wkv0, conv_hist, past):
        b, t, _ = x.shape
        h = rmsnorm(x, norm_mix[l])
        p = h @ w_in[l]
        o1 = POOL_WIDTH
        o2 = o1 + RWKV_COLS
        o3 = o2 + NSA_COLS
        y_a, pool_new = pool_mixer(p[..., :o1], pool_hist, pos, pool_w[l], pool_scale[l])
        y_b, wkv_new, shift_new = rwkv7_mixer(p[..., o1:o2], shift_prev, wkv0, rwkv_mu[l], rwkv_w0[l], rwkv_w2[l],
                                              rwkv_a0[l], rwkv_a2[l], rwkv_g2[l], rwkv_k_k[l], rwkv_k_a[l],
                                              rwkv_r_k[l], rwkv_ln_w[l], rwkv_ln_b[l])
        y_c, kv_rows, win_new = nsa_mixer(p[..., o2:o3], pos, past)
        gates = jax.nn.sigmoid(p[..., o3:].astype(jnp.float32)).astype(x.dtype).reshape(b, t, N_BRANCH, D_MODEL)
        br = jnp.einsum('btiw,iwd->btid', jnp.stack([y_a, y_b, y_c], axis=2), w_branch[l])
        x = x + jnp.sum(br * gates, axis=2) @ w_out[l]
        f, conv_new = conv_ffn(rmsnorm(x, norm_ffn[l]), conv_hist, ffn_up[l], ffn_conv[l], ffn_conv_b[l], ffn_down[l])
        x = x + f
        return x, (kv_rows, win_new, wkv_new, shift_new, pool_new, conv_new)

    bp, sp, _ = x_prompt.shape
    dt = x_prompt.dtype
    pos_p = jnp.arange(sp, dtype=jnp.int32)
    xp = x_prompt
    st_p = []
    for l in range(DEPTH):
        xp, st = layer(l, xp, pos_p,
                       jnp.zeros((bp, POOL_HIST, POOL_WIDTH), dt),
                       jnp.zeros((bp, 1, RWKV_COLS), dt),
                       jnp.zeros((bp, RWKV_HEADS, RWKV_HEAD, RWKV_HEAD), dt),
                       jnp.zeros((bp, CONV_W - 1, 2 * D_FF), dt),
                       None)
        st_p.append(st)
    y_prompt = rmsnorm(xp, norm_final)

    bs, ts, _ = x_sample.shape
    past_len = page_table.shape[1] * cache_nsa_kv.shape[2]
    pos_s = past_len + jnp.arange(ts, dtype=jnp.int32)
    xs_ = x_sample
    st_s = []
    for l in range(DEPTH):
        kv_past = cache_nsa_kv[l][page_table].reshape(bs, past_len, 4, NSA_KV_HEADS, NSA_HEAD)
        xs_, st = layer(l, xs_, pos_s, state_pool[l], state_rwkv_shift[l], state_wkv[l], state_ffn_conv[l],
                        (kv_past, state_nsa_window[l]))
        st_s.append(st)
    y_sample = rmsnorm(xs_, norm_final)

    p_kv, p_win, p_wkv, p_shift, p_pool, p_conv = [jnp.stack([st[i] for st in st_p]) for i in range(6)]
    s_kv, s_win, s_wkv, s_shift, s_pool, s_conv = [jnp.stack([st[i] for st in st_s]) for i in range(6)]
    return (y_prompt, y_sample, p_kv, p_win, p_wkv, p_shift, p_pool, p_conv,
            s_kv, s_win, s_wkv, s_shift, s_pool, s_conv)
```

```python
import functools
import math

import jax
import jax.numpy as jnp
from jax import lax
from jax.experimental import pallas as pl
from jax.experimental.pallas import tpu as pltpu

F32 = jnp.float32
BF16 = jnp.bfloat16
HIGHEST = lax.Precision.HIGHEST

POOL_GROUPS = 4
POOL_WINDOWS = (2, 4, 8, 16)
POOL_HIST = 15
RWKV_HEAD = 64
RWKV_W_RANK = 64
RWKV_A_RANK = 64
RWKV_G_RANK = 128
RWKV_LORA = RWKV_W_RANK + RWKV_A_RANK + RWKV_G_RANK
RWKV_DECAY_SCALE = 0.6065306597126334
GN_EPS = 64e-5
NSA_HEAD = 128
NSA_KV_HEADS = 2
NSA_BLOCK = 64
NSA_TOPK = 16
NSA_WINDOW = 512
NSA_QBLOCK = 128
N_BRANCH = 3
CONV_W = 3
ROPE_THETA = 10000.0
RMS_EPS = 1e-6
NEG = -1e30

LANES = 128
SUBLANES = 8
VMEM_LIMIT = 56 * 1024 * 1024
RWKV_CHUNK = 64


class Dims:
    def __init__(self, d_model):
        d = d_model
        self.d = d
        self.pw = d // 2
        self.cg = self.pw // POOL_GROUPS
        self.rw = d // 2
        self.rh = self.rw // RWKV_HEAD
        self.rcols = 3 * self.rw + RWKV_LORA
        self.nw = d // 2
        self.nh = self.nw // NSA_HEAD
        self.g = self.nh // NSA_KV_HEADS
        self.kvc = 6 * NSA_KV_HEADS * NSA_HEAD
        self.ngate = 3 * self.nh
        self.ff = 256 * ((8 * d // 3 + 255) // 256)
        self.o1 = self.pw
        self.o2 = self.o1 + self.rcols
        self.o3 = self.o2 + self.nw + self.kvc + self.ngate
        self.c_gate = 0
        self.c_pool = N_BRANCH * d
        self.c_r = self.c_pool + self.pw
        self.c_q = self.c_r + 3 * self.rw
        self.c_kv = self.c_q + self.nw
        self.c_lora = self.c_kv + self.kvc
        self.c_ng = self.c_lora + RWKV_LORA
        used = self.c_ng + LANES
        self.tn_in = 512
        self.np = -(-used // self.tn_in) * self.tn_in


def _blk(offset, width):
    assert offset % width == 0, (offset, width)
    return offset // width


def _cparams(sem):
    return pltpu.CompilerParams(dimension_semantics=sem, vmem_limit_bytes=VMEM_LIMIT)


def _dot(a, b):
    return jnp.dot(a.astype(BF16), b.astype(BF16), preferred_element_type=F32)


def _dot_nt(a, b):
    return lax.dot_general(a.astype(BF16), b.astype(BF16), (((1,), (1,)), ((), ())),
                           preferred_element_type=F32)


def _dot_hi(a, b):
    return jnp.dot(a, b, precision=HIGHEST, preferred_element_type=F32)


def _dot_hi_nt(a, b):
    return lax.dot_general(a, b, (((1,), (1,)), ((), ())), precision=HIGHEST,
                           preferred_element_type=F32)


def _dot_hi_tn(a, b):
    return lax.dot_general(a, b, (((0,), (0,)), ((), ())), precision=HIGHEST,
                           preferred_element_type=F32)


def _split3(x):
    hi = x.astype(BF16)
    r1 = x - hi.astype(F32)
    mid = r1.astype(BF16)
    lo = (r1 - mid.astype(F32)).astype(BF16)
    return hi, mid, lo


def _dot_exact_lhs(m_bf16, x):
    hi, mid, lo = _split3(x)
    d = lambda p: jnp.dot(m_bf16, p, preferred_element_type=F32)
    return d(hi) + d(mid) + d(lo)


def _dot_exact_rhs(x, m_bf16):
    hi, mid, lo = _split3(x)
    d = lambda p: jnp.dot(p, m_bf16, preferred_element_type=F32)
    return d(hi) + d(mid) + d(lo)


def _head_ones():
    i = lax.broadcasted_iota(jnp.int32, (LANES, LANES), 0) // RWKV_HEAD
    j = lax.broadcasted_iota(jnp.int32, (LANES, LANES), 1) // RWKV_HEAD
    return (i == j).astype(BF16)


def _head_sum(x, ones):
    parts = [_dot_exact_rhs(x[:, c:c + LANES], ones) for c in range(0, x.shape[1], LANES)]
    return parts[0] if len(parts) == 1 else jnp.concatenate(parts, axis=1)


def _rmsnorm_val(x, g):
    ms = jnp.mean(x * x, axis=-1, keepdims=True)
    return x * lax.rsqrt(ms + RMS_EPS) * g


def _sigmoid(x):
    return 1.0 / (1.0 + jnp.exp(-x))


def _norm_matmul_kernel(x_ref, g_ref, w_ref, o_ref, h_ref):
    @pl.when(pl.program_id(1) == 0)
    def _():
        h_ref[...] = _rmsnorm_val(x_ref[...], g_ref[...]).astype(BF16)

    o_ref[...] = jnp.dot(h_ref[...], w_ref[...], preferred_element_type=F32)


def _norm_matmul(x, g, w, tm, tn):
    m, d = x.shape
    n = w.shape[1]
    return pl.pallas_call(
        _norm_matmul_kernel,
        out_shape=jax.ShapeDtypeStruct((m, n), F32),
        grid=(m // tm, n // tn),
        in_specs=[pl.BlockSpec((tm, d), lambda i, j: (i, 0)),
                  pl.BlockSpec((1, d), lambda i, j: (0, 0)),
                  pl.BlockSpec((d, tn), lambda i, j: (0, j))],
        out_specs=pl.BlockSpec((tm, tn), lambda i, j: (i, j)),
        scratch_shapes=[pltpu.VMEM((tm, d), BF16)],
        compiler_params=_cparams(("parallel", "arbitrary")),
        name="norm_matmul",
    )(x, g, w)


def _branch_merge_kernel(ya_ref, yb_ref, yc_ref, ga_ref, gb_ref, gc_ref, w_ref, o_ref):
    acc = _dot(ya_ref[...], w_ref[0]) * _sigmoid(ga_ref[...])
    acc = acc + _dot(yb_ref[...], w_ref[1]) * _sigmoid(gb_ref[...])
    acc = acc + _dot(yc_ref[...], w_ref[2]) * _sigmoid(gc_ref[...])
    o_ref[...] = acc.astype(BF16)


def _branch_merge(ya, yb, yc, p, wb, dm, tm, tn):
    m = ya.shape[0]
    d, hw = dm.d, dm.d // 2
    nb = d // tn
    yspec = pl.BlockSpec((tm, hw), lambda j, i: (i, 0))
    gspec = lambda k: pl.BlockSpec((tm, tn), lambda j, i, k=k: (i, k * nb + j))
    return pl.pallas_call(
        _branch_merge_kernel,
        out_shape=jax.ShapeDtypeStruct((m, d), BF16),
        grid=(nb, m // tm),
        in_specs=[yspec, yspec, yspec, gspec(0), gspec(1), gspec(2),
                  pl.BlockSpec((N_BRANCH, hw, tn), lambda j, i: (0, 0, j))],
        out_specs=pl.BlockSpec((tm, tn), lambda j, i: (i, j)),
        compiler_params=_cparams(("parallel", "parallel")),
        name="branch_merge",
    )(ya, yb, yc, p, p, p, wb)


def _out_proj_kernel(x_ref, m_ref, w_ref, o_ref):
    o_ref[...] = x_ref[...] + jnp.dot(m_ref[...], w_ref[...], preferred_element_type=F32)


def _out_proj(x, mrg, wo, tm, tn):
    m, d = x.shape
    return pl.pallas_call(
        _out_proj_kernel,
        out_shape=jax.ShapeDtypeStruct((m, d), F32),
        grid=(m // tm, d // tn),
        in_specs=[pl.BlockSpec((tm, tn), lambda i, j: (i, j)),
                  pl.BlockSpec((tm, d), lambda i, j: (i, 0)),
                  pl.BlockSpec((d, tn), lambda i, j: (0, j))],
        out_specs=pl.BlockSpec((tm, tn), lambda i, j: (i, j)),
        compiler_params=_cparams(("parallel", "parallel")),
        name="out_proj",
    )(x, mrg, wo)


def _ffn_kernel(x_ref, g_ref, wa_ref, wv_ref, cwa_ref, cwv_ref, cba_ref, cbv_ref, wd_ref,
                ha_ref, hv_ref, o_ref, za_ref, zv_ref, h_ref, ca_ref, cv_ref, *, tm, tps, rows_are_seqs):
    i = pl.program_id(0)
    j = pl.program_id(1)

    @pl.when(j == 0)
    def _():
        x = x_ref[...]
        h_ref[...] = _rmsnorm_val(x, g_ref[...]).astype(BF16)
        o_ref[...] = x

    h = h_ref[...]

    def conv(z, cw_ref, cb_ref, hist_ref, carry_ref, zlast_ref):
        if rows_are_seqs:
            zs2 = hist_ref[:, 0, :]
            zs1 = hist_ref[:, 1, :]
            zlast_ref[...] = z
        else:
            @pl.when((i % tps) == 0)
            def _():
                carry_ref[j] = hist_ref[...]

            prev = carry_ref[j]
            row = lax.broadcasted_iota(jnp.int32, z.shape, 0)
            zs1 = jnp.where(row == 0, prev[1:2], pltpu.roll(z, 1, axis=0))
            zs2 = jnp.where(row == 0, prev[0:1], jnp.where(row == 1, prev[1:2], pltpu.roll(z, 2, axis=0)))
            last = z[tm - 2:tm]
            carry_ref[j] = last
            zlast_ref[...] = last
        return zs2 * cw_ref[0:1] + zs1 * cw_ref[1:2] + z * cw_ref[2:3] + cb_ref[...]

    za = conv(jnp.dot(h, wa_ref[...], preferred_element_type=F32), cwa_ref, cba_ref, ha_ref, ca_ref, za_ref)
    zv = conv(jnp.dot(h, wv_ref[...], preferred_element_type=F32), cwv_ref, cbv_ref, hv_ref, cv_ref, zv_ref)
    act = za * _sigmoid(za) * zv
    o_ref[...] += _dot(act, wd_ref[...])


def _ffn(x, g, w_up, cw, cb, w_down, hist, ff, tm, tn, tps, rows_are_seqs):
    m, d = x.shape
    nj = ff // tn
    nm = m // tm
    if rows_are_seqs:
        assert nm == 1
        hspec_a = pl.BlockSpec((m, 2, tn), lambda i, j: (0, 0, j))
        hspec_v = pl.BlockSpec((m, 2, tn), lambda i, j: (0, 0, nj + j))
        zshape = jax.ShapeDtypeStruct((m, ff), F32)
        zspec = pl.BlockSpec((m, tn), lambda i, j: (0, j))
    else:
        hspec_a = pl.BlockSpec((None, 2, tn), lambda i, j: (i // tps, 0, j))
        hspec_v = pl.BlockSpec((None, 2, tn), lambda i, j: (i // tps, 0, nj + j))
        zshape = jax.ShapeDtypeStruct((nm, 2, ff), F32)
        zspec = pl.BlockSpec((None, 2, tn), lambda i, j: (i, 0, j))
    kern = functools.partial(_ffn_kernel, tm=tm, tps=tps, rows_are_seqs=rows_are_seqs)
    return pl.pallas_call(
        kern,
        out_shape=(jax.ShapeDtypeStruct((m, d), F32), zshape, zshape),
        grid=(nm, nj),
        in_specs=[pl.BlockSpec((tm, d), lambda i, j: (i, 0)),
                  pl.BlockSpec((1, d), lambda i, j: (0, 0)),
                  pl.BlockSpec((d, tn), lambda i, j: (0, j)),
                  pl.BlockSpec((d, tn), lambda i, j: (0, nj + j)),
                  pl.BlockSpec((CONV_W, tn), lambda i, j: (0, j)),
                  pl.BlockSpec((CONV_W, tn), lambda i, j: (0, nj + j)),
                  pl.BlockSpec((1, tn), lambda i, j: (0, j)),
                  pl.BlockSpec((1, tn), lambda i, j: (0, nj + j)),
                  pl.BlockSpec((tn, d), lambda i, j: (j, 0)),
                  hspec_a, hspec_v],
        out_specs=(pl.BlockSpec((tm, d), lambda i, j: (i, 0)), zspec, zspec),
        scratch_shapes=[pltpu.VMEM((tm, d), BF16),
                        pltpu.VMEM((nj, 2, tn), F32),
                        pltpu.VMEM((nj, 2, tn), F32)],
        compiler_params=_cparams(("arbitrary", "arbitrary")),
        name="conv_ffn",
    )(x, g, w_up, w_up, cw, cw, cb, cb, w_down, hist, hist)


def _final_norm_kernel(x_ref, g_ref, o_ref):
    o_ref[...] = _rmsnorm_val(x_ref[...], g_ref[...])


def _final_norm(x, g, tm):
    m, d = x.shape
    return pl.pallas_call(
        _final_norm_kernel,
        out_shape=jax.ShapeDtypeStruct((m, d), F32),
        grid=(m // tm,),
        in_specs=[pl.BlockSpec((tm, d), lambda i: (i, 0)), pl.BlockSpec((1, d), lambda i: (0, 0))],
        out_specs=pl.BlockSpec((tm, d), lambda i: (i, 0)),
        compiler_params=_cparams(("parallel",)),
        name="final_norm",
    )(x, g)


def _pool_kernel(u_ref, hist_ref, w_ref, sc_ref, o_ref, e_ref, *, tt, cg):
    i = pl.program_id(1)
    hrows = POOL_HIST + 1

    @pl.when(i == 0)
    def _():
        e_ref[0:hrows] = hist_ref[...]

    @pl.when(i > 0)
    def _():
        e_ref[0:hrows] = e_ref[tt:tt + hrows]

    u = u_ref[...]
    e_ref[hrows:hrows + tt] = u
    pos = i * tt + lax.broadcasted_iota(jnp.int32, (tt, 1), 0)
    for gi, w in enumerate(POOL_WINDOWS):
        cs = slice(gi * cg, (gi + 1) * cg)
        s = e_ref[:, cs]
        sh = 1
        while sh < w:
            s = s + pltpu.roll(s, sh, axis=0)
            sh *= 2
        cnt = jnp.minimum(w, pos + 1).astype(F32)
        dlt = s[hrows:] / cnt - u[:, cs]
        o_ref[:, cs] = _dot(dlt, w_ref[gi]) * sc_ref[:, cs]


def _pool_prompt(p, hist16, w, scale, dm, tt):
    b, t, _ = p.shape
    kern = functools.partial(_pool_kernel, tt=tt, cg=dm.cg)
    return pl.pallas_call(
        kern,
        out_shape=jax.ShapeDtypeStruct((b, t, dm.pw), F32),
        grid=(b, t // tt),
        in_specs=[pl.BlockSpec((None, tt, dm.pw), lambda bi, i: (bi, i, _blk(dm.c_pool, dm.pw))),
                  pl.BlockSpec((None, POOL_HIST + 1, dm.pw), lambda bi, i: (bi, 0, 0)),
                  pl.BlockSpec((POOL_GROUPS, dm.cg, dm.cg), lambda bi, i: (0, 0, 0)),
                  pl.BlockSpec((1, dm.pw), lambda bi, i: (0, 0))],
        out_specs=pl.BlockSpec((None, tt, dm.pw), lambda bi, i: (bi, i, 0)),
        scratch_shapes=[pltpu.VMEM((tt + POOL_HIST + 1, dm.pw), F32)],
        compiler_params=_cparams(("parallel", "arbitrary")),
        name="pool_mixer",
    )(p, hist16, w, scale)


def _pool_sample_kernel(e_ref, w_ref, sc_ref, o_ref, *, cg, pos):
    e = e_ref[...]
    hrows = POOL_HIST + 1
    row = lax.broadcasted_iota(jnp.int32, e.shape, 1)
    u = e[:, hrows - 1, :]
    for gi, w in enumerate(POOL_WINDOWS):
        cs = slice(gi * cg, (gi + 1) * cg)
        win = jnp.sum(jnp.where(row >= hrows - w, e, 0.0)[:, :, cs], axis=1)
        dlt = win / float(min(w, pos + 1)) - u[:, cs]
        o_ref[:, cs] = _dot(dlt, w_ref[gi]) * sc_ref[:, cs]


def _pool_sample(e16, w, scale, dm, pos):
    b = e16.shape[0]
    kern = functools.partial(_pool_sample_kernel, cg=dm.cg, pos=pos)
    return pl.pallas_call(
        kern,
        out_shape=jax.ShapeDtypeStruct((b, dm.pw), F32),
        compiler_params=pltpu.CompilerParams(vmem_limit_bytes=VMEM_LIMIT),
        name="pool_mixer_step",
    )(e16, w, scale)


def _rwkv_mix_cols(x, xprev, mu):
    return x + (xprev - x) * mu


def _rwkv_lora(pl_, w0, w2, a0, a2, g2):
    w_in = pl_[:, 0:RWKV_W_RANK]
    a_in = pl_[:, RWKV_W_RANK:RWKV_W_RANK + RWKV_A_RANK]
    g_in = pl_[:, RWKV_W_RANK + RWKV_A_RANK:]
    lw = -RWKV_DECAY_SCALE * _sigmoid(w0 + _dot_hi(jnp.tanh(w_in), w2))
    a = _sigmoid(a0 + _dot_hi(a_in, a2))
    g = _dot(_sigmoid(g_in), g2)
    return lw, a, g


def _rwkv_prep_kernel(r_ref, k_ref, v_ref, l_ref, rp_ref, kp_ref, vp_ref, lp_ref, sh_ref, mu_ref,
                      w0_ref, w2_ref, a0_ref, a2_ref, g2_ref, kk_ref, ka_ref, rk_ref,
                      rt_ref, at_ref, kh_ref, bh_ref, kg_ref, bg_ref, vo_ref, bo_ref, go_ref, gc_ref,
                      *, tt, rw, chunk):
    i = pl.program_id(1)
    first = i == 0
    row1 = lax.broadcasted_iota(jnp.int32, (tt, 1), 0)

    def mixed(x_ref, xp_ref, c0, c1):
        x = x_ref[...]
        carry = jnp.where(first, sh_ref[:, c0:c1], xp_ref[SUBLANES - 1:SUBLANES, :])
        xprev = jnp.where(row1 == 0, carry, pltpu.roll(x, 1, axis=0))
        return _rwkv_mix_cols(x, xprev, mu_ref[:, c0:c1])

    r = mixed(r_ref, rp_ref, 0, rw)
    k = mixed(k_ref, kp_ref, rw, 2 * rw)
    v = mixed(v_ref, vp_ref, 2 * rw, 3 * rw)
    lo = mixed(l_ref, lp_ref, 3 * rw, 3 * rw + RWKV_LORA)
    lw, a, g = _rwkv_lora(lo, w0_ref[...], w2_ref[...], a0_ref[...], a2_ref[...], g2_ref[...])

    ones = _head_ones()
    kk = k * kk_ref[...]
    kk = kk * lax.rsqrt(jnp.maximum(_head_sum(kk * kk, ones), 1e-12))
    k2 = k * (1.0 + (a - 1.0) * ka_ref[...])
    bonus = _head_sum(r * k2 * rk_ref[...], ones) * v
    bvec = kk * a

    ti = lax.broadcasted_iota(jnp.int32, (tt, tt), 0)
    si = lax.broadcasted_iota(jnp.int32, (tt, tt), 1)
    same = (ti // chunk) == (si // chunk)
    tri = (same & (si <= ti)).astype(BF16)
    blk = same.astype(BF16)
    cum = _dot_exact_lhs(tri, lw)
    tot = _dot_exact_lhs(blk, lw)
    e_in = jnp.exp(cum)
    e_out = jnp.exp(-cum)
    e_rest = jnp.exp(tot - cum)
    rt_ref[...] = r * e_in
    at_ref[...] = -kk * jnp.exp(cum - lw)
    kh_ref[...] = k2 * e_out
    bh_ref[...] = bvec * e_out
    kg_ref[...] = k2 * e_rest
    bg_ref[...] = bvec * e_rest
    vo_ref[...] = v
    bo_ref[...] = bonus
    go_ref[...] = g
    etot = jnp.exp(tot)
    for c in range(tt // chunk):
        gc_ref[c] = etot[c * chunk:c * chunk + 1, :]


def _rwkv_prep(p, shift, mu, w0, w2, a0, a2, g2, k_k, k_a, r_k, dm, tt):
    b, t, _ = p.shape
    rw = dm.rw
    chunk = RWKV_CHUNK
    kern = functools.partial(_rwkv_prep_kernel, tt=tt, rw=rw, chunk=chunk)
    cur = lambda c0, w: pl.BlockSpec((None, tt, w), lambda bi, i: (bi, i, _blk(c0, w)))
    prv = lambda c0, w: pl.BlockSpec(
        (None, SUBLANES, w), lambda bi, i: (bi, jnp.maximum(i * (tt // SUBLANES) - 1, 0), _blk(c0, w)))
    full = lambda shp: pl.BlockSpec(shp, lambda bi, i: (0,) * len(shp))
    tok = jax.ShapeDtypeStruct((b, t, rw), F32)
    tspec = pl.BlockSpec((None, tt, rw), lambda bi, i: (bi, i, 0))
    return pl.pallas_call(
        kern,
        out_shape=(tok,) * 9 + (jax.ShapeDtypeStruct((b, t // chunk, 1, rw), F32),),
        grid=(b, t // tt),
        in_specs=[cur(dm.c_r, rw), cur(dm.c_r + rw, rw), cur(dm.c_r + 2 * rw, rw), cur(dm.c_lora, RWKV_LORA),
                  prv(dm.c_r, rw), prv(dm.c_r + rw, rw), prv(dm.c_r + 2 * rw, rw), prv(dm.c_lora, RWKV_LORA),
                  pl.BlockSpec((None, 1, dm.rcols), lambda bi, i: (bi, 0, 0)),
                  full((1, dm.rcols)), full((1, rw)), full((RWKV_W_RANK, rw)), full((1, rw)),
                  full((RWKV_A_RANK, rw)), full((RWKV_G_RANK, rw)), full((1, rw)), full((1, rw)), full((1, rw))],
        out_specs=(tspec,) * 9 + (pl.BlockSpec((None, tt // chunk, 1, rw), lambda bi, i: (bi, i, 0, 0)),),
        compiler_params=_cparams(("parallel", "parallel")),
        name="rwkv_prep",
    )(p, p, p, p, p, p, p, p, shift, mu, w0, w2, a0, a2, g2, k_k, k_a, r_k)


def _group_norm_out(y, bonus, g, lnw, lnb, ones):
    ym = _head_sum(y, ones) * (1.0 / RWKV_HEAD)
    d = y - ym
    yv = _head_sum(d * d, ones) * (1.0 / RWKV_HEAD)
    return (d * lax.rsqrt(yv + GN_EPS) * lnw + lnb + bonus) * g


def _rwkv_scan_kernel(rt_ref, at_ref, kh_ref, bh_ref, kg_ref, bg_ref, v_ref, bo_ref, g_ref, gc_ref,
                      lnw_ref, lnb_ref, s0_ref, y_ref, so_ref, st_ref, yacc_ref, *, chunk, heads):
    c = pl.program_id(1)
    n = RWKV_HEAD

    @pl.when(c == 0)
    def _():
        st_ref[...] = s0_ref[...]

    ri = lax.broadcasted_iota(jnp.int32, (chunk, chunk), 0)
    ci = lax.broadcasted_iota(jnp.int32, (chunk, chunk), 1)
    strict = ci < ri
    incl = ci <= ri
    eye = lax.broadcasted_iota(jnp.int32, (n, n), 0) == lax.broadcasted_iota(jnp.int32, (n, n), 1)
    n_apply = int(math.log2(chunk))
    assert 2 ** n_apply == chunk

    for h in range(heads):
        sl = slice(h * n, (h + 1) * n)
        rt, at, kh, bh = rt_ref[:, sl], at_ref[:, sl], kh_ref[:, sl], bh_ref[:, sl]
        kg, bg, v = kg_ref[:, sl], bg_ref[:, sl], v_ref[:, sl]
        prod = _dot_hi_nt(jnp.concatenate([at, rt], axis=0), jnp.concatenate([kh, bh], axis=0))
        m_ak = jnp.where(strict, prod[:chunk, :chunk], 0.0)
        m_ab = jnp.where(strict, prod[:chunk, chunk:], 0.0)
        a_rk = jnp.where(incl, prod[chunk:, :chunk], 0.0)
        a_rb = jnp.where(incl, prod[chunk:, chunk:], 0.0)
        x = jnp.concatenate([at, _dot_hi(m_ak, v)], axis=1)
        pw = m_ab
        x = x + _dot_hi(pw, x)
        for _ in range(n_apply - 1):
            pw = _dot_hi(pw, pw)
            x = x + _dot_hi(pw, x)
        wt, p1 = x[:, :n], x[:, n:]
        s0 = st_ref[h]
        ws = _dot_hi(jnp.concatenate([wt, rt], axis=0), s0)
        u = p1 + ws[:chunk]
        yacc_ref[:, sl] = ws[chunk:] + _dot_hi(a_rk, v) + _dot_hi(a_rb, u)
        dg = jnp.where(eye, jnp.broadcast_to(gc_ref[:, sl], (n, n)), 0.0)
        st_ref[h] = _dot_hi_tn(jnp.concatenate([kg, bg, dg], axis=0), jnp.concatenate([v, u, s0], axis=0))

    y_ref[...] = _group_norm_out(yacc_ref[...], bo_ref[...], g_ref[...], lnw_ref[...], lnb_ref[...], _head_ones())

    @pl.when(c == pl.num_programs(1) - 1)
    def _():
        so_ref[...] = st_ref[...]


def _rwkv_scan(prep, lnw, lnb, s0t, dm):
    rt = prep[0]
    b, t, rw = rt.shape
    chunk = RWKV_CHUNK
    heads = dm.rh
    kern = functools.partial(_rwkv_scan_kernel, chunk=chunk, heads=heads)
    tspec = pl.BlockSpec((None, chunk, rw), lambda bi, c: (bi, c, 0))
    sspec = pl.BlockSpec((None, heads, RWKV_HEAD, RWKV_HEAD), lambda bi, c: (bi, 0, 0, 0))
    return pl.pallas_call(
        kern,
        out_shape=(jax.ShapeDtypeStruct((b, t, rw), F32),
                   jax.ShapeDtypeStruct((b, heads, RWKV_HEAD, RWKV_HEAD), F32)),
        grid=(b, t // chunk),
        in_specs=[tspec] * 9 + [pl.BlockSpec((None, None, 1, rw), lambda bi, c: (bi, c, 0, 0)),
                                pl.BlockSpec((1, rw), lambda bi, c: (0, 0)),
                                pl.BlockSpec((1, rw), lambda bi, c: (0, 0)),
                                sspec],
        out_specs=(tspec, sspec),
        scratch_shapes=[pltpu.VMEM((heads, RWKV_HEAD, RWKV_HEAD), F32), pltpu.VMEM((chunk, rw), F32)],
        compiler_params=_cparams(("parallel", "arbitrary")),
        name="rwkv_scan",
    )(*prep, lnw, lnb, s0t)


def _rwkv_step_kernel(r_ref, k_ref, v_ref, l_ref, sh_ref, mu_ref, w0_ref, w2_ref, a0_ref, a2_ref, g2_ref,
                      kk_ref, ka_ref, rk_ref, lnw_ref, lnb_ref, s_ref, y_ref, so_ref, *, rw, heads):
    n = RWKV_HEAD
    rows = SUBLANES

    def mixed(x_ref, c0, c1):
        x = jnp.broadcast_to(x_ref[...], (rows, c1 - c0))
        return _rwkv_mix_cols(x, sh_ref[:, c0:c1], mu_ref[:, c0:c1])

    r = mixed(r_ref, 0, rw)
    k = mixed(k_ref, rw, 2 * rw)
    v = mixed(v_ref, 2 * rw, 3 * rw)
    lo = mixed(l_ref, 3 * rw, 3 * rw + RWKV_LORA)
    lw, a, g = _rwkv_lora(lo, w0_ref[...], w2_ref[...], a0_ref[...], a2_ref[...], g2_ref[...])
    w = jnp.exp(lw)
    kk = k * kk_ref[...]
    k2 = k * (1.0 + (a - 1.0) * ka_ref[...])
    eye = lax.broadcasted_iota(jnp.int32, (n, n), 0) == lax.broadcasted_iota(jnp.int32, (n, n), 1)

    def col(rowvec):
        return jnp.sum(jnp.where(eye, jnp.broadcast_to(rowvec, (n, n)), 0.0), axis=1, keepdims=True)

    outs = []
    for h in range(heads):
        sl = slice(h * n, (h + 1) * n)
        kkh = kk[0:1, sl]
        kkh = kkh * lax.rsqrt(jnp.maximum(jnp.sum(kkh * kkh, axis=1, keepdims=True), 1e-12))
        ah, wh, k2h, rh, vh = a[0:1, sl], w[0:1, sl], k2[0:1, sl], r[0:1, sl], v[0:1, sl]
        s = s_ref[h]
        sa = -jnp.sum(s * kkh, axis=1, keepdims=True)
        s = s * wh + sa * (kkh * ah) + col(vh) * k2h
        so_ref[h] = s
        ycol = jnp.sum(s * rh, axis=1, keepdims=True)
        yrow = jnp.sum(jnp.where(eye, jnp.broadcast_to(ycol, (n, n)), 0.0), axis=0, keepdims=True)
        ym = jnp.mean(yrow, axis=1, keepdims=True)
        d = yrow - ym
        yv = jnp.mean(d * d, axis=1, keepdims=True)
        yn = d * lax.rsqrt(yv + GN_EPS) * lnw_ref[:, sl] + lnb_ref[:, sl]
        bonus = jnp.sum(rh * k2h * rk_ref[:, sl], axis=1, keepdims=True) * vh
        outs.append((yn + bonus) * g[0:1, sl])
    y_ref[...] = jnp.concatenate(outs, axis=1)


def _rwkv_step(p, shift, s0, mu, w0, w2, a0, a2, g2, k_k, k_a, r_k, lnw, lnb, dm):
    b = p.shape[0]
    rw, heads = dm.rw, dm.rh
    kern = functools.partial(_rwkv_step_kernel, rw=rw, heads=heads)
    cur = lambda c0, w: pl.BlockSpec((None, 1, w), lambda bi: (bi, 0, _blk(c0, w)))
    full = lambda shp: pl.BlockSpec(shp, lambda bi: (0,) * len(shp))
    sspec = pl.BlockSpec((None, heads, RWKV_HEAD, RWKV_HEAD), lambda bi: (bi, 0, 0, 0))
    return pl.pallas_call(
        kern,
        out_shape=(jax.ShapeDtypeStruct((b, 1, rw), F32),
                   jax.ShapeDtypeStruct((b, heads, RWKV_HEAD, RWKV_HEAD), F32)),
        grid=(b,),
        in_specs=[cur(dm.c_r, rw), cur(dm.c_r + rw, rw), cur(dm.c_r + 2 * rw, rw), cur(dm.c_lora, RWKV_LORA),
                  pl.BlockSpec((None, 1, dm.rcols), lambda bi: (bi, 0, 0)),
                  full((1, dm.rcols)), full((1, rw)), full((RWKV_W_RANK, rw)), full((1, rw)),
                  full((RWKV_A_RANK, rw)), full((RWKV_G_RANK, rw)), full((1, rw)), full((1, rw)), full((1, rw)),
                  full((1, rw)), full((1, rw)), sspec],
        out_specs=(pl.BlockSpec((None, 1, rw), lambda bi: (bi, 0, 0)), sspec),
        compiler_params=_cparams(("parallel",)),
        name="rwkv_step",
    )(p, p, p, p, shift, mu, w0, w2, a0, a2, g2, k_k, k_a, r_k, lnw, lnb, s0)


def _rope_blocks(x, cos2, sin2):
    outs = []
    for c in range(0, x.shape[1], NSA_HEAD):
        xb = x[:, c:c + NSA_HEAD]
        outs.append(xb * cos2 + pltpu.roll(xb, NSA_HEAD // 2, axis=1) * sin2)
    return outs[0] if len(outs) == 1 else jnp.concatenate(outs, axis=1)


def _nsa_prep_kernel(q_ref, c_ref, s_ref, w_ref, cos_ref, sin_ref, qo_ref, kv_ref, *rest, tt, with_means):
    cos2, sin2 = cos_ref[...], sin_ref[...]
    hw = NSA_KV_HEADS * NSA_HEAD
    qo_ref[...] = (_rope_blocks(q_ref[...], cos2, sin2) * (NSA_HEAD ** -0.5)).astype(BF16)
    for idx, ref in enumerate((c_ref, s_ref, w_ref)):
        x = ref[...]
        kr = _rope_blocks(x[:, :hw], cos2, sin2)
        kv_ref[:, idx * 2 * hw:idx * 2 * hw + hw] = kr
        kv_ref[:, idx * 2 * hw + hw:(idx + 1) * 2 * hw] = x[:, hw:]
        if idx == 0 and with_means:
            m_ref = rest[0]
            full = jnp.concatenate([kr, x[:, hw:]], axis=1)
            m_ref[...] = jnp.sum(full.reshape(tt // NSA_BLOCK, NSA_BLOCK, 2 * hw), axis=1) * (1.0 / NSA_BLOCK)


def _nsa_prep(p, cos2, sin2, dm, tt, with_means):
    b, t, _ = p.shape
    pw2 = 2 * NSA_KV_HEADS * NSA_HEAD
    kern = functools.partial(_nsa_prep_kernel, tt=tt, with_means=with_means)
    cur = lambda c0, w: pl.BlockSpec((None, tt, w), lambda bi, i: (bi, i, _blk(c0, w)))
    outs = [jax.ShapeDtypeStruct((b, t, dm.nw), BF16), jax.ShapeDtypeStruct((b, t, dm.kvc), F32)]
    ospecs = [pl.BlockSpec((None, tt, dm.nw), lambda bi, i: (bi, i, 0)),
              pl.BlockSpec((None, tt, dm.kvc), lambda bi, i: (bi, i, 0))]
    if with_means:
        outs.append(jax.ShapeDtypeStruct((b, t // NSA_BLOCK, pw2), F32))
        ospecs.append(pl.BlockSpec((None, tt // NSA_BLOCK, pw2), lambda bi, i: (bi, i, 0)))
    return pl.pallas_call(
        kern,
        out_shape=tuple(outs),
        grid=(b, t // tt),
        in_specs=[cur(dm.c_q, dm.nw), cur(dm.c_kv, pw2), cur(dm.c_kv + pw2, pw2), cur(dm.c_kv + 2 * pw2, pw2),
                  pl.BlockSpec((tt, NSA_HEAD), lambda bi, i: (i, 0)),
                  pl.BlockSpec((tt, NSA_HEAD), lambda bi, i: (i, 0))],
        out_specs=tuple(ospecs),
        compiler_params=_cparams(("parallel", "parallel")),
        name="nsa_prep",
    )(p, p, p, p, cos2, sin2)


def _masked_softmax(s, ok):
    m = jnp.max(jnp.where(ok, s, NEG), axis=1, keepdims=True)
    e = jnp.where(ok, jnp.exp(s - m), 0.0)
    den = jnp.sum(e, axis=1, keepdims=True)
    return e / jnp.where(den > 0.0, den, 1.0)


def _nsa_prompt_kernel(q_ref, kc_ref, vc_ref, ks_ref, vs_ref, kw_ref, vw_ref, gt_ref, o_ref,
                       m_ref, l_ref, acc_ref, *, g, nb, n_sel, tk, lw):
    h = pl.program_id(1)
    i = pl.program_id(2)
    qb = NSA_QBLOCK
    rows = g * qb
    q = jnp.concatenate([q_ref[:, gi * NSA_HEAD:(gi + 1) * NSA_HEAD] for gi in range(g)], axis=0)
    trow = lax.broadcasted_iota(jnp.int32, (rows, 1), 0) % qb
    qpos = i * qb + trow

    blk = lax.broadcasted_iota(jnp.int32, (1, nb), 1)
    s = _dot_nt(q, kc_ref[...])
    ok = ((blk + 1) * NSA_BLOCK - 1) <= qpos
    p_c = _masked_softmax(s, ok)
    o_cmp = _dot(p_c, vc_ref[...])

    imp = p_c[0:qb]
    for gi in range(1, g):
        imp = imp + p_c[gi * qb:(gi + 1) * qb]
    qp1 = qpos[0:qb]
    imp = jnp.where(blk == qp1 // NSA_BLOCK, g + 1.0, jnp.where(blk * NSA_BLOCK <= qp1, imp, -1.0))
    cnt = jnp.zeros((qb, nb), F32)
    for bi in range(nb):
        ci = imp[:, bi:bi + 1]
        beats = (ci > imp) | ((ci == imp) & (blk > bi))
        cnt = cnt + jnp.where(beats, 1.0, 0.0)
    sel = jnp.where(cnt < n_sel, 1.0, 0.0).astype(BF16)
    sel = jnp.concatenate([sel] * g, axis=0)

    m_ref[...] = jnp.full((rows, 1), NEG, F32)
    l_ref[...] = jnp.zeros((rows, 1), F32)
    acc_ref[...] = jnp.zeros((rows, NSA_HEAD), F32)
    bpt = tk // NSA_BLOCK
    brow = lax.broadcasted_iota(jnp.int32, (nb, tk), 0)
    bcol = lax.broadcasted_iota(jnp.int32, (nb, tk), 1) // NSA_BLOCK
    kcol = lax.broadcasted_iota(jnp.int32, (1, tk), 1)

    def body(kt, carry):
        k0 = pl.multiple_of(kt * tk, tk)
        kb = ks_ref[pl.ds(k0, tk), :]
        vb = vs_ref[pl.ds(k0, tk), :]
        sc = _dot_nt(q, kb)
        expand = (brow == bcol + kt * bpt).astype(BF16)
        picked = jnp.dot(sel, expand, preferred_element_type=F32) > 0.5
        valid = picked & ((k0 + kcol) <= qpos)
        m_old = m_ref[...]
        m_new = jnp.maximum(m_old, jnp.max(jnp.where(valid, sc, NEG), axis=1, keepdims=True))
        pr = jnp.where(valid, jnp.exp(sc - m_new), 0.0)
        alpha = jnp.exp(m_old - m_new)
        l_ref[...] = alpha * l_ref[...] + jnp.sum(pr, axis=1, keepdims=True)
        acc_ref[...] = alpha * acc_ref[...] + _dot(pr, vb)
        m_ref[...] = m_new
        return carry

    n_kt = ((i + 1) * qb + tk - 1) // tk
    lax.fori_loop(0, n_kt, body, 0)
    o_sel = acc_ref[...] / l_ref[...]

    w0 = pl.multiple_of(jnp.maximum(i * qb - NSA_WINDOW, 0), qb)
    kwb = kw_ref[pl.ds(w0, lw), :]
    vwb = vw_ref[pl.ds(w0, lw), :]
    dp = qpos - (w0 + lax.broadcasted_iota(jnp.int32, (1, lw), 1))
    p_w = _masked_softmax(_dot_nt(q, kwb), (dp >= 0) & (dp <= NSA_WINDOW))
    o_win = _dot(p_w, vwb)

    gs = _sigmoid(gt_ref[...])
    for gi in range(g):
        acc = None
        for ci, ob in enumerate((o_cmp, o_sel, o_win)):
            c0 = gi * 3 + ci
            c1 = (g + gi) * 3 + ci
            gate = jnp.where(h == 0, gs[:, c0:c0 + 1], gs[:, c1:c1 + 1])
            term = ob[gi * qb:(gi + 1) * qb] * gate
            acc = term if acc is None else acc + term
        o_ref[:, gi * NSA_HEAD:(gi + 1) * NSA_HEAD] = acc


def _nsa_prompt(qr, kvr, kcvc, p, dm):
    b, t, _ = qr.shape
    assert NSA_KV_HEADS == 2
    g = dm.g
    nb = t // NSA_BLOCK
    n_sel = min(NSA_TOPK, nb)
    tk = min(512, t)
    lw = NSA_WINDOW + NSA_QBLOCK
    assert t % tk == 0 and t >= lw
    hd = NSA_HEAD
    kern = functools.partial(_nsa_prompt_kernel, g=g, nb=nb, n_sel=n_sel, tk=tk, lw=lw)
    kvspec = lambda c: pl.BlockSpec((None, t, hd), lambda bi, h, i, c=c: (bi, 0, 2 * c + h))
    rows = g * NSA_QBLOCK
    return pl.pallas_call(
        kern,
        out_shape=jax.ShapeDtypeStruct((b, t, dm.nw), F32),
        grid=(b, NSA_KV_HEADS, t // NSA_QBLOCK),
        in_specs=[pl.BlockSpec((None, NSA_QBLOCK, g * hd), lambda bi, h, i: (bi, i, h)),
                  pl.BlockSpec((None, nb, hd), lambda bi, h, i: (bi, 0, h)),
                  pl.BlockSpec((None, nb, hd), lambda bi, h, i: (bi, 0, 2 + h)),
                  kvspec(2), kvspec(3), kvspec(4), kvspec(5),
                  pl.BlockSpec((None, NSA_QBLOCK, LANES), lambda bi, h, i: (bi, i, _blk(dm.c_ng, LANES)))],
        out_specs=pl.BlockSpec((None, NSA_QBLOCK, g * hd), lambda bi, h, i: (bi, i, h)),
        scratch_shapes=[pltpu.VMEM((rows, 1), F32), pltpu.VMEM((rows, 1), F32), pltpu.VMEM((rows, hd), F32)],
        compiler_params=_cparams(("parallel", "parallel", "arbitrary")),
        name="nsa_prompt",
    )(qr, kcvc, kcvc, kvr, kvr, kvr, kvr, p)


def _page_means_kernel(pt_ref, c_ref, o_ref):
    x = c_ref[...]
    nbp = x.shape[0] // NSA_BLOCK
    o_ref[...] = jnp.sum(x.reshape(nbp, NSA_BLOCK, x.shape[1]), axis=1) * (1.0 / NSA_BLOCK)


def _page_means(cache_l, page_table):
    _, page, cols = cache_l.shape
    b, n_pages = page_table.shape
    half = cols // 2
    nbp = page // NSA_BLOCK
    out = pl.pallas_call(
        _page_means_kernel,
        out_shape=jax.ShapeDtypeStruct((b, n_pages, nbp, half), F32),
        grid_spec=pltpu.PrefetchScalarGridSpec(
            num_scalar_prefetch=1,
            grid=(b, n_pages),
            in_specs=[pl.BlockSpec((None, page, half), lambda bi, j, pt: (pt[bi, j], 0, 0))],
            out_specs=pl.BlockSpec((None, None, nbp, half), lambda bi, j, pt: (bi, j, 0, 0)),
        ),
        compiler_params=_cparams(("parallel", "parallel")),
        name="nsa_page_means",
    )(page_table, cache_l)
    return out.reshape(b, n_pages * nbp, half)


def _nsa_choose_kernel(q_ref, m_ref, oc_ref, idx_ref, *, g, nbp, n_sel):
    hd = NSA_HEAD
    blk = lax.broadcasted_iota(jnp.int32, (1, nbp), 1)
    lane = lax.broadcasted_iota(jnp.int32, (1, LANES), 1)
    for h in range(NSA_KV_HEADS):
        q = jnp.concatenate([q_ref[:, (h * g + gi) * hd:(h * g + gi + 1) * hd] for gi in range(g)], axis=0)
        q = jnp.concatenate([q, jnp.zeros((SUBLANES - g, hd), q.dtype)], axis=0) if g < SUBLANES else q
        kc = m_ref[:, h * hd:(h + 1) * hd]
        vc = m_ref[:, (NSA_KV_HEADS + h) * hd:(NSA_KV_HEADS + h + 1) * hd]
        s = _dot_nt(q, kc)
        m = jnp.max(s, axis=1, keepdims=True)
        e = jnp.exp(s - m)
        p_c = e / jnp.sum(e, axis=1, keepdims=True)
        oc_ref[h] = _dot(p_c, vc)[0:g]
        imp = jnp.sum(p_c[0:g], axis=0, keepdims=True)
        idx = jnp.where(lane == 0, nbp, 0)
        for it in range(1, n_sel):
            best = jnp.max(imp, axis=1, keepdims=True)
            j = jnp.min(jnp.where(imp == best, blk, nbp), axis=1, keepdims=True)
            idx = jnp.where(lane == it, j, idx)
            imp = jnp.where(blk == j, -2.0, imp)
        idx_ref[h] = idx


def _nsa_choose(qr, means, dm, n_sel):
    b = qr.shape[0]
    nbp = means.shape[1]
    g = dm.g
    kern = functools.partial(_nsa_choose_kernel, g=g, nbp=nbp, n_sel=n_sel)
    return pl.pallas_call(
        kern,
        out_shape=(jax.ShapeDtypeStruct((b, NSA_KV_HEADS, g, NSA_HEAD), F32),
                   jax.ShapeDtypeStruct((b, NSA_KV_HEADS, 1, LANES), jnp.int32)),
        grid=(b,),
        in_specs=[pl.BlockSpec((None, 1, dm.nw), lambda bi: (bi, 0, 0)),
                  pl.BlockSpec((None, nbp, means.shape[2]), lambda bi: (bi, 0, 0))],
        out_specs=(pl.BlockSpec((None, NSA_KV_HEADS, g, NSA_HEAD), lambda bi: (bi, 0, 0, 0)),
                   pl.BlockSpec((None, NSA_KV_HEADS, 1, LANES), lambda bi: (bi, 0, 0, 0))),
        compiler_params=_cparams(("parallel",)),
        name="nsa_choose",
    )(qr, means)


def _nsa_sample_kernel(idx_ref, pt_ref, q_ref, ks_ref, vs_ref, kn_ref, vn_ref, kw_ref, vw_ref, kwn_ref, vwn_ref,
                       oc_ref, gt_ref, o_ref, m_ref, l_ref, acc_ref, *, g, nbp, n_sel):
    h = pl.program_id(1)
    j = pl.program_id(2)
    hd = NSA_HEAD
    rows = SUBLANES
    q = jnp.concatenate([q_ref[:, gi * hd:(gi + 1) * hd] for gi in range(g)], axis=0)
    if g < rows:
        q = jnp.concatenate([q, jnp.zeros((rows - g, hd), q.dtype)], axis=0)

    @pl.when(j == 0)
    def _():
        s0 = jnp.sum(q.astype(F32) * kn_ref[...].astype(BF16).astype(F32), axis=1, keepdims=True)
        m_ref[...] = s0
        l_ref[...] = jnp.ones((rows, 1), F32)
        acc_ref[...] = jnp.broadcast_to(vn_ref[...].astype(BF16).astype(F32), (rows, hd))

    @pl.when(j > 0)
    def _():
        sc = _dot_nt(q, ks_ref[...])
        m_old = m_ref[...]
        m_new = jnp.maximum(m_old, jnp.max(sc, axis=1, keepdims=True))
        pr = jnp.exp(sc - m_new)
        alpha = jnp.exp(m_old - m_new)
        l_ref[...] = alpha * l_ref[...] + jnp.sum(pr, axis=1, keepdims=True)
        acc_ref[...] = alpha * acc_ref[...] + _dot(pr, vs_ref[...])
        m_ref[...] = m_new

    @pl.when(j == n_sel - 1)
    def _():
        o_sel = acc_ref[...] / l_ref[...]
        sw = _dot_nt(q, kw_ref[...])
        sn = jnp.sum(q.astype(F32) * kwn_ref[...].astype(BF16).astype(F32), axis=1, keepdims=True)
        mw = jnp.maximum(jnp.max(sw, axis=1, keepdims=True), sn)
        ew = jnp.exp(sw - mw)
        en = jnp.exp(sn - mw)
        den = jnp.sum(ew, axis=1, keepdims=True) + en
        pn = (en / den).astype(BF16).astype(F32)
        o_win = _dot(ew / den, vw_ref[...]) + pn * vwn_ref[...].astype(BF16).astype(F32)
        gs = _sigmoid(gt_ref[...])
        o_cmp = oc_ref[...]
        for gi in range(g):
            acc = None
            for ci, ob in enumerate((o_cmp, o_sel, o_win)):
                c0 = gi * 3 + ci
                c1 = (g + gi) * 3 + ci
                gate = jnp.where(h == 0, gs[:, c0:c0 + 1], gs[:, c1:c1 + 1])
                term = ob[gi:gi + 1] * gate
                acc = term if acc is None else acc + term
            o_ref[:, gi * hd:(gi + 1) * hd] = acc


def _nsa_sample(qr, kvr, cache_l, page_table, idx, win, o_cmp, p, dm, n_sel):
    b = qr.shape[0]
    g, hd = dm.g, NSA_HEAD
    bpp = cache_l.shape[1]
    nbp = page_table.shape[1] * bpp
    lwin = win.shape[1]
    kern = functools.partial(_nsa_sample_kernel, g=g, nbp=nbp, n_sel=n_sel)

    def sel_map(c):
        def f(bi, h, j, idx_ref, pt_ref):
            blk = idx_ref[(bi * NSA_KV_HEADS + h) * LANES + j]
            n_past = pt_ref.shape[1] * bpp
            blk = jnp.minimum(blk, n_past - 1)
            return (pt_ref[bi, blk // bpp], blk % bpp, 0, 2 * c + h)
        return f

    newspec = lambda c: pl.BlockSpec((None, 1, hd), lambda bi, h, j, ir, pr, c=c: (bi, 0, 2 * c + h))
    return pl.pallas_call(
        kern,
        out_shape=jax.ShapeDtypeStruct((b, 1, dm.nw), F32),
        grid_spec=pltpu.PrefetchScalarGridSpec(
            num_scalar_prefetch=2,
            grid=(b, NSA_KV_HEADS, n_sel),
            in_specs=[pl.BlockSpec((None, 1, g * hd), lambda bi, h, j, ir, pr: (bi, 0, h)),
                      pl.BlockSpec((None, None, NSA_BLOCK, hd), sel_map(2)),
                      pl.BlockSpec((None, None, NSA_BLOCK, hd), sel_map(3)),
                      newspec(2), newspec(3),
                      pl.BlockSpec((None, lwin, hd), lambda bi, h, j, ir, pr: (bi, 0, h)),
                      pl.BlockSpec((None, lwin, hd), lambda bi, h, j, ir, pr: (bi, 0, NSA_KV_HEADS + h)),
                      newspec(4), newspec(5),
                      pl.BlockSpec((None, None, g, hd), lambda bi, h, j, ir, pr: (bi, h, 0, 0)),
                      pl.BlockSpec((None, 1, LANES), lambda bi, h, j, ir, pr: (bi, 0, _blk(dm.c_ng, LANES)))],
            out_specs=pl.BlockSpec((None, 1, g * hd), lambda bi, h, j, ir, pr: (bi, 0, h)),
            scratch_shapes=[pltpu.VMEM((SUBLANES, 1), F32), pltpu.VMEM((SUBLANES, 1), F32),
                            pltpu.VMEM((SUBLANES, hd), F32)],
        ),
        compiler_params=_cparams(("parallel", "parallel", "arbitrary")),
        name="nsa_sample",
    )(idx, page_table, qr, cache_l, cache_l, kvr, kvr, win, win, kvr, kvr, o_cmp, p)


def _rope_tables(pos):
    half = NSA_HEAD // 2
    inv = jnp.exp(-math.log(ROPE_THETA) * jnp.arange(half, dtype=F32) / half)
    ang = pos.astype(F32)[:, None] * inv[None, :]
    cos, sin = jnp.cos(ang), jnp.sin(ang)
    return jnp.concatenate([cos, cos], axis=1), jnp.concatenate([-sin, sin], axis=1)


def _reorder_w_in(w_in, dm):
    o1, o2, o3 = dm.o1, dm.o2, dm.o3
    r0 = o1
    lora0 = o1 + 3 * dm.rw
    q0 = o2
    kv0 = o2 + dm.nw
    ng0 = kv0 + dm.kvc
    parts = [w_in[..., o3:], w_in[..., :o1], w_in[..., r0:lora0], w_in[..., q0:kv0], w_in[..., kv0:ng0],
             w_in[..., lora0:o2], w_in[..., ng0:o3]]
    used = sum(x.shape[-1] for x in parts)
    parts.append(jnp.zeros(w_in.shape[:-1] + (dm.np - used,), w_in.dtype))
    return jnp.concatenate(parts, axis=-1).astype(BF16)


def _shift_cols(p_last, dm):
    return jnp.concatenate([p_last[..., dm.c_r:dm.c_r + 3 * dm.rw], p_last[..., dm.c_lora:dm.c_lora + RWKV_LORA]],
                           axis=-1)


def _row(x):
    return x.reshape(1, -1)


def kernel(x_prompt, x_sample, cache_nsa_kv, page_table, state_nsa_window, state_wkv, state_rwkv_shift,
           state_pool, state_ffn_conv, norm_mix, w_in, pool_w, pool_scale, rwkv_mu, rwkv_w0, rwkv_w2,
           rwkv_a0, rwkv_a2, rwkv_g2, rwkv_k_k, rwkv_k_a, rwkv_r_k, rwkv_ln_w, rwkv_ln_b, w_branch, w_out,
           norm_ffn, ffn_up, ffn_conv, ffn_conv_b, ffn_down, norm_final):
    bp, sp, d = x_prompt.shape
    bs, ts, _ = x_sample.shape
    assert ts == 1
    depth = w_in.shape[0]
    dm = Dims(d)
    ff = dm.ff
    hd = NSA_HEAD
    page = cache_nsa_kv.shape[2]
    n_pages = page_table.shape[1]
    past = n_pages * page
    kvw = 4 * NSA_KV_HEADS * hd

    w_in_r = _reorder_w_in(w_in, dm)
    pool_w_b = pool_w.astype(BF16)
    wb_b = w_branch.astype(BF16)
    wo_b = w_out.astype(BF16)
    up_b = ffn_up.astype(BF16)
    down_b = ffn_down.astype(BF16)
    g2_b = rwkv_g2.astype(BF16)
    kk_r = rwkv_k_k.reshape(depth, 1, dm.rw)
    ka_r = rwkv_k_a.reshape(depth, 1, dm.rw)
    rk_r = rwkv_r_k.reshape(depth, 1, dm.rw)

    mp = bp * sp
    tm_p = min(512, sp)
    tn_ff = 512 if ff % 512 == 0 else 256
    tt_pool = min(512, sp)
    tt_rwkv = min(256, sp)
    tt_nsa = min(512, sp)
    tps = sp // tm_p

    def rwkv_params(l):
        return (_row(rwkv_mu[l]), _row(rwkv_w0[l]), rwkv_w2[l], _row(rwkv_a0[l]), rwkv_a2[l], g2_b[l],
                kk_r[l], ka_r[l], rk_r[l])

    cos_p, sin_p = _rope_tables(jnp.arange(sp, dtype=jnp.int32))
    x = x_prompt.reshape(mp, d)
    zeros_hist16 = jnp.zeros((bp, POOL_HIST + 1, dm.pw), F32)
    zeros_shift = jnp.zeros((bp, 1, dm.rcols), F32)
    zeros_wkv = jnp.zeros((bp, dm.rh, RWKV_HEAD, RWKV_HEAD), F32)
    zeros_conv = jnp.zeros((bp, CONV_W - 1, 2 * ff), F32)
    wl = min(NSA_WINDOW, sp)
    p_kv, p_win, p_wkv, p_shift, p_pool, p_conv = [], [], [], [], [], []
    for l in range(depth):
        p2 = _norm_matmul(x, _row(norm_mix[l]), w_in_r[l], tm_p, dm.tn_in)
        p = p2.reshape(bp, sp, dm.np)
        ya = _pool_prompt(p, zeros_hist16, pool_w_b[l], _row(pool_scale[l]), dm, tt_pool)
        prep = _rwkv_prep(p, zeros_shift, *rwkv_params(l), dm, tt_rwkv)
        yb, s_t = _rwkv_scan(prep, _row(rwkv_ln_w[l]), _row(rwkv_ln_b[l]), zeros_wkv, dm)
        qr, kvr, kcvc = _nsa_prep(p, cos_p, sin_p, dm, tt_nsa, True)
        yc = _nsa_prompt(qr, kvr, kcvc, p, dm)
        mrg = _branch_merge(ya.reshape(mp, -1), yb.reshape(mp, -1), yc.reshape(mp, -1), p2, wb_b[l], dm,
                            tm_p, min(1024, d))
        x = _out_proj(x, mrg, wo_b[l], tm_p, min(1024, d))
        x, za, zv = _ffn(x, _row(norm_ffn[l]), up_b[l], ffn_conv[l], _row(ffn_conv_b[l]), down_b[l],
                         zeros_conv, ff, tm_p, tn_ff, tps, False)
        p_kv.append(kvr[:, :, :kvw].reshape(bp, sp, 4, NSA_KV_HEADS, hd))
        p_win.append(kvr[:, sp - wl:, kvw:].reshape(bp, wl, 2, NSA_KV_HEADS, hd))
        p_wkv.append(jnp.swapaxes(s_t, -1, -2))
        p_shift.append(_shift_cols(p[:, sp - 1:, :], dm))
        p_pool.append(p[:, sp - POOL_HIST:, dm.c_pool:dm.c_pool + dm.pw])
        zl = jnp.concatenate([za, zv], axis=-1).reshape(bp, tps, 2, 2 * ff)
        p_conv.append(zl[:, tps - 1])
    y_prompt = _final_norm(x, _row(norm_final), tm_p).reshape(bp, sp, d)

    pos_s = past
    cos_s, sin_s = _rope_tables(jnp.full((1,), pos_s, dtype=jnp.int32))
    xs = x_sample.reshape(bs, d)
    nbp = past // NSA_BLOCK
    n_sel = min(NSA_TOPK, nbp + 1)
    s_kv, s_win, s_wkv, s_shift, s_pool, s_conv = [], [], [], [], [], []
    for l in range(depth):
        p2 = _norm_matmul(xs, _row(norm_mix[l]), w_in_r[l], bs, dm.tn_in)
        p = p2.reshape(bs, 1, dm.np)
        e16 = jnp.concatenate([state_pool[l], p[:, :, dm.c_pool:dm.c_pool + dm.pw]], axis=1)
        ya = _pool_sample(e16, pool_w_b[l], _row(pool_scale[l]), dm, pos_s)
        yb, s_new = _rwkv_step(p, state_rwkv_shift[l], state_wkv[l], *rwkv_params(l),
                               _row(rwkv_ln_w[l]), _row(rwkv_ln_b[l]), dm)
        qr, kvr = _nsa_prep(p, cos_s, sin_s, dm, 1, False)
        cache_l = cache_nsa_kv[l].reshape(-1, page, kvw)
        means = _page_means(cache_l, page_table)
        o_cmp, idx = _nsa_choose(qr, means, dm, n_sel)
        win = state_nsa_window[l].reshape(bs, -1, 2 * NSA_KV_HEADS * hd)
        yc = _nsa_sample(qr, kvr, cache_l.reshape(-1, page // NSA_BLOCK, NSA_BLOCK, kvw), page_table,
                         idx.reshape(-1), win, o_cmp, p, dm, n_sel)
        mrg = _branch_merge(ya, yb.reshape(bs, -1), yc.reshape(bs, -1), p2, wb_b[l], dm, bs, min(1024, d))
        xs = _out_proj(xs, mrg, wo_b[l], bs, min(1024, d))
        xs, za, zv = _ffn(xs, _row(norm_ffn[l]), up_b[l], ffn_conv[l], _row(ffn_conv_b[l]), down_b[l],
                          state_ffn_conv[l], ff, bs, tn_ff, 1, True)
        s_kv.append(kvr[:, :, :kvw].reshape(bs, 1, 4, NSA_KV_HEADS, hd))
        wk = jnp.concatenate([win, kvr[:, :, kvw:]], axis=1)
        s_win.append(wk[:, -NSA_WINDOW:].reshape(bs, -1, 2, NSA_KV_HEADS, hd))
        s_wkv.append(s_new)
        s_shift.append(_shift_cols(p, dm))
        s_pool.append(e16[:, 1:])
        z_new = jnp.concatenate([za, zv], axis=-1)[:, None, :]
        s_conv.append(jnp.concatenate([state_ffn_conv[l][:, 1:], z_new], axis=1))
    y_sample = _final_norm(xs, _row(norm_final), bs).reshape(bs, 1, d)

    st = lambda xs_: jnp.stack(xs_)
    return (y_prompt, y_sample, st(p_kv), st(p_win), st(p_wkv), st(p_shift), st(p_pool), st(p_conv),
            st(s_kv), st(s_win), st(s_wkv), st(s_shift), st(s_pool), st(s_conv))
```

```python
import functools
import math

import jax
import jax.numpy as jnp
from jax import lax
from jax.experimental import pallas as pl
from jax.experimental.pallas import tpu as pltpu

F32 = jnp.float32
BF16 = jnp.bfloat16
HIGHEST = lax.Precision.HIGHEST

POOL_GROUPS = 4
POOL_WINDOWS = (2, 4, 8, 16)
POOL_HIST = 15
RWKV_HEAD = 64
RWKV_W_RANK = 64
RWKV_A_RANK = 64
RWKV_G_RANK = 128
RWKV_LORA = RWKV_W_RANK + RWKV_A_RANK + RWKV_G_RANK
RWKV_DECAY_SCALE = 0.6065306597126334
GN_EPS = 64e-5
NSA_HEAD = 128
NSA_KV_HEADS = 2
NSA_BLOCK = 64
NSA_TOPK = 16
NSA_WINDOW = 512
NSA_QBLOCK = 128
N_BRANCH = 3
CONV_W = 3
ROPE_THETA = 10000.0
RMS_EPS = 1e-6
NEG = -1e30
M_INIT = -1e29

LANES = 128
SUBLANES = 8
VMEM_LIMIT = 56 * 1024 * 1024
RWKV_CHUNK = 64


class Dims:
    def __init__(self, d_model):
        d = d_model
        self.d = d
        self.pw = d // 2
        self.cg = self.pw // POOL_GROUPS
        self.rw = d // 2
        self.rh = self.rw // RWKV_HEAD
        self.rcols = 3 * self.rw + RWKV_LORA
        self.nw = d // 2
        self.nh = self.nw // NSA_HEAD
        self.g = self.nh // NSA_KV_HEADS
        self.kvc = 6 * NSA_KV_HEADS * NSA_HEAD
        self.ngate = 3 * self.nh
        self.ff = 256 * ((8 * d // 3 + 255) // 256)
        self.o1 = self.pw
        self.o2 = self.o1 + self.rcols
        self.o3 = self.o2 + self.nw + self.kvc + self.ngate
        self.c_pool = 0
        self.c_r = self.o1
        self.c_lora = self.o1 + 3 * self.rw
        self.c_q = self.o2
        self.c_kv = self.o2 + self.nw
        self.c_ng = self.c_kv + self.kvc
        self.tn_in = 512
        self.np = -(-self.o3 // self.tn_in) * self.tn_in
        self.part = NSA_KV_HEADS * NSA_HEAD


def _blk(offset, width):
    assert offset % width == 0, (offset, width)
    return offset // width


def _cparams(sem):
    return pltpu.CompilerParams(dimension_semantics=sem, vmem_limit_bytes=VMEM_LIMIT)


def _dot(a, b):
    return jnp.dot(a.astype(BF16), b.astype(BF16), preferred_element_type=F32)


def _dot_nt(a, b):
    return lax.dot_general(a.astype(BF16), b.astype(BF16), (((1,), (1,)), ((), ())),
                           preferred_element_type=F32)


def _dot_hi(a, b):
    return jnp.dot(a, b, precision=HIGHEST, preferred_element_type=F32)


_NN = (((1,), (0,)), ((), ()))
_NT = (((1,), (1,)), ((), ()))
_TN = (((0,), (0,)), ((), ()))


def _split2(x):
    hi = x.astype(BF16)
    return hi, (x - hi.astype(F32)).astype(BF16)


def _mm3(a, b, dims):
    d = lambda p, q: lax.dot_general(p, q, dims, preferred_element_type=F32)
    return d(a[0], b[0]) + d(a[0], b[1]) + d(a[1], b[0])


def _split3(x):
    hi = x.astype(BF16)
    r1 = x - hi.astype(F32)
    mid = r1.astype(BF16)
    lo = (r1 - mid.astype(F32)).astype(BF16)
    return hi, mid, lo


def _dot_exact_lhs(m_bf16, x):
    hi, mid, lo = _split3(x)
    d = lambda p: jnp.dot(m_bf16, p, preferred_element_type=F32)
    return d(hi) + d(mid) + d(lo)


def _dot_exact_rhs(x, m_bf16):
    hi, mid, lo = _split3(x)
    d = lambda p: jnp.dot(p, m_bf16, preferred_element_type=F32)
    return d(hi) + d(mid) + d(lo)


def _head_ones():
    i = lax.broadcasted_iota(jnp.int32, (LANES, LANES), 0) // RWKV_HEAD
    j = lax.broadcasted_iota(jnp.int32, (LANES, LANES), 1) // RWKV_HEAD
    return (i == j).astype(BF16)


def _head_sum(x, ones):
    parts = [_dot_exact_rhs(x[:, c:c + LANES], ones) for c in range(0, x.shape[1], LANES)]
    return parts[0] if len(parts) == 1 else jnp.concatenate(parts, axis=1)


def _rmsnorm_val(x, g):
    ms = jnp.mean(x * x, axis=-1, keepdims=True)
    return x * lax.rsqrt(ms + RMS_EPS) * g


def _sigmoid(x):
    return 1.0 / (1.0 + jnp.exp(-x))


def _norm_matmul_kernel(x_ref, g_ref, w_ref, o_ref, h_ref):
    @pl.when(pl.program_id(1) == 0)
    def _():
        h_ref[...] = _rmsnorm_val(x_ref[...], g_ref[...]).astype(BF16)

    o_ref[...] = jnp.dot(h_ref[...], w_ref[...], preferred_element_type=F32)


def _norm_matmul(x, g, w, tm, tn):
    m, d = x.shape
    n = w.shape[1]
    return pl.pallas_call(
        _norm_matmul_kernel,
        out_shape=jax.ShapeDtypeStruct((m, n), F32),
        grid=(m // tm, n // tn),
        in_specs=[pl.BlockSpec((tm, d), lambda i, j: (i, 0)),
                  pl.BlockSpec((1, d), lambda i, j: (0, 0)),
                  pl.BlockSpec((d, tn), lambda i, j: (0, j))],
        out_specs=pl.BlockSpec((tm, tn), lambda i, j: (i, j)),
        scratch_shapes=[pltpu.VMEM((tm, d), BF16)],
        compiler_params=_cparams(("parallel", "arbitrary")),
        name="norm_matmul",
    )(x, g, w)


def _branch_merge_kernel(ya_ref, yb_ref, yc_ref, ga_ref, gb_ref, gc_ref, w_ref, o_ref):
    acc = _dot(ya_ref[...], w_ref[0]) * _sigmoid(ga_ref[...])
    acc = acc + _dot(yb_ref[...], w_ref[1]) * _sigmoid(gb_ref[...])
    acc = acc + _dot(yc_ref[...], w_ref[2]) * _sigmoid(gc_ref[...])
    o_ref[...] = acc.astype(BF16)


def _branch_merge(ya, yb, yc, p, wb, dm, tm, tn):
    m = ya.shape[0]
    d, hw = dm.d, dm.d // 2
    nb = d // tn
    yspec = pl.BlockSpec((tm, hw), lambda j, i: (i, 0))
    gspec = lambda k: pl.BlockSpec((tm, tn), lambda j, i, k=k: (i, k * nb + j))
    return pl.pallas_call(
        _branch_merge_kernel,
        out_shape=jax.ShapeDtypeStruct((m, d), BF16),
        grid=(nb, m // tm),
        in_specs=[yspec, yspec, yspec, gspec(0), gspec(1), gspec(2),
                  pl.BlockSpec((N_BRANCH, hw, tn), lambda j, i: (0, 0, j))],
        out_specs=pl.BlockSpec((tm, tn), lambda j, i: (i, j)),
        compiler_params=_cparams(("parallel", "parallel")),
        name="branch_merge",
    )(ya, yb, yc, p, p, p, wb)


def _out_proj_kernel(x_ref, m_ref, w_ref, o_ref):
    o_ref[...] = x_ref[...] + jnp.dot(m_ref[...], w_ref[...], preferred_element_type=F32)


def _out_proj(x, mrg, wo, tm, tn):
    m, d = x.shape
    return pl.pallas_call(
        _out_proj_kernel,
        out_shape=jax.ShapeDtypeStruct((m, d), F32),
        grid=(m // tm, d // tn),
        in_specs=[pl.BlockSpec((tm, tn), lambda i, j: (i, j)),
                  pl.BlockSpec((tm, d), lambda i, j: (i, 0)),
                  pl.BlockSpec((d, tn), lambda i, j: (0, j))],
        out_specs=pl.BlockSpec((tm, tn), lambda i, j: (i, j)),
        compiler_params=_cparams(("parallel", "parallel")),
        name="out_proj",
    )(x, mrg, wo)


def _ffn_kernel(x_ref, g_ref, wa_ref, wv_ref, cwa_ref, cwv_ref, cba_ref, cbv_ref, wd_ref,
                ha_ref, hv_ref, o_ref, za_ref, zv_ref, h_ref, ca_ref, cv_ref, *, tm, tps, rows_are_seqs):
    i = pl.program_id(0)
    j = pl.program_id(1)

    @pl.when(j == 0)
    def _():
        x = x_ref[...]
        h_ref[...] = _rmsnorm_val(x, g_ref[...]).astype(BF16)
        o_ref[...] = x

    h = h_ref[...]

    def conv(z, cw_ref, cb_ref, hist_ref, carry_ref, zlast_ref):
        if rows_are_seqs:
            zs2 = hist_ref[:, 0, :]
            zs1 = hist_ref[:, 1, :]
            zlast_ref[...] = z
        else:
            @pl.when((i % tps) == 0)
            def _():
                carry_ref[j] = hist_ref[...]

            prev = carry_ref[j]
            row = lax.broadcasted_iota(jnp.int32, z.shape, 0)
            zs1 = jnp.where(row == 0, prev[1:2], pltpu.roll(z, 1, axis=0))
            zs2 = jnp.where(row == 0, prev[0:1], jnp.where(row == 1, prev[1:2], pltpu.roll(z, 2, axis=0)))
            last = z[tm - 2:tm]
            carry_ref[j] = last
            zlast_ref[...] = last
        return zs2 * cw_ref[0:1] + zs1 * cw_ref[1:2] + z * cw_ref[2:3] + cb_ref[...]

    za = conv(jnp.dot(h, wa_ref[...], preferred_element_type=F32), cwa_ref, cba_ref, ha_ref, ca_ref, za_ref)
    zv = conv(jnp.dot(h, wv_ref[...], preferred_element_type=F32), cwv_ref, cbv_ref, hv_ref, cv_ref, zv_ref)
    act = za * _sigmoid(za) * zv
    o_ref[...] += _dot(act, wd_ref[...])


def _ffn(x, g, w_up, cw, cb, w_down, hist, ff, tm, tn, tps, rows_are_seqs):
    m, d = x.shape
    nj = ff // tn
    nm = m // tm
    if rows_are_seqs:
        assert nm == 1
        hspec_a = pl.BlockSpec((m, 2, tn), lambda i, j: (0, 0, j))
        hspec_v = pl.BlockSpec((m, 2, tn), lambda i, j: (0, 0, nj + j))
        zshape = jax.ShapeDtypeStruct((m, ff), F32)
        zspec = pl.BlockSpec((m, tn), lambda i, j: (0, j))
    else:
        hspec_a = pl.BlockSpec((None, 2, tn), lambda i, j: (i // tps, 0, j))
        hspec_v = pl.BlockSpec((None, 2, tn), lambda i, j: (i // tps, 0, nj + j))
        zshape = jax.ShapeDtypeStruct((nm, 2, ff), F32)
        zspec = pl.BlockSpec((None, 2, tn), lambda i, j: (i, 0, j))
    kern = functools.partial(_ffn_kernel, tm=tm, tps=tps, rows_are_seqs=rows_are_seqs)
    return pl.pallas_call(
        kern,
        out_shape=(jax.ShapeDtypeStruct((m, d), F32), zshape, zshape),
        grid=(nm, nj),
        in_specs=[pl.BlockSpec((tm, d), lambda i, j: (i, 0)),
                  pl.BlockSpec((1, d), lambda i, j: (0, 0)),
                  pl.BlockSpec((d, tn), lambda i, j: (0, j)),
                  pl.BlockSpec((d, tn), lambda i, j: (0, nj + j)),
                  pl.BlockSpec((CONV_W, tn), lambda i, j: (0, j)),
                  pl.BlockSpec((CONV_W, tn), lambda i, j: (0, nj + j)),
                  pl.BlockSpec((1, tn), lambda i, j: (0, j)),
                  pl.BlockSpec((1, tn), lambda i, j: (0, nj + j)),
                  pl.BlockSpec((tn, d), lambda i, j: (j, 0)),
                  hspec_a, hspec_v],
        out_specs=(pl.BlockSpec((tm, d), lambda i, j: (i, 0)), zspec, zspec),
        scratch_shapes=[pltpu.VMEM((tm, d), BF16),
                        pltpu.VMEM((nj, 2, tn), F32),
                        pltpu.VMEM((nj, 2, tn), F32)],
        compiler_params=_cparams(("arbitrary", "arbitrary")),
        name="conv_ffn",
    )(x, g, w_up, w_up, cw, cw, cb, cb, w_down, hist, hist)


def _final_norm_kernel(x_ref, g_ref, o_ref):
    o_ref[...] = _rmsnorm_val(x_ref[...], g_ref[...])


def _final_norm(x, g, tm):
    m, d = x.shape
    return pl.pallas_call(
        _final_norm_kernel,
        out_shape=jax.ShapeDtypeStruct((m, d), F32),
        grid=(m // tm,),
        in_specs=[pl.BlockSpec((tm, d), lambda i: (i, 0)), pl.BlockSpec((1, d), lambda i: (0, 0))],
        out_specs=pl.BlockSpec((tm, d), lambda i: (i, 0)),
        compiler_params=_cparams(("parallel",)),
        name="final_norm",
    )(x, g)


def _pool_kernel(u_ref, hist_ref, w_ref, sc_ref, o_ref, e_ref, *, tt, cg):
    i = pl.program_id(1)
    hrows = POOL_HIST + 1

    @pl.when(i == 0)
    def _():
        e_ref[0:hrows] = hist_ref[...]

    @pl.when(i > 0)
    def _():
        e_ref[0:hrows] = e_ref[tt:tt + hrows]

    u = u_ref[...]
    e_ref[hrows:hrows + tt] = u
    pos = i * tt + lax.broadcasted_iota(jnp.int32, (tt, 1), 0)
    for gi, w in enumerate(POOL_WINDOWS):
        cs = slice(gi * cg, (gi + 1) * cg)
        s = e_ref[:, cs]
        sh = 1
        while sh < w:
            s = s + pltpu.roll(s, sh, axis=0)
            sh *= 2
        cnt = jnp.minimum(w, pos + 1).astype(F32)
        dlt = s[hrows:] / cnt - u[:, cs]
        o_ref[:, cs] = _dot(dlt, w_ref[gi]) * sc_ref[:, cs]


def _pool_prompt(p, hist16, w, scale, dm, tt):
    b, t, _ = p.shape
    kern = functools.partial(_pool_kernel, tt=tt, cg=dm.cg)
    return pl.pallas_call(
        kern,
        out_shape=jax.ShapeDtypeStruct((b, t, dm.pw), F32),
        grid=(b, t // tt),
        in_specs=[pl.BlockSpec((None, tt, dm.pw), lambda bi, i: (bi, i, _blk(dm.c_pool, dm.pw))),
                  pl.BlockSpec((None, POOL_HIST + 1, dm.pw), lambda bi, i: (bi, 0, 0)),
                  pl.BlockSpec((POOL_GROUPS, dm.cg, dm.cg), lambda bi, i: (0, 0, 0)),
                  pl.BlockSpec((1, dm.pw), lambda bi, i: (0, 0))],
        out_specs=pl.BlockSpec((None, tt, dm.pw), lambda bi, i: (bi, i, 0)),
        scratch_shapes=[pltpu.VMEM((tt + POOL_HIST + 1, dm.pw), F32)],
        compiler_params=_cparams(("parallel", "arbitrary")),
        name="pool_mixer",
    )(p, hist16, w, scale)


def _pool_sample_kernel(e_ref, w_ref, sc_ref, o_ref, *, cg, pos):
    e = e_ref[...]
    hrows = POOL_HIST + 1
    row = lax.broadcasted_iota(jnp.int32, e.shape, 1)
    u = e[:, hrows - 1, :]
    for gi, w in enumerate(POOL_WINDOWS):
        cs = slice(gi * cg, (gi + 1) * cg)
        win = jnp.sum(jnp.where(row >= hrows - w, e, 0.0)[:, :, cs], axis=1)
        dlt = win / float(min(w, pos + 1)) - u[:, cs]
        o_ref[:, cs] = _dot(dlt, w_ref[gi]) * sc_ref[:, cs]


def _pool_sample(e16, w, scale, dm, pos):
    b = e16.shape[0]
    kern = functools.partial(_pool_sample_kernel, cg=dm.cg, pos=pos)
    return pl.pallas_call(
        kern,
        out_shape=jax.ShapeDtypeStruct((b, dm.pw), F32),
        compiler_params=pltpu.CompilerParams(vmem_limit_bytes=VMEM_LIMIT),
        name="pool_mixer_step",
    )(e16, w, scale)


def _rwkv_mix_cols(x, xprev, mu):
    return x + (xprev - x) * mu


def _rwkv_lora(pl_, w0, w2, a0, a2, g2):
    w_in = pl_[:, 0:RWKV_W_RANK]
    a_in = pl_[:, RWKV_W_RANK:RWKV_W_RANK + RWKV_A_RANK]
    g_in = pl_[:, RWKV_W_RANK + RWKV_A_RANK:]
    lw = -RWKV_DECAY_SCALE * _sigmoid(w0 + _dot_hi(jnp.tanh(w_in), w2))
    a = _sigmoid(a0 + _dot_hi(a_in, a2))
    g = _dot(_sigmoid(g_in), g2)
    return lw, a, g


def _rwkv_prep_kernel(r_ref, k_ref, v_ref, l_ref, rp_ref, kp_ref, vp_ref, lp_ref, sh_ref, mu_ref,
                      w0_ref, w2_ref, a0_ref, a2_ref, g2_ref, kk_ref, ka_ref, rk_ref,
                      rt_ref, at_ref, kh_ref, bh_ref, kg_ref, bg_ref, vo_ref, bo_ref, go_ref, gc_ref,
                      *, tt, rw, chunk):
    i = pl.program_id(1)
    first = i == 0
    row1 = lax.broadcasted_iota(jnp.int32, (tt, 1), 0)

    def mixed(x_ref, xp_ref, c0, c1):
        x = x_ref[...]
        carry = jnp.where(first, sh_ref[:, c0:c1], xp_ref[SUBLANES - 1:SUBLANES, :])
        xprev = jnp.where(row1 == 0, carry, pltpu.roll(x, 1, axis=0))
        return _rwkv_mix_cols(x, xprev, mu_ref[:, c0:c1])

    r = mixed(r_ref, rp_ref, 0, rw)
    k = mixed(k_ref, kp_ref, rw, 2 * rw)
    v = mixed(v_ref, vp_ref, 2 * rw, 3 * rw)
    lo = mixed(l_ref, lp_ref, 3 * rw, 3 * rw + RWKV_LORA)
    lw, a, g = _rwkv_lora(lo, w0_ref[...], w2_ref[...], a0_ref[...], a2_ref[...], g2_ref[...])

    ones = _head_ones()
    kk = k * kk_ref[...]
    kk = kk * lax.rsqrt(jnp.maximum(_head_sum(kk * kk, ones), 1e-12))
    k2 = k * (1.0 + (a - 1.0) * ka_ref[...])
    bonus = _head_sum(r * k2 * rk_ref[...], ones) * v
    bvec = kk * a

    ti = lax.broadcasted_iota(jnp.int32, (tt, tt), 0)
    si = lax.broadcasted_iota(jnp.int32, (tt, tt), 1)
    same = (ti // chunk) == (si // chunk)
    tri = (same & (si <= ti)).astype(BF16)
    blk = same.astype(BF16)
    cum = _dot_exact_lhs(tri, lw)
    tot = _dot_exact_lhs(blk, lw)
    e_in = jnp.exp(cum)
    e_out = jnp.exp(-cum)
    e_rest = jnp.exp(tot - cum)
    rt_ref[...] = r * e_in
    at_ref[...] = -kk * jnp.exp(cum - lw)
    kh_ref[...] = k2 * e_out
    bh_ref[...] = bvec * e_out
    kg_ref[...] = k2 * e_rest
    bg_ref[...] = bvec * e_rest
    vo_ref[...] = v
    bo_ref[...] = bonus
    go_ref[...] = g
    etot = jnp.exp(tot)
    for c in range(tt // chunk):
        gc_ref[c] = etot[c * chunk:c * chunk + 1, :]


def _rwkv_prep(p, shift, mu, w0, w2, a0, a2, g2, k_k, k_a, r_k, dm, tt):
    b, t, _ = p.shape
    rw = dm.rw
    chunk = RWKV_CHUNK
    kern = functools.partial(_rwkv_prep_kernel, tt=tt, rw=rw, chunk=chunk)
    cur = lambda c0, w: pl.BlockSpec((None, tt, w), lambda bi, i: (bi, i, _blk(c0, w)))
    prv = lambda c0, w: pl.BlockSpec(
        (None, SUBLANES, w), lambda bi, i: (bi, jnp.maximum(i * (tt // SUBLANES) - 1, 0), _blk(c0, w)))
    full = lambda shp: pl.BlockSpec(shp, lambda bi, i: (0,) * len(shp))
    tok = jax.ShapeDtypeStruct((b, t, rw), F32)
    tspec = pl.BlockSpec((None, tt, rw), lambda bi, i: (bi, i, 0))
    return pl.pallas_call(
        kern,
        out_shape=(tok,) * 9 + (jax.ShapeDtypeStruct((b, t // chunk, 1, rw), F32),),
        grid=(b, t // tt),
        in_specs=[cur(dm.c_r, rw), cur(dm.c_r + rw, rw), cur(dm.c_r + 2 * rw, rw), cur(dm.c_lora, RWKV_LORA),
                  prv(dm.c_r, rw), prv(dm.c_r + rw, rw), prv(dm.c_r + 2 * rw, rw), prv(dm.c_lora, RWKV_LORA),
                  pl.BlockSpec((None, 1, dm.rcols), lambda bi, i: (bi, 0, 0)),
                  full((1, dm.rcols)), full((1, rw)), full((RWKV_W_RANK, rw)), full((1, rw)),
                  full((RWKV_A_RANK, rw)), full((RWKV_G_RANK, rw)), full((1, rw)), full((1, rw)), full((1, rw))],
        out_specs=(tspec,) * 9 + (pl.BlockSpec((None, tt // chunk, 1, rw), lambda bi, i: (bi, i, 0, 0)),),
        compiler_params=_cparams(("parallel", "parallel")),
        name="rwkv_prep",
    )(p, p, p, p, p, p, p, p, shift, mu, w0, w2, a0, a2, g2, k_k, k_a, r_k)


def _group_norm_out(y, bonus, g, lnw, lnb, ones):
    ym = _head_sum(y, ones) * (1.0 / RWKV_HEAD)
    d = y - ym
    yv = _head_sum(d * d, ones) * (1.0 / RWKV_HEAD)
    return (d * lax.rsqrt(yv + GN_EPS) * lnw + lnb + bonus) * g


def _rwkv_solve_kernel(at_ref, kh_ref, bh_ref, rt_ref, v_ref, wt_ref, p1_ref, y1_ref, arb_ref,
                       *, chunk, heads, group):
    n = RWKV_HEAD
    ri = lax.broadcasted_iota(jnp.int32, (chunk, chunk), 0)
    ci = lax.broadcasted_iota(jnp.int32, (chunk, chunk), 1)
    strict = ci < ri
    incl = ci <= ri
    n_apply = int(math.log2(chunk))
    assert 2 ** n_apply == chunk

    for h0 in range(0, heads, group):
        sls = [slice(h * n, (h + 1) * n) for h in range(h0, min(h0 + group, heads))]
        at = [at_ref[:, sl] for sl in sls]
        vs = [_split2(v_ref[:, sl]) for sl in sls]
        prod = [_mm3(_split2(jnp.concatenate([at[i], rt_ref[:, sl]], axis=0)),
                     _split2(jnp.concatenate([kh_ref[:, sl], bh_ref[:, sl]], axis=0)), _NT)
                for i, sl in enumerate(sls)]
        for i, sl in enumerate(sls):
            arb_ref[:, sl] = jnp.where(incl, prod[i][chunk:, chunk:], 0.0)
            y1_ref[:, sl] = _mm3(_split2(jnp.where(incl, prod[i][chunk:, :chunk], 0.0)), vs[i], _NN)
        x = [jnp.concatenate([at[i], _mm3(_split2(jnp.where(strict, prod[i][:chunk, :chunk], 0.0)), vs[i], _NN)],
                             axis=1) for i in range(len(sls))]
        pw = [_split2(jnp.where(strict, prod[i][:chunk, chunk:], 0.0)) for i in range(len(sls))]
        for lvl in range(n_apply):
            x = [x[i] + _mm3(pw[i], _split2(x[i]), _NN) for i in range(len(sls))]
            if lvl + 1 < n_apply:
                pw = [_split2(_mm3(pw[i], pw[i], _NN)) for i in range(len(sls))]
        for i, sl in enumerate(sls):
            wt_ref[:, sl] = x[i][:, :n]
            p1_ref[:, sl] = x[i][:, n:]


def _rwkv_solve(at, kh, bh, rt, v, dm):
    b, t, rw = rt.shape
    chunk = RWKV_CHUNK
    kern = functools.partial(_rwkv_solve_kernel, chunk=chunk, heads=dm.rh, group=min(8, dm.rh))
    tspec = pl.BlockSpec((None, chunk, rw), lambda bi, c: (bi, c, 0))
    tok = jax.ShapeDtypeStruct((b, t, rw), F32)
    return pl.pallas_call(
        kern,
        out_shape=(tok,) * 4,
        grid=(b, t // chunk),
        in_specs=[tspec] * 5,
        out_specs=(tspec,) * 4,
        compiler_params=_cparams(("parallel", "parallel")),
        name="rwkv_solve",
    )(at, kh, bh, rt, v)


def _rwkv_scan_kernel(wt_ref, rt_ref, p1_ref, y1_ref, arb_ref, kg_ref, bg_ref, v_ref, bo_ref, g_ref, gc_ref,
                      lnw_ref, lnb_ref, s0_ref, y_ref, so_ref, st_ref, yacc_ref, *, chunk, heads):
    c = pl.program_id(1)
    n = RWKV_HEAD

    @pl.when(c == 0)
    def _():
        st_ref[...] = s0_ref[...]

    eye = lax.broadcasted_iota(jnp.int32, (n, n), 0) == lax.broadcasted_iota(jnp.int32, (n, n), 1)
    sls = [slice(h * n, (h + 1) * n) for h in range(heads)]

    s0 = [st_ref[h] for h in range(heads)]
    ws = [_mm3(_split2(jnp.concatenate([wt_ref[:, sl], rt_ref[:, sl]], axis=0)), _split2(s0[h]), _NN)
          for h, sl in enumerate(sls)]
    u = [p1_ref[:, sl] + ws[h][:chunk] for h, sl in enumerate(sls)]
    for h, sl in enumerate(sls):
        yacc_ref[:, sl] = ws[h][chunk:] + y1_ref[:, sl] + _mm3(_split2(arb_ref[:, sl]), _split2(u[h]), _NN)
    for h, sl in enumerate(sls):
        dg = jnp.where(eye, jnp.broadcast_to(gc_ref[:, sl], (n, n)), 0.0)
        lhs = jnp.concatenate([kg_ref[:, sl], bg_ref[:, sl], dg], axis=0)
        rhs = jnp.concatenate([v_ref[:, sl], u[h], s0[h]], axis=0)
        st_ref[h] = _mm3(_split2(lhs), _split2(rhs), _TN)

    y_ref[...] = _group_norm_out(yacc_ref[...], bo_ref[...], g_ref[...], lnw_ref[...], lnb_ref[...], _head_ones())

    @pl.when(c == pl.num_programs(1) - 1)
    def _():
        so_ref[...] = st_ref[...]


def _rwkv_scan(prep, lnw, lnb, s0t, dm):
    rt, at, kh, bh, kg, bg, v, bonus, g, gc = prep
    wt, p1, y1, arb = _rwkv_solve(at, kh, bh, rt, v, dm)
    b, t, rw = rt.shape
    chunk = RWKV_CHUNK
    heads = dm.rh
    kern = functools.partial(_rwkv_scan_kernel, chunk=chunk, heads=heads)
    tspec = pl.BlockSpec((None, chunk, rw), lambda bi, c: (bi, c, 0))
    sspec = pl.BlockSpec((None, heads, RWKV_HEAD, RWKV_HEAD), lambda bi, c: (bi, 0, 0, 0))
    return pl.pallas_call(
        kern,
        out_shape=(jax.ShapeDtypeStruct((b, t, rw), F32),
                   jax.ShapeDtypeStruct((b, heads, RWKV_HEAD, RWKV_HEAD), F32)),
        grid=(b, t // chunk),
        in_specs=[tspec] * 10 + [pl.BlockSpec((None, None, 1, rw), lambda bi, c: (bi, c, 0, 0)),
                                 pl.BlockSpec((1, rw), lambda bi, c: (0, 0)),
                                 pl.BlockSpec((1, rw), lambda bi, c: (0, 0)),
                                 sspec],
        out_specs=(tspec, sspec),
        scratch_shapes=[pltpu.VMEM((heads, RWKV_HEAD, RWKV_HEAD), F32), pltpu.VMEM((chunk, rw), F32)],
        compiler_params=_cparams(("parallel", "arbitrary")),
        name="rwkv_scan",
    )(wt, rt, p1, y1, arb, kg, bg, v, bonus, g, gc, lnw, lnb, s0t)


def _rwkv_step_kernel(r_ref, k_ref, v_ref, l_ref, sh_ref, mu_ref, w0_ref, w2_ref, a0_ref, a2_ref, g2_ref,
                      kk_ref, ka_ref, rk_ref, lnw_ref, lnb_ref, s_ref, y_ref, so_ref, *, rw, heads):
    n = RWKV_HEAD
    rows = SUBLANES

    def mixed(x_ref, c0, c1):
        x = jnp.broadcast_to(x_ref[...], (rows, c1 - c0))
        return _rwkv_mix_cols(x, sh_ref[:, c0:c1], mu_ref[:, c0:c1])

    r = mixed(r_ref, 0, rw)
    k = mixed(k_ref, rw, 2 * rw)
    v = mixed(v_ref, 2 * rw, 3 * rw)
    lo = mixed(l_ref, 3 * rw, 3 * rw + RWKV_LORA)
    lw, a, g = _rwkv_lora(lo, w0_ref[...], w2_ref[...], a0_ref[...], a2_ref[...], g2_ref[...])
    w = jnp.exp(lw)
    kk = k * kk_ref[...]
    k2 = k * (1.0 + (a - 1.0) * ka_ref[...])
    eye = lax.broadcasted_iota(jnp.int32, (n, n), 0) == lax.broadcasted_iota(jnp.int32, (n, n), 1)

    def col(rowvec):
        return jnp.sum(jnp.where(eye, jnp.broadcast_to(rowvec, (n, n)), 0.0), axis=1, keepdims=True)

    outs = []
    for h in range(heads):
        sl = slice(h * n, (h + 1) * n)
        kkh = kk[0:1, sl]
        kkh = kkh * lax.rsqrt(jnp.maximum(jnp.sum(kkh * kkh, axis=1, keepdims=True), 1e-12))
        ah, wh, k2h, rh, vh = a[0:1, sl], w[0:1, sl], k2[0:1, sl], r[0:1, sl], v[0:1, sl]
        s = s_ref[h]
        sa = -jnp.sum(s * kkh, axis=1, keepdims=True)
        s = s * wh + sa * (kkh * ah) + col(vh) * k2h
        so_ref[h] = s
        ycol = jnp.sum(s * rh, axis=1, keepdims=True)
        yrow = jnp.sum(jnp.where(eye, jnp.broadcast_to(ycol, (n, n)), 0.0), axis=0, keepdims=True)
        ym = jnp.mean(yrow, axis=1, keepdims=True)
        d = yrow - ym
        yv = jnp.mean(d * d, axis=1, keepdims=True)
        yn = d * lax.rsqrt(yv + GN_EPS) * lnw_ref[:, sl] + lnb_ref[:, sl]
        bonus = jnp.sum(rh * k2h * rk_ref[:, sl], axis=1, keepdims=True) * vh
        outs.append((yn + bonus) * g[0:1, sl])
    y_ref[...] = jnp.concatenate(outs, axis=1)


def _rwkv_step(p, shift, s0, mu, w0, w2, a0, a2, g2, k_k, k_a, r_k, lnw, lnb, dm):
    b = p.shape[0]
    rw, heads = dm.rw, dm.rh
    kern = functools.partial(_rwkv_step_kernel, rw=rw, heads=heads)
    cur = lambda c0, w: pl.BlockSpec((None, 1, w), lambda bi: (bi, 0, _blk(c0, w)))
    full = lambda shp: pl.BlockSpec(shp, lambda bi: (0,) * len(shp))
    sspec = pl.BlockSpec((None, heads, RWKV_HEAD, RWKV_HEAD), lambda bi: (bi, 0, 0, 0))
    return pl.pallas_call(
        kern,
        out_shape=(jax.ShapeDtypeStruct((b, 1, rw), F32),
                   jax.ShapeDtypeStruct((b, heads, RWKV_HEAD, RWKV_HEAD), F32)),
        grid=(b,),
        in_specs=[cur(dm.c_r, rw), cur(dm.c_r + rw, rw), cur(dm.c_r + 2 * rw, rw), cur(dm.c_lora, RWKV_LORA),
                  pl.BlockSpec((None, 1, dm.rcols), lambda bi: (bi, 0, 0)),
                  full((1, dm.rcols)), full((1, rw)), full((RWKV_W_RANK, rw)), full((1, rw)),
                  full((RWKV_A_RANK, rw)), full((RWKV_G_RANK, rw)), full((1, rw)), full((1, rw)), full((1, rw)),
                  full((1, rw)), full((1, rw)), sspec],
        out_specs=(pl.BlockSpec((None, 1, rw), lambda bi: (bi, 0, 0)), sspec),
        compiler_params=_cparams(("parallel",)),
        name="rwkv_step",
    )(p, p, p, p, shift, mu, w0, w2, a0, a2, g2, k_k, k_a, r_k, lnw, lnb, s0)


def _rope_blocks(x, cos2, sin2):
    outs = []
    for c in range(0, x.shape[1], NSA_HEAD):
        xb = x[:, c:c + NSA_HEAD]
        outs.append(xb * cos2 + pltpu.roll(xb, NSA_HEAD // 2, axis=1) * sin2)
    return outs[0] if len(outs) == 1 else jnp.concatenate(outs, axis=1)


def _nsa_prep_kernel(*refs, tt, nq, with_means):
    q_refs, kv_refs = refs[:nq], refs[nq:nq + 6]
    cos_ref, sin_ref, qo_ref, kv_ref, kw_ref = refs[nq + 6:nq + 11]
    cos2, sin2 = cos_ref[...], sin_ref[...]
    pw = NSA_KV_HEADS * NSA_HEAD
    for idx, ref in enumerate(q_refs):
        qo_ref[:, idx * pw:(idx + 1) * pw] = (_rope_blocks(ref[...], cos2, sin2) * (NSA_HEAD ** -0.5)).astype(BF16)
    for idx, ref in enumerate(kv_refs):
        x = ref[...]
        if idx % 2 == 0:
            x = _rope_blocks(x, cos2, sin2)
        if idx < 4:
            kv_ref[:, idx * pw:(idx + 1) * pw] = x
        else:
            kw_ref[:, (idx - 4) * pw:(idx - 3) * pw] = x
        if idx < 2 and with_means:
            m_ref = refs[nq + 11]
            m_ref[:, idx * pw:(idx + 1) * pw] = (
                jnp.sum(x.reshape(tt // NSA_BLOCK, NSA_BLOCK, pw), axis=1) * (1.0 / NSA_BLOCK))


def _nsa_prep(p, cos2, sin2, dm, tt, with_means):
    b, t, _ = p.shape
    pw = dm.part
    nq = dm.nw // pw
    kern = functools.partial(_nsa_prep_kernel, tt=tt, nq=nq, with_means=with_means)
    cur = lambda c0: pl.BlockSpec((None, tt, pw), lambda bi, i: (bi, i, _blk(c0, pw)))
    outs = [jax.ShapeDtypeStruct((b, t, dm.nw), BF16), jax.ShapeDtypeStruct((b, t, 4 * pw), F32),
            jax.ShapeDtypeStruct((b, t, 2 * pw), F32)]
    ospecs = [pl.BlockSpec((None, tt, dm.nw), lambda bi, i: (bi, i, 0)),
              pl.BlockSpec((None, tt, 4 * pw), lambda bi, i: (bi, i, 0)),
              pl.BlockSpec((None, tt, 2 * pw), lambda bi, i: (bi, i, 0))]
    if with_means:
        outs.append(jax.ShapeDtypeStruct((b, t // NSA_BLOCK, 2 * pw), F32))
        ospecs.append(pl.BlockSpec((None, tt // NSA_BLOCK, 2 * pw), lambda bi, i: (bi, i, 0)))
    in_specs = [cur(dm.c_q + k * pw) for k in range(nq)] + [cur(dm.c_kv + k * pw) for k in range(6)]
    in_specs += [pl.BlockSpec((tt, NSA_HEAD), lambda bi, i: (i, 0))] * 2
    return pl.pallas_call(
        kern,
        out_shape=tuple(outs),
        grid=(b, t // tt),
        in_specs=in_specs,
        out_specs=tuple(ospecs),
        compiler_params=_cparams(("parallel", "parallel")),
        name="nsa_prep",
    )(*([p] * (nq + 6)), cos2, sin2)


def _nsa_prompt_kernel(q_ref, kc_ref, vc_ref, ks_ref, vs_ref, kw_ref, vw_ref, gt_ref, o_ref,
                       m_ref, l_ref, acc_ref, *, g, nb, n_sel, tk, lw):
    h = pl.program_id(1)
    i = pl.program_id(2)
    qb = NSA_QBLOCK
    rows = g * qb
    q = jnp.concatenate([q_ref[:, gi * NSA_HEAD:(gi + 1) * NSA_HEAD] for gi in range(g)], axis=0)
    gsl = [slice(gi * qb, (gi + 1) * qb) for gi in range(g)]
    qpos_c = i * qb + lax.broadcasted_iota(jnp.int32, (qb, 1), 0)
    qpos_r = i * qb + lax.broadcasted_iota(jnp.int32, (1, rows), 1) % qb

    blk = lax.broadcasted_iota(jnp.int32, (nb, 1), 0)
    st = _dot_nt(kc_ref[...], q)
    ok = ((blk + 1) * NSA_BLOCK - 1) <= qpos_r
    mc = jnp.max(jnp.where(ok, st, NEG), axis=0, keepdims=True)
    ec = jnp.where(ok, jnp.exp(st - mc), 0.0)
    den = jnp.sum(ec, axis=0, keepdims=True)
    p_c = ec / jnp.where(den > 0.0, den, 1.0)
    o_cmp = lax.dot_general(p_c.astype(BF16), vc_ref[...].astype(BF16), _TN, preferred_element_type=F32)

    imp = p_c[:, gsl[0]]
    for gi in range(1, g):
        imp = imp + p_c[:, gsl[gi]]
    qp1 = qpos_r[:, 0:qb]
    imp = jnp.where(blk == qp1 // NSA_BLOCK, g + 1.0, jnp.where(blk * NSA_BLOCK <= qp1, imp, -1.0))
    cnt = jnp.zeros((nb, qb), F32)
    for bi in range(nb):
        ci = imp[bi:bi + 1, :]
        beats = (ci > imp) | ((ci == imp) & (blk > bi))
        cnt = cnt + jnp.where(beats, 1.0, 0.0)
    sel_t = jnp.where(cnt < n_sel, 1.0, 0.0).astype(BF16)

    m_ref[...] = jnp.full((rows, 1), M_INIT, F32)
    l_ref[...] = jnp.zeros((rows, 1), F32)
    acc_ref[...] = jnp.zeros((rows, NSA_HEAD), F32)
    bpt = tk // NSA_BLOCK
    brow = lax.broadcasted_iota(jnp.int32, (nb, tk), 0)
    bcol = lax.broadcasted_iota(jnp.int32, (nb, tk), 1) // NSA_BLOCK
    kcol = lax.broadcasted_iota(jnp.int32, (1, tk), 1)

    def body(kt, carry):
        k0 = pl.multiple_of(kt * tk, tk)
        kb = ks_ref[pl.ds(k0, tk), :].astype(BF16)
        vb = vs_ref[pl.ds(k0, tk), :].astype(BF16)
        sc = _dot_nt(q, kb)
        expand = (brow == bcol + kt * bpt).astype(BF16)
        picked = lax.dot_general(sel_t, expand, _TN, preferred_element_type=F32)
        bias = jnp.where((picked > 0.5) & ((k0 + kcol) <= qpos_c), 0.0, NEG)
        for sl in gsl:
            s_g = sc[sl] + bias
            m_old = m_ref[sl]
            m_new = jnp.maximum(m_old, jnp.max(s_g, axis=1, keepdims=True))
            pr = jnp.exp(s_g - m_new)
            alpha = jnp.exp(m_old - m_new)
            l_ref[sl] = alpha * l_ref[sl] + jnp.sum(pr, axis=1, keepdims=True)
            acc_ref[sl] = alpha * acc_ref[sl] + _dot(pr, vb)
            m_ref[sl] = m_new
        return carry

    n_kt = ((i + 1) * qb + tk - 1) // tk
    lax.fori_loop(0, n_kt, body, 0)
    o_sel = acc_ref[...] / l_ref[...]

    w0 = pl.multiple_of(jnp.maximum(i * qb - NSA_WINDOW, 0), qb)
    kwb = kw_ref[pl.ds(w0, lw), :].astype(BF16)
    vwb = vw_ref[pl.ds(w0, lw), :].astype(BF16)
    dp = qpos_c - (w0 + lax.broadcasted_iota(jnp.int32, (1, lw), 1))
    bias_w = jnp.where((dp >= 0) & (dp <= NSA_WINDOW), 0.0, NEG)
    sw = _dot_nt(q, kwb)
    o_parts = []
    for sl in gsl:
        s_g = sw[sl] + bias_w
        e_g = jnp.exp(s_g - jnp.max(s_g, axis=1, keepdims=True))
        o_parts.append(_dot(e_g / jnp.sum(e_g, axis=1, keepdims=True), vwb))
    o_win = jnp.concatenate(o_parts, axis=0)

    gs = _sigmoid(gt_ref[...])
    for gi in range(g):
        acc = None
        for ci, ob in enumerate((o_cmp, o_sel, o_win)):
            c0 = gi * 3 + ci
            c1 = (g + gi) * 3 + ci
            gate = jnp.where(h == 0, gs[:, c0:c0 + 1], gs[:, c1:c1 + 1])
            term = ob[gi * qb:(gi + 1) * qb] * gate
            acc = term if acc is None else acc + term
        o_ref[:, gi * NSA_HEAD:(gi + 1) * NSA_HEAD] = acc


def _nsa_prompt(qr, kv4, kwv, kcvc, p, dm):
    b, t, _ = qr.shape
    assert NSA_KV_HEADS == 2
    g = dm.g
    nb = t // NSA_BLOCK
    n_sel = min(NSA_TOPK, nb)
    tk = min(512, t)
    lw = NSA_WINDOW + NSA_QBLOCK
    assert t % tk == 0 and t >= lw
    hd = NSA_HEAD
    kern = functools.partial(_nsa_prompt_kernel, g=g, nb=nb, n_sel=n_sel, tk=tk, lw=lw)
    kvspec = lambda c: pl.BlockSpec((None, t, hd), lambda bi, h, i, c=c: (bi, 0, 2 * c + h))
    rows = g * NSA_QBLOCK
    return pl.pallas_call(
        kern,
        out_shape=jax.ShapeDtypeStruct((b, t, dm.nw), F32),
        grid=(b, NSA_KV_HEADS, t // NSA_QBLOCK),
        in_specs=[pl.BlockSpec((None, NSA_QBLOCK, g * hd), lambda bi, h, i: (bi, i, h)),
                  pl.BlockSpec((None, nb, hd), lambda bi, h, i: (bi, 0, h)),
                  pl.BlockSpec((None, nb, hd), lambda bi, h, i: (bi, 0, 2 + h)),
                  kvspec(2), kvspec(3), kvspec(0), kvspec(1),
                  pl.BlockSpec((None, NSA_QBLOCK, LANES), lambda bi, h, i: (bi, i, _blk(dm.c_ng, LANES)))],
        out_specs=pl.BlockSpec((None, NSA_QBLOCK, g * hd), lambda bi, h, i: (bi, i, h)),
        scratch_shapes=[pltpu.VMEM((rows, 1), F32), pltpu.VMEM((rows, 1), F32), pltpu.VMEM((rows, hd), F32)],
        compiler_params=_cparams(("parallel", "parallel", "arbitrary")),
        name="nsa_prompt",
    )(qr, kcvc, kcvc, kv4, kv4, kwv, kwv, p)


def _page_means_kernel(pt_ref, *refs, ppb):
    o_ref = refs[ppb]
    for k in range(ppb):
        x = refs[k][...]
        nbp = x.shape[0] // NSA_BLOCK
        o_ref[k] = jnp.sum(x.reshape(nbp, NSA_BLOCK, x.shape[1]), axis=1) * (1.0 / NSA_BLOCK)


def _page_means(cache, layer, page_table):
    _, _, page, cols = cache.shape
    b, n_pages = page_table.shape
    half = cols // 2
    nbp = page // NSA_BLOCK
    ppb = math.gcd(n_pages, 16)
    kern = functools.partial(_page_means_kernel, ppb=ppb)
    pspec = lambda k: pl.BlockSpec((None, None, page, half),
                                   lambda bi, j, pt, k=k: (layer, pt[bi, j * ppb + k], 0, 0))
    out = pl.pallas_call(
        kern,
        out_shape=jax.ShapeDtypeStruct((b, n_pages, nbp, half), F32),
        grid_spec=pltpu.PrefetchScalarGridSpec(
            num_scalar_prefetch=1,
            grid=(b, n_pages // ppb),
            in_specs=[pspec(k) for k in range(ppb)],
            out_specs=pl.BlockSpec((None, ppb, nbp, half), lambda bi, j, pt: (bi, j, 0, 0)),
        ),
        compiler_params=_cparams(("parallel", "parallel")),
        name="nsa_page_means",
    )(page_table, *([cache] * ppb))
    return out.reshape(b, n_pages * nbp, half)


def _nsa_choose_kernel(q_ref, m_ref, oc_ref, idx_ref, *, g, nbp, n_sel):
    hd = NSA_HEAD
    blk = lax.broadcasted_iota(jnp.int32, (1, nbp), 1)
    lane = lax.broadcasted_iota(jnp.int32, (1, LANES), 1)
    for h in range(NSA_KV_HEADS):
        q = jnp.concatenate([q_ref[:, (h * g + gi) * hd:(h * g + gi + 1) * hd] for gi in range(g)], axis=0)
        q = jnp.concatenate([q, jnp.zeros((SUBLANES - g, hd), q.dtype)], axis=0) if g < SUBLANES else q
        kc = m_ref[:, h * hd:(h + 1) * hd]
        vc = m_ref[:, (NSA_KV_HEADS + h) * hd:(NSA_KV_HEADS + h + 1) * hd]
        s = _dot_nt(q, kc)
        m = jnp.max(s, axis=1, keepdims=True)
        e = jnp.exp(s - m)
        p_c = e / jnp.sum(e, axis=1, keepdims=True)
        oc_ref[h] = _dot(p_c, vc)[0:g]
        imp = jnp.sum(p_c[0:g], axis=0, keepdims=True)
        idx = jnp.where(lane == 0, nbp, 0)
        for it in range(1, n_sel):
            best = jnp.max(imp, axis=1, keepdims=True)
            j = jnp.min(jnp.where(imp == best, blk, nbp), axis=1, keepdims=True)
            idx = jnp.where(lane == it, j, idx)
            imp = jnp.where(blk == j, -2.0, imp)
        idx_ref[h] = idx


def _nsa_choose(qr, means, dm, n_sel):
    b = qr.shape[0]
    nbp = means.shape[1]
    g = dm.g
    kern = functools.partial(_nsa_choose_kernel, g=g, nbp=nbp, n_sel=n_sel)
    return pl.pallas_call(
        kern,
        out_shape=(jax.ShapeDtypeStruct((b, NSA_KV_HEADS, g, NSA_HEAD), F32),
                   jax.ShapeDtypeStruct((b, NSA_KV_HEADS, 1, LANES), jnp.int32)),
        grid=(b,),
        in_specs=[pl.BlockSpec((None, 1, dm.nw), lambda bi: (bi, 0, 0)),
                  pl.BlockSpec((None, nbp, means.shape[2]), lambda bi: (bi, 0, 0))],
        out_specs=(pl.BlockSpec((None, NSA_KV_HEADS, g, NSA_HEAD), lambda bi: (bi, 0, 0, 0)),
                   pl.BlockSpec((None, NSA_KV_HEADS, 1, LANES), lambda bi: (bi, 0, 0, 0))),
        compiler_params=_cparams(("parallel",)),
        name="nsa_choose",
    )(qr, means)


def _nsa_sample_kernel(idx_ref, pt_ref, q_ref, ks_ref, vs_ref, kn_ref, vn_ref, kw_ref, vw_ref, kwn_ref, vwn_ref,
                       oc_ref, gt_ref, o_ref, m_ref, l_ref, acc_ref, *, g, nbp, n_sel):
    h = pl.program_id(1)
    j = pl.program_id(2)
    hd = NSA_HEAD
    rows = SUBLANES
    q = jnp.concatenate([q_ref[:, gi * hd:(gi + 1) * hd] for gi in range(g)], axis=0)
    if g < rows:
        q = jnp.concatenate([q, jnp.zeros((rows - g, hd), q.dtype)], axis=0)

    @pl.when(j == 0)
    def _():
        s0 = jnp.sum(q.astype(F32) * kn_ref[...].astype(BF16).astype(F32), axis=1, keepdims=True)
        m_ref[...] = s0
        l_ref[...] = jnp.ones((rows, 1), F32)
        acc_ref[...] = jnp.broadcast_to(vn_ref[...].astype(BF16).astype(F32), (rows, hd))

    @pl.when(j > 0)
    def _():
        sc = _dot_nt(q, ks_ref[...])
        m_old = m_ref[...]
        m_new = jnp.maximum(m_old, jnp.max(sc, axis=1, keepdims=True))
        pr = jnp.exp(sc - m_new)
        alpha = jnp.exp(m_old - m_new)
        l_ref[...] = alpha * l_ref[...] + jnp.sum(pr, axis=1, keepdims=True)
        acc_ref[...] = alpha * acc_ref[...] + _dot(pr, vs_ref[...])
        m_ref[...] = m_new

    @pl.when(j == n_sel - 1)
    def _():
        o_sel = acc_ref[...] / l_ref[...]
        sw = _dot_nt(q, kw_ref[...])
        sn = jnp.sum(q.astype(F32) * kwn_ref[...].astype(BF16).astype(F32), axis=1, keepdims=True)
        mw = jnp.maximum(jnp.max(sw, axis=1, keepdims=True), sn)
        ew = jnp.exp(sw - mw)
        en = jnp.exp(sn - mw)
        den = jnp.sum(ew, axis=1, keepdims=True) + en
        pn = (en / den).astype(BF16).astype(F32)
        o_win = _dot(ew / den, vw_ref[...]) + pn * vwn_ref[...].astype(BF16).astype(F32)
        gs = _sigmoid(gt_ref[...])
        o_cmp = oc_ref[...]
        for gi in range(g):
            acc = None
            for ci, ob in enumerate((o_cmp, o_sel, o_win)):
                c0 = gi * 3 + ci
                c1 = (g + gi) * 3 + ci
                gate = jnp.where(h == 0, gs[:, c0:c0 + 1], gs[:, c1:c1 + 1])
                term = ob[gi:gi + 1] * gate
                acc = term if acc is None else acc + term
            o_ref[:, gi * hd:(gi + 1) * hd] = acc


def _nsa_sample(qr, kv4, kwv, cache, layer, page_table, idx, win, o_cmp, p, dm, n_sel):
    b = qr.shape[0]
    g, hd = dm.g, NSA_HEAD
    bpp = cache.shape[2]
    nbp = page_table.shape[1] * bpp
    lwin = win.shape[1]
    kern = functools.partial(_nsa_sample_kernel, g=g, nbp=nbp, n_sel=n_sel)

    def sel_map(c):
        def f(bi, h, j, idx_ref, pt_ref):
            blk = idx_ref[(bi * NSA_KV_HEADS + h) * LANES + j]
            n_past = pt_ref.shape[1] * bpp
            blk = jnp.minimum(blk, n_past - 1)
            return (layer, pt_ref[bi, blk // bpp], blk % bpp, 0, 2 * c + h)
        return f

    newspec = lambda c: pl.BlockSpec((None, 1, hd), lambda bi, h, j, ir, pr, c=c: (bi, 0, 2 * c + h))
    return pl.pallas_call(
        kern,
        out_shape=jax.ShapeDtypeStruct((b, 1, dm.nw), F32),
        grid_spec=pltpu.PrefetchScalarGridSpec(
            num_scalar_prefetch=2,
            grid=(b, NSA_KV_HEADS, n_sel),
            in_specs=[pl.BlockSpec((None, 1, g * hd), lambda bi, h, j, ir, pr: (bi, 0, h)),
                      pl.BlockSpec((None, None, None, NSA_BLOCK, hd), sel_map(2)),
                      pl.BlockSpec((None, None, None, NSA_BLOCK, hd), sel_map(3)),
                      newspec(2), newspec(3),
                      pl.BlockSpec((None, lwin, hd), lambda bi, h, j, ir, pr: (bi, 0, h)),
                      pl.BlockSpec((None, lwin, hd), lambda bi, h, j, ir, pr: (bi, 0, NSA_KV_HEADS + h)),
                      newspec(0), newspec(1),
                      pl.BlockSpec((None, None, g, hd), lambda bi, h, j, ir, pr: (bi, h, 0, 0)),
                      pl.BlockSpec((None, 1, LANES), lambda bi, h, j, ir, pr: (bi, 0, _blk(dm.c_ng, LANES)))],
            out_specs=pl.BlockSpec((None, 1, g * hd), lambda bi, h, j, ir, pr: (bi, 0, h)),
            scratch_shapes=[pltpu.VMEM((SUBLANES, 1), F32), pltpu.VMEM((SUBLANES, 1), F32),
                            pltpu.VMEM((SUBLANES, hd), F32)],
        ),
        compiler_params=_cparams(("parallel", "parallel", "arbitrary")),
        name="nsa_sample",
    )(idx, page_table, qr, cache, cache, kv4, kv4, win, win, kwv, kwv, o_cmp, p)


def _rope_tables(pos):
    half = NSA_HEAD // 2
    inv = jnp.exp(-math.log(ROPE_THETA) * jnp.arange(half, dtype=F32) / half)
    ang = pos.astype(F32)[:, None] * inv[None, :]
    cos, sin = jnp.cos(ang), jnp.sin(ang)
    return jnp.concatenate([cos, cos], axis=1), jnp.concatenate([-sin, sin], axis=1)


def _shift_cols(p_last, dm):
    return jnp.concatenate([p_last[..., dm.c_r:dm.c_r + 3 * dm.rw], p_last[..., dm.c_lora:dm.c_lora + RWKV_LORA]],
                           axis=-1)


def _row(x):
    return x.reshape(1, -1)


def kernel(x_prompt, x_sample, cache_nsa_kv, page_table, state_nsa_window, state_wkv, state_rwkv_shift,
           state_pool, state_ffn_conv, norm_mix, w_in, pool_w, pool_scale, rwkv_mu, rwkv_w0, rwkv_w2,
           rwkv_a0, rwkv_a2, rwkv_g2, rwkv_k_k, rwkv_k_a, rwkv_r_k, rwkv_ln_w, rwkv_ln_b, w_branch, w_out,
           norm_ffn, ffn_up, ffn_conv, ffn_conv_b, ffn_down, norm_final):
    bp, sp, d = x_prompt.shape
    bs, ts, _ = x_sample.shape
    assert ts == 1
    depth = w_in.shape[0]
    dm = Dims(d)
    ff = dm.ff
    hd = NSA_HEAD
    page = cache_nsa_kv.shape[2]
    n_pages = page_table.shape[1]
    past = n_pages * page
    kvw = 4 * NSA_KV_HEADS * hd

    w_mix_b = w_in[:, :, :dm.np].astype(BF16)
    w_gate_b = w_in[:, :, dm.o3:].astype(BF16)
    pool_w_b = pool_w.astype(BF16)
    wb_b = w_branch.astype(BF16)
    wo_b = w_out.astype(BF16)
    up_b = ffn_up.astype(BF16)
    down_b = ffn_down.astype(BF16)
    g2_b = rwkv_g2.astype(BF16)
    kk_r = rwkv_k_k.reshape(depth, 1, dm.rw)
    ka_r = rwkv_k_a.reshape(depth, 1, dm.rw)
    rk_r = rwkv_r_k.reshape(depth, 1, dm.rw)

    mp = bp * sp
    tm_p = min(512, sp)
    tn_ff = 512 if ff % 512 == 0 else 256
    tt_pool = min(512, sp)
    tt_rwkv = min(256, sp)
    tt_nsa = min(512, sp)
    tps = sp // tm_p

    def rwkv_params(l):
        return (_row(rwkv_mu[l]), _row(rwkv_w0[l]), rwkv_w2[l], _row(rwkv_a0[l]), rwkv_a2[l], g2_b[l],
                kk_r[l], ka_r[l], rk_r[l])

    cos_p, sin_p = _rope_tables(jnp.arange(sp, dtype=jnp.int32))
    x = x_prompt.reshape(mp, d)
    zeros_hist16 = jnp.zeros((bp, POOL_HIST + 1, dm.pw), F32)
    zeros_shift = jnp.zeros((bp, 1, dm.rcols), F32)
    zeros_wkv = jnp.zeros((bp, dm.rh, RWKV_HEAD, RWKV_HEAD), F32)
    zeros_conv = jnp.zeros((bp, CONV_W - 1, 2 * ff), F32)
    wl = min(NSA_WINDOW, sp)
    p_kv, p_win, p_wkv, p_shift, p_pool, p_conv = [], [], [], [], [], []
    for l in range(depth):
        p = _norm_matmul(x, _row(norm_mix[l]), w_mix_b[l], tm_p, dm.tn_in).reshape(bp, sp, dm.np)
        pg = _norm_matmul(x, _row(norm_mix[l]), w_gate_b[l], tm_p, dm.tn_in)
        ya = _pool_prompt(p, zeros_hist16, pool_w_b[l], _row(pool_scale[l]), dm, tt_pool)
        prep = _rwkv_prep(p, zeros_shift, *rwkv_params(l), dm, tt_rwkv)
        yb, s_t = _rwkv_scan(prep, _row(rwkv_ln_w[l]), _row(rwkv_ln_b[l]), zeros_wkv, dm)
        qr, kv4, kwv, kcvc = _nsa_prep(p, cos_p, sin_p, dm, tt_nsa, True)
        yc = _nsa_prompt(qr, kv4, kwv, kcvc, p, dm)
        mrg = _branch_merge(ya.reshape(mp, -1), yb.reshape(mp, -1), yc.reshape(mp, -1), pg, wb_b[l], dm,
                            tm_p, min(1024, d))
        x = _out_proj(x, mrg, wo_b[l], tm_p, min(1024, d))
        x, za, zv = _ffn(x, _row(norm_ffn[l]), up_b[l], ffn_conv[l], _row(ffn_conv_b[l]), down_b[l],
                         zeros_conv, ff, tm_p, tn_ff, tps, False)
        p_kv.append(kv4.reshape(bp, sp, 4, NSA_KV_HEADS, hd))
        p_win.append(kwv[:, sp - wl:].reshape(bp, wl, 2, NSA_KV_HEADS, hd))
        p_wkv.append(jnp.swapaxes(s_t, -1, -2))
        p_shift.append(_shift_cols(p[:, sp - 1:, :], dm))
        p_pool.append(p[:, sp - POOL_HIST:, dm.c_pool:dm.c_pool + dm.pw])
        zl = jnp.concatenate([za, zv], axis=-1).reshape(bp, tps, 2, 2 * ff)
        p_conv.append(zl[:, tps - 1])
    y_prompt = _final_norm(x, _row(norm_final), tm_p).reshape(bp, sp, d)

    pos_s = past
    cos_s, sin_s = _rope_tables(jnp.full((1,), pos_s, dtype=jnp.int32))
    xs = x_sample.reshape(bs, d)
    nbp = past // NSA_BLOCK
    n_sel = min(NSA_TOPK, nbp + 1)
    cache_pages = cache_nsa_kv.reshape(depth, -1, page, kvw)
    cache_blocks = cache_nsa_kv.reshape(depth, -1, page // NSA_BLOCK, NSA_BLOCK, kvw)
    s_kv, s_win, s_wkv, s_shift, s_pool, s_conv = [], [], [], [], [], []
    for l in range(depth):
        p = _norm_matmul(xs, _row(norm_mix[l]), w_mix_b[l], bs, dm.tn_in).reshape(bs, 1, dm.np)
        pg = _norm_matmul(xs, _row(norm_mix[l]), w_gate_b[l], bs, dm.tn_in)
        e16 = jnp.concatenate([state_pool[l], p[:, :, dm.c_pool:dm.c_pool + dm.pw]], axis=1)
        ya = _pool_sample(e16, pool_w_b[l], _row(pool_scale[l]), dm, pos_s)
        yb, s_new = _rwkv_step(p, state_rwkv_shift[l], state_wkv[l], *rwkv_params(l),
                               _row(rwkv_ln_w[l]), _row(rwkv_ln_b[l]), dm)
        qr, kv4, kwv = _nsa_prep(p, cos_s, sin_s, dm, 1, False)
        means = _page_means(cache_pages, l, page_table)
        o_cmp, idx = _nsa_choose(qr, means, dm, n_sel)
        win = state_nsa_window[l].reshape(bs, -1, 2 * NSA_KV_HEADS * hd)
        yc = _nsa_sample(qr, kv4, kwv, cache_blocks, l, page_table, idx.reshape(-1), win, o_cmp, p, dm, n_sel)
        mrg = _branch_merge(ya, yb.reshape(bs, -1), yc.reshape(bs, -1), pg, wb_b[l], dm, bs, min(1024, d))
        xs = _out_proj(xs, mrg, wo_b[l], bs, min(1024, d))
        xs, za, zv = _ffn(xs, _row(norm_ffn[l]), up_b[l], ffn_conv[l], _row(ffn_conv_b[l]), down_b[l],
                          state_ffn_conv[l], ff, bs, tn_ff, 1, True)
        s_kv.append(kv4.reshape(bs, 1, 4, NSA_KV_HEADS, hd))
        wk = jnp.concatenate([win, kwv], axis=1)
        s_win.append(wk[:, -NSA_WINDOW:].reshape(bs, -1, 2, NSA_KV_HEADS, hd))
        s_wkv.append(s_new)
        s_shift.append(_shift_cols(p, dm))
        s_pool.append(e16[:, 1:])
        z_new = jnp.concatenate([za, zv], axis=-1)[:, None, :]
        s_conv.append(jnp.concatenate([state_ffn_conv[l][:, 1:], z_new], axis=1))
    y_sample = _final_norm(xs, _row(norm_final), bs).reshape(bs, 1, d)

    st = lambda xs_: jnp.stack(xs_)
    return (y_prompt, y_sample, st(p_kv), st(p_win), st(p_wkv), st(p_shift), st(p_pool), st(p_conv),
            st(s_kv), st(s_win), st(s_wkv), st(s_shift), st(s_pool), st(s_conv))
```

```python
import functools
import math

import jax
import jax.numpy as jnp
from jax import lax
from jax.experimental import pallas as pl
from jax.experimental.pallas import tpu as pltpu

F32 = jnp.float32
BF16 = jnp.bfloat16
HIGHEST = lax.Precision.HIGHEST

POOL_GROUPS = 4
POOL_WINDOWS = (2, 4, 8, 16)
POOL_HIST = 15
RWKV_HEAD = 64
RWKV_W_RANK = 64
RWKV_A_RANK = 64
RWKV_G_RANK = 128
RWKV_LORA = RWKV_W_RANK + RWKV_A_RANK + RWKV_G_RANK
RWKV_DECAY_SCALE = 0.6065306597126334
GN_EPS = 64e-5
NSA_HEAD = 128
NSA_KV_HEADS = 2
NSA_BLOCK = 64
NSA_TOPK = 16
NSA_WINDOW = 512
NSA_QBLOCK = 128
N_BRANCH = 3
CONV_W = 3
ROPE_THETA = 10000.0
RMS_EPS = 1e-6
NEG = -1e30
M_INIT = -1e29

LANES = 128
SUBLANES = 8
VMEM_LIMIT = 56 * 1024 * 1024
RWKV_CHUNK = 64
FFN_SUB_ROWS = 128


class Dims:
    def __init__(self, d_model):
        d = d_model
        self.d = d
        self.pw = d // 2
        self.cg = self.pw // POOL_GROUPS
        self.rw = d // 2
        self.rh = self.rw // RWKV_HEAD
        self.rcols = 3 * self.rw + RWKV_LORA
        self.nw = d // 2
        self.nh = self.nw // NSA_HEAD
        self.g = self.nh // NSA_KV_HEADS
        self.kvc = 6 * NSA_KV_HEADS * NSA_HEAD
        self.ngate = 3 * self.nh
        self.ff = 256 * ((8 * d // 3 + 255) // 256)
        self.o1 = self.pw
        self.o2 = self.o1 + self.rcols
        self.o3 = self.o2 + self.nw + self.kvc + self.ngate
        self.c_pool = 0
        self.c_r = self.o1
        self.c_lora = self.o1 + 3 * self.rw
        self.c_q = self.o2
        self.c_kv = self.o2 + self.nw
        self.c_ng = self.c_kv + self.kvc
        self.tn_in = 512
        self.np = -(-self.o3 // self.tn_in) * self.tn_in
        self.part = NSA_KV_HEADS * NSA_HEAD


def _blk(offset, width):
    assert offset % width == 0, (offset, width)
    return offset // width


def _cparams(sem):
    return pltpu.CompilerParams(dimension_semantics=sem, vmem_limit_bytes=VMEM_LIMIT)


def _dot(a, b):
    return jnp.dot(a.astype(BF16), b.astype(BF16), preferred_element_type=F32)


def _dot_nt(a, b):
    return lax.dot_general(a.astype(BF16), b.astype(BF16), (((1,), (1,)), ((), ())),
                           preferred_element_type=F32)


def _dot_hi(a, b):
    return jnp.dot(a, b, precision=HIGHEST, preferred_element_type=F32)


_NN = (((1,), (0,)), ((), ()))
_NT = (((1,), (1,)), ((), ()))
_TN = (((0,), (0,)), ((), ()))


def _split2(x):
    hi = x.astype(BF16)
    return hi, (x - hi.astype(F32)).astype(BF16)


def _mm3(a, b, dims):
    d = lambda p, q: lax.dot_general(p, q, dims, preferred_element_type=F32)
    return d(a[0], b[0]) + d(a[0], b[1]) + d(a[1], b[0])


def _split3(x):
    hi = x.astype(BF16)
    r1 = x - hi.astype(F32)
    mid = r1.astype(BF16)
    lo = (r1 - mid.astype(F32)).astype(BF16)
    return hi, mid, lo


def _dot_exact_lhs(m_bf16, x):
    hi, mid, lo = _split3(x)
    d = lambda p: jnp.dot(m_bf16, p, preferred_element_type=F32)
    return d(hi) + d(mid) + d(lo)


def _dot_exact_rhs(x, m_bf16):
    hi, mid, lo = _split3(x)
    d = lambda p: jnp.dot(p, m_bf16, preferred_element_type=F32)
    return d(hi) + d(mid) + d(lo)


def _head_ones():
    i = lax.broadcasted_iota(jnp.int32, (LANES, LANES), 0) // RWKV_HEAD
    j = lax.broadcasted_iota(jnp.int32, (LANES, LANES), 1) // RWKV_HEAD
    return (i == j).astype(BF16)


def _head_sum(x, ones):
    parts = [_dot_exact_rhs(x[:, c:c + LANES], ones) for c in range(0, x.shape[1], LANES)]
    return parts[0] if len(parts) == 1 else jnp.concatenate(parts, axis=1)


def _rmsnorm_val(x, g):
    ms = jnp.mean(x * x, axis=-1, keepdims=True)
    return x * lax.rsqrt(ms + RMS_EPS) * g


def _sigmoid(x):
    return 1.0 / (1.0 + jnp.exp(-x))


def _norm_matmul_kernel(x_ref, g_ref, w_ref, o_ref, h_ref):
    @pl.when(pl.program_id(1) == 0)
    def _():
        h_ref[...] = _rmsnorm_val(x_ref[...], g_ref[...]).astype(BF16)

    o_ref[...] = jnp.dot(h_ref[...], w_ref[...], preferred_element_type=F32)


def _norm_matmul(x, g, w, tm, tn):
    m, d = x.shape
    n = w.shape[1]
    return pl.pallas_call(
        _norm_matmul_kernel,
        out_shape=jax.ShapeDtypeStruct((m, n), F32),
        grid=(m // tm, n // tn),
        in_specs=[pl.BlockSpec((tm, d), lambda i, j: (i, 0)),
                  pl.BlockSpec((1, d), lambda i, j: (0, 0)),
                  pl.BlockSpec((d, tn), lambda i, j: (0, j))],
        out_specs=pl.BlockSpec((tm, tn), lambda i, j: (i, j)),
        scratch_shapes=[pltpu.VMEM((tm, d), BF16)],
        compiler_params=_cparams(("parallel", "arbitrary")),
        name="norm_matmul",
    )(x, g, w)


def _branch_merge_kernel(ya_ref, yb_ref, yc_ref, ga_ref, gb_ref, gc_ref, w_ref, o_ref):
    acc = _dot(ya_ref[...], w_ref[0]) * _sigmoid(ga_ref[...])
    acc = acc + _dot(yb_ref[...], w_ref[1]) * _sigmoid(gb_ref[...])
    acc = acc + _dot(yc_ref[...], w_ref[2]) * _sigmoid(gc_ref[...])
    o_ref[...] = acc.astype(BF16)


def _branch_merge(ya, yb, yc, p, wb, dm, tm, tn):
    m = ya.shape[0]
    d, hw = dm.d, dm.d // 2
    nb = d // tn
    yspec = pl.BlockSpec((tm, hw), lambda j, i: (i, 0))
    gspec = lambda k: pl.BlockSpec((tm, tn), lambda j, i, k=k: (i, k * nb + j))
    return pl.pallas_call(
        _branch_merge_kernel,
        out_shape=jax.ShapeDtypeStruct((m, d), BF16),
        grid=(nb, m // tm),
        in_specs=[yspec, yspec, yspec, gspec(0), gspec(1), gspec(2),
                  pl.BlockSpec((N_BRANCH, hw, tn), lambda j, i: (0, 0, j))],
        out_specs=pl.BlockSpec((tm, tn), lambda j, i: (i, j)),
        compiler_params=_cparams(("parallel", "parallel")),
        name="branch_merge",
    )(ya, yb, yc, p, p, p, wb)


def _out_proj_kernel(x_ref, m_ref, w_ref, o_ref):
    o_ref[...] = x_ref[...] + jnp.dot(m_ref[...], w_ref[...], preferred_element_type=F32)


def _out_proj(x, mrg, wo, tm, tn):
    m, d = x.shape
    return pl.pallas_call(
        _out_proj_kernel,
        out_shape=jax.ShapeDtypeStruct((m, d), F32),
        grid=(m // tm, d // tn),
        in_specs=[pl.BlockSpec((tm, tn), lambda i, j: (i, j)),
                  pl.BlockSpec((tm, d), lambda i, j: (i, 0)),
                  pl.BlockSpec((d, tn), lambda i, j: (0, j))],
        out_specs=pl.BlockSpec((tm, tn), lambda i, j: (i, j)),
        compiler_params=_cparams(("parallel", "parallel")),
        name="out_proj",
    )(x, mrg, wo)


def _ffn_kernel(x_ref, g_ref, wa_ref, wv_ref, cwa_ref, cwv_ref, cba_ref, cbv_ref, wd_ref,
                ha_ref, hv_ref, o_ref, za_ref, zv_ref, h_ref, ca_ref, cv_ref, *, tm, sub, tps, rows_are_seqs):
    i = pl.program_id(0)
    j = pl.program_id(1)

    @pl.when(j == 0)
    def _():
        x = x_ref[...]
        h_ref[...] = _rmsnorm_val(x, g_ref[...]).astype(BF16)
        o_ref[...] = x

    def conv_mix(z, zs1, zs2, cw_ref, cb_ref):
        return zs2 * cw_ref[0:1] + zs1 * cw_ref[1:2] + z * cw_ref[2:3] + cb_ref[...]

    if rows_are_seqs:
        h = h_ref[...]
        za = jnp.dot(h, wa_ref[...], preferred_element_type=F32)
        zv = jnp.dot(h, wv_ref[...], preferred_element_type=F32)
        za_ref[...] = za
        zv_ref[...] = zv
        ca = conv_mix(za, ha_ref[:, 1, :], ha_ref[:, 0, :], cwa_ref, cba_ref)
        cv = conv_mix(zv, hv_ref[:, 1, :], hv_ref[:, 0, :], cwv_ref, cbv_ref)
        o_ref[...] += _dot(ca * _sigmoid(ca) * cv, wd_ref[...])
        return

    @pl.when((i % tps) == 0)
    def _():
        ca_ref[j] = ha_ref[...]
        cv_ref[j] = hv_ref[...]

    rs = [slice(s * sub, (s + 1) * sub) for s in range(tm // sub)]
    zas = [jnp.dot(h_ref[r], wa_ref[...], preferred_element_type=F32) for r in rs]
    zvs = [jnp.dot(h_ref[r], wv_ref[...], preferred_element_type=F32) for r in rs]
    row = lax.broadcasted_iota(jnp.int32, (sub, zas[0].shape[1]), 0)

    def conv(zs, s, carry_ref, cw_ref, cb_ref):
        z = zs[s]
        prev = carry_ref[j] if s == 0 else zs[s - 1][sub - 2:sub]
        zs1 = jnp.where(row == 0, prev[1:2], pltpu.roll(z, 1, axis=0))
        zs2 = jnp.where(row == 0, prev[0:1], jnp.where(row == 1, prev[1:2], pltpu.roll(z, 2, axis=0)))
        return conv_mix(z, zs1, zs2, cw_ref, cb_ref)

    for s, r in enumerate(rs):
        ca = conv(zas, s, ca_ref, cwa_ref, cba_ref)
        cv = conv(zvs, s, cv_ref, cwv_ref, cbv_ref)
        o_ref[r] += _dot(ca * _sigmoid(ca) * cv, wd_ref[...])
    za_last = zas[-1][sub - 2:sub]
    zv_last = zvs[-1][sub - 2:sub]
    ca_ref[j] = za_last
    cv_ref[j] = zv_last
    za_ref[...] = za_last
    zv_ref[...] = zv_last


def _ffn(x, g, w_up, cw, cb, w_down, hist, ff, tm, tn, tps, rows_are_seqs):
    m, d = x.shape
    nj = ff // tn
    nm = m // tm
    if rows_are_seqs:
        assert nm == 1
        hspec_a = pl.BlockSpec((m, 2, tn), lambda i, j: (0, 0, j))
        hspec_v = pl.BlockSpec((m, 2, tn), lambda i, j: (0, 0, nj + j))
        zshape = jax.ShapeDtypeStruct((m, ff), F32)
        zspec = pl.BlockSpec((m, tn), lambda i, j: (0, j))
    else:
        hspec_a = pl.BlockSpec((None, 2, tn), lambda i, j: (i // tps, 0, j))
        hspec_v = pl.BlockSpec((None, 2, tn), lambda i, j: (i // tps, 0, nj + j))
        zshape = jax.ShapeDtypeStruct((nm, 2, ff), F32)
        zspec = pl.BlockSpec((None, 2, tn), lambda i, j: (i, 0, j))
    sub = FFN_SUB_ROWS if tm % FFN_SUB_ROWS == 0 else tm
    kern = functools.partial(_ffn_kernel, tm=tm, sub=sub, tps=tps, rows_are_seqs=rows_are_seqs)
    return pl.pallas_call(
        kern,
        out_shape=(jax.ShapeDtypeStruct((m, d), F32), zshape, zshape),
        grid=(nm, nj),
        in_specs=[pl.BlockSpec((tm, d), lambda i, j: (i, 0)),
                  pl.BlockSpec((1, d), lambda i, j: (0, 0)),
                  pl.BlockSpec((d, tn), lambda i, j: (0, j)),
                  pl.BlockSpec((d, tn), lambda i, j: (0, nj + j)),
                  pl.BlockSpec((CONV_W, tn), lambda i, j: (0, j)),
                  pl.BlockSpec((CONV_W, tn), lambda i, j: (0, nj + j)),
                  pl.BlockSpec((1, tn), lambda i, j: (0, j)),
                  pl.BlockSpec((1, tn), lambda i, j: (0, nj + j)),
                  pl.BlockSpec((tn, d), lambda i, j: (j, 0)),
                  hspec_a, hspec_v],
        out_specs=(pl.BlockSpec((tm, d), lambda i, j: (i, 0)), zspec, zspec),
        scratch_shapes=[pltpu.VMEM((tm, d), BF16),
                        pltpu.VMEM((nj, 2, tn), F32),
                        pltpu.VMEM((nj, 2, tn), F32)],
        compiler_params=_cparams(("arbitrary", "arbitrary")),
        name="conv_ffn",
    )(x, g, w_up, w_up, cw, cw, cb, cb, w_down, hist, hist)


def _final_norm_kernel(x_ref, g_ref, o_ref):
    o_ref[...] = _rmsnorm_val(x_ref[...], g_ref[...])


def _final_norm(x, g, tm):
    m, d = x.shape
    return pl.pallas_call(
        _final_norm_kernel,
        out_shape=jax.ShapeDtypeStruct((m, d), F32),
        grid=(m // tm,),
        in_specs=[pl.BlockSpec((tm, d), lambda i: (i, 0)), pl.BlockSpec((1, d), lambda i: (0, 0))],
        out_specs=pl.BlockSpec((tm, d), lambda i: (i, 0)),
        compiler_params=_cparams(("parallel",)),
        name="final_norm",
    )(x, g)


def _pool_kernel(u_ref, hist_ref, w_ref, sc_ref, o_ref, e_ref, *, tt, cg):
    i = pl.program_id(1)
    hrows = POOL_HIST + 1

    @pl.when(i == 0)
    def _():
        e_ref[0:hrows] = hist_ref[...]

    @pl.when(i > 0)
    def _():
        e_ref[0:hrows] = e_ref[tt:tt + hrows]

    u = u_ref[...]
    e_ref[hrows:hrows + tt] = u
    pos = i * tt + lax.broadcasted_iota(jnp.int32, (tt, 1), 0)
    for gi, w in enumerate(POOL_WINDOWS):
        cs = slice(gi * cg, (gi + 1) * cg)
        s = e_ref[:, cs]
        sh = 1
        while sh < w:
            s = s + pltpu.roll(s, sh, axis=0)
            sh *= 2
        cnt = jnp.minimum(w, pos + 1).astype(F32)
        dlt = s[hrows:] / cnt - u[:, cs]
        o_ref[:, cs] = _dot(dlt, w_ref[gi]) * sc_ref[:, cs]


def _pool_prompt(p, hist16, w, scale, dm, tt):
    b, t, _ = p.shape
    kern = functools.partial(_pool_kernel, tt=tt, cg=dm.cg)
    return pl.pallas_call(
        kern,
        out_shape=jax.ShapeDtypeStruct((b, t, dm.pw), F32),
        grid=(b, t // tt),
        in_specs=[pl.BlockSpec((None, tt, dm.pw), lambda bi, i: (bi, i, _blk(dm.c_pool, dm.pw))),
                  pl.BlockSpec((None, POOL_HIST + 1, dm.pw), lambda bi, i: (bi, 0, 0)),
                  pl.BlockSpec((POOL_GROUPS, dm.cg, dm.cg), lambda bi, i: (0, 0, 0)),
                  pl.BlockSpec((1, dm.pw), lambda bi, i: (0, 0))],
        out_specs=pl.BlockSpec((None, tt, dm.pw), lambda bi, i: (bi, i, 0)),
        scratch_shapes=[pltpu.VMEM((tt + POOL_HIST + 1, dm.pw), F32)],
        compiler_params=_cparams(("parallel", "arbitrary")),
        name="pool_mixer",
    )(p, hist16, w, scale)


def _pool_sample_kernel(e_ref, w_ref, sc_ref, o_ref, *, cg, pos):
    e = e_ref[...]
    hrows = POOL_HIST + 1
    row = lax.broadcasted_iota(jnp.int32, e.shape, 1)
    u = e[:, hrows - 1, :]
    for gi, w in enumerate(POOL_WINDOWS):
        cs = slice(gi * cg, (gi + 1) * cg)
        win = jnp.sum(jnp.where(row >= hrows - w, e, 0.0)[:, :, cs], axis=1)
        dlt = win / float(min(w, pos + 1)) - u[:, cs]
        o_ref[:, cs] = _dot(dlt, w_ref[gi]) * sc_ref[:, cs]


def _pool_sample(e16, w, scale, dm, pos):
    b = e16.shape[0]
    kern = functools.partial(_pool_sample_kernel, cg=dm.cg, pos=pos)
    return pl.pallas_call(
        kern,
        out_shape=jax.ShapeDtypeStruct((b, dm.pw), F32),
        compiler_params=pltpu.CompilerParams(vmem_limit_bytes=VMEM_LIMIT),
        name="pool_mixer_step",
    )(e16, w, scale)


def _rwkv_mix_cols(x, xprev, mu):
    return x + (xprev - x) * mu


def _rwkv_lora(pl_, w0, w2, a0, a2, g2):
    w_in = pl_[:, 0:RWKV_W_RANK]
    a_in = pl_[:, RWKV_W_RANK:RWKV_W_RANK + RWKV_A_RANK]
    g_in = pl_[:, RWKV_W_RANK + RWKV_A_RANK:]
    lw = -RWKV_DECAY_SCALE * _sigmoid(w0 + _dot_hi(jnp.tanh(w_in), w2))
    a = _sigmoid(a0 + _dot_hi(a_in, a2))
    g = _dot(_sigmoid(g_in), g2)
    return lw, a, g


def _rwkv_prep_kernel(r_ref, k_ref, v_ref, l_ref, rp_ref, kp_ref, vp_ref, lp_ref, sh_ref, mu_ref,
                      w0_ref, w2_ref, a0_ref, a2_ref, g2_ref, kk_ref, ka_ref, rk_ref,
                      rt_ref, at_ref, kh_ref, bh_ref, kg_ref, bg_ref, vo_ref, bo_ref, go_ref, gc_ref,
                      *, tt, rw, chunk):
    i = pl.program_id(1)
    first = i == 0
    row1 = lax.broadcasted_iota(jnp.int32, (tt, 1), 0)

    def mixed(x_ref, xp_ref, c0, c1):
        x = x_ref[...]
        carry = jnp.where(first, sh_ref[:, c0:c1], xp_ref[SUBLANES - 1:SUBLANES, :])
        xprev = jnp.where(row1 == 0, carry, pltpu.roll(x, 1, axis=0))
        return _rwkv_mix_cols(x, xprev, mu_ref[:, c0:c1])

    r = mixed(r_ref, rp_ref, 0, rw)
    k = mixed(k_ref, kp_ref, rw, 2 * rw)
    v = mixed(v_ref, vp_ref, 2 * rw, 3 * rw)
    lo = mixed(l_ref, lp_ref, 3 * rw, 3 * rw + RWKV_LORA)
    lw, a, g = _rwkv_lora(lo, w0_ref[...], w2_ref[...], a0_ref[...], a2_ref[...], g2_ref[...])

    ones = _head_ones()
    kk = k * kk_ref[...]
    kk = kk * lax.rsqrt(jnp.maximum(_head_sum(kk * kk, ones), 1e-12))
    k2 = k * (1.0 + (a - 1.0) * ka_ref[...])
    bonus = _head_sum(r * k2 * rk_ref[...], ones) * v
    bvec = kk * a

    ti = lax.broadcasted_iota(jnp.int32, (tt, tt), 0)
    si = lax.broadcasted_iota(jnp.int32, (tt, tt), 1)
    same = (ti // chunk) == (si // chunk)
    tri = (same & (si <= ti)).astype(BF16)
    blk = same.astype(BF16)
    cum = _dot_exact_lhs(tri, lw)
    tot = _dot_exact_lhs(blk, lw)
    e_in = jnp.exp(cum)
    e_out = jnp.exp(-cum)
    e_rest = jnp.exp(tot - cum)
    rt_ref[...] = r * e_in
    at_ref[...] = -kk * jnp.exp(cum - lw)
    kh_ref[...] = k2 * e_out
    bh_ref[...] = bvec * e_out
    kg_ref[...] = k2 * e_rest
    bg_ref[...] = bvec * e_rest
    vo_ref[...] = v
    bo_ref[...] = bonus
    go_ref[...] = g
    etot = jnp.exp(tot)
    for c in range(tt // chunk):
        gc_ref[c] = etot[c * chunk:c * chunk + 1, :]


def _rwkv_prep(p, shift, mu, w0, w2, a0, a2, g2, k_k, k_a, r_k, dm, tt):
    b, t, _ = p.shape
    rw = dm.rw
    chunk = RWKV_CHUNK
    kern = functools.partial(_rwkv_prep_kernel, tt=tt, rw=rw, chunk=chunk)
    cur = lambda c0, w: pl.BlockSpec((None, tt, w), lambda bi, i: (bi, i, _blk(c0, w)))
    prv = lambda c0, w: pl.BlockSpec(
        (None, SUBLANES, w), lambda bi, i: (bi, jnp.maximum(i * (tt // SUBLANES) - 1, 0), _blk(c0, w)))
    full = lambda shp: pl.BlockSpec(shp, lambda bi, i: (0,) * len(shp))
    tok = jax.ShapeDtypeStruct((b, t, rw), F32)
    tspec = pl.BlockSpec((None, tt, rw), lambda bi, i: (bi, i, 0))
    return pl.pallas_call(
        kern,
        out_shape=(tok,) * 9 + (jax.ShapeDtypeStruct((b, t // chunk, 1, rw), F32),),
        grid=(b, t // tt),
        in_specs=[cur(dm.c_r, rw), cur(dm.c_r + rw, rw), cur(dm.c_r + 2 * rw, rw), cur(dm.c_lora, RWKV_LORA),
                  prv(dm.c_r, rw), prv(dm.c_r + rw, rw), prv(dm.c_r + 2 * rw, rw), prv(dm.c_lora, RWKV_LORA),
                  pl.BlockSpec((None, 1, dm.rcols), lambda bi, i: (bi, 0, 0)),
                  full((1, dm.rcols)), full((1, rw)), full((RWKV_W_RANK, rw)), full((1, rw)),
                  full((RWKV_A_RANK, rw)), full((RWKV_G_RANK, rw)), full((1, rw)), full((1, rw)), full((1, rw))],
        out_specs=(tspec,) * 9 + (pl.BlockSpec((None, tt // chunk, 1, rw), lambda bi, i: (bi, i, 0, 0)),),
        compiler_params=_cparams(("parallel", "parallel")),
        name="rwkv_prep",
    )(p, p, p, p, p, p, p, p, shift, mu, w0, w2, a0, a2, g2, k_k, k_a, r_k)


def _group_norm_out(y, bonus, g, lnw, lnb, ones):
    ym = _head_sum(y, ones) * (1.0 / RWKV_HEAD)
    d = y - ym
    yv = _head_sum(d * d, ones) * (1.0 / RWKV_HEAD)
    return (d * lax.rsqrt(yv + GN_EPS) * lnw + lnb + bonus) * g


def _rwkv_solve_kernel(at_ref, kh_ref, bh_ref, rt_ref, v_ref, wt_ref, p1_ref, y1_ref, arb_ref,
                       *, chunk, heads, group):
    n = RWKV_HEAD
    ri = lax.broadcasted_iota(jnp.int32, (chunk, chunk), 0)
    ci = lax.broadcasted_iota(jnp.int32, (chunk, chunk), 1)
    strict = ci < ri
    incl = ci <= ri
    n_apply = int(math.log2(chunk))
    assert 2 ** n_apply == chunk

    for h0 in range(0, heads, group):
        sls = [slice(h * n, (h + 1) * n) for h in range(h0, min(h0 + group, heads))]
        at = [at_ref[:, sl] for sl in sls]
        vs = [_split2(v_ref[:, sl]) for sl in sls]
        prod = [_mm3(_split2(jnp.concatenate([at[i], rt_ref[:, sl]], axis=0)),
                     _split2(jnp.concatenate([kh_ref[:, sl], bh_ref[:, sl]], axis=0)), _NT)
                for i, sl in enumerate(sls)]
        tri2 = jnp.concatenate([strict, incl], axis=0)
        kv = [_mm3(_split2(jnp.where(tri2, prod[i][:, :chunk], 0.0)), vs[i], _NN) for i in range(len(sls))]
        for i, sl in enumerate(sls):
            arb_ref[:, sl] = jnp.where(incl, prod[i][chunk:, chunk:], 0.0)
            y1_ref[:, sl] = kv[i][chunk:]
        x = [jnp.concatenate([at[i], kv[i][:chunk]], axis=1) for i in range(len(sls))]
        pw = [jnp.where(strict, prod[i][:chunk, chunk:], 0.0) for i in range(len(sls))]
        for lvl in range(n_apply):
            if lvl + 1 < n_apply:
                r = [_mm3(_split2(pw[i]), _split2(jnp.concatenate([x[i], pw[i]], axis=1)), _NN)
                     for i in range(len(sls))]
                x = [x[i] + r[i][:, :2 * n] for i in range(len(sls))]
                pw = [r[i][:, 2 * n:] for i in range(len(sls))]
            else:
                x = [x[i] + _mm3(_split2(pw[i]), _split2(x[i]), _NN) for i in range(len(sls))]
        for i, sl in enumerate(sls):
            wt_ref[:, sl] = x[i][:, :n]
            p1_ref[:, sl] = x[i][:, n:]


def _rwkv_solve(at, kh, bh, rt, v, dm):
    b, t, rw = rt.shape
    chunk = RWKV_CHUNK
    kern = functools.partial(_rwkv_solve_kernel, chunk=chunk, heads=dm.rh, group=min(8, dm.rh))
    tspec = pl.BlockSpec((None, chunk, rw), lambda bi, c: (bi, c, 0))
    tok = jax.ShapeDtypeStruct((b, t, rw), F32)
    return pl.pallas_call(
        kern,
        out_shape=(tok,) * 4,
        grid=(b, t // chunk),
        in_specs=[tspec] * 5,
        out_specs=(tspec,) * 4,
        compiler_params=_cparams(("parallel", "parallel")),
        name="rwkv_solve",
    )(at, kh, bh, rt, v)


def _rwkv_scan_kernel(wt_ref, rt_ref, p1_ref, y1_ref, arb_ref, kg_ref, bg_ref, v_ref, bo_ref, g_ref, gc_ref,
                      lnw_ref, lnb_ref, s0_ref, y_ref, so_ref, st_ref, yacc_ref, *, chunk, heads):
    c = pl.program_id(1)
    n = RWKV_HEAD

    @pl.when(c == 0)
    def _():
        st_ref[...] = s0_ref[...]

    eye = lax.broadcasted_iota(jnp.int32, (n, n), 0) == lax.broadcasted_iota(jnp.int32, (n, n), 1)
    sls = [slice(h * n, (h + 1) * n) for h in range(heads)]

    s0 = [st_ref[h] for h in range(heads)]
    ws = [_mm3(_split2(jnp.concatenate([wt_ref[:, sl], rt_ref[:, sl]], axis=0)), _split2(s0[h]), _NN)
          for h, sl in enumerate(sls)]
    u = [p1_ref[:, sl] + ws[h][:chunk] for h, sl in enumerate(sls)]
    for h, sl in enumerate(sls):
        yacc_ref[:, sl] = ws[h][chunk:] + y1_ref[:, sl] + _mm3(_split2(arb_ref[:, sl]), _split2(u[h]), _NN)
    for h, sl in enumerate(sls):
        dg = jnp.where(eye, jnp.broadcast_to(gc_ref[:, sl], (n, n)), 0.0)
        lhs = jnp.concatenate([kg_ref[:, sl], bg_ref[:, sl], dg], axis=0)
        rhs = jnp.concatenate([v_ref[:, sl], u[h], s0[h]], axis=0)
        st_ref[h] = _mm3(_split2(lhs), _split2(rhs), _TN)

    y_ref[...] = _group_norm_out(yacc_ref[...], bo_ref[...], g_ref[...], lnw_ref[...], lnb_ref[...], _head_ones())

    @pl.when(c == pl.num_programs(1) - 1)
    def _():
        so_ref[...] = st_ref[...]


def _rwkv_scan(prep, lnw, lnb, s0t, dm):
    rt, at, kh, bh, kg, bg, v, bonus, g, gc = prep
    wt, p1, y1, arb = _rwkv_solve(at, kh, bh, rt, v, dm)
    b, t, rw = rt.shape
    chunk = RWKV_CHUNK
    heads = dm.rh
    kern = functools.partial(_rwkv_scan_kernel, chunk=chunk, heads=heads)
    tspec = pl.BlockSpec((None, chunk, rw), lambda bi, c: (bi, c, 0))
    sspec = pl.BlockSpec((None, heads, RWKV_HEAD, RWKV_HEAD), lambda bi, c: (bi, 0, 0, 0))
    return pl.pallas_call(
        kern,
        out_shape=(jax.ShapeDtypeStruct((b, t, rw), F32),
                   jax.ShapeDtypeStruct((b, heads, RWKV_HEAD, RWKV_HEAD), F32)),
        grid=(b, t // chunk),
        in_specs=[tspec] * 10 + [pl.BlockSpec((None, None, 1, rw), lambda bi, c: (bi, c, 0, 0)),
                                 pl.BlockSpec((1, rw), lambda bi, c: (0, 0)),
                                 pl.BlockSpec((1, rw), lambda bi, c: (0, 0)),
                                 sspec],
        out_specs=(tspec, sspec),
        scratch_shapes=[pltpu.VMEM((heads, RWKV_HEAD, RWKV_HEAD), F32), pltpu.VMEM((chunk, rw), F32)],
        compiler_params=_cparams(("parallel", "arbitrary")),
        name="rwkv_scan",
    )(wt, rt, p1, y1, arb, kg, bg, v, bonus, g, gc, lnw, lnb, s0t)


def _rwkv_step_kernel(r_ref, k_ref, v_ref, l_ref, sh_ref, mu_ref, w0_ref, w2_ref, a0_ref, a2_ref, g2_ref,
                      kk_ref, ka_ref, rk_ref, lnw_ref, lnb_ref, s_ref, y_ref, so_ref, *, rw, heads):
    n = RWKV_HEAD
    rows = SUBLANES

    def mixed(x_ref, c0, c1):
        x = jnp.broadcast_to(x_ref[...], (rows, c1 - c0))
        return _rwkv_mix_cols(x, sh_ref[:, c0:c1], mu_ref[:, c0:c1])

    r = mixed(r_ref, 0, rw)
    k = mixed(k_ref, rw, 2 * rw)
    v = mixed(v_ref, 2 * rw, 3 * rw)
    lo = mixed(l_ref, 3 * rw, 3 * rw + RWKV_LORA)
    lw, a, g = _rwkv_lora(lo, w0_ref[...], w2_ref[...], a0_ref[...], a2_ref[...], g2_ref[...])
    w = jnp.exp(lw)
    kk = k * kk_ref[...]
    k2 = k * (1.0 + (a - 1.0) * ka_ref[...])
    eye = lax.broadcasted_iota(jnp.int32, (n, n), 0) == lax.broadcasted_iota(jnp.int32, (n, n), 1)

    def col(rowvec):
        return jnp.sum(jnp.where(eye, jnp.broadcast_to(rowvec, (n, n)), 0.0), axis=1, keepdims=True)

    outs = []
    for h in range(heads):
        sl = slice(h * n, (h + 1) * n)
        kkh = kk[0:1, sl]
        kkh = kkh * lax.rsqrt(jnp.maximum(jnp.sum(kkh * kkh, axis=1, keepdims=True), 1e-12))
        ah, wh, k2h, rh, vh = a[0:1, sl], w[0:1, sl], k2[0:1, sl], r[0:1, sl], v[0:1, sl]
        s = s_ref[h]
        sa = -jnp.sum(s * kkh, axis=1, keepdims=True)
        s = s * wh + sa * (kkh * ah) + col(vh) * k2h
        so_ref[h] = s
        ycol = jnp.sum(s * rh, axis=1, keepdims=True)
        yrow = jnp.sum(jnp.where(eye, jnp.broadcast_to(ycol, (n, n)), 0.0), axis=0, keepdims=True)
        ym = jnp.mean(yrow, axis=1, keepdims=True)
        d = yrow - ym
        yv = jnp.mean(d * d, axis=1, keepdims=True)
        yn = d * lax.rsqrt(yv + GN_EPS) * lnw_ref[:, sl] + lnb_ref[:, sl]
        bonus = jnp.sum(rh * k2h * rk_ref[:, sl], axis=1, keepdims=True) * vh
        outs.append((yn + bonus) * g[0:1, sl])
    y_ref[...] = jnp.concatenate(outs, axis=1)


def _rwkv_step(p, shift, s0, mu, w0, w2, a0, a2, g2, k_k, k_a, r_k, lnw, lnb, dm):
    b = p.shape[0]
    rw, heads = dm.rw, dm.rh
    kern = functools.partial(_rwkv_step_kernel, rw=rw, heads=heads)
    cur = lambda c0, w: pl.BlockSpec((None, 1, w), lambda bi: (bi, 0, _blk(c0, w)))
    full = lambda shp: pl.BlockSpec(shp, lambda bi: (0,) * len(shp))
    sspec = pl.BlockSpec((None, heads, RWKV_HEAD, RWKV_HEAD), lambda bi: (bi, 0, 0, 0))
    return pl.pallas_call(
        kern,
        out_shape=(jax.ShapeDtypeStruct((b, 1, rw), F32),
                   jax.ShapeDtypeStruct((b, heads, RWKV_HEAD, RWKV_HEAD), F32)),
        grid=(b,),
        in_specs=[cur(dm.c_r, rw), cur(dm.c_r + rw, rw), cur(dm.c_r + 2 * rw, rw), cur(dm.c_lora, RWKV_LORA),
                  pl.BlockSpec((None, 1, dm.rcols), lambda bi: (bi, 0, 0)),
                  full((1, dm.rcols)), full((1, rw)), full((RWKV_W_RANK, rw)), full((1, rw)),
                  full((RWKV_A_RANK, rw)), full((RWKV_G_RANK, rw)), full((1, rw)), full((1, rw)), full((1, rw)),
                  full((1, rw)), full((1, rw)), sspec],
        out_specs=(pl.BlockSpec((None, 1, rw), lambda bi: (bi, 0, 0)), sspec),
        compiler_params=_cparams(("parallel",)),
        name="rwkv_step",
    )(p, p, p, p, shift, mu, w0, w2, a0, a2, g2, k_k, k_a, r_k, lnw, lnb, s0)


def _rope_blocks(x, cos2, sin2):
    outs = []
    for c in range(0, x.shape[1], NSA_HEAD):
        xb = x[:, c:c + NSA_HEAD]
        outs.append(xb * cos2 + pltpu.roll(xb, NSA_HEAD // 2, axis=1) * sin2)
    return outs[0] if len(outs) == 1 else jnp.concatenate(outs, axis=1)


def _nsa_prep_kernel(*refs, tt, nq, with_means):
    q_refs, kv_refs = refs[:nq], refs[nq:nq + 6]
    cos_ref, sin_ref, qo_ref, kv_ref, kw_ref = refs[nq + 6:nq + 11]
    cos2, sin2 = cos_ref[...], sin_ref[...]
    pw = NSA_KV_HEADS * NSA_HEAD
    for idx, ref in enumerate(q_refs):
        qo_ref[:, idx * pw:(idx + 1) * pw] = (_rope_blocks(ref[...], cos2, sin2) * (NSA_HEAD ** -0.5)).astype(BF16)
    for idx, ref in enumerate(kv_refs):
        x = ref[...]
        if idx % 2 == 0:
            x = _rope_blocks(x, cos2, sin2)
        if idx < 4:
            kv_ref[:, idx * pw:(idx + 1) * pw] = x
        else:
            kw_ref[:, (idx - 4) * pw:(idx - 3) * pw] = x
        if idx < 2 and with_means:
            m_ref = refs[nq + 11]
            m_ref[:, idx * pw:(idx + 1) * pw] = (
                jnp.sum(x.reshape(tt // NSA_BLOCK, NSA_BLOCK, pw), axis=1) * (1.0 / NSA_BLOCK))


def _nsa_prep(p, cos2, sin2, dm, tt, with_means):
    b, t, _ = p.shape
    pw = dm.part
    nq = dm.nw // pw
    kern = functools.partial(_nsa_prep_kernel, tt=tt, nq=nq, with_means=with_means)
    cur = lambda c0: pl.BlockSpec((None, tt, pw), lambda bi, i: (bi, i, _blk(c0, pw)))
    outs = [jax.ShapeDtypeStruct((b, t, dm.nw), BF16), jax.ShapeDtypeStruct((b, t, 4 * pw), F32),
            jax.ShapeDtypeStruct((b, t, 2 * pw), F32)]
    ospecs = [pl.BlockSpec((None, tt, dm.nw), lambda bi, i: (bi, i, 0)),
              pl.BlockSpec((None, tt, 4 * pw), lambda bi, i: (bi, i, 0)),
              pl.BlockSpec((None, tt, 2 * pw), lambda bi, i: (bi, i, 0))]
    if with_means:
        outs.append(jax.ShapeDtypeStruct((b, t // NSA_BLOCK, 2 * pw), F32))
        ospecs.append(pl.BlockSpec((None, tt // NSA_BLOCK, 2 * pw), lambda bi, i: (bi, i, 0)))
    in_specs = [cur(dm.c_q + k * pw) for k in range(nq)] + [cur(dm.c_kv + k * pw) for k in range(6)]
    in_specs += [pl.BlockSpec((tt, NSA_HEAD), lambda bi, i: (i, 0))] * 2
    return pl.pallas_call(
        kern,
        out_shape=tuple(outs),
        grid=(b, t // tt),
        in_specs=in_specs,
        out_specs=tuple(ospecs),
        compiler_params=_cparams(("parallel", "parallel")),
        name="nsa_prep",
    )(*([p] * (nq + 6)), cos2, sin2)


def _nsa_prompt_kernel(q_ref, kc_ref, vc_ref, ks_ref, vs_ref, kw_ref, vw_ref, gt_ref, o_ref,
                       m_ref, l_ref, acc_ref, *, g, nb, n_sel, tk, lw):
    h = pl.program_id(1)
    i = pl.program_id(2)
    qb = NSA_QBLOCK
    rows = g * qb
    q = jnp.concatenate([q_ref[:, gi * NSA_HEAD:(gi + 1) * NSA_HEAD] for gi in range(g)], axis=0)
    gsl = [slice(gi * qb, (gi + 1) * qb) for gi in range(g)]
    qpos_c = i * qb + lax.broadcasted_iota(jnp.int32, (qb, 1), 0)
    qpos_r = i * qb + lax.broadcasted_iota(jnp.int32, (1, rows), 1) % qb

    blk = lax.broadcasted_iota(jnp.int32, (nb, 1), 0)
    st = _dot_nt(kc_ref[...], q)
    ok = ((blk + 1) * NSA_BLOCK - 1) <= qpos_r
    mc = jnp.max(jnp.where(ok, st, NEG), axis=0, keepdims=True)
    ec = jnp.where(ok, jnp.exp(st - mc), 0.0)
    den = jnp.sum(ec, axis=0, keepdims=True)
    p_c = ec / jnp.where(den > 0.0, den, 1.0)
    o_cmp = lax.dot_general(p_c.astype(BF16), vc_ref[...].astype(BF16), _TN, preferred_element_type=F32)

    imp = p_c[:, gsl[0]]
    for gi in range(1, g):
        imp = imp + p_c[:, gsl[gi]]
    qp1 = qpos_r[:, 0:qb]
    imp = jnp.where(blk == qp1 // NSA_BLOCK, g + 1.0, jnp.where(blk * NSA_BLOCK <= qp1, imp, -1.0))
    cnt = jnp.zeros((nb, qb), F32)
    for bi in range(nb):
        ci = imp[bi:bi + 1, :]
        beats = (ci > imp) | ((ci == imp) & (blk > bi))
        cnt = cnt + jnp.where(beats, 1.0, 0.0)
    sel_t = jnp.where(cnt < n_sel, 1.0, 0.0).astype(BF16)

    m_ref[...] = jnp.full((rows, 1), M_INIT, F32)
    l_ref[...] = jnp.zeros((rows, 1), F32)
    acc_ref[...] = jnp.zeros((rows, NSA_HEAD), F32)
    bpt = tk // NSA_BLOCK
    brow = lax.broadcasted_iota(jnp.int32, (nb, tk), 0)
    bcol = lax.broadcasted_iota(jnp.int32, (nb, tk), 1) // NSA_BLOCK
    kcol = lax.broadcasted_iota(jnp.int32, (1, tk), 1)

    def body(kt, carry):
        k0 = pl.multiple_of(kt * tk, tk)
        kb = ks_ref[pl.ds(k0, tk), :].astype(BF16)
        vb = vs_ref[pl.ds(k0, tk), :].astype(BF16)
        sc = _dot_nt(q, kb)
        expand = (brow == bcol + kt * bpt).astype(BF16)
        picked = lax.dot_general(sel_t, expand, _TN, preferred_element_type=F32)
        bias = jnp.where((picked > 0.5) & ((k0 + kcol) <= qpos_c), 0.0, NEG)
        for sl in gsl:
            s_g = sc[sl] + bias
            m_old = m_ref[sl]
            m_new = jnp.maximum(m_old, jnp.max(s_g, axis=1, keepdims=True))
            pr = jnp.exp(s_g - m_new)
            alpha = jnp.exp(m_old - m_new)
            l_ref[sl] = alpha * l_ref[sl] + jnp.sum(pr, axis=1, keepdims=True)
            acc_ref[sl] = alpha * acc_ref[sl] + _dot(pr, vb)
            m_ref[sl] = m_new
        return carry

    n_kt = ((i + 1) * qb + tk - 1) // tk
    lax.fori_loop(0, n_kt, body, 0)
    o_sel = acc_ref[...] / l_ref[...]

    w0 = pl.multiple_of(jnp.maximum(i * qb - NSA_WINDOW, 0), qb)
    kwb = kw_ref[pl.ds(w0, lw), :].astype(BF16)
    vwb = vw_ref[pl.ds(w0, lw), :].astype(BF16)
    dp = qpos_c - (w0 + lax.broadcasted_iota(jnp.int32, (1, lw), 1))
    bias_w = jnp.where((dp >= 0) & (dp <= NSA_WINDOW), 0.0, NEG)
    sw = _dot_nt(q, kwb)
    o_parts = []
    for sl in gsl:
        s_g = sw[sl] + bias_w
        e_g = jnp.exp(s_g - jnp.max(s_g, axis=1, keepdims=True))
        o_parts.append(_dot(e_g / jnp.sum(e_g, axis=1, keepdims=True), vwb))
    o_win = jnp.concatenate(o_parts, axis=0)

    gs = _sigmoid(gt_ref[...])
    for gi in range(g):
        acc = None
        for ci, ob in enumerate((o_cmp, o_sel, o_win)):
            c0 = gi * 3 + ci
            c1 = (g + gi) * 3 + ci
            gate = jnp.where(h == 0, gs[:, c0:c0 + 1], gs[:, c1:c1 + 1])
            term = ob[gi * qb:(gi + 1) * qb] * gate
            acc = term if acc is None else acc + term
        o_ref[:, gi * NSA_HEAD:(gi + 1) * NSA_HEAD] = acc


def _nsa_prompt(qr, kv4, kwv, kcvc, p, dm):
    b, t, _ = qr.shape
    assert NSA_KV_HEADS == 2
    g = dm.g
    nb = t // NSA_BLOCK
    n_sel = min(NSA_TOPK, nb)
    tk = min(512, t)
    lw = NSA_WINDOW + NSA_QBLOCK
    assert t % tk == 0 and t >= lw
    hd = NSA_HEAD
    kern = functools.partial(_nsa_prompt_kernel, g=g, nb=nb, n_sel=n_sel, tk=tk, lw=lw)
    kvspec = lambda c: pl.BlockSpec((None, t, hd), lambda bi, h, i, c=c: (bi, 0, 2 * c + h))
    rows = g * NSA_QBLOCK
    return pl.pallas_call(
        kern,
        out_shape=jax.ShapeDtypeStruct((b, t, dm.nw), F32),
        grid=(b, NSA_KV_HEADS, t // NSA_QBLOCK),
        in_specs=[pl.BlockSpec((None, NSA_QBLOCK, g * hd), lambda bi, h, i: (bi, i, h)),
                  pl.BlockSpec((None, nb, hd), lambda bi, h, i: (bi, 0, h)),
                  pl.BlockSpec((None, nb, hd), lambda bi, h, i: (bi, 0, 2 + h)),
                  kvspec(2), kvspec(3), kvspec(0), kvspec(1),
                  pl.BlockSpec((None, NSA_QBLOCK, LANES), lambda bi, h, i: (bi, i, _blk(dm.c_ng, LANES)))],
        out_specs=pl.BlockSpec((None, NSA_QBLOCK, g * hd), lambda bi, h, i: (bi, i, h)),
        scratch_shapes=[pltpu.VMEM((rows, 1), F32), pltpu.VMEM((rows, 1), F32), pltpu.VMEM((rows, hd), F32)],
        compiler_params=_cparams(("parallel", "parallel", "arbitrary")),
        name="nsa_prompt",
    )(qr, kcvc, kcvc, kv4, kv4, kwv, kwv, p)


def _page_means_kernel(pt_ref, *refs, ppb):
    o_ref = refs[ppb]
    for k in range(ppb):
        x = refs[k][...]
        nbp = x.shape[0] // NSA_BLOCK
        o_ref[k] = jnp.sum(x.reshape(nbp, NSA_BLOCK, x.shape[1], x.shape[2]), axis=1) * (1.0 / NSA_BLOCK)


def _page_means(cache, layer, page_table):
    _, _, page, _, prow, hd = cache.shape
    b, n_pages = page_table.shape
    nbp = page // NSA_BLOCK
    ppb = math.gcd(n_pages, 16)
    kern = functools.partial(_page_means_kernel, ppb=ppb)
    pspec = lambda k: pl.BlockSpec((None, None, page, None, prow, hd),
                                   lambda bi, j, pt, k=k: (layer, pt[bi, j * ppb + k], 0, 0, 0, 0))
    out = pl.pallas_call(
        kern,
        out_shape=jax.ShapeDtypeStruct((b, n_pages, nbp, prow, hd), F32),
        grid_spec=pltpu.PrefetchScalarGridSpec(
            num_scalar_prefetch=1,
            grid=(b, n_pages // ppb),
            in_specs=[pspec(k) for k in range(ppb)],
            out_specs=pl.BlockSpec((None, ppb, nbp, prow, hd), lambda bi, j, pt: (bi, j, 0, 0, 0)),
        ),
        compiler_params=_cparams(("parallel", "parallel")),
        name="nsa_page_means",
    )(page_table, *([cache] * ppb))
    return out.reshape(b, n_pages * nbp, prow * hd)


def _nsa_choose_kernel(q_ref, m_ref, oc_ref, idx_ref, *, g, nbp, n_sel):
    hd = NSA_HEAD
    blk = lax.broadcasted_iota(jnp.int32, (1, nbp), 1)
    lane = lax.broadcasted_iota(jnp.int32, (1, LANES), 1)
    for h in range(NSA_KV_HEADS):
        q = jnp.concatenate([q_ref[:, (h * g + gi) * hd:(h * g + gi + 1) * hd] for gi in range(g)], axis=0)
        q = jnp.concatenate([q, jnp.zeros((SUBLANES - g, hd), q.dtype)], axis=0) if g < SUBLANES else q
        kc = m_ref[:, h * hd:(h + 1) * hd]
        vc = m_ref[:, (NSA_KV_HEADS + h) * hd:(NSA_KV_HEADS + h + 1) * hd]
        s = _dot_nt(q, kc)
        m = jnp.max(s, axis=1, keepdims=True)
        e = jnp.exp(s - m)
        p_c = e / jnp.sum(e, axis=1, keepdims=True)
        oc_ref[h] = _dot(p_c, vc)[0:g]
        imp = jnp.sum(p_c[0:g], axis=0, keepdims=True)
        idx = jnp.where(lane == 0, nbp, 0)
        for it in range(1, n_sel):
            best = jnp.max(imp, axis=1, keepdims=True)
            j = jnp.min(jnp.where(imp == best, blk, nbp), axis=1, keepdims=True)
            idx = jnp.where(lane == it, j, idx)
            imp = jnp.where(blk == j, -2.0, imp)
        idx_ref[h] = idx


def _nsa_choose(qr, means, dm, n_sel):
    b = qr.shape[0]
    nbp = means.shape[1]
    g = dm.g
    kern = functools.partial(_nsa_choose_kernel, g=g, nbp=nbp, n_sel=n_sel)
    return pl.pallas_call(
        kern,
        out_shape=(jax.ShapeDtypeStruct((b, NSA_KV_HEADS, g, NSA_HEAD), F32),
                   jax.ShapeDtypeStruct((b, NSA_KV_HEADS, 1, LANES), jnp.int32)),
        grid=(b,),
        in_specs=[pl.BlockSpec((None, 1, dm.nw), lambda bi: (bi, 0, 0)),
                  pl.BlockSpec((None, nbp, means.shape[2]), lambda bi: (bi, 0, 0))],
        out_specs=(pl.BlockSpec((None, NSA_KV_HEADS, g, NSA_HEAD), lambda bi: (bi, 0, 0, 0)),
                   pl.BlockSpec((None, NSA_KV_HEADS, 1, LANES), lambda bi: (bi, 0, 0, 0))),
        compiler_params=_cparams(("parallel",)),
        name="nsa_choose",
    )(qr, means)


def _nsa_sample_kernel(idx_ref, pt_ref, q_ref, cb_ref, kn_ref, vn_ref, wb_ref, kwn_ref, vwn_ref,
                       oc_ref, gt_ref, o_ref, m_ref, l_ref, acc_ref, *, g, n_sel):
    h = pl.program_id(1)
    j = pl.program_id(2)
    hd = NSA_HEAD
    rows = SUBLANES
    q = jnp.concatenate([q_ref[:, gi * hd:(gi + 1) * hd] for gi in range(g)], axis=0)
    if g < rows:
        q = jnp.concatenate([q, jnp.zeros((rows - g, hd), q.dtype)], axis=0)

    @pl.when(j == 0)
    def _():
        s0 = jnp.sum(q.astype(F32) * kn_ref[...].astype(BF16).astype(F32), axis=1, keepdims=True)
        m_ref[...] = s0
        l_ref[...] = jnp.ones((rows, 1), F32)
        acc_ref[...] = jnp.broadcast_to(vn_ref[...].astype(BF16).astype(F32), (rows, hd))

    kvh = NSA_KV_HEADS

    @pl.when(j > 0)
    def _():
        xb = cb_ref[...].astype(BF16)
        srow = lax.broadcasted_iota(jnp.int32, (1, xb.shape[0]), 1) % (4 * kvh)
        sc = _dot_nt(q, xb) + jnp.where(srow == 2 * kvh + h, 0.0, NEG)
        m_old = m_ref[...]
        m_new = jnp.maximum(m_old, jnp.max(sc, axis=1, keepdims=True))
        pr = jnp.exp(sc - m_new)
        alpha = jnp.exp(m_old - m_new)
        l_ref[...] = alpha * l_ref[...] + jnp.sum(pr, axis=1, keepdims=True)
        acc_ref[...] = alpha * acc_ref[...] + _dot(pltpu.roll(pr, kvh, axis=1), xb)
        m_ref[...] = m_new

    @pl.when(j == n_sel - 1)
    def _():
        o_sel = acc_ref[...] / l_ref[...]
        xw = wb_ref[...].astype(BF16)
        wrow = lax.broadcasted_iota(jnp.int32, (1, xw.shape[0]), 1) % (2 * kvh)
        sw = _dot_nt(q, xw) + jnp.where(wrow == h, 0.0, NEG)
        sn = jnp.sum(q.astype(F32) * kwn_ref[...].astype(BF16).astype(F32), axis=1, keepdims=True)
        mw = jnp.maximum(jnp.max(sw, axis=1, keepdims=True), sn)
        ew = jnp.exp(sw - mw)
        en = jnp.exp(sn - mw)
        den = jnp.sum(ew, axis=1, keepdims=True) + en
        pn = (en / den).astype(BF16).astype(F32)
        o_win = _dot(pltpu.roll(ew / den, kvh, axis=1), xw) + pn * vwn_ref[...].astype(BF16).astype(F32)
        gs = _sigmoid(gt_ref[...])
        o_cmp = oc_ref[...]
        for gi in range(g):
            acc = None
            for ci, ob in enumerate((o_cmp, o_sel, o_win)):
                c0 = gi * 3 + ci
                c1 = (g + gi) * 3 + ci
                gate = jnp.where(h == 0, gs[:, c0:c0 + 1], gs[:, c1:c1 + 1])
                term = ob[gi:gi + 1] * gate
                acc = term if acc is None else acc + term
            o_ref[:, gi * hd:(gi + 1) * hd] = acc


def _nsa_sample(qr, kv4, kwv, cache_rows, win_rows, layer, n_pool, page, lwin, page_table, idx, o_cmp, p, dm, n_sel):
    b = qr.shape[0]
    g, hd = dm.g, NSA_HEAD
    bpp = page // NSA_BLOCK
    brows = NSA_BLOCK * 4 * NSA_KV_HEADS
    wrows = lwin * 2 * NSA_KV_HEADS
    kern = functools.partial(_nsa_sample_kernel, g=g, n_sel=n_sel)

    def sel_map(bi, h, j, idx_ref, pt_ref):
        blk = idx_ref[(bi * NSA_KV_HEADS + h) * LANES + j]
        n_past = pt_ref.shape[1] * bpp
        blk = jnp.minimum(blk, n_past - 1)
        return ((layer * n_pool + pt_ref[bi, blk // bpp]) * bpp + blk % bpp, 0)

    newspec = lambda c: pl.BlockSpec((None, 1, hd), lambda bi, h, j, ir, pr, c=c: (bi, 0, 2 * c + h))
    return pl.pallas_call(
        kern,
        out_shape=jax.ShapeDtypeStruct((b, 1, dm.nw), F32),
        grid_spec=pltpu.PrefetchScalarGridSpec(
            num_scalar_prefetch=2,
            grid=(b, NSA_KV_HEADS, n_sel),
            in_specs=[pl.BlockSpec((None, 1, g * hd), lambda bi, h, j, ir, pr: (bi, 0, h)),
                      pl.BlockSpec((brows, hd), sel_map),
                      newspec(2), newspec(3),
                      pl.BlockSpec((wrows, hd), lambda bi, h, j, ir, pr: (layer * b + bi, 0)),
                      newspec(0), newspec(1),
                      pl.BlockSpec((None, None, g, hd), lambda bi, h, j, ir, pr: (bi, h, 0, 0)),
                      pl.BlockSpec((None, 1, LANES), lambda bi, h, j, ir, pr: (bi, 0, _blk(dm.c_ng, LANES)))],
            out_specs=pl.BlockSpec((None, 1, g * hd), lambda bi, h, j, ir, pr: (bi, 0, h)),
            scratch_shapes=[pltpu.VMEM((SUBLANES, 1), F32), pltpu.VMEM((SUBLANES, 1), F32),
                            pltpu.VMEM((SUBLANES, hd), F32)],
        ),
        compiler_params=_cparams(("parallel", "parallel", "arbitrary")),
        name="nsa_sample",
    )(idx, page_table, qr, cache_rows, kv4, kv4, win_rows, kwv, kwv, o_cmp, p)


def _rope_tables(pos):
    half = NSA_HEAD // 2
    inv = jnp.exp(-math.log(ROPE_THETA) * jnp.arange(half, dtype=F32) / half)
    ang = pos.astype(F32)[:, None] * inv[None, :]
    cos, sin = jnp.cos(ang), jnp.sin(ang)
    return jnp.concatenate([cos, cos], axis=1), jnp.concatenate([-sin, sin], axis=1)


def _shift_cols(p_last, dm):
    return jnp.concatenate([p_last[..., dm.c_r:dm.c_r + 3 * dm.rw], p_last[..., dm.c_lora:dm.c_lora + RWKV_LORA]],
                           axis=-1)


def _row(x):
    return x.reshape(1, -1)


def kernel(x_prompt, x_sample, cache_nsa_kv, page_table, state_nsa_window, state_wkv, state_rwkv_shift,
           state_pool, state_ffn_conv, norm_mix, w_in, pool_w, pool_scale, rwkv_mu, rwkv_w0, rwkv_w2,
           rwkv_a0, rwkv_a2, rwkv_g2, rwkv_k_k, rwkv_k_a, rwkv_r_k, rwkv_ln_w, rwkv_ln_b, w_branch, w_out,
           norm_ffn, ffn_up, ffn_conv, ffn_conv_b, ffn_down, norm_final):
    bp, sp, d = x_prompt.shape
    bs, ts, _ = x_sample.shape
    assert ts == 1
    depth = w_in.shape[0]
    dm = Dims(d)
    ff = dm.ff
    hd = NSA_HEAD
    page = cache_nsa_kv.shape[2]
    n_pages = page_table.shape[1]
    past = n_pages * page
    kvw = 4 * NSA_KV_HEADS * hd

    w_mix_b = w_in[:, :, :dm.np].astype(BF16)
    w_gate_b = w_in[:, :, dm.o3:].astype(BF16)
    pool_w_b = pool_w.astype(BF16)
    wb_b = w_branch.astype(BF16)
    wo_b = w_out.astype(BF16)
    up_b = ffn_up.astype(BF16)
    down_b = ffn_down.astype(BF16)
    g2_b = rwkv_g2.astype(BF16)
    kk_r = rwkv_k_k.reshape(depth, 1, dm.rw)
    ka_r = rwkv_k_a.reshape(depth, 1, dm.rw)
    rk_r = rwkv_r_k.reshape(depth, 1, dm.rw)

    mp = bp * sp
    tm_p = min(512, sp)
    tm_in = min(1024, sp)
    wide = lambda n: 1024 if n % 1024 == 0 else 512
    tn_ff = 512 if ff % 512 == 0 else 256
    tt_pool = min(512, sp)
    tt_rwkv = min(256, sp)
    tt_nsa = min(512, sp)
    tps = sp // tm_p

    def rwkv_params(l):
        return (_row(rwkv_mu[l]), _row(rwkv_w0[l]), rwkv_w2[l], _row(rwkv_a0[l]), rwkv_a2[l], g2_b[l],
                kk_r[l], ka_r[l], rk_r[l])

    cos_p, sin_p = _rope_tables(jnp.arange(sp, dtype=jnp.int32))
    x = x_prompt.reshape(mp, d)
    zeros_hist16 = jnp.zeros((bp, POOL_HIST + 1, dm.pw), F32)
    zeros_shift = jnp.zeros((bp, 1, dm.rcols), F32)
    zeros_wkv = jnp.zeros((bp, dm.rh, RWKV_HEAD, RWKV_HEAD), F32)
    zeros_conv = jnp.zeros((bp, CONV_W - 1, 2 * ff), F32)
    wl = min(NSA_WINDOW, sp)
    p_kv, p_win, p_wkv, p_shift, p_pool, p_conv = [], [], [], [], [], []
    for l in range(depth):
        p = _norm_matmul(x, _row(norm_mix[l]), w_mix_b[l], tm_in, wide(dm.np)).reshape(bp, sp, dm.np)
        pg = _norm_matmul(x, _row(norm_mix[l]), w_gate_b[l], tm_in, wide(N_BRANCH * d))
        ya = _pool_prompt(p, zeros_hist16, pool_w_b[l], _row(pool_scale[l]), dm, tt_pool)
        prep = _rwkv_prep(p, zeros_shift, *rwkv_params(l), dm, tt_rwkv)
        yb, s_t = _rwkv_scan(prep, _row(rwkv_ln_w[l]), _row(rwkv_ln_b[l]), zeros_wkv, dm)
        qr, kv4, kwv, kcvc = _nsa_prep(p, cos_p, sin_p, dm, tt_nsa, True)
        yc = _nsa_prompt(qr, kv4, kwv, kcvc, p, dm)
        mrg = _branch_merge(ya.reshape(mp, -1), yb.reshape(mp, -1), yc.reshape(mp, -1), pg, wb_b[l], dm,
                            tm_p, min(1024, d))
        x = _out_proj(x, mrg, wo_b[l], tm_p, min(1024, d))
        x, za, zv = _ffn(x, _row(norm_ffn[l]), up_b[l], ffn_conv[l], _row(ffn_conv_b[l]), down_b[l],
                         zeros_conv, ff, tm_p, tn_ff, tps, False)
        p_kv.append(kv4.reshape(bp, sp, 4, NSA_KV_HEADS, hd))
        p_win.append(kwv[:, sp - wl:].reshape(bp, wl, 2, NSA_KV_HEADS, hd))
        p_wkv.append(jnp.swapaxes(s_t, -1, -2))
        p_shift.append(_shift_cols(p[:, sp - 1:, :], dm))
        p_pool.append(p[:, sp - POOL_HIST:, dm.c_pool:dm.c_pool + dm.pw])
        zl = jnp.concatenate([za, zv], axis=-1).reshape(bp, tps, 2, 2 * ff)
        p_conv.append(zl[:, tps - 1])
    y_prompt = _final_norm(x, _row(norm_final), tm_p).reshape(bp, sp, d)

    pos_s = past
    cos_s, sin_s = _rope_tables(jnp.full((1,), pos_s, dtype=jnp.int32))
    xs = x_sample.reshape(bs, d)
    nbp = past // NSA_BLOCK
    n_sel = min(NSA_TOPK, nbp + 1)
    n_pool = cache_nsa_kv.shape[1]
    lwin = state_nsa_window.shape[2]
    cache_pairs = cache_nsa_kv.reshape(depth, n_pool, page, 2, 2 * NSA_KV_HEADS, hd)
    cache_rows = cache_nsa_kv.reshape(-1, hd)
    win_rows = state_nsa_window.reshape(-1, hd)
    s_kv, s_win, s_wkv, s_shift, s_pool, s_conv = [], [], [], [], [], []
    for l in range(depth):
        p = _norm_matmul(xs, _row(norm_mix[l]), w_mix_b[l], bs, wide(dm.np)).reshape(bs, 1, dm.np)
        pg = _norm_matmul(xs, _row(norm_mix[l]), w_gate_b[l], bs, wide(N_BRANCH * d))
        e16 = jnp.concatenate([state_pool[l], p[:, :, dm.c_pool:dm.c_pool + dm.pw]], axis=1)
        ya = _pool_sample(e16, pool_w_b[l], _row(pool_scale[l]), dm, pos_s)
        yb, s_new = _rwkv_step(p, state_rwkv_shift[l], state_wkv[l], *rwkv_params(l),
                               _row(rwkv_ln_w[l]), _row(rwkv_ln_b[l]), dm)
        qr, kv4, kwv = _nsa_prep(p, cos_s, sin_s, dm, 1, False)
        means = _page_means(cache_pairs, l, page_table)
        o_cmp, idx = _nsa_choose(qr, means, dm, n_sel)
        yc = _nsa_sample(qr, kv4, kwv, cache_rows, win_rows, l, n_pool, page, lwin, page_table, idx.reshape(-1),
                         o_cmp, p, dm, n_sel)
        mrg = _branch_merge(ya, yb.reshape(bs, -1), yc.reshape(bs, -1), pg, wb_b[l], dm, bs, min(1024, d))
        xs = _out_proj(xs, mrg, wo_b[l], bs, min(1024, d))
        xs, za, zv = _ffn(xs, _row(norm_ffn[l]), up_b[l], ffn_conv[l], _row(ffn_conv_b[l]), down_b[l],
                          state_ffn_conv[l], ff, bs, tn_ff, 1, True)
        s_kv.append(kv4.reshape(bs, 1, 4, NSA_KV_HEADS, hd))
        wk = jnp.concatenate([state_nsa_window[l], kwv.reshape(bs, 1, 2, NSA_KV_HEADS, hd)], axis=1)
        s_win.append(wk[:, -NSA_WINDOW:])
        s_wkv.append(s_new)
        s_shift.append(_shift_cols(p, dm))
        s_pool.append(e16[:, 1:])
        z_new = jnp.concatenate([za, zv], axis=-1)[:, None, :]
        s_conv.append(jnp.concatenate([state_ffn_conv[l][:, 1:], z_new], axis=1))
    y_sample = _final_norm(xs, _row(norm_final), bs).reshape(bs, 1, d)

    st = lambda xs_: jnp.stack(xs_)
    return (y_prompt, y_sample, st(p_kv), st(p_win), st(p_wkv), st(p_shift), st(p_pool), st(p_conv),
            st(s_kv), st(s_win), st(s_wkv), st(s_shift), st(s_pool), st(s_conv))
```

```python
import functools
import math

import jax
import jax.numpy as jnp
from jax import lax
from jax.experimental import pallas as pl
from jax.experimental.pallas import tpu as pltpu

F32 = jnp.float32
BF16 = jnp.bfloat16
HIGHEST = lax.Precision.HIGHEST

POOL_GROUPS = 4
POOL_WINDOWS = (2, 4, 8, 16)
POOL_HIST = 15
RWKV_HEAD = 64
RWKV_W_RANK = 64
RWKV_A_RANK = 64
RWKV_G_RANK = 128
RWKV_LORA = RWKV_W_RANK + RWKV_A_RANK + RWKV_G_RANK
RWKV_DECAY_SCALE = 0.6065306597126334
GN_EPS = 64e-5
NSA_HEAD = 128
NSA_KV_HEADS = 2
NSA_BLOCK = 64
NSA_TOPK = 16
NSA_WINDOW = 512
NSA_QTILE = 128
N_BRANCH = 3
CONV_W = 3
ROPE_THETA = 10000.0
RMS_EPS = 1e-6
NEG = -1e30
M_INIT = -1e29

LANES = 128
SUBLANES = 8
VMEM_LIMIT = 56 * 1024 * 1024
RWKV_CHUNK = 64
FFN_SUB_ROWS = 128


class Dims:
    def __init__(self, d_model):
        d = d_model
        self.d = d
        self.pw = d // 2
        self.cg = self.pw // POOL_GROUPS
        self.rw = d // 2
        self.rh = self.rw // RWKV_HEAD
        self.rcols = 3 * self.rw + RWKV_LORA
        self.nw = d // 2
        self.nh = self.nw // NSA_HEAD
        self.g = self.nh // NSA_KV_HEADS
        self.kvc = 6 * NSA_KV_HEADS * NSA_HEAD
        self.ngate = 3 * self.nh
        self.ff = 256 * ((8 * d // 3 + 255) // 256)
        self.o1 = self.pw
        self.o2 = self.o1 + self.rcols
        self.o3 = self.o2 + self.nw + self.kvc + self.ngate
        self.c_pool = 0
        self.c_r = self.o1
        self.c_lora = self.o1 + 3 * self.rw
        self.c_q = self.o2
        self.c_kv = self.o2 + self.nw
        self.c_ng = self.c_kv + self.kvc
        self.tn_in = 512
        self.np = -(-self.o3 // self.tn_in) * self.tn_in
        self.part = NSA_KV_HEADS * NSA_HEAD


def _blk(offset, width):
    assert offset % width == 0, (offset, width)
    return offset // width


def _cparams(sem):
    return pltpu.CompilerParams(dimension_semantics=sem, vmem_limit_bytes=VMEM_LIMIT)


def _dot(a, b):
    return jnp.dot(a.astype(BF16), b.astype(BF16), preferred_element_type=F32)


def _dot_nt(a, b):
    return lax.dot_general(a.astype(BF16), b.astype(BF16), (((1,), (1,)), ((), ())),
                           preferred_element_type=F32)


def _dot_hi(a, b):
    return jnp.dot(a, b, precision=HIGHEST, preferred_element_type=F32)


_NN = (((1,), (0,)), ((), ()))
_NT = (((1,), (1,)), ((), ()))
_TN = (((0,), (0,)), ((), ()))


def _split2(x):
    hi = x.astype(BF16)
    return hi, (x - hi.astype(F32)).astype(BF16)


def _mm3(a, b, dims):
    d = lambda p, q: lax.dot_general(p, q, dims, preferred_element_type=F32)
    return d(a[0], b[0]) + d(a[0], b[1]) + d(a[1], b[0])


def _split3(x):
    hi = x.astype(BF16)
    r1 = x - hi.astype(F32)
    mid = r1.astype(BF16)
    lo = (r1 - mid.astype(F32)).astype(BF16)
    return hi, mid, lo


def _dot_exact_lhs(m_bf16, x):
    hi, mid, lo = _split3(x)
    d = lambda p: jnp.dot(m_bf16, p, preferred_element_type=F32)
    return d(hi) + d(mid) + d(lo)


def _dot_exact_rhs(x, m_bf16):
    hi, mid, lo = _split3(x)
    d = lambda p: jnp.dot(p, m_bf16, preferred_element_type=F32)
    return d(hi) + d(mid) + d(lo)


def _head_ones():
    i = lax.broadcasted_iota(jnp.int32, (LANES, LANES), 0) // RWKV_HEAD
    j = lax.broadcasted_iota(jnp.int32, (LANES, LANES), 1) // RWKV_HEAD
    return (i == j).astype(BF16)


def _head_sum(x, ones):
    parts = [_dot_exact_rhs(x[:, c:c + LANES], ones) for c in range(0, x.shape[1], LANES)]
    return parts[0] if len(parts) == 1 else jnp.concatenate(parts, axis=1)


def _rmsnorm_val(x, g):
    ms = jnp.mean(x * x, axis=-1, keepdims=True)
    return x * lax.rsqrt(ms + RMS_EPS) * g


def _sigmoid(x):
    return 1.0 / (1.0 + jnp.exp(-x))


def _norm_matmul_kernel(x_ref, g_ref, w_ref, o_ref, h_ref):
    @pl.when(pl.program_id(1) == 0)
    def _():
        h_ref[...] = _rmsnorm_val(x_ref[...], g_ref[...]).astype(BF16)

    o_ref[...] = jnp.dot(h_ref[...], w_ref[...], preferred_element_type=F32)


def _norm_matmul(x, g, w, layer, tm, tn):
    m, d = x.shape
    n = w.shape[2]
    return pl.pallas_call(
        _norm_matmul_kernel,
        out_shape=jax.ShapeDtypeStruct((m, n), F32),
        grid=(m // tm, n // tn),
        in_specs=[pl.BlockSpec((tm, d), lambda i, j: (i, 0)),
                  pl.BlockSpec((1, d), lambda i, j: (0, 0)),
                  pl.BlockSpec((None, d, tn), lambda i, j: (layer, 0, j))],
        out_specs=pl.BlockSpec((tm, tn), lambda i, j: (i, j)),
        scratch_shapes=[pltpu.VMEM((tm, d), BF16)],
        compiler_params=_cparams(("parallel", "arbitrary")),
        name="norm_matmul",
    )(x, g, w)


def _branch_merge_kernel(ya_ref, yb_ref, yc_ref, ga_ref, gb_ref, gc_ref, w_ref, o_ref):
    acc = _dot(ya_ref[...], w_ref[0]) * _sigmoid(ga_ref[...])
    acc = acc + _dot(yb_ref[...], w_ref[1]) * _sigmoid(gb_ref[...])
    acc = acc + _dot(yc_ref[...], w_ref[2]) * _sigmoid(gc_ref[...])
    o_ref[...] = acc.astype(BF16)


def _branch_merge(ya, yb, yc, p, wb, layer, dm, tm, tn):
    m = ya.shape[0]
    d, hw = dm.d, dm.d // 2
    nb = d // tn
    yspec = pl.BlockSpec((tm, hw), lambda j, i: (i, 0))
    gspec = lambda k: pl.BlockSpec((tm, tn), lambda j, i, k=k: (i, k * nb + j))
    return pl.pallas_call(
        _branch_merge_kernel,
        out_shape=jax.ShapeDtypeStruct((m, d), BF16),
        grid=(nb, m // tm),
        in_specs=[yspec, yspec, yspec, gspec(0), gspec(1), gspec(2),
                  pl.BlockSpec((None, N_BRANCH, hw, tn), lambda j, i: (layer, 0, 0, j))],
        out_specs=pl.BlockSpec((tm, tn), lambda j, i: (i, j)),
        compiler_params=_cparams(("parallel", "parallel")),
        name="branch_merge",
    )(ya, yb, yc, p, p, p, wb)


def _out_proj_kernel(x_ref, m_ref, w_ref, o_ref):
    o_ref[...] = x_ref[...] + jnp.dot(m_ref[...], w_ref[...], preferred_element_type=F32)


def _out_proj(x, mrg, wo, layer, tm, tn):
    m, d = x.shape
    return pl.pallas_call(
        _out_proj_kernel,
        out_shape=jax.ShapeDtypeStruct((m, d), F32),
        grid=(m // tm, d // tn),
        in_specs=[pl.BlockSpec((tm, tn), lambda i, j: (i, j)),
                  pl.BlockSpec((tm, d), lambda i, j: (i, 0)),
                  pl.BlockSpec((None, d, tn), lambda i, j: (layer, 0, j))],
        out_specs=pl.BlockSpec((tm, tn), lambda i, j: (i, j)),
        compiler_params=_cparams(("parallel", "parallel")),
        name="out_proj",
    )(x, mrg, wo)


def _ffn_kernel(x_ref, g_ref, wa_ref, wv_ref, cwa_ref, cwv_ref, cba_ref, cbv_ref, wd_ref,
                ha_ref, hv_ref, o_ref, za_ref, zv_ref, h_ref, ca_ref, cv_ref, *, tm, sub, tps, rows_are_seqs):
    i = pl.program_id(0)
    j = pl.program_id(1)

    @pl.when(j == 0)
    def _():
        x = x_ref[...]
        h_ref[...] = _rmsnorm_val(x, g_ref[...]).astype(BF16)
        o_ref[...] = x

    def conv_mix(z, zs1, zs2, cw_ref, cb_ref):
        return zs2 * cw_ref[0:1] + zs1 * cw_ref[1:2] + z * cw_ref[2:3] + cb_ref[...]

    if rows_are_seqs:
        h = h_ref[...]
        za = jnp.dot(h, wa_ref[...], preferred_element_type=F32)
        zv = jnp.dot(h, wv_ref[...], preferred_element_type=F32)
        za_ref[...] = za
        zv_ref[...] = zv
        ca = conv_mix(za, ha_ref[:, 1, :], ha_ref[:, 0, :], cwa_ref, cba_ref)
        cv = conv_mix(zv, hv_ref[:, 1, :], hv_ref[:, 0, :], cwv_ref, cbv_ref)
        o_ref[...] += _dot(ca * _sigmoid(ca) * cv, wd_ref[...])
        return

    @pl.when((i % tps) == 0)
    def _():
        ca_ref[j] = ha_ref[...]
        cv_ref[j] = hv_ref[...]

    rs = [slice(s * sub, (s + 1) * sub) for s in range(tm // sub)]
    zas = [jnp.dot(h_ref[r], wa_ref[...], preferred_element_type=F32) for r in rs]
    zvs = [jnp.dot(h_ref[r], wv_ref[...], preferred_element_type=F32) for r in rs]
    row = lax.broadcasted_iota(jnp.int32, (sub, zas[0].shape[1]), 0)

    def conv(zs, s, carry_ref, cw_ref, cb_ref):
        z = zs[s]
        prev = carry_ref[j] if s == 0 else zs[s - 1][sub - 2:sub]
        zs1 = jnp.where(row == 0, prev[1:2], pltpu.roll(z, 1, axis=0))
        zs2 = jnp.where(row == 0, prev[0:1], jnp.where(row == 1, prev[1:2], pltpu.roll(z, 2, axis=0)))
        return conv_mix(z, zs1, zs2, cw_ref, cb_ref)

    for s, r in enumerate(rs):
        ca = conv(zas, s, ca_ref, cwa_ref, cba_ref)
        cv = conv(zvs, s, cv_ref, cwv_ref, cbv_ref)
        o_ref[r] += _dot(ca * _sigmoid(ca) * cv, wd_ref[...])
    za_last = zas[-1][sub - 2:sub]
    zv_last = zvs[-1][sub - 2:sub]
    ca_ref[j] = za_last
    cv_ref[j] = zv_last
    za_ref[...] = za_last
    zv_ref[...] = zv_last


def _ffn(x, g, w_up, cw, cb, w_down, layer, hist, ff, tm, tn, tps, rows_are_seqs):
    m, d = x.shape
    nj = ff // tn
    nm = m // tm
    if rows_are_seqs:
        assert nm == 1
        hspec_a = pl.BlockSpec((m, 2, tn), lambda i, j: (0, 0, j))
        hspec_v = pl.BlockSpec((m, 2, tn), lambda i, j: (0, 0, nj + j))
        zshape = jax.ShapeDtypeStruct((m, ff), F32)
        zspec = pl.BlockSpec((m, tn), lambda i, j: (0, j))
    else:
        hspec_a = pl.BlockSpec((None, 2, tn), lambda i, j: (i // tps, 0, j))
        hspec_v = pl.BlockSpec((None, 2, tn), lambda i, j: (i // tps, 0, nj + j))
        zshape = jax.ShapeDtypeStruct((nm, 2, ff), F32)
        zspec = pl.BlockSpec((None, 2, tn), lambda i, j: (i, 0, j))
    sub = FFN_SUB_ROWS if tm % FFN_SUB_ROWS == 0 else tm
    kern = functools.partial(_ffn_kernel, tm=tm, sub=sub, tps=tps, rows_are_seqs=rows_are_seqs)
    return pl.pallas_call(
        kern,
        out_shape=(jax.ShapeDtypeStruct((m, d), F32), zshape, zshape),
        grid=(nm, nj),
        in_specs=[pl.BlockSpec((tm, d), lambda i, j: (i, 0)),
                  pl.BlockSpec((1, d), lambda i, j: (0, 0)),
                  pl.BlockSpec((None, d, tn), lambda i, j: (layer, 0, j)),
                  pl.BlockSpec((None, d, tn), lambda i, j: (layer, 0, nj + j)),
                  pl.BlockSpec((CONV_W, tn), lambda i, j: (0, j)),
                  pl.BlockSpec((CONV_W, tn), lambda i, j: (0, nj + j)),
                  pl.BlockSpec((1, tn), lambda i, j: (0, j)),
                  pl.BlockSpec((1, tn), lambda i, j: (0, nj + j)),
                  pl.BlockSpec((None, tn, d), lambda i, j: (layer, j, 0)),
                  hspec_a, hspec_v],
        out_specs=(pl.BlockSpec((tm, d), lambda i, j: (i, 0)), zspec, zspec),
        scratch_shapes=[pltpu.VMEM((tm, d), BF16),
                        pltpu.VMEM((nj, 2, tn), F32),
                        pltpu.VMEM((nj, 2, tn), F32)],
        compiler_params=_cparams(("arbitrary", "arbitrary")),
        name="conv_ffn",
    )(x, g, w_up, w_up, cw, cw, cb, cb, w_down, hist, hist)


def _final_norm_kernel(x_ref, g_ref, o_ref):
    o_ref[...] = _rmsnorm_val(x_ref[...], g_ref[...])


def _final_norm(x, g, tm):
    m, d = x.shape
    return pl.pallas_call(
        _final_norm_kernel,
        out_shape=jax.ShapeDtypeStruct((m, d), F32),
        grid=(m // tm,),
        in_specs=[pl.BlockSpec((tm, d), lambda i: (i, 0)), pl.BlockSpec((1, d), lambda i: (0, 0))],
        out_specs=pl.BlockSpec((tm, d), lambda i: (i, 0)),
        compiler_params=_cparams(("parallel",)),
        name="final_norm",
    )(x, g)


def _pool_kernel(u_ref, hist_ref, w_ref, sc_ref, o_ref, e_ref, *, tt, cg):
    i = pl.program_id(1)
    hrows = POOL_HIST + 1

    @pl.when(i == 0)
    def _():
        e_ref[0:hrows] = hist_ref[...]

    @pl.when(i > 0)
    def _():
        e_ref[0:hrows] = e_ref[tt:tt + hrows]

    u = u_ref[...]
    e_ref[hrows:hrows + tt] = u
    pos = i * tt + lax.broadcasted_iota(jnp.int32, (tt, 1), 0)
    for gi, w in enumerate(POOL_WINDOWS):
        cs = slice(gi * cg, (gi + 1) * cg)
        s = e_ref[:, cs]
        sh = 1
        while sh < w:
            s = s + pltpu.roll(s, sh, axis=0)
            sh *= 2
        cnt = jnp.minimum(w, pos + 1).astype(F32)
        dlt = s[hrows:] / cnt - u[:, cs]
        o_ref[:, cs] = _dot(dlt, w_ref[gi]) * sc_ref[:, cs]


def _pool_prompt(p, hist16, w, scale, dm, tt):
    b, t, _ = p.shape
    kern = functools.partial(_pool_kernel, tt=tt, cg=dm.cg)
    return pl.pallas_call(
        kern,
        out_shape=jax.ShapeDtypeStruct((b, t, dm.pw), F32),
        grid=(b, t // tt),
        in_specs=[pl.BlockSpec((None, tt, dm.pw), lambda bi, i: (bi, i, _blk(dm.c_pool, dm.pw))),
                  pl.BlockSpec((None, POOL_HIST + 1, dm.pw), lambda bi, i: (bi, 0, 0)),
                  pl.BlockSpec((POOL_GROUPS, dm.cg, dm.cg), lambda bi, i: (0, 0, 0)),
                  pl.BlockSpec((1, dm.pw), lambda bi, i: (0, 0))],
        out_specs=pl.BlockSpec((None, tt, dm.pw), lambda bi, i: (bi, i, 0)),
        scratch_shapes=[pltpu.VMEM((tt + POOL_HIST + 1, dm.pw), F32)],
        compiler_params=_cparams(("parallel", "arbitrary")),
        name="pool_mixer",
    )(p, hist16, w, scale)


def _pool_sample_kernel(e_ref, w_ref, sc_ref, o_ref, *, cg, pos):
    e = e_ref[...]
    hrows = POOL_HIST + 1
    row = lax.broadcasted_iota(jnp.int32, e.shape, 1)
    u = e[:, hrows - 1, :]
    for gi, w in enumerate(POOL_WINDOWS):
        cs = slice(gi * cg, (gi + 1) * cg)
        win = jnp.sum(jnp.where(row >= hrows - w, e, 0.0)[:, :, cs], axis=1)
        dlt = win / float(min(w, pos + 1)) - u[:, cs]
        o_ref[:, cs] = _dot(dlt, w_ref[gi]) * sc_ref[:, cs]


def _pool_sample(e16, w, scale, dm, pos):
    b = e16.shape[0]
    kern = functools.partial(_pool_sample_kernel, cg=dm.cg, pos=pos)
    return pl.pallas_call(
        kern,
        out_shape=jax.ShapeDtypeStruct((b, dm.pw), F32),
        compiler_params=pltpu.CompilerParams(vmem_limit_bytes=VMEM_LIMIT),
        name="pool_mixer_step",
    )(e16, w, scale)


def _rwkv_mix_cols(x, xprev, mu):
    return x + (xprev - x) * mu


def _rwkv_lora(pl_, w0, w2, a0, a2, g2):
    w_in = pl_[:, 0:RWKV_W_RANK]
    a_in = pl_[:, RWKV_W_RANK:RWKV_W_RANK + RWKV_A_RANK]
    g_in = pl_[:, RWKV_W_RANK + RWKV_A_RANK:]
    lw = -RWKV_DECAY_SCALE * _sigmoid(w0 + _dot_hi(jnp.tanh(w_in), w2))
    a = _sigmoid(a0 + _dot_hi(a_in, a2))
    g = _dot(_sigmoid(g_in), g2)
    return lw, a, g


def _rwkv_prep_kernel(r_ref, k_ref, v_ref, l_ref, rp_ref, kp_ref, vp_ref, lp_ref, sh_ref, mu_ref,
                      w0_ref, w2_ref, a0_ref, a2_ref, g2_ref, kk_ref, ka_ref, rk_ref,
                      rt_ref, at_ref, kh_ref, bh_ref, kg_ref, bg_ref, vo_ref, bo_ref, go_ref, gc_ref,
                      *, tt, rw, chunk):
    i = pl.program_id(1)
    first = i == 0
    row1 = lax.broadcasted_iota(jnp.int32, (tt, 1), 0)

    def mixed(x_ref, xp_ref, c0, c1):
        x = x_ref[...]
        carry = jnp.where(first, sh_ref[:, c0:c1], xp_ref[SUBLANES - 1:SUBLANES, :])
        xprev = jnp.where(row1 == 0, carry, pltpu.roll(x, 1, axis=0))
        return _rwkv_mix_cols(x, xprev, mu_ref[:, c0:c1])

    r = mixed(r_ref, rp_ref, 0, rw)
    k = mixed(k_ref, kp_ref, rw, 2 * rw)
    v = mixed(v_ref, vp_ref, 2 * rw, 3 * rw)
    lo = mixed(l_ref, lp_ref, 3 * rw, 3 * rw + RWKV_LORA)
    lw, a, g = _rwkv_lora(lo, w0_ref[...], w2_ref[...], a0_ref[...], a2_ref[...], g2_ref[...])

    ones = _head_ones()
    kk = k * kk_ref[...]
    kk = kk * lax.rsqrt(jnp.maximum(_head_sum(kk * kk, ones), 1e-12))
    k2 = k * (1.0 + (a - 1.0) * ka_ref[...])
    bonus = _head_sum(r * k2 * rk_ref[...], ones) * v
    bvec = kk * a

    ti = lax.broadcasted_iota(jnp.int32, (tt, tt), 0)
    si = lax.broadcasted_iota(jnp.int32, (tt, tt), 1)
    same = (ti // chunk) == (si // chunk)
    tri = (same & (si <= ti)).astype(BF16)
    blk = same.astype(BF16)
    cum = _dot_exact_lhs(tri, lw)
    tot = _dot_exact_lhs(blk, lw)
    e_in = jnp.exp(cum)
    e_out = jnp.exp(-cum)
    e_rest = jnp.exp(tot - cum)
    rt_ref[...] = r * e_in
    at_ref[...] = -kk * jnp.exp(cum - lw)
    kh_ref[...] = k2 * e_out
    bh_ref[...] = bvec * e_out
    kg_ref[...] = k2 * e_rest
    bg_ref[...] = bvec * e_rest
    vo_ref[...] = v
    bo_ref[...] = bonus
    go_ref[...] = g
    etot = jnp.exp(tot)
    for c in range(tt // chunk):
        gc_ref[c] = etot[c * chunk:c * chunk + 1, :]


def _rwkv_prep(p, shift, mu, w0, w2, a0, a2, g2, k_k, k_a, r_k, dm, tt):
    b, t, _ = p.shape
    rw = dm.rw
    chunk = RWKV_CHUNK
    kern = functools.partial(_rwkv_prep_kernel, tt=tt, rw=rw, chunk=chunk)
    cur = lambda c0, w: pl.BlockSpec((None, tt, w), lambda bi, i: (bi, i, _blk(c0, w)))
    prv = lambda c0, w: pl.BlockSpec(
        (None, SUBLANES, w), lambda bi, i: (bi, jnp.maximum(i * (tt // SUBLANES) - 1, 0), _blk(c0, w)))
    full = lambda shp: pl.BlockSpec(shp, lambda bi, i: (0,) * len(shp))
    tok = jax.ShapeDtypeStruct((b, t, rw), F32)
    tspec = pl.BlockSpec((None, tt, rw), lambda bi, i: (bi, i, 0))
    return pl.pallas_call(
        kern,
        out_shape=(tok,) * 9 + (jax.ShapeDtypeStruct((b, t // chunk, 1, rw), F32),),
        grid=(b, t // tt),
        in_specs=[cur(dm.c_r, rw), cur(dm.c_r + rw, rw), cur(dm.c_r + 2 * rw, rw), cur(dm.c_lora, RWKV_LORA),
                  prv(dm.c_r, rw), prv(dm.c_r + rw, rw), prv(dm.c_r + 2 * rw, rw), prv(dm.c_lora, RWKV_LORA),
                  pl.BlockSpec((None, 1, dm.rcols), lambda bi, i: (bi, 0, 0)),
                  full((1, dm.rcols)), full((1, rw)), full((RWKV_W_RANK, rw)), full((1, rw)),
                  full((RWKV_A_RANK, rw)), full((RWKV_G_RANK, rw)), full((1, rw)), full((1, rw)), full((1, rw))],
        out_specs=(tspec,) * 9 + (pl.BlockSpec((None, tt // chunk, 1, rw), lambda bi, i: (bi, i, 0, 0)),),
        compiler_params=_cparams(("parallel", "parallel")),
        name="rwkv_prep",
    )(p, p, p, p, p, p, p, p, shift, mu, w0, w2, a0, a2, g2, k_k, k_a, r_k)


def _group_norm_out(y, bonus, g, lnw, lnb, ones):
    ym = _head_sum(y, ones) * (1.0 / RWKV_HEAD)
    d = y - ym
    yv = _head_sum(d * d, ones) * (1.0 / RWKV_HEAD)
    return (d * lax.rsqrt(yv + GN_EPS) * lnw + lnb + bonus) * g


def _rwkv_solve_kernel(at_ref, kh_ref, bh_ref, rt_ref, v_ref, wt_ref, p1_ref, y1_ref, arb_ref,
                       *, chunk, heads, group):
    n = RWKV_HEAD
    ri = lax.broadcasted_iota(jnp.int32, (chunk, chunk), 0)
    ci = lax.broadcasted_iota(jnp.int32, (chunk, chunk), 1)
    strict = ci < ri
    incl = ci <= ri
    n_apply = int(math.log2(chunk))
    assert 2 ** n_apply == chunk

    for h0 in range(0, heads, group):
        sls = [slice(h * n, (h + 1) * n) for h in range(h0, min(h0 + group, heads))]
        at = [at_ref[:, sl] for sl in sls]
        vs = [_split2(v_ref[:, sl]) for sl in sls]
        prod = [_mm3(_split2(jnp.concatenate([at[i], rt_ref[:, sl]], axis=0)),
                     _split2(jnp.concatenate([kh_ref[:, sl], bh_ref[:, sl]], axis=0)), _NT)
                for i, sl in enumerate(sls)]
        tri2 = jnp.concatenate([strict, incl], axis=0)
        kv = [_mm3(_split2(jnp.where(tri2, prod[i][:, :chunk], 0.0)), vs[i], _NN) for i in range(len(sls))]
        for i, sl in enumerate(sls):
            arb_ref[:, sl] = jnp.where(incl, prod[i][chunk:, chunk:], 0.0)
            y1_ref[:, sl] = kv[i][chunk:]
        x = [jnp.concatenate([at[i], kv[i][:chunk]], axis=1) for i in range(len(sls))]
        pw = [jnp.where(strict, prod[i][:chunk, chunk:], 0.0) for i in range(len(sls))]
        for lvl in range(n_apply):
            if lvl + 1 < n_apply:
                r = [_mm3(_split2(pw[i]), _split2(jnp.concatenate([x[i], pw[i]], axis=1)), _NN)
                     for i in range(len(sls))]
                x = [x[i] + r[i][:, :2 * n] for i in range(len(sls))]
                pw = [r[i][:, 2 * n:] for i in range(len(sls))]
            else:
                x = [x[i] + _mm3(_split2(pw[i]), _split2(x[i]), _NN) for i in range(len(sls))]
        for i, sl in enumerate(sls):
            wt_ref[:, sl] = x[i][:, :n]
            p1_ref[:, sl] = x[i][:, n:]


def _rwkv_solve(at, kh, bh, rt, v, dm):
    b, t, rw = rt.shape
    chunk = RWKV_CHUNK
    kern = functools.partial(_rwkv_solve_kernel, chunk=chunk, heads=dm.rh, group=min(8, dm.rh))
    tspec = pl.BlockSpec((None, chunk, rw), lambda bi, c: (bi, c, 0))
    tok = jax.ShapeDtypeStruct((b, t, rw), F32)
    return pl.pallas_call(
        kern,
        out_shape=(tok,) * 4,
        grid=(b, t // chunk),
        in_specs=[tspec] * 5,
        out_specs=(tspec,) * 4,
        compiler_params=_cparams(("parallel", "parallel")),
        name="rwkv_solve",
    )(at, kh, bh, rt, v)


def _rwkv_scan_kernel(wt_ref, rt_ref, p1_ref, y1_ref, arb_ref, kg_ref, bg_ref, v_ref, bo_ref, g_ref, gc_ref,
                      lnw_ref, lnb_ref, s0_ref, y_ref, so_ref, st_ref, yacc_ref, *, chunk, heads):
    c = pl.program_id(1)
    n = RWKV_HEAD

    @pl.when(c == 0)
    def _():
        st_ref[...] = s0_ref[...]

    eye = lax.broadcasted_iota(jnp.int32, (n, n), 0) == lax.broadcasted_iota(jnp.int32, (n, n), 1)
    sls = [slice(h * n, (h + 1) * n) for h in range(heads)]

    s0 = [st_ref[h] for h in range(heads)]
    ws = [_mm3(_split2(jnp.concatenate([wt_ref[:, sl], rt_ref[:, sl]], axis=0)), _split2(s0[h]), _NN)
          for h, sl in enumerate(sls)]
    u = [p1_ref[:, sl] + ws[h][:chunk] for h, sl in enumerate(sls)]
    for h, sl in enumerate(sls):
        yacc_ref[:, sl] = ws[h][chunk:] + y1_ref[:, sl] + _mm3(_split2(arb_ref[:, sl]), _split2(u[h]), _NN)
    for h, sl in enumerate(sls):
        dg = jnp.where(eye, jnp.broadcast_to(gc_ref[:, sl], (n, n)), 0.0)
        lhs = jnp.concatenate([kg_ref[:, sl], bg_ref[:, sl], dg], axis=0)
        rhs = jnp.concatenate([v_ref[:, sl], u[h], s0[h]], axis=0)
        st_ref[h] = _mm3(_split2(lhs), _split2(rhs), _TN)

    y_ref[...] = _group_norm_out(yacc_ref[...], bo_ref[...], g_ref[...], lnw_ref[...], lnb_ref[...], _head_ones())

    @pl.when(c == pl.num_programs(1) - 1)
    def _():
        so_ref[...] = st_ref[...]


def _rwkv_scan(prep, lnw, lnb, s0t, dm):
    rt, at, kh, bh, kg, bg, v, bonus, g, gc = prep
    wt, p1, y1, arb = _rwkv_solve(at, kh, bh, rt, v, dm)
    b, t, rw = rt.shape
    chunk = RWKV_CHUNK
    heads = dm.rh
    kern = functools.partial(_rwkv_scan_kernel, chunk=chunk, heads=heads)
    tspec = pl.BlockSpec((None, chunk, rw), lambda bi, c: (bi, c, 0))
    sspec = pl.BlockSpec((None, heads, RWKV_HEAD, RWKV_HEAD), lambda bi, c: (bi, 0, 0, 0))
    return pl.pallas_call(
        kern,
        out_shape=(jax.ShapeDtypeStruct((b, t, rw), F32),
                   jax.ShapeDtypeStruct((b, heads, RWKV_HEAD, RWKV_HEAD), F32)),
        grid=(b, t // chunk),
        in_specs=[tspec] * 10 + [pl.BlockSpec((None, None, 1, rw), lambda bi, c: (bi, c, 0, 0)),
                                 pl.BlockSpec((1, rw), lambda bi, c: (0, 0)),
                                 pl.BlockSpec((1, rw), lambda bi, c: (0, 0)),
                                 sspec],
        out_specs=(tspec, sspec),
        scratch_shapes=[pltpu.VMEM((heads, RWKV_HEAD, RWKV_HEAD), F32), pltpu.VMEM((chunk, rw), F32)],
        compiler_params=_cparams(("parallel", "arbitrary")),
        name="rwkv_scan",
    )(wt, rt, p1, y1, arb, kg, bg, v, bonus, g, gc, lnw, lnb, s0t)


def _rwkv_step_kernel(r_ref, k_ref, v_ref, l_ref, sh_ref, mu_ref, w0_ref, w2_ref, a0_ref, a2_ref, g2_ref,
                      kk_ref, ka_ref, rk_ref, lnw_ref, lnb_ref, s_ref, y_ref, so_ref, *, rw, heads):
    n = RWKV_HEAD
    rows = SUBLANES

    def mixed(x_ref, c0, c1):
        x = jnp.broadcast_to(x_ref[...], (rows, c1 - c0))
        return _rwkv_mix_cols(x, sh_ref[:, c0:c1], mu_ref[:, c0:c1])

    r = mixed(r_ref, 0, rw)
    k = mixed(k_ref, rw, 2 * rw)
    v = mixed(v_ref, 2 * rw, 3 * rw)
    lo = mixed(l_ref, 3 * rw, 3 * rw + RWKV_LORA)
    lw, a, g = _rwkv_lora(lo, w0_ref[...], w2_ref[...], a0_ref[...], a2_ref[...], g2_ref[...])
    w = jnp.exp(lw)
    kk = k * kk_ref[...]
    k2 = k * (1.0 + (a - 1.0) * ka_ref[...])
    eye = lax.broadcasted_iota(jnp.int32, (n, n), 0) == lax.broadcasted_iota(jnp.int32, (n, n), 1)

    def col(rowvec):
        return jnp.sum(jnp.where(eye, jnp.broadcast_to(rowvec, (n, n)), 0.0), axis=1, keepdims=True)

    outs = []
    for h in range(heads):
        sl = slice(h * n, (h + 1) * n)
        kkh = kk[0:1, sl]
        kkh = kkh * lax.rsqrt(jnp.maximum(jnp.sum(kkh * kkh, axis=1, keepdims=True), 1e-12))
        ah, wh, k2h, rh, vh = a[0:1, sl], w[0:1, sl], k2[0:1, sl], r[0:1, sl], v[0:1, sl]
        s = s_ref[h]
        sa = -jnp.sum(s * kkh, axis=1, keepdims=True)
        s = s * wh + sa * (kkh * ah) + col(vh) * k2h
        so_ref[h] = s
        ycol = jnp.sum(s * rh, axis=1, keepdims=True)
        yrow = jnp.sum(jnp.where(eye, jnp.broadcast_to(ycol, (n, n)), 0.0), axis=0, keepdims=True)
        ym = jnp.mean(yrow, axis=1, keepdims=True)
        d = yrow - ym
        yv = jnp.mean(d * d, axis=1, keepdims=True)
        yn = d * lax.rsqrt(yv + GN_EPS) * lnw_ref[:, sl] + lnb_ref[:, sl]
        bonus = jnp.sum(rh * k2h * rk_ref[:, sl], axis=1, keepdims=True) * vh
        outs.append((yn + bonus) * g[0:1, sl])
    y_ref[...] = jnp.concatenate(outs, axis=1)


def _rwkv_step(p, shift, s0, mu, w0, w2, a0, a2, g2, k_k, k_a, r_k, lnw, lnb, dm):
    b = p.shape[0]
    rw, heads = dm.rw, dm.rh
    kern = functools.partial(_rwkv_step_kernel, rw=rw, heads=heads)
    cur = lambda c0, w: pl.BlockSpec((None, 1, w), lambda bi: (bi, 0, _blk(c0, w)))
    full = lambda shp: pl.BlockSpec(shp, lambda bi: (0,) * len(shp))
    sspec = pl.BlockSpec((None, heads, RWKV_HEAD, RWKV_HEAD), lambda bi: (bi, 0, 0, 0))
    return pl.pallas_call(
        kern,
        out_shape=(jax.ShapeDtypeStruct((b, 1, rw), F32),
                   jax.ShapeDtypeStruct((b, heads, RWKV_HEAD, RWKV_HEAD), F32)),
        grid=(b,),
        in_specs=[cur(dm.c_r, rw), cur(dm.c_r + rw, rw), cur(dm.c_r + 2 * rw, rw), cur(dm.c_lora, RWKV_LORA),
                  pl.BlockSpec((None, 1, dm.rcols), lambda bi: (bi, 0, 0)),
                  full((1, dm.rcols)), full((1, rw)), full((RWKV_W_RANK, rw)), full((1, rw)),
                  full((RWKV_A_RANK, rw)), full((RWKV_G_RANK, rw)), full((1, rw)), full((1, rw)), full((1, rw)),
                  full((1, rw)), full((1, rw)), sspec],
        out_specs=(pl.BlockSpec((None, 1, rw), lambda bi: (bi, 0, 0)), sspec),
        compiler_params=_cparams(("parallel",)),
        name="rwkv_step",
    )(p, p, p, p, shift, mu, w0, w2, a0, a2, g2, k_k, k_a, r_k, lnw, lnb, s0)


def _rope_blocks(x, cos2, sin2):
    outs = []
    for c in range(0, x.shape[1], NSA_HEAD):
        xb = x[:, c:c + NSA_HEAD]
        outs.append(xb * cos2 + pltpu.roll(xb, NSA_HEAD // 2, axis=1) * sin2)
    return outs[0] if len(outs) == 1 else jnp.concatenate(outs, axis=1)


def _nsa_prep_kernel(*refs, tt, nq, with_means):
    q_refs, kv_refs = refs[:nq], refs[nq:nq + 6]
    cos_ref, sin_ref, qo_ref, kv_ref, kw_ref = refs[nq + 6:nq + 11]
    cos2, sin2 = cos_ref[...], sin_ref[...]
    pw = NSA_KV_HEADS * NSA_HEAD
    for idx, ref in enumerate(q_refs):
        qo_ref[:, idx * pw:(idx + 1) * pw] = (_rope_blocks(ref[...], cos2, sin2) * (NSA_HEAD ** -0.5)).astype(BF16)
    for idx, ref in enumerate(kv_refs):
        x = ref[...]
        if idx % 2 == 0:
            x = _rope_blocks(x, cos2, sin2)
        if idx < 4:
            kv_ref[:, idx * pw:(idx + 1) * pw] = x
        else:
            kw_ref[:, (idx - 4) * pw:(idx - 3) * pw] = x
        if idx < 2 and with_means:
            m_ref = refs[nq + 11]
            m_ref[:, idx * pw:(idx + 1) * pw] = (
                jnp.sum(x.reshape(tt // NSA_BLOCK, NSA_BLOCK, pw), axis=1) * (1.0 / NSA_BLOCK))


def _nsa_prep(p, cos2, sin2, dm, tt, with_means):
    b, t, _ = p.shape
    pw = dm.part
    nq = dm.nw // pw
    kern = functools.partial(_nsa_prep_kernel, tt=tt, nq=nq, with_means=with_means)
    cur = lambda c0: pl.BlockSpec((None, tt, pw), lambda bi, i: (bi, i, _blk(c0, pw)))
    outs = [jax.ShapeDtypeStruct((b, t, dm.nw), BF16), jax.ShapeDtypeStruct((b, t, 4 * pw), F32),
            jax.ShapeDtypeStruct((b, t, 2 * pw), F32)]
    ospecs = [pl.BlockSpec((None, tt, dm.nw), lambda bi, i: (bi, i, 0)),
              pl.BlockSpec((None, tt, 4 * pw), lambda bi, i: (bi, i, 0)),
              pl.BlockSpec((None, tt, 2 * pw), lambda bi, i: (bi, i, 0))]
    if with_means:
        outs.append(jax.ShapeDtypeStruct((b, t // NSA_BLOCK, 2 * pw), F32))
        ospecs.append(pl.BlockSpec((None, tt // NSA_BLOCK, 2 * pw), lambda bi, i: (bi, i, 0)))
    in_specs = [cur(dm.c_q + k * pw) for k in range(nq)] + [cur(dm.c_kv + k * pw) for k in range(6)]
    in_specs += [pl.BlockSpec((tt, NSA_HEAD), lambda bi, i: (i, 0))] * 2
    return pl.pallas_call(
        kern,
        out_shape=tuple(outs),
        grid=(b, t // tt),
        in_specs=in_specs,
        out_specs=tuple(ospecs),
        compiler_params=_cparams(("parallel", "parallel")),
        name="nsa_prep",
    )(*([p] * (nq + 6)), cos2, sin2)


def _nsa_prompt_kernel(q_ref, kc_ref, vc_ref, ks_ref, vs_ref, kw_ref, vw_ref, gt_ref, o_ref,
                       m_ref, l_ref, acc_ref, *, g, nb, n_sel, tk, lw):
    h = pl.program_id(1)
    i = pl.program_id(2)
    qb = NSA_QTILE
    rows = g * qb
    q = jnp.concatenate([q_ref[:, gi * NSA_HEAD:(gi + 1) * NSA_HEAD] for gi in range(g)], axis=0)
    gsl = [slice(gi * qb, (gi + 1) * qb) for gi in range(g)]
    qpos_c = i * qb + lax.broadcasted_iota(jnp.int32, (qb, 1), 0)
    qpos_r = i * qb + lax.broadcasted_iota(jnp.int32, (1, rows), 1) % qb

    blk = lax.broadcasted_iota(jnp.int32, (nb, 1), 0)
    st = _dot_nt(kc_ref[...], q)
    ok = ((blk + 1) * NSA_BLOCK - 1) <= qpos_r
    mc = jnp.max(jnp.where(ok, st, NEG), axis=0, keepdims=True)
    ec = jnp.where(ok, jnp.exp(st - mc), 0.0)
    den = jnp.sum(ec, axis=0, keepdims=True)
    p_c = ec / jnp.where(den > 0.0, den, 1.0)
    o_cmp = lax.dot_general(p_c.astype(BF16), vc_ref[...].astype(BF16), _TN, preferred_element_type=F32)

    imp = p_c[:, gsl[0]]
    for gi in range(1, g):
        imp = imp + p_c[:, gsl[gi]]
    qp1 = qpos_r[:, 0:qb]
    imp = jnp.where(blk == qp1 // NSA_BLOCK, g + 1.0, jnp.where(blk * NSA_BLOCK <= qp1, imp, -1.0))
    cnt = jnp.zeros((nb, qb), F32)
    for bi in range(nb):
        ci = imp[bi:bi + 1, :]
        beats = (ci > imp) | ((ci == imp) & (blk > bi))
        cnt = cnt + jnp.where(beats, 1.0, 0.0)
    sel_t = jnp.where(cnt < n_sel, 1.0, 0.0).astype(BF16)

    m_ref[...] = jnp.full((rows, 1), M_INIT, F32)
    l_ref[...] = jnp.zeros((rows, 1), F32)
    acc_ref[...] = jnp.zeros((rows, NSA_HEAD), F32)
    bpt = tk // NSA_BLOCK
    brow = lax.broadcasted_iota(jnp.int32, (nb, tk), 0)
    bcol = lax.broadcasted_iota(jnp.int32, (nb, tk), 1) // NSA_BLOCK
    kcol = lax.broadcasted_iota(jnp.int32, (1, tk), 1)

    def scores(kt):
        k0 = pl.multiple_of(kt * tk, tk)
        kb = ks_ref[pl.ds(k0, tk), :].astype(BF16)
        sc = _dot_nt(q, kb)
        expand = (brow == bcol + kt * bpt).astype(BF16)
        picked = lax.dot_general(sel_t, expand, _TN, preferred_element_type=F32)
        return sc, jnp.where((picked > 0.5) & ((k0 + kcol) <= qpos_c), 0.0, NEG)

    def accumulate(kt, sc, bias):
        k0 = pl.multiple_of(kt * tk, tk)
        vb = vs_ref[pl.ds(k0, tk), :].astype(BF16)
        s_g = [sc[sl] + bias for sl in gsl]
        m_old = [m_ref[sl] for sl in gsl]
        m_new = [jnp.maximum(m_old[gi], jnp.max(s_g[gi], axis=1, keepdims=True)) for gi in range(g)]
        pr = [jnp.exp(s_g[gi] - m_new[gi]) for gi in range(g)]
        pv = [_dot(pr[gi], vb) for gi in range(g)]
        for gi, sl in enumerate(gsl):
            alpha = jnp.exp(m_old[gi] - m_new[gi])
            l_ref[sl] = alpha * l_ref[sl] + jnp.sum(pr[gi], axis=1, keepdims=True)
            acc_ref[sl] = alpha * acc_ref[sl] + pv[gi]
            m_ref[sl] = m_new[gi]

    def body(kt, carry):
        nxt = scores(kt + 1)
        accumulate(kt, *carry)
        return nxt

    n_kt = ((i + 1) * qb + tk - 1) // tk
    last = lax.fori_loop(0, n_kt - 1, body, scores(0))
    accumulate(n_kt - 1, *last)
    o_sel = acc_ref[...] / l_ref[...]

    w0 = pl.multiple_of(jnp.maximum(i * qb - NSA_WINDOW, 0), qb)
    kwb = kw_ref[pl.ds(w0, lw), :].astype(BF16)
    vwb = vw_ref[pl.ds(w0, lw), :].astype(BF16)
    dp = qpos_c - (w0 + lax.broadcasted_iota(jnp.int32, (1, lw), 1))
    bias_w = jnp.where((dp >= 0) & (dp <= NSA_WINDOW), 0.0, NEG)
    sw = _dot_nt(q, kwb)
    s_w = [sw[sl] + bias_w for sl in gsl]
    e_w = [jnp.exp(s_w[gi] - jnp.max(s_w[gi], axis=1, keepdims=True)) for gi in range(g)]
    o_win = jnp.concatenate([_dot(e_w[gi], vwb) / jnp.sum(e_w[gi], axis=1, keepdims=True) for gi in range(g)],
                            axis=0)

    gs = _sigmoid(gt_ref[...])
    for gi in range(g):
        acc = None
        for ci, ob in enumerate((o_cmp, o_sel, o_win)):
            c0 = gi * 3 + ci
            c1 = (g + gi) * 3 + ci
            gate = jnp.where(h == 0, gs[:, c0:c0 + 1], gs[:, c1:c1 + 1])
            term = ob[gi * qb:(gi + 1) * qb] * gate
            acc = term if acc is None else acc + term
        o_ref[:, gi * NSA_HEAD:(gi + 1) * NSA_HEAD] = acc


def _nsa_prompt(qr, kv4, kwv, kcvc, p, dm):
    b, t, _ = qr.shape
    assert NSA_KV_HEADS == 2
    g = dm.g
    nb = t // NSA_BLOCK
    n_sel = min(NSA_TOPK, nb)
    tk = min(512, t)
    lw = NSA_WINDOW + NSA_QTILE
    assert t % tk == 0 and t >= lw
    hd = NSA_HEAD
    kern = functools.partial(_nsa_prompt_kernel, g=g, nb=nb, n_sel=n_sel, tk=tk, lw=lw)
    kvspec = lambda c: pl.BlockSpec((None, t, hd), lambda bi, h, i, c=c: (bi, 0, 2 * c + h))
    rows = g * NSA_QTILE
    return pl.pallas_call(
        kern,
        out_shape=jax.ShapeDtypeStruct((b, t, dm.nw), F32),
        grid=(b, NSA_KV_HEADS, t // NSA_QTILE),
        in_specs=[pl.BlockSpec((None, NSA_QTILE, g * hd), lambda bi, h, i: (bi, i, h)),
                  pl.BlockSpec((None, nb, hd), lambda bi, h, i: (bi, 0, h)),
                  pl.BlockSpec((None, nb, hd), lambda bi, h, i: (bi, 0, 2 + h)),
                  kvspec(2), kvspec(3), kvspec(0), kvspec(1),
                  pl.BlockSpec((None, NSA_QTILE, LANES), lambda bi, h, i: (bi, i, _blk(dm.c_ng, LANES)))],
        out_specs=pl.BlockSpec((None, NSA_QTILE, g * hd), lambda bi, h, i: (bi, i, h)),
        scratch_shapes=[pltpu.VMEM((rows, 1), F32), pltpu.VMEM((rows, 1), F32), pltpu.VMEM((rows, hd), F32)],
        compiler_params=_cparams(("parallel", "parallel", "arbitrary")),
        name="nsa_prompt",
    )(qr, kcvc, kcvc, kv4, kv4, kwv, kwv, p)


def _page_means_kernel(pt_ref, *refs, ppb):
    o_ref = refs[ppb]
    for k in range(ppb):
        x = refs[k][...]
        nbp = x.shape[0] // NSA_BLOCK
        o_ref[k] = jnp.sum(x.reshape(nbp, NSA_BLOCK, x.shape[1], x.shape[2]), axis=1) * (1.0 / NSA_BLOCK)


def _page_means(cache, layer, page_table):
    _, _, page, _, prow, hd = cache.shape
    b, n_pages = page_table.shape
    nbp = page // NSA_BLOCK
    ppb = math.gcd(n_pages, 16)
    kern = functools.partial(_page_means_kernel, ppb=ppb)
    pspec = lambda k: pl.BlockSpec((None, None, page, None, prow, hd),
                                   lambda bi, j, pt, k=k: (layer, pt[bi, j * ppb + k], 0, 0, 0, 0))
    out = pl.pallas_call(
        kern,
        out_shape=jax.ShapeDtypeStruct((b, n_pages, nbp, prow, hd), F32),
        grid_spec=pltpu.PrefetchScalarGridSpec(
            num_scalar_prefetch=1,
            grid=(b, n_pages // ppb),
            in_specs=[pspec(k) for k in range(ppb)],
            out_specs=pl.BlockSpec((None, ppb, nbp, prow, hd), lambda bi, j, pt: (bi, j, 0, 0, 0)),
        ),
        compiler_params=_cparams(("parallel", "parallel")),
        name="nsa_page_means",
    )(page_table, *([cache] * ppb))
    return out.reshape(b, n_pages * nbp, prow * hd)


def _nsa_choose_kernel(q_ref, m_ref, oc_ref, idx_ref, *, g, nbp, n_sel):
    hd = NSA_HEAD
    blk = lax.broadcasted_iota(jnp.int32, (1, nbp), 1)
    lane = lax.broadcasted_iota(jnp.int32, (1, LANES), 1)
    for h in range(NSA_KV_HEADS):
        q = jnp.concatenate([q_ref[:, (h * g + gi) * hd:(h * g + gi + 1) * hd] for gi in range(g)], axis=0)
        q = jnp.concatenate([q, jnp.zeros((SUBLANES - g, hd), q.dtype)], axis=0) if g < SUBLANES else q
        kc = m_ref[:, h * hd:(h + 1) * hd]
        vc = m_ref[:, (NSA_KV_HEADS + h) * hd:(NSA_KV_HEADS + h + 1) * hd]
        s = _dot_nt(q, kc)
        m = jnp.max(s, axis=1, keepdims=True)
        e = jnp.exp(s - m)
        p_c = e / jnp.sum(e, axis=1, keepdims=True)
        oc_ref[h] = _dot(p_c, vc)[0:g]
        imp = jnp.sum(p_c[0:g], axis=0, keepdims=True)
        idx = jnp.where(lane == 0, nbp, 0)
        for it in range(1, n_sel):
            best = jnp.max(imp, axis=1, keepdims=True)
            j = jnp.min(jnp.where(imp == best, blk, nbp), axis=1, keepdims=True)
            idx = jnp.where(lane == it, j, idx)
            imp = jnp.where(blk == j, -2.0, imp)
        idx_ref[h] = idx


def _nsa_choose(qr, means, dm, n_sel):
    b = qr.shape[0]
    nbp = means.shape[1]
    g = dm.g
    kern = functools.partial(_nsa_choose_kernel, g=g, nbp=nbp, n_sel=n_sel)
    return pl.pallas_call(
        kern,
        out_shape=(jax.ShapeDtypeStruct((b, NSA_KV_HEADS, g, NSA_HEAD), F32),
                   jax.ShapeDtypeStruct((b, NSA_KV_HEADS, 1, LANES), jnp.int32)),
        grid=(b,),
        in_specs=[pl.BlockSpec((None, 1, dm.nw), lambda bi: (bi, 0, 0)),
                  pl.BlockSpec((None, nbp, means.shape[2]), lambda bi: (bi, 0, 0))],
        out_specs=(pl.BlockSpec((None, NSA_KV_HEADS, g, NSA_HEAD), lambda bi: (bi, 0, 0, 0)),
                   pl.BlockSpec((None, NSA_KV_HEADS, 1, LANES), lambda bi: (bi, 0, 0, 0))),
        compiler_params=_cparams(("parallel",)),
        name="nsa_choose",
    )(qr, means)


def _nsa_sample_kernel(idx_ref, pt_ref, q_ref, cb_ref, kn_ref, vn_ref, wb_ref, kwn_ref, vwn_ref,
                       oc_ref, gt_ref, o_ref, m_ref, l_ref, acc_ref, *, g, n_sel):
    h = pl.program_id(1)
    j = pl.program_id(2)
    hd = NSA_HEAD
    rows = SUBLANES
    q = jnp.concatenate([q_ref[:, gi * hd:(gi + 1) * hd] for gi in range(g)], axis=0)
    if g < rows:
        q = jnp.concatenate([q, jnp.zeros((rows - g, hd), q.dtype)], axis=0)

    @pl.when(j == 0)
    def _():
        s0 = jnp.sum(q.astype(F32) * kn_ref[...].astype(BF16).astype(F32), axis=1, keepdims=True)
        m_ref[...] = s0
        l_ref[...] = jnp.ones((rows, 1), F32)
        acc_ref[...] = jnp.broadcast_to(vn_ref[...].astype(BF16).astype(F32), (rows, hd))

    kvh = NSA_KV_HEADS

    @pl.when(j > 0)
    def _():
        xb = cb_ref[...].astype(BF16)
        srow = lax.broadcasted_iota(jnp.int32, (1, xb.shape[0]), 1) % (4 * kvh)
        sc = _dot_nt(q, xb) + jnp.where(srow == 2 * kvh + h, 0.0, NEG)
        m_old = m_ref[...]
        m_new = jnp.maximum(m_old, jnp.max(sc, axis=1, keepdims=True))
        pr = jnp.exp(sc - m_new)
        alpha = jnp.exp(m_old - m_new)
        l_ref[...] = alpha * l_ref[...] + jnp.sum(pr, axis=1, keepdims=True)
        acc_ref[...] = alpha * acc_ref[...] + _dot(pltpu.roll(pr, kvh, axis=1), xb)
        m_ref[...] = m_new

    @pl.when(j == n_sel - 1)
    def _():
        o_sel = acc_ref[...] / l_ref[...]
        xw = wb_ref[...].astype(BF16)
        wrow = lax.broadcasted_iota(jnp.int32, (1, xw.shape[0]), 1) % (2 * kvh)
        sw = _dot_nt(q, xw) + jnp.where(wrow == h, 0.0, NEG)
        sn = jnp.sum(q.astype(F32) * kwn_ref[...].astype(BF16).astype(F32), axis=1, keepdims=True)
        mw = jnp.maximum(jnp.max(sw, axis=1, keepdims=True), sn)
        ew = jnp.exp(sw - mw)
        en = jnp.exp(sn - mw)
        den = jnp.sum(ew, axis=1, keepdims=True) + en
        pn = (en / den).astype(BF16).astype(F32)
        o_win = _dot(pltpu.roll(ew / den, kvh, axis=1), xw) + pn * vwn_ref[...].astype(BF16).astype(F32)
        gs = _sigmoid(gt_ref[...])
        o_cmp = oc_ref[...]
        for gi in range(g):
            acc = None
            for ci, ob in enumerate((o_cmp, o_sel, o_win)):
                c0 = gi * 3 + ci
                c1 = (g + gi) * 3 + ci
                gate = jnp.where(h == 0, gs[:, c0:c0 + 1], gs[:, c1:c1 + 1])
                term = ob[gi:gi + 1] * gate
                acc = term if acc is None else acc + term
            o_ref[:, gi * hd:(gi + 1) * hd] = acc


def _nsa_sample(qr, kv4, kwv, cache_rows, win_rows, layer, n_pool, page, lwin, page_table, idx, o_cmp, p, dm, n_sel):
    b = qr.shape[0]
    g, hd = dm.g, NSA_HEAD
    bpp = page // NSA_BLOCK
    brows = NSA_BLOCK * 4 * NSA_KV_HEADS
    wrows = lwin * 2 * NSA_KV_HEADS
    kern = functools.partial(_nsa_sample_kernel, g=g, n_sel=n_sel)

    def sel_map(bi, h, j, idx_ref, pt_ref):
        blk = idx_ref[(bi * NSA_KV_HEADS + h) * LANES + j]
        n_past = pt_ref.shape[1] * bpp
        blk = jnp.minimum(blk, n_past - 1)
        return ((layer * n_pool + pt_ref[bi, blk // bpp]) * bpp + blk % bpp, 0)

    newspec = lambda c: pl.BlockSpec((None, 1, hd), lambda bi, h, j, ir, pr, c=c: (bi, 0, 2 * c + h))
    return pl.pallas_call(
        kern,
        out_shape=jax.ShapeDtypeStruct((b, 1, dm.nw), F32),
        grid_spec=pltpu.PrefetchScalarGridSpec(
            num_scalar_prefetch=2,
            grid=(b, NSA_KV_HEADS, n_sel),
            in_specs=[pl.BlockSpec((None, 1, g * hd), lambda bi, h, j, ir, pr: (bi, 0, h)),
                      pl.BlockSpec((brows, hd), sel_map),
                      newspec(2), newspec(3),
                      pl.BlockSpec((wrows, hd), lambda bi, h, j, ir, pr: (layer * b + bi, 0)),
                      newspec(0), newspec(1),
                      pl.BlockSpec((None, None, g, hd), lambda bi, h, j, ir, pr: (bi, h, 0, 0)),
                      pl.BlockSpec((None, 1, LANES), lambda bi, h, j, ir, pr: (bi, 0, _blk(dm.c_ng, LANES)))],
            out_specs=pl.BlockSpec((None, 1, g * hd), lambda bi, h, j, ir, pr: (bi, 0, h)),
            scratch_shapes=[pltpu.VMEM((SUBLANES, 1), F32), pltpu.VMEM((SUBLANES, 1), F32),
                            pltpu.VMEM((SUBLANES, hd), F32)],
        ),
        compiler_params=_cparams(("parallel", "parallel", "arbitrary")),
        name="nsa_sample",
    )(idx, page_table, qr, cache_rows, kv4, kv4, win_rows, kwv, kwv, o_cmp, p)


def _rope_tables(pos):
    half = NSA_HEAD // 2
    inv = jnp.exp(-math.log(ROPE_THETA) * jnp.arange(half, dtype=F32) / half)
    ang = pos.astype(F32)[:, None] * inv[None, :]
    cos, sin = jnp.cos(ang), jnp.sin(ang)
    return jnp.concatenate([cos, cos], axis=1), jnp.concatenate([-sin, sin], axis=1)


def _shift_cols(p_last, dm):
    return jnp.concatenate([p_last[..., dm.c_r:dm.c_r + 3 * dm.rw], p_last[..., dm.c_lora:dm.c_lora + RWKV_LORA]],
                           axis=-1)


def _row(x):
    return x.reshape(1, -1)


def kernel(x_prompt, x_sample, cache_nsa_kv, page_table, state_nsa_window, state_wkv, state_rwkv_shift,
           state_pool, state_ffn_conv, norm_mix, w_in, pool_w, pool_scale, rwkv_mu, rwkv_w0, rwkv_w2,
           rwkv_a0, rwkv_a2, rwkv_g2, rwkv_k_k, rwkv_k_a, rwkv_r_k, rwkv_ln_w, rwkv_ln_b, w_branch, w_out,
           norm_ffn, ffn_up, ffn_conv, ffn_conv_b, ffn_down, norm_final):
    bp, sp, d = x_prompt.shape
    bs, ts, _ = x_sample.shape
    assert ts == 1
    depth = w_in.shape[0]
    dm = Dims(d)
    ff = dm.ff
    hd = NSA_HEAD
    page = cache_nsa_kv.shape[2]
    n_pages = page_table.shape[1]
    past = n_pages * page
    kvw = 4 * NSA_KV_HEADS * hd

    w_mix_b = w_in[:, :, :dm.np].astype(BF16)
    w_gate_b = w_in[:, :, dm.o3:].astype(BF16)
    pool_w_b = pool_w.astype(BF16)
    wb_b = w_branch.astype(BF16)
    wo_b = w_out.astype(BF16)
    up_b = ffn_up.astype(BF16)
    down_b = ffn_down.astype(BF16)
    g2_b = rwkv_g2.astype(BF16)
    kk_r = rwkv_k_k.reshape(depth, 1, dm.rw)
    ka_r = rwkv_k_a.reshape(depth, 1, dm.rw)
    rk_r = rwkv_r_k.reshape(depth, 1, dm.rw)

    mp = bp * sp
    tm_p = min(512, sp)
    tm_in = min(1024, sp)
    wide = lambda n: 1024 if n % 1024 == 0 else 512
    tn_ff = 512 if ff % 512 == 0 else 256
    tt_pool = min(512, sp)
    tt_rwkv = min(256, sp)
    tt_nsa = min(512, sp)
    tm_ff = min(1024, sp)
    tps = sp // tm_ff

    def rwkv_params(l):
        return (_row(rwkv_mu[l]), _row(rwkv_w0[l]), rwkv_w2[l], _row(rwkv_a0[l]), rwkv_a2[l], g2_b[l],
                kk_r[l], ka_r[l], rk_r[l])

    cos_p, sin_p = _rope_tables(jnp.arange(sp, dtype=jnp.int32))
    x = x_prompt.reshape(mp, d)
    zeros_hist16 = jnp.zeros((bp, POOL_HIST + 1, dm.pw), F32)
    zeros_shift = jnp.zeros((bp, 1, dm.rcols), F32)
    zeros_wkv = jnp.zeros((bp, dm.rh, RWKV_HEAD, RWKV_HEAD), F32)
    zeros_conv = jnp.zeros((bp, CONV_W - 1, 2 * ff), F32)
    wl = min(NSA_WINDOW, sp)
    p_kv, p_win, p_wkv, p_shift, p_pool, p_conv = [], [], [], [], [], []
    for l in range(depth):
        p = _norm_matmul(x, _row(norm_mix[l]), w_mix_b, l,tm_in, wide(dm.np)).reshape(bp, sp, dm.np)
        pg = _norm_matmul(x, _row(norm_mix[l]), w_gate_b, l,tm_in, wide(N_BRANCH * d))
        ya = _pool_prompt(p, zeros_hist16, pool_w_b[l], _row(pool_scale[l]), dm, tt_pool)
        prep = _rwkv_prep(p, zeros_shift, *rwkv_params(l), dm, tt_rwkv)
        yb, s_t = _rwkv_scan(prep, _row(rwkv_ln_w[l]), _row(rwkv_ln_b[l]), zeros_wkv, dm)
        qr, kv4, kwv, kcvc = _nsa_prep(p, cos_p, sin_p, dm, tt_nsa, True)
        yc = _nsa_prompt(qr, kv4, kwv, kcvc, p, dm)
        mrg = _branch_merge(ya.reshape(mp, -1), yb.reshape(mp, -1), yc.reshape(mp, -1), pg, wb_b, l, dm,
                            tm_p, min(1024, d))
        x = _out_proj(x, mrg, wo_b, l,tm_p, min(1024, d))
        x, za, zv = _ffn(x, _row(norm_ffn[l]), up_b, ffn_conv[l], _row(ffn_conv_b[l]), down_b, l,
                         zeros_conv, ff, tm_ff, tn_ff, tps, False)
        p_kv.append(kv4.reshape(bp, sp, 4, NSA_KV_HEADS, hd))
        p_win.append(kwv[:, sp - wl:].reshape(bp, wl, 2, NSA_KV_HEADS, hd))
        p_wkv.append(jnp.swapaxes(s_t, -1, -2))
        p_shift.append(_shift_cols(p[:, sp - 1:, :], dm))
        p_pool.append(p[:, sp - POOL_HIST:, dm.c_pool:dm.c_pool + dm.pw])
        zl = jnp.concatenate([za, zv], axis=-1).reshape(bp, tps, 2, 2 * ff)
        p_conv.append(zl[:, tps - 1])
    y_prompt = _final_norm(x, _row(norm_final), tm_p).reshape(bp, sp, d)

    pos_s = past
    cos_s, sin_s = _rope_tables(jnp.full((1,), pos_s, dtype=jnp.int32))
    xs = x_sample.reshape(bs, d)
    nbp = past // NSA_BLOCK
    n_sel = min(NSA_TOPK, nbp + 1)
    n_pool = cache_nsa_kv.shape[1]
    lwin = state_nsa_window.shape[2]
    cache_pairs = cache_nsa_kv.reshape(depth, n_pool, page, 2, 2 * NSA_KV_HEADS, hd)
    cache_rows = cache_nsa_kv.reshape(-1, hd)
    win_rows = state_nsa_window.reshape(-1, hd)
    s_kv, s_win, s_wkv, s_shift, s_pool, s_conv = [], [], [], [], [], []
    for l in range(depth):
        p = _norm_matmul(xs, _row(norm_mix[l]), w_mix_b, l,bs, wide(dm.np)).reshape(bs, 1, dm.np)
        pg = _norm_matmul(xs, _row(norm_mix[l]), w_gate_b, l,bs, wide(N_BRANCH * d))
        e16 = jnp.concatenate([state_pool[l], p[:, :, dm.c_pool:dm.c_pool + dm.pw]], axis=1)
        ya = _pool_sample(e16, pool_w_b[l], _row(pool_scale[l]), dm, pos_s)
        yb, s_new = _rwkv_step(p, state_rwkv_shift[l], state_wkv[l], *rwkv_params(l),
                               _row(rwkv_ln_w[l]), _row(rwkv_ln_b[l]), dm)
        qr, kv4, kwv = _nsa_prep(p, cos_s, sin_s, dm, 1, False)
        means = _page_means(cache_pairs, l, page_table)
        o_cmp, idx = _nsa_choose(qr, means, dm, n_sel)
        yc = _nsa_sample(qr, kv4, kwv, cache_rows, win_rows, l, n_pool, page, lwin, page_table, idx.reshape(-1),
                         o_cmp, p, dm, n_sel)
        mrg = _branch_merge(ya, yb.reshape(bs, -1), yc.reshape(bs, -1), pg, wb_b, l, dm, bs, min(1024, d))
        xs = _out_proj(xs, mrg, wo_b, l,bs, min(1024, d))
        xs, za, zv = _ffn(xs, _row(norm_ffn[l]), up_b, ffn_conv[l], _row(ffn_conv_b[l]), down_b, l,
                          state_ffn_conv[l], ff, bs, tn_ff, 1, True)
        s_kv.append(kv4.reshape(bs, 1, 4, NSA_KV_HEADS, hd))
        wk = jnp.concatenate([state_nsa_window[l], kwv.reshape(bs, 1, 2, NSA_KV_HEADS, hd)], axis=1)
        s_win.append(wk[:, -NSA_WINDOW:])
        s_wkv.append(s_new)
        s_shift.append(_shift_cols(p, dm))
        s_pool.append(e16[:, 1:])
        z_new = jnp.concatenate([za, zv], axis=-1)[:, None, :]
        s_conv.append(jnp.concatenate([state_ffn_conv[l][:, 1:], z_new], axis=1))
    y_sample = _final_norm(xs, _row(norm_final), bs).reshape(bs, 1, d)

    st = lambda xs_: jnp.stack(xs_)
    return (y_prompt, y_sample, st(p_kv), st(p_win), st(p_wkv), st(p_shift), st(p_pool), st(p_conv),
            st(s_kv), st(s_win), st(s_wkv), st(s_shift), st(s_pool), st(s_conv))
```

```python
import functools
import math

import jax
import jax.numpy as jnp
from jax import lax
from jax.experimental import pallas as pl
from jax.experimental.pallas import tpu as pltpu

F32 = jnp.float32
BF16 = jnp.bfloat16
HIGHEST = lax.Precision.HIGHEST

POOL_GROUPS = 4
POOL_WINDOWS = (2, 4, 8, 16)
POOL_HIST = 15
RWKV_HEAD = 64
RWKV_W_RANK = 64
RWKV_A_RANK = 64
RWKV_G_RANK = 128
RWKV_LORA = RWKV_W_RANK + RWKV_A_RANK + RWKV_G_RANK
RWKV_DECAY_SCALE = 0.6065306597126334
GN_EPS = 64e-5
NSA_HEAD = 128
NSA_KV_HEADS = 2
NSA_BLOCK = 64
NSA_TOPK = 16
NSA_WINDOW = 512
NSA_QTILE = 128
N_BRANCH = 3
CONV_W = 3
ROPE_THETA = 10000.0
RMS_EPS = 1e-6
NEG = -1e30
M_INIT = -1e29

LANES = 128
SUBLANES = 8
VMEM_LIMIT = 56 * 1024 * 1024
RWKV_CHUNK = 64
FFN_SUB_ROWS = 128


class Dims:
    def __init__(self, d_model):
        d = d_model
        self.d = d
        self.pw = d // 2
        self.cg = self.pw // POOL_GROUPS
        self.rw = d // 2
        self.rh = self.rw // RWKV_HEAD
        self.rcols = 3 * self.rw + RWKV_LORA
        self.nw = d // 2
        self.nh = self.nw // NSA_HEAD
        self.g = self.nh // NSA_KV_HEADS
        self.kvc = 6 * NSA_KV_HEADS * NSA_HEAD
        self.ngate = 3 * self.nh
        self.ff = 256 * ((8 * d // 3 + 255) // 256)
        self.o1 = self.pw
        self.o2 = self.o1 + self.rcols
        self.o3 = self.o2 + self.nw + self.kvc + self.ngate
        self.c_pool = 0
        self.c_r = self.o1
        self.c_lora = self.o1 + 3 * self.rw
        self.c_q = self.o2
        self.c_kv = self.o2 + self.nw
        self.c_ng = self.c_kv + self.kvc
        self.tn_in = 512
        self.np = -(-self.o3 // self.tn_in) * self.tn_in
        self.part = NSA_KV_HEADS * NSA_HEAD


def _blk(offset, width):
    assert offset % width == 0, (offset, width)
    return offset // width


def _cparams(sem):
    return pltpu.CompilerParams(dimension_semantics=sem, vmem_limit_bytes=VMEM_LIMIT)


def _dot(a, b):
    return jnp.dot(a.astype(BF16), b.astype(BF16), preferred_element_type=F32)


def _dot_nt(a, b):
    return lax.dot_general(a.astype(BF16), b.astype(BF16), (((1,), (1,)), ((), ())),
                           preferred_element_type=F32)


def _dot_hi(a, b):
    return jnp.dot(a, b, precision=HIGHEST, preferred_element_type=F32)


_NN = (((1,), (0,)), ((), ()))
_NT = (((1,), (1,)), ((), ()))
_TN = (((0,), (0,)), ((), ()))


def _split2(x):
    hi = x.astype(BF16)
    return hi, (x - hi.astype(F32)).astype(BF16)


def _mm3(a, b, dims):
    d = lambda p, q: lax.dot_general(p, q, dims, preferred_element_type=F32)
    return d(a[0], b[0]) + d(a[0], b[1]) + d(a[1], b[0])


def _split3(x):
    hi = x.astype(BF16)
    r1 = x - hi.astype(F32)
    mid = r1.astype(BF16)
    lo = (r1 - mid.astype(F32)).astype(BF16)
    return hi, mid, lo


def _dot_exact_lhs(m_bf16, x):
    hi, mid, lo = _split3(x)
    d = lambda p: jnp.dot(m_bf16, p, preferred_element_type=F32)
    return d(hi) + d(mid) + d(lo)


def _dot_exact_rhs(x, m_bf16):
    hi, mid, lo = _split3(x)
    d = lambda p: jnp.dot(p, m_bf16, preferred_element_type=F32)
    return d(hi) + d(mid) + d(lo)


def _head_ones():
    i = lax.broadcasted_iota(jnp.int32, (LANES, LANES), 0) // RWKV_HEAD
    j = lax.broadcasted_iota(jnp.int32, (LANES, LANES), 1) // RWKV_HEAD
    return (i == j).astype(BF16)


def _head_sum(x, ones):
    parts = [_dot_exact_rhs(x[:, c:c + LANES], ones) for c in range(0, x.shape[1], LANES)]
    return parts[0] if len(parts) == 1 else jnp.concatenate(parts, axis=1)


def _rmsnorm_val(x, g):
    ms = jnp.mean(x * x, axis=-1, keepdims=True)
    return x * lax.rsqrt(ms + RMS_EPS) * g


def _sigmoid(x):
    return 1.0 / (1.0 + jnp.exp(-x))


def _norm_matmul_kernel(x_ref, g_ref, w_ref, o_ref, h_ref):
    @pl.when(pl.program_id(1) == 0)
    def _():
        h_ref[...] = _rmsnorm_val(x_ref[...], g_ref[...]).astype(BF16)

    o_ref[...] = jnp.dot(h_ref[...], w_ref[...], preferred_element_type=F32)


def _norm_matmul(x, g, w, layer, n, tm, tn):
    m, d = x.shape
    assert n % tn == 0 and n <= w.shape[2]
    return pl.pallas_call(
        _norm_matmul_kernel,
        out_shape=jax.ShapeDtypeStruct((m, n), F32),
        grid=(m // tm, n // tn),
        in_specs=[pl.BlockSpec((tm, d), lambda i, j: (i, 0)),
                  pl.BlockSpec((1, d), lambda i, j: (0, 0)),
                  pl.BlockSpec((None, d, tn), lambda i, j: (layer, 0, j))],
        out_specs=pl.BlockSpec((tm, tn), lambda i, j: (i, j)),
        scratch_shapes=[pltpu.VMEM((tm, d), BF16)],
        compiler_params=_cparams(("parallel", "arbitrary")),
        name="norm_matmul",
    )(x, g, w)


def _branch_merge_kernel(ya_ref, yb_ref, yc_ref, ga_ref, gb_ref, gc_ref, w_ref, o_ref):
    acc = _dot(ya_ref[...], w_ref[0]) * _sigmoid(ga_ref[...])
    acc = acc + _dot(yb_ref[...], w_ref[1]) * _sigmoid(gb_ref[...])
    acc = acc + _dot(yc_ref[...], w_ref[2]) * _sigmoid(gc_ref[...])
    o_ref[...] = acc.astype(BF16)


def _branch_merge(ya, yb, yc, p, wb, layer, dm, tm, tn):
    m = ya.shape[0]
    d, hw = dm.d, dm.d // 2
    nb = d // tn
    yspec = pl.BlockSpec((tm, hw), lambda j, i: (i, 0))
    gspec = lambda k: pl.BlockSpec((tm, tn), lambda j, i, k=k: (i, k * nb + j))
    return pl.pallas_call(
        _branch_merge_kernel,
        out_shape=jax.ShapeDtypeStruct((m, d), BF16),
        grid=(nb, m // tm),
        in_specs=[yspec, yspec, yspec, gspec(0), gspec(1), gspec(2),
                  pl.BlockSpec((None, N_BRANCH, hw, tn), lambda j, i: (layer, 0, 0, j))],
        out_specs=pl.BlockSpec((tm, tn), lambda j, i: (i, j)),
        compiler_params=_cparams(("parallel", "parallel")),
        name="branch_merge",
    )(ya, yb, yc, p, p, p, wb)


def _out_proj_kernel(x_ref, m_ref, w_ref, o_ref):
    o_ref[...] = x_ref[...] + jnp.dot(m_ref[...], w_ref[...], preferred_element_type=F32)


def _out_proj(x, mrg, wo, layer, tm, tn):
    m, d = x.shape
    return pl.pallas_call(
        _out_proj_kernel,
        out_shape=jax.ShapeDtypeStruct((m, d), F32),
        grid=(m // tm, d // tn),
        in_specs=[pl.BlockSpec((tm, tn), lambda i, j: (i, j)),
                  pl.BlockSpec((tm, d), lambda i, j: (i, 0)),
                  pl.BlockSpec((None, d, tn), lambda i, j: (layer, 0, j))],
        out_specs=pl.BlockSpec((tm, tn), lambda i, j: (i, j)),
        compiler_params=_cparams(("parallel", "parallel")),
        name="out_proj",
    )(x, mrg, wo)


def _ffn_kernel(x_ref, g_ref, wa_ref, wv_ref, cwa_ref, cwv_ref, cba_ref, cbv_ref, wd_ref,
                ha_ref, hv_ref, o_ref, za_ref, zv_ref, h_ref, ca_ref, cv_ref, *, tm, sub, tps, rows_are_seqs):
    i = pl.program_id(0)
    j = pl.program_id(1)

    @pl.when(j == 0)
    def _():
        x = x_ref[...]
        h_ref[...] = _rmsnorm_val(x, g_ref[...]).astype(BF16)
        o_ref[...] = x

    def conv_mix(z, zs1, zs2, cw_ref, cb_ref):
        return zs2 * cw_ref[0:1] + zs1 * cw_ref[1:2] + z * cw_ref[2:3] + cb_ref[...]

    if rows_are_seqs:
        h = h_ref[...]
        za = jnp.dot(h, wa_ref[...], preferred_element_type=F32)
        zv = jnp.dot(h, wv_ref[...], preferred_element_type=F32)
        za_ref[...] = za
        zv_ref[...] = zv
        ca = conv_mix(za, ha_ref[:, 1, :], ha_ref[:, 0, :], cwa_ref, cba_ref)
        cv = conv_mix(zv, hv_ref[:, 1, :], hv_ref[:, 0, :], cwv_ref, cbv_ref)
        o_ref[...] += _dot(ca * _sigmoid(ca) * cv, wd_ref[...])
        return

    @pl.when((i % tps) == 0)
    def _():
        ca_ref[j] = ha_ref[...]
        cv_ref[j] = hv_ref[...]

    rs = [slice(s * sub, (s + 1) * sub) for s in range(tm // sub)]
    zas = [jnp.dot(h_ref[r], wa_ref[...], preferred_element_type=F32) for r in rs]
    zvs = [jnp.dot(h_ref[r], wv_ref[...], preferred_element_type=F32) for r in rs]
    row = lax.broadcasted_iota(jnp.int32, (sub, zas[0].shape[1]), 0)

    def conv(zs, s, carry_ref, cw_ref, cb_ref):
        z = zs[s]
        prev = carry_ref[j] if s == 0 else zs[s - 1][sub - 2:sub]
        zs1 = jnp.where(row == 0, prev[1:2], pltpu.roll(z, 1, axis=0))
        zs2 = jnp.where(row == 0, prev[0:1], jnp.where(row == 1, prev[1:2], pltpu.roll(z, 2, axis=0)))
        return conv_mix(z, zs1, zs2, cw_ref, cb_ref)

    for s, r in enumerate(rs):
        ca = conv(zas, s, ca_ref, cwa_ref, cba_ref)
        cv = conv(zvs, s, cv_ref, cwv_ref, cbv_ref)
        o_ref[r] += _dot(ca * _sigmoid(ca) * cv, wd_ref[...])
    za_last = zas[-1][sub - 2:sub]
    zv_last = zvs[-1][sub - 2:sub]
    ca_ref[j] = za_last
    cv_ref[j] = zv_last
    za_ref[...] = za_last
    zv_ref[...] = zv_last


def _ffn(x, g, w_up, cw, cb, w_down, layer, hist, ff, tm, tn, tps, rows_are_seqs):
    m, d = x.shape
    nj = ff // tn
    nm = m // tm
    if rows_are_seqs:
        assert nm == 1
        hspec_a = pl.BlockSpec((m, 2, tn), lambda i, j: (0, 0, j))
        hspec_v = pl.BlockSpec((m, 2, tn), lambda i, j: (0, 0, nj + j))
        zshape = jax.ShapeDtypeStruct((m, ff), F32)
        zspec = pl.BlockSpec((m, tn), lambda i, j: (0, j))
    else:
        hspec_a = pl.BlockSpec((None, 2, tn), lambda i, j: (i // tps, 0, j))
        hspec_v = pl.BlockSpec((None, 2, tn), lambda i, j: (i // tps, 0, nj + j))
        zshape = jax.ShapeDtypeStruct((nm, 2, ff), F32)
        zspec = pl.BlockSpec((None, 2, tn), lambda i, j: (i, 0, j))
    sub = FFN_SUB_ROWS if tm % FFN_SUB_ROWS == 0 else tm
    kern = functools.partial(_ffn_kernel, tm=tm, sub=sub, tps=tps, rows_are_seqs=rows_are_seqs)
    return pl.pallas_call(
        kern,
        out_shape=(jax.ShapeDtypeStruct((m, d), F32), zshape, zshape),
        grid=(nm, nj),
        in_specs=[pl.BlockSpec((tm, d), lambda i, j: (i, 0)),
                  pl.BlockSpec((1, d), lambda i, j: (0, 0)),
                  pl.BlockSpec((None, d, tn), lambda i, j: (layer, 0, j)),
                  pl.BlockSpec((None, d, tn), lambda i, j: (layer, 0, nj + j)),
                  pl.BlockSpec((CONV_W, tn), lambda i, j: (0, j)),
                  pl.BlockSpec((CONV_W, tn), lambda i, j: (0, nj + j)),
                  pl.BlockSpec((1, tn), lambda i, j: (0, j)),
                  pl.BlockSpec((1, tn), lambda i, j: (0, nj + j)),
                  pl.BlockSpec((None, tn, d), lambda i, j: (layer, j, 0)),
                  hspec_a, hspec_v],
        out_specs=(pl.BlockSpec((tm, d), lambda i, j: (i, 0)), zspec, zspec),
        scratch_shapes=[pltpu.VMEM((tm, d), BF16),
                        pltpu.VMEM((nj, 2, tn), F32),
                        pltpu.VMEM((nj, 2, tn), F32)],
        compiler_params=_cparams(("arbitrary", "arbitrary")),
        name="conv_ffn",
    )(x, g, w_up, w_up, cw, cw, cb, cb, w_down, hist, hist)


def _final_norm_kernel(x_ref, g_ref, o_ref):
    o_ref[...] = _rmsnorm_val(x_ref[...], g_ref[...])


def _final_norm(x, g, tm):
    m, d = x.shape
    return pl.pallas_call(
        _final_norm_kernel,
        out_shape=jax.ShapeDtypeStruct((m, d), F32),
        grid=(m // tm,),
        in_specs=[pl.BlockSpec((tm, d), lambda i: (i, 0)), pl.BlockSpec((1, d), lambda i: (0, 0))],
        out_specs=pl.BlockSpec((tm, d), lambda i: (i, 0)),
        compiler_params=_cparams(("parallel",)),
        name="final_norm",
    )(x, g)


def _pool_kernel(u_ref, hist_ref, w_ref, sc_ref, o_ref, e_ref, *, tt, cg):
    i = pl.program_id(1)
    hrows = POOL_HIST + 1

    @pl.when(i == 0)
    def _():
        e_ref[0:hrows] = hist_ref[...]

    @pl.when(i > 0)
    def _():
        e_ref[0:hrows] = e_ref[tt:tt + hrows]

    u = u_ref[...]
    e_ref[hrows:hrows + tt] = u
    pos = i * tt + lax.broadcasted_iota(jnp.int32, (tt, 1), 0)
    for gi, w in enumerate(POOL_WINDOWS):
        cs = slice(gi * cg, (gi + 1) * cg)
        s = e_ref[:, cs]
        sh = 1
        while sh < w:
            s = s + pltpu.roll(s, sh, axis=0)
            sh *= 2
        cnt = jnp.minimum(w, pos + 1).astype(F32)
        dlt = s[hrows:] / cnt - u[:, cs]
        o_ref[:, cs] = _dot(dlt, w_ref[gi]) * sc_ref[:, cs]


def _pool_prompt(p, hist16, w, scale, dm, tt):
    b, t, _ = p.shape
    kern = functools.partial(_pool_kernel, tt=tt, cg=dm.cg)
    return pl.pallas_call(
        kern,
        out_shape=jax.ShapeDtypeStruct((b, t, dm.pw), F32),
        grid=(b, t // tt),
        in_specs=[pl.BlockSpec((None, tt, dm.pw), lambda bi, i: (bi, i, _blk(dm.c_pool, dm.pw))),
                  pl.BlockSpec((None, POOL_HIST + 1, dm.pw), lambda bi, i: (bi, 0, 0)),
                  pl.BlockSpec((POOL_GROUPS, dm.cg, dm.cg), lambda bi, i: (0, 0, 0)),
                  pl.BlockSpec((1, dm.pw), lambda bi, i: (0, 0))],
        out_specs=pl.BlockSpec((None, tt, dm.pw), lambda bi, i: (bi, i, 0)),
        scratch_shapes=[pltpu.VMEM((tt + POOL_HIST + 1, dm.pw), F32)],
        compiler_params=_cparams(("parallel", "arbitrary")),
        name="pool_mixer",
    )(p, hist16, w, scale)


def _pool_sample_kernel(e_ref, w_ref, sc_ref, o_ref, *, cg, pos):
    e = e_ref[...]
    hrows = POOL_HIST + 1
    row = lax.broadcasted_iota(jnp.int32, e.shape, 1)
    u = e[:, hrows - 1, :]
    for gi, w in enumerate(POOL_WINDOWS):
        cs = slice(gi * cg, (gi + 1) * cg)
        win = jnp.sum(jnp.where(row >= hrows - w, e, 0.0)[:, :, cs], axis=1)
        dlt = win / float(min(w, pos + 1)) - u[:, cs]
        o_ref[:, cs] = _dot(dlt, w_ref[gi]) * sc_ref[:, cs]


def _pool_sample(e16, w, scale, dm, pos):
    b = e16.shape[0]
    kern = functools.partial(_pool_sample_kernel, cg=dm.cg, pos=pos)
    return pl.pallas_call(
        kern,
        out_shape=jax.ShapeDtypeStruct((b, dm.pw), F32),
        compiler_params=pltpu.CompilerParams(vmem_limit_bytes=VMEM_LIMIT),
        name="pool_mixer_step",
    )(e16, w, scale)


def _rwkv_mix_cols(x, xprev, mu):
    return x + (xprev - x) * mu


def _rwkv_lora(pl_, w0, w2, a0, a2, g2):
    w_in = pl_[:, 0:RWKV_W_RANK]
    a_in = pl_[:, RWKV_W_RANK:RWKV_W_RANK + RWKV_A_RANK]
    g_in = pl_[:, RWKV_W_RANK + RWKV_A_RANK:]
    lw = -RWKV_DECAY_SCALE * _sigmoid(w0 + _dot_hi(jnp.tanh(w_in), w2))
    a = _sigmoid(a0 + _dot_hi(a_in, a2))
    g = _dot(_sigmoid(g_in), g2)
    return lw, a, g


def _rwkv_prep_kernel(r_ref, k_ref, v_ref, l_ref, rp_ref, kp_ref, vp_ref, lp_ref, sh_ref, mu_ref,
                      w0_ref, w2_ref, a0_ref, a2_ref, g2_ref, kk_ref, ka_ref, rk_ref,
                      rt_ref, at_ref, kh_ref, bh_ref, kg_ref, bg_ref, vo_ref, bo_ref, go_ref, gc_ref,
                      *, tt, rw, chunk):
    i = pl.program_id(1)
    first = i == 0
    row1 = lax.broadcasted_iota(jnp.int32, (tt, 1), 0)

    def mixed(x_ref, xp_ref, c0, c1):
        x = x_ref[...]
        carry = jnp.where(first, sh_ref[:, c0:c1], xp_ref[SUBLANES - 1:SUBLANES, :])
        xprev = jnp.where(row1 == 0, carry, pltpu.roll(x, 1, axis=0))
        return _rwkv_mix_cols(x, xprev, mu_ref[:, c0:c1])

    r = mixed(r_ref, rp_ref, 0, rw)
    k = mixed(k_ref, kp_ref, rw, 2 * rw)
    v = mixed(v_ref, vp_ref, 2 * rw, 3 * rw)
    lo = mixed(l_ref, lp_ref, 3 * rw, 3 * rw + RWKV_LORA)
    lw, a, g = _rwkv_lora(lo, w0_ref[...], w2_ref[...], a0_ref[...], a2_ref[...], g2_ref[...])

    ones = _head_ones()
    kk = k * kk_ref[...]
    kk = kk * lax.rsqrt(jnp.maximum(_head_sum(kk * kk, ones), 1e-12))
    k2 = k * (1.0 + (a - 1.0) * ka_ref[...])
    bonus = _head_sum(r * k2 * rk_ref[...], ones) * v
    bvec = kk * a

    ti = lax.broadcasted_iota(jnp.int32, (tt, tt), 0)
    si = lax.broadcasted_iota(jnp.int32, (tt, tt), 1)
    same = (ti // chunk) == (si // chunk)
    tri = (same & (si <= ti)).astype(BF16)
    blk = same.astype(BF16)
    cum = _dot_exact_lhs(tri, lw)
    tot = _dot_exact_lhs(blk, lw)
    e_in = jnp.exp(cum)
    e_out = jnp.exp(-cum)
    e_rest = jnp.exp(tot - cum)
    rt_ref[...] = r * e_in
    at_ref[...] = -kk * jnp.exp(cum - lw)
    kh_ref[...] = k2 * e_out
    bh_ref[...] = bvec * e_out
    kg_ref[...] = k2 * e_rest
    bg_ref[...] = bvec * e_rest
    vo_ref[...] = v
    bo_ref[...] = bonus
    go_ref[...] = g
    etot = jnp.exp(tot)
    for c in range(tt // chunk):
        gc_ref[c] = etot[c * chunk:c * chunk + 1, :]


def _rwkv_prep(p, shift, mu, w0, w2, a0, a2, g2, k_k, k_a, r_k, dm, tt):
    b, t, _ = p.shape
    rw = dm.rw
    chunk = RWKV_CHUNK
    kern = functools.partial(_rwkv_prep_kernel, tt=tt, rw=rw, chunk=chunk)
    cur = lambda c0, w: pl.BlockSpec((None, tt, w), lambda bi, i: (bi, i, _blk(c0, w)))
    prv = lambda c0, w: pl.BlockSpec(
        (None, SUBLANES, w), lambda bi, i: (bi, jnp.maximum(i * (tt // SUBLANES) - 1, 0), _blk(c0, w)))
    full = lambda shp: pl.BlockSpec(shp, lambda bi, i: (0,) * len(shp))
    tok = jax.ShapeDtypeStruct((b, t, rw), F32)
    tspec = pl.BlockSpec((None, tt, rw), lambda bi, i: (bi, i, 0))
    return pl.pallas_call(
        kern,
        out_shape=(tok,) * 9 + (jax.ShapeDtypeStruct((b, t // chunk, 1, rw), F32),),
        grid=(b, t // tt),
        in_specs=[cur(dm.c_r, rw), cur(dm.c_r + rw, rw), cur(dm.c_r + 2 * rw, rw), cur(dm.c_lora, RWKV_LORA),
                  prv(dm.c_r, rw), prv(dm.c_r + rw, rw), prv(dm.c_r + 2 * rw, rw), prv(dm.c_lora, RWKV_LORA),
                  pl.BlockSpec((None, 1, dm.rcols), lambda bi, i: (bi, 0, 0)),
                  full((1, dm.rcols)), full((1, rw)), full((RWKV_W_RANK, rw)), full((1, rw)),
                  full((RWKV_A_RANK, rw)), full((RWKV_G_RANK, rw)), full((1, rw)), full((1, rw)), full((1, rw))],
        out_specs=(tspec,) * 9 + (pl.BlockSpec((None, tt // chunk, 1, rw), lambda bi, i: (bi, i, 0, 0)),),
        compiler_params=_cparams(("parallel", "parallel")),
        name="rwkv_prep",
    )(p, p, p, p, p, p, p, p, shift, mu, w0, w2, a0, a2, g2, k_k, k_a, r_k)


def _group_norm_out(y, bonus, g, lnw, lnb, ones):
    ym = _head_sum(y, ones) * (1.0 / RWKV_HEAD)
    d = y - ym
    yv = _head_sum(d * d, ones) * (1.0 / RWKV_HEAD)
    return (d * lax.rsqrt(yv + GN_EPS) * lnw + lnb + bonus) * g


def _rwkv_solve_kernel(at_ref, kh_ref, bh_ref, rt_ref, v_ref, kg_ref, bg_ref,
                       wt_ref, p1_ref, y1_ref, arb_ref, kbt_ref, *, chunk, pairs, group):
    n = RWKV_HEAD
    assert chunk == n and 2 * n == LANES
    row = lax.broadcasted_iota(jnp.int32, (chunk, LANES), 0)
    lane = lax.broadcasted_iota(jnp.int32, (chunk, LANES), 1)
    la = lane < n
    src = lane % n
    strict_a, strict_b = (src < row) & la, (src < row) & ~la
    incl_a, incl_b = (src <= row) & la, (src <= row) & ~la
    la2 = jnp.concatenate([la, la], axis=0)
    n_apply = int(math.log2(chunk))
    assert 2 ** n_apply == chunk
    w = lambda m, x: jnp.where(m, x, 0.0)
    stack = lambda a, b: jnp.concatenate([a, b], axis=0)

    for p0 in range(0, pairs, group):
        sls = [slice(p * LANES, (p + 1) * LANES) for p in range(p0, min(p0 + group, pairs))]
        ng = len(sls)
        at = [at_ref[:, sl] for sl in sls]
        v = [v_ref[:, sl] for sl in sls]
        a2 = [_split2(stack(at[i], rt_ref[:, sl])) for i, sl in enumerate(sls)]
        kh = [_split2(kh_ref[:, sl]) for sl in sls]
        bh = [_split2(bh_ref[:, sl]) for sl in sls]
        zero = jnp.zeros((2 * chunk, LANES), BF16)
        pa = [_mm3(tuple(jnp.where(la2, t, zero) for t in a2[i]),
                   tuple(stack(b, k) for b, k in zip(bh[i], kh[i])), _NT) for i in range(ng)]
        pb = [_mm3(tuple(jnp.where(la2, zero, t) for t in a2[i]),
                   tuple(stack(k, b) for b, k in zip(bh[i], kh[i])), _NT) for i in range(ng)]
        pw = [stack(w(strict_a, pa[i][:chunk]), w(strict_b, pb[i][:chunk])) for i in range(ng)]
        mak = [stack(w(strict_b, pa[i][:chunk]), w(strict_a, pb[i][:chunk])) for i in range(ng)]
        ark = [stack(w(incl_b, pa[i][chunk:]), w(incl_a, pb[i][chunk:])) for i in range(ng)]
        vsw = [pltpu.roll(v[i], n, axis=1) for i in range(ng)]
        vx = [_split2(stack(w(la, vsw[i]), w(~la, vsw[i]))) for i in range(ng)]
        kvy = [_mm3(_split2(stack(mak[i], ark[i])), vx[i], _NN) for i in range(ng)]
        for i, sl in enumerate(sls):
            arb_ref[:, sl] = stack(w(incl_a, pa[i][chunk:]), w(incl_b, pb[i][chunk:]))
            y1_ref[:, sl] = pltpu.roll(kvy[i][2 * chunk:3 * chunk] + kvy[i][3 * chunk:], n, axis=1)
            kbt_ref[:, sl] = stack(kg_ref[:, sl], bg_ref[:, sl]).T
        x = [stack(w(la, at[i]), w(~la, at[i])) + kvy[i][:2 * chunk] for i in range(ng)]
        for lvl in range(n_apply):
            if lvl + 1 < n_apply:
                r = [_mm3(_split2(pw[i]), _split2(jnp.concatenate([x[i], pw[i]], axis=1)), _NN) for i in range(ng)]
                x = [x[i] + r[i][:, :LANES] for i in range(ng)]
                pw = [r[i][:, LANES:] for i in range(ng)]
            else:
                x = [x[i] + _mm3(_split2(pw[i]), _split2(x[i]), _NN) for i in range(ng)]
        for i, sl in enumerate(sls):
            wt_ref[:, sl] = w(la, x[i][:chunk]) + w(~la, x[i][chunk:])
            p1_ref[:, sl] = pltpu.roll(w(~la, x[i][:chunk]) + w(la, x[i][chunk:]), n, axis=1)


def _rwkv_solve(at, kh, bh, rt, v, kg, bg, dm):
    b, t, rw = rt.shape
    chunk = RWKV_CHUNK
    pairs = dm.rh // 2
    kern = functools.partial(_rwkv_solve_kernel, chunk=chunk, pairs=pairs, group=min(4, pairs))
    tspec = pl.BlockSpec((None, chunk, rw), lambda bi, c: (bi, c, 0))
    sspec = pl.BlockSpec((None, 2 * chunk, rw), lambda bi, c: (bi, c, 0))
    tok = jax.ShapeDtypeStruct((b, t, rw), F32)
    stk = jax.ShapeDtypeStruct((b, 2 * t, rw), F32)
    return pl.pallas_call(
        kern,
        out_shape=(tok, tok, tok, stk, stk),
        grid=(b, t // chunk),
        in_specs=[tspec] * 7,
        out_specs=(tspec, tspec, tspec, sspec, sspec),
        compiler_params=_cparams(("parallel", "parallel")),
        name="rwkv_solve",
    )(at, kh, bh, rt, v, kg, bg)


def _rwkv_scan_kernel(wt_ref, rt_ref, p1_ref, y1_ref, arb_ref, kbt_ref, v_ref, bo_ref, g_ref, gc_ref,
                      lnw_ref, lnb_ref, s0_ref, y_ref, so_ref, st_ref, yacc_ref, *, chunk, pairs):
    c = pl.program_id(1)
    n = RWKV_HEAD
    row = lax.broadcasted_iota(jnp.int32, (LANES, LANES), 0)
    lane = lax.broadcasted_iota(jnp.int32, (LANES, LANES), 1)
    eye = row == lane
    same_head = (row < n) == (lane < n)
    la = lax.broadcasted_iota(jnp.int32, (chunk, LANES), 1) < n

    @pl.when(c == 0)
    def _():
        z = jnp.zeros((n, n), F32)
        for p in range(pairs):
            st_ref[p] = jnp.concatenate([jnp.concatenate([s0_ref[2 * p], z], axis=1),
                                         jnp.concatenate([z, s0_ref[2 * p + 1]], axis=1)], axis=0)

    sls = [slice(p * LANES, (p + 1) * LANES) for p in range(pairs)]
    s0 = [st_ref[p] for p in range(pairs)]
    ws = [_mm3(_split2(jnp.concatenate([wt_ref[:, sl], rt_ref[:, sl]], axis=0)), _split2(s0[p]), _NN)
          for p, sl in enumerate(sls)]
    u = [p1_ref[:, sl] + ws[p][:chunk] for p, sl in enumerate(sls)]
    for p, sl in enumerate(sls):
        ust = jnp.concatenate([jnp.where(la, u[p], 0.0), jnp.where(la, 0.0, u[p])], axis=0)
        yb = _mm3(_split2(arb_ref[:, sl]), _split2(ust), _NN)
        yacc_ref[:, sl] = ws[p][chunk:] + y1_ref[:, sl] + (yb[:chunk] + yb[chunk:])
    for p, sl in enumerate(sls):
        dg = jnp.where(eye, jnp.broadcast_to(gc_ref[:, sl], (LANES, LANES)), 0.0)
        lhs = jnp.concatenate([kbt_ref[:, sl], dg], axis=1)
        rhs = jnp.concatenate([v_ref[:, sl], u[p], s0[p]], axis=0)
        st_ref[p] = jnp.where(same_head, _mm3(_split2(lhs), _split2(rhs), _NN), 0.0)

    y_ref[...] = _group_norm_out(yacc_ref[...], bo_ref[...], g_ref[...], lnw_ref[...], lnb_ref[...], _head_ones())

    @pl.when(c == pl.num_programs(1) - 1)
    def _():
        for p in range(pairs):
            s = st_ref[p]
            so_ref[2 * p] = s[:n, :n]
            so_ref[2 * p + 1] = s[n:, n:]


def _rwkv_scan(prep, lnw, lnb, s0t, dm):
    rt, at, kh, bh, kg, bg, v, bonus, g, gc = prep
    wt, p1, y1, arb, kbt = _rwkv_solve(at, kh, bh, rt, v, kg, bg, dm)
    b, t, rw = rt.shape
    chunk = RWKV_CHUNK
    heads = dm.rh
    pairs = heads // 2
    kern = functools.partial(_rwkv_scan_kernel, chunk=chunk, pairs=pairs)
    tspec = pl.BlockSpec((None, chunk, rw), lambda bi, c: (bi, c, 0))
    kspec = pl.BlockSpec((None, 2 * chunk, rw), lambda bi, c: (bi, c, 0))
    sspec = pl.BlockSpec((None, heads, RWKV_HEAD, RWKV_HEAD), lambda bi, c: (bi, 0, 0, 0))
    row = pl.BlockSpec((1, rw), lambda bi, c: (0, 0))
    return pl.pallas_call(
        kern,
        out_shape=(jax.ShapeDtypeStruct((b, t, rw), F32),
                   jax.ShapeDtypeStruct((b, heads, RWKV_HEAD, RWKV_HEAD), F32)),
        grid=(b, t // chunk),
        in_specs=[tspec, tspec, tspec, tspec, kspec, kspec, tspec, tspec, tspec,
                  pl.BlockSpec((None, None, 1, rw), lambda bi, c: (bi, c, 0, 0)), row, row, sspec],
        out_specs=(tspec, sspec),
        scratch_shapes=[pltpu.VMEM((pairs, LANES, LANES), F32), pltpu.VMEM((chunk, rw), F32)],
        compiler_params=_cparams(("parallel", "arbitrary")),
        name="rwkv_scan",
    )(wt, rt, p1, y1, arb, kbt, v, bonus, g, gc, lnw, lnb, s0t)


def _rwkv_step_kernel(r_ref, k_ref, v_ref, l_ref, sh_ref, mu_ref, w0_ref, w2_ref, a0_ref, a2_ref, g2_ref,
                      kk_ref, ka_ref, rk_ref, lnw_ref, lnb_ref, s_ref, y_ref, so_ref, *, rw, heads):
    n = RWKV_HEAD
    rows = SUBLANES

    def mixed(x_ref, c0, c1):
        x = jnp.broadcast_to(x_ref[...], (rows, c1 - c0))
        return _rwkv_mix_cols(x, sh_ref[:, c0:c1], mu_ref[:, c0:c1])

    r = mixed(r_ref, 0, rw)
    k = mixed(k_ref, rw, 2 * rw)
    v = mixed(v_ref, 2 * rw, 3 * rw)
    lo = mixed(l_ref, 3 * rw, 3 * rw + RWKV_LORA)
    lw, a, g = _rwkv_lora(lo, w0_ref[...], w2_ref[...], a0_ref[...], a2_ref[...], g2_ref[...])
    w = jnp.exp(lw)
    kk = k * kk_ref[...]
    k2 = k * (1.0 + (a - 1.0) * ka_ref[...])
    eye = lax.broadcasted_iota(jnp.int32, (n, n), 0) == lax.broadcasted_iota(jnp.int32, (n, n), 1)

    def col(rowvec):
        return jnp.sum(jnp.where(eye, jnp.broadcast_to(rowvec, (n, n)), 0.0), axis=1, keepdims=True)

    outs = []
    for h in range(heads):
        sl = slice(h * n, (h + 1) * n)
        kkh = kk[0:1, sl]
        kkh = kkh * lax.rsqrt(jnp.maximum(jnp.sum(kkh * kkh, axis=1, keepdims=True), 1e-12))
        ah, wh, k2h, rh, vh = a[0:1, sl], w[0:1, sl], k2[0:1, sl], r[0:1, sl], v[0:1, sl]
        s = s_ref[h]
        sa = -jnp.sum(s * kkh, axis=1, keepdims=True)
        s = s * wh + sa * (kkh * ah) + col(vh) * k2h
        so_ref[h] = s
        ycol = jnp.sum(s * rh, axis=1, keepdims=True)
        yrow = jnp.sum(jnp.where(eye, jnp.broadcast_to(ycol, (n, n)), 0.0), axis=0, keepdims=True)
        ym = jnp.mean(yrow, axis=1, keepdims=True)
        d = yrow - ym
        yv = jnp.mean(d * d, axis=1, keepdims=True)
        yn = d * lax.rsqrt(yv + GN_EPS) * lnw_ref[:, sl] + lnb_ref[:, sl]
        bonus = jnp.sum(rh * k2h * rk_ref[:, sl], axis=1, keepdims=True) * vh
        outs.append((yn + bonus) * g[0:1, sl])
    y_ref[...] = jnp.concatenate(outs, axis=1)


def _rwkv_step(p, shift, s0, mu, w0, w2, a0, a2, g2, k_k, k_a, r_k, lnw, lnb, dm):
    b = p.shape[0]
    rw, heads = dm.rw, dm.rh
    kern = functools.partial(_rwkv_step_kernel, rw=rw, heads=heads)
    cur = lambda c0, w: pl.BlockSpec((None, 1, w), lambda bi: (bi, 0, _blk(c0, w)))
    full = lambda shp: pl.BlockSpec(shp, lambda bi: (0,) * len(shp))
    sspec = pl.BlockSpec((None, heads, RWKV_HEAD, RWKV_HEAD), lambda bi: (bi, 0, 0, 0))
    return pl.pallas_call(
        kern,
        out_shape=(jax.ShapeDtypeStruct((b, 1, rw), F32),
                   jax.ShapeDtypeStruct((b, heads, RWKV_HEAD, RWKV_HEAD), F32)),
        grid=(b,),
        in_specs=[cur(dm.c_r, rw), cur(dm.c_r + rw, rw), cur(dm.c_r + 2 * rw, rw), cur(dm.c_lora, RWKV_LORA),
                  pl.BlockSpec((None, 1, dm.rcols), lambda bi: (bi, 0, 0)),
                  full((1, dm.rcols)), full((1, rw)), full((RWKV_W_RANK, rw)), full((1, rw)),
                  full((RWKV_A_RANK, rw)), full((RWKV_G_RANK, rw)), full((1, rw)), full((1, rw)), full((1, rw)),
                  full((1, rw)), full((1, rw)), sspec],
        out_specs=(pl.BlockSpec((None, 1, rw), lambda bi: (bi, 0, 0)), sspec),
        compiler_params=_cparams(("parallel",)),
        name="rwkv_step",
    )(p, p, p, p, shift, mu, w0, w2, a0, a2, g2, k_k, k_a, r_k, lnw, lnb, s0)


def _rope_blocks(x, cos2, sin2):
    outs = []
    for c in range(0, x.shape[1], NSA_HEAD):
        xb = x[:, c:c + NSA_HEAD]
        outs.append(xb * cos2 + pltpu.roll(xb, NSA_HEAD // 2, axis=1) * sin2)
    return outs[0] if len(outs) == 1 else jnp.concatenate(outs, axis=1)


def _nsa_prep_kernel(*refs, tt, nq, with_means):
    q_refs, kv_refs = refs[:nq], refs[nq:nq + 6]
    cos_ref, sin_ref, qo_ref, kv_ref, kw_ref = refs[nq + 6:nq + 11]
    cos2, sin2 = cos_ref[...], sin_ref[...]
    pw = NSA_KV_HEADS * NSA_HEAD
    for idx, ref in enumerate(q_refs):
        qo_ref[:, idx * pw:(idx + 1) * pw] = (_rope_blocks(ref[...], cos2, sin2) * (NSA_HEAD ** -0.5)).astype(BF16)
    for idx, ref in enumerate(kv_refs):
        x = ref[...]
        if idx % 2 == 0:
            x = _rope_blocks(x, cos2, sin2)
        if idx < 4:
            kv_ref[:, idx * pw:(idx + 1) * pw] = x
        else:
            kw_ref[:, (idx - 4) * pw:(idx - 3) * pw] = x
        if idx < 2 and with_means:
            m_ref = refs[nq + 11]
            m_ref[:, idx * pw:(idx + 1) * pw] = (
                jnp.sum(x.reshape(tt // NSA_BLOCK, NSA_BLOCK, pw), axis=1) * (1.0 / NSA_BLOCK))


def _nsa_prep(p, cos2, sin2, dm, tt, with_means):
    b, t, _ = p.shape
    pw = dm.part
    nq = dm.nw // pw
    kern = functools.partial(_nsa_prep_kernel, tt=tt, nq=nq, with_means=with_means)
    cur = lambda c0: pl.BlockSpec((None, tt, pw), lambda bi, i: (bi, i, _blk(c0, pw)))
    outs = [jax.ShapeDtypeStruct((b, t, dm.nw), BF16), jax.ShapeDtypeStruct((b, t, 4 * pw), F32),
            jax.ShapeDtypeStruct((b, t, 2 * pw), F32)]
    ospecs = [pl.BlockSpec((None, tt, dm.nw), lambda bi, i: (bi, i, 0)),
              pl.BlockSpec((None, tt, 4 * pw), lambda bi, i: (bi, i, 0)),
              pl.BlockSpec((None, tt, 2 * pw), lambda bi, i: (bi, i, 0))]
    if with_means:
        outs.append(jax.ShapeDtypeStruct((b, t // NSA_BLOCK, 2 * pw), F32))
        ospecs.append(pl.BlockSpec((None, tt // NSA_BLOCK, 2 * pw), lambda bi, i: (bi, i, 0)))
    in_specs = [cur(dm.c_q + k * pw) for k in range(nq)] + [cur(dm.c_kv + k * pw) for k in range(6)]
    in_specs += [pl.BlockSpec((tt, NSA_HEAD), lambda bi, i: (i, 0))] * 2
    return pl.pallas_call(
        kern,
        out_shape=tuple(outs),
        grid=(b, t // tt),
        in_specs=in_specs,
        out_specs=tuple(ospecs),
        compiler_params=_cparams(("parallel", "parallel")),
        name="nsa_prep",
    )(*([p] * (nq + 6)), cos2, sin2)


def _nsa_prompt_kernel(q_ref, kc_ref, vc_ref, ks_ref, vs_ref, kw_ref, vw_ref, gt_ref, o_ref,
                       m_ref, l_ref, acc_ref, *, g, nb, n_sel, tk, lw):
    h = pl.program_id(1)
    i = pl.program_id(2)
    qb = NSA_QTILE
    rows = g * qb
    q = jnp.concatenate([q_ref[:, gi * NSA_HEAD:(gi + 1) * NSA_HEAD] for gi in range(g)], axis=0)
    gsl = [slice(gi * qb, (gi + 1) * qb) for gi in range(g)]
    qpos_c = i * qb + lax.broadcasted_iota(jnp.int32, (qb, 1), 0)
    qpos_r = i * qb + lax.broadcasted_iota(jnp.int32, (1, rows), 1) % qb

    blk = lax.broadcasted_iota(jnp.int32, (nb, 1), 0)
    st = _dot_nt(kc_ref[...], q)
    ok = ((blk + 1) * NSA_BLOCK - 1) <= qpos_r
    mc = jnp.max(jnp.where(ok, st, NEG), axis=0, keepdims=True)
    ec = jnp.where(ok, jnp.exp(st - mc), 0.0)
    den = jnp.sum(ec, axis=0, keepdims=True)
    p_c = ec / jnp.where(den > 0.0, den, 1.0)
    o_cmp = lax.dot_general(p_c.astype(BF16), vc_ref[...].astype(BF16), _TN, preferred_element_type=F32)

    imp = p_c[:, gsl[0]]
    for gi in range(1, g):
        imp = imp + p_c[:, gsl[gi]]
    qp1 = qpos_r[:, 0:qb]
    imp = jnp.where(blk == qp1 // NSA_BLOCK, g + 1.0, jnp.where(blk * NSA_BLOCK <= qp1, imp, -1.0))
    cnt = jnp.zeros((nb, qb), F32)
    for bi in range(nb):
        ci = imp[bi:bi + 1, :]
        beats = (ci > imp) | ((ci == imp) & (blk > bi))
        cnt = cnt + jnp.where(beats, 1.0, 0.0)
    sel_t = jnp.where(cnt < n_sel, 1.0, 0.0).astype(BF16)

    m_ref[...] = jnp.full((rows, 1), M_INIT, F32)
    l_ref[...] = jnp.zeros((rows, 1), F32)
    acc_ref[...] = jnp.zeros((rows, NSA_HEAD), F32)
    bpt = tk // NSA_BLOCK
    brow = lax.broadcasted_iota(jnp.int32, (nb, tk), 0)
    bcol = lax.broadcasted_iota(jnp.int32, (nb, tk), 1) // NSA_BLOCK
    kcol = lax.broadcasted_iota(jnp.int32, (1, tk), 1)

    def scores(kt):
        k0 = pl.multiple_of(kt * tk, tk)
        kb = ks_ref[pl.ds(k0, tk), :].astype(BF16)
        sc = _dot_nt(q, kb)
        expand = (brow == bcol + kt * bpt).astype(BF16)
        picked = lax.dot_general(sel_t, expand, _TN, preferred_element_type=F32)
        return sc, jnp.where((picked > 0.5) & ((k0 + kcol) <= qpos_c), 0.0, NEG)

    def accumulate(kt, sc, bias):
        k0 = pl.multiple_of(kt * tk, tk)
        vb = vs_ref[pl.ds(k0, tk), :].astype(BF16)
        s_g = [sc[sl] + bias for sl in gsl]
        m_old = [m_ref[sl] for sl in gsl]
        m_new = [jnp.maximum(m_old[gi], jnp.max(s_g[gi], axis=1, keepdims=True)) for gi in range(g)]
        pr = [jnp.exp(s_g[gi] - m_new[gi]) for gi in range(g)]
        pv = [_dot(pr[gi], vb) for gi in range(g)]
        for gi, sl in enumerate(gsl):
            alpha = jnp.exp(m_old[gi] - m_new[gi])
            l_ref[sl] = alpha * l_ref[sl] + jnp.sum(pr[gi], axis=1, keepdims=True)
            acc_ref[sl] = alpha * acc_ref[sl] + pv[gi]
            m_ref[sl] = m_new[gi]

    def body(kt, carry):
        nxt = scores(kt + 1)
        accumulate(kt, *carry)
        return nxt

    n_kt = ((i + 1) * qb + tk - 1) // tk
    last = lax.fori_loop(0, n_kt - 1, body, scores(0))
    accumulate(n_kt - 1, *last)
    o_sel = acc_ref[...] / l_ref[...]

    w0 = pl.multiple_of(jnp.maximum(i * qb - NSA_WINDOW, 0), qb)
    kwb = kw_ref[pl.ds(w0, lw), :].astype(BF16)
    vwb = vw_ref[pl.ds(w0, lw), :].astype(BF16)
    dp = qpos_c - (w0 + lax.broadcasted_iota(jnp.int32, (1, lw), 1))
    bias_w = jnp.where((dp >= 0) & (dp <= NSA_WINDOW), 0.0, NEG)
    sw = _dot_nt(q, kwb)
    s_w = [sw[sl] + bias_w for sl in gsl]
    e_w = [jnp.exp(s_w[gi] - jnp.max(s_w[gi], axis=1, keepdims=True)) for gi in range(g)]
    o_win = jnp.concatenate([_dot(e_w[gi], vwb) / jnp.sum(e_w[gi], axis=1, keepdims=True) for gi in range(g)],
                            axis=0)

    gs = _sigmoid(gt_ref[...])
    for gi in range(g):
        acc = None
        for ci, ob in enumerate((o_cmp, o_sel, o_win)):
            c0 = gi * 3 + ci
            c1 = (g + gi) * 3 + ci
            gate = jnp.where(h == 0, gs[:, c0:c0 + 1], gs[:, c1:c1 + 1])
            term = ob[gi * qb:(gi + 1) * qb] * gate
            acc = term if acc is None else acc + term
        o_ref[:, gi * NSA_HEAD:(gi + 1) * NSA_HEAD] = acc


def _nsa_prompt(qr, kv4, kwv, kcvc, p, dm):
    b, t, _ = qr.shape
    assert NSA_KV_HEADS == 2
    g = dm.g
    nb = t // NSA_BLOCK
    n_sel = min(NSA_TOPK, nb)
    tk = min(512, t)
    lw = NSA_WINDOW + NSA_QTILE
    assert t % tk == 0 and t >= lw
    hd = NSA_HEAD
    kern = functools.partial(_nsa_prompt_kernel, g=g, nb=nb, n_sel=n_sel, tk=tk, lw=lw)
    kvspec = lambda c: pl.BlockSpec((None, t, hd), lambda bi, h, i, c=c: (bi, 0, 2 * c + h))
    rows = g * NSA_QTILE
    return pl.pallas_call(
        kern,
        out_shape=jax.ShapeDtypeStruct((b, t, dm.nw), F32),
        grid=(b, NSA_KV_HEADS, t // NSA_QTILE),
        in_specs=[pl.BlockSpec((None, NSA_QTILE, g * hd), lambda bi, h, i: (bi, i, h)),
                  pl.BlockSpec((None, nb, hd), lambda bi, h, i: (bi, 0, h)),
                  pl.BlockSpec((None, nb, hd), lambda bi, h, i: (bi, 0, 2 + h)),
                  kvspec(2), kvspec(3), kvspec(0), kvspec(1),
                  pl.BlockSpec((None, NSA_QTILE, LANES), lambda bi, h, i: (bi, i, _blk(dm.c_ng, LANES)))],
        out_specs=pl.BlockSpec((None, NSA_QTILE, g * hd), lambda bi, h, i: (bi, i, h)),
        scratch_shapes=[pltpu.VMEM((rows, 1), F32), pltpu.VMEM((rows, 1), F32), pltpu.VMEM((rows, hd), F32)],
        compiler_params=_cparams(("parallel", "parallel", "arbitrary")),
        name="nsa_prompt",
    )(qr, kcvc, kcvc, kv4, kv4, kwv, kwv, p)


def _page_means_kernel(pt_ref, *refs, ppb):
    o_ref = refs[ppb]
    for k in range(ppb):
        x = refs[k][...]
        nbp = x.shape[0] // NSA_BLOCK
        o_ref[k] = jnp.sum(x.reshape(nbp, NSA_BLOCK, x.shape[1], x.shape[2]), axis=1) * (1.0 / NSA_BLOCK)


def _page_means(cache, layer, page_table):
    _, _, page, _, prow, hd = cache.shape
    b, n_pages = page_table.shape
    nbp = page // NSA_BLOCK
    ppb = math.gcd(n_pages, 16)
    kern = functools.partial(_page_means_kernel, ppb=ppb)
    pspec = lambda k: pl.BlockSpec((None, None, page, None, prow, hd),
                                   lambda bi, j, pt, k=k: (layer, pt[bi, j * ppb + k], 0, 0, 0, 0))
    out = pl.pallas_call(
        kern,
        out_shape=jax.ShapeDtypeStruct((b, n_pages, nbp, prow, hd), F32),
        grid_spec=pltpu.PrefetchScalarGridSpec(
            num_scalar_prefetch=1,
            grid=(b, n_pages // ppb),
            in_specs=[pspec(k) for k in range(ppb)],
            out_specs=pl.BlockSpec((None, ppb, nbp, prow, hd), lambda bi, j, pt: (bi, j, 0, 0, 0)),
        ),
        compiler_params=_cparams(("parallel", "parallel")),
        name="nsa_page_means",
    )(page_table, *([cache] * ppb))
    return out.reshape(b, n_pages * nbp, prow * hd)


def _nsa_choose_kernel(q_ref, m_ref, oc_ref, idx_ref, *, g, nbp, n_sel):
    hd = NSA_HEAD
    blk = lax.broadcasted_iota(jnp.int32, (1, nbp), 1)
    lane = lax.broadcasted_iota(jnp.int32, (1, LANES), 1)
    for h in range(NSA_KV_HEADS):
        q = jnp.concatenate([q_ref[:, (h * g + gi) * hd:(h * g + gi + 1) * hd] for gi in range(g)], axis=0)
        q = jnp.concatenate([q, jnp.zeros((SUBLANES - g, hd), q.dtype)], axis=0) if g < SUBLANES else q
        kc = m_ref[:, h * hd:(h + 1) * hd]
        vc = m_ref[:, (NSA_KV_HEADS + h) * hd:(NSA_KV_HEADS + h + 1) * hd]
        s = _dot_nt(q, kc)
        m = jnp.max(s, axis=1, keepdims=True)
        e = jnp.exp(s - m)
        p_c = e / jnp.sum(e, axis=1, keepdims=True)
        oc_ref[h] = _dot(p_c, vc)[0:g]
        imp = jnp.sum(p_c[0:g], axis=0, keepdims=True)
        idx = jnp.where(lane == 0, nbp, 0)
        for it in range(1, n_sel):
            best = jnp.max(imp, axis=1, keepdims=True)
            j = jnp.min(jnp.where(imp == best, blk, nbp), axis=1, keepdims=True)
            idx = jnp.where(lane == it, j, idx)
            imp = jnp.where(blk == j, -2.0, imp)
        idx_ref[h] = idx


def _nsa_choose(qr, means, dm, n_sel):
    b = qr.shape[0]
    nbp = means.shape[1]
    g = dm.g
    kern = functools.partial(_nsa_choose_kernel, g=g, nbp=nbp, n_sel=n_sel)
    return pl.pallas_call(
        kern,
        out_shape=(jax.ShapeDtypeStruct((b, NSA_KV_HEADS, g, NSA_HEAD), F32),
                   jax.ShapeDtypeStruct((b, NSA_KV_HEADS, 1, LANES), jnp.int32)),
        grid=(b,),
        in_specs=[pl.BlockSpec((None, 1, dm.nw), lambda bi: (bi, 0, 0)),
                  pl.BlockSpec((None, nbp, means.shape[2]), lambda bi: (bi, 0, 0))],
        out_specs=(pl.BlockSpec((None, NSA_KV_HEADS, g, NSA_HEAD), lambda bi: (bi, 0, 0, 0)),
                   pl.BlockSpec((None, NSA_KV_HEADS, 1, LANES), lambda bi: (bi, 0, 0, 0))),
        compiler_params=_cparams(("parallel",)),
        name="nsa_choose",
    )(qr, means)


def _nsa_sample_kernel(idx_ref, pt_ref, q_ref, cb_ref, kn_ref, vn_ref, wb_ref, kwn_ref, vwn_ref,
                       oc_ref, gt_ref, o_ref, m_ref, l_ref, acc_ref, *, g, n_sel):
    h = pl.program_id(1)
    j = pl.program_id(2)
    hd = NSA_HEAD
    rows = SUBLANES
    q = jnp.concatenate([q_ref[:, gi * hd:(gi + 1) * hd] for gi in range(g)], axis=0)
    if g < rows:
        q = jnp.concatenate([q, jnp.zeros((rows - g, hd), q.dtype)], axis=0)

    @pl.when(j == 0)
    def _():
        s0 = jnp.sum(q.astype(F32) * kn_ref[...].astype(BF16).astype(F32), axis=1, keepdims=True)
        m_ref[...] = s0
        l_ref[...] = jnp.ones((rows, 1), F32)
        acc_ref[...] = jnp.broadcast_to(vn_ref[...].astype(BF16).astype(F32), (rows, hd))

    kvh = NSA_KV_HEADS

    @pl.when(j > 0)
    def _():
        xb = cb_ref[...].astype(BF16)
        srow = lax.broadcasted_iota(jnp.int32, (1, xb.shape[0]), 1) % (4 * kvh)
        sc = _dot_nt(q, xb) + jnp.where(srow == 2 * kvh + h, 0.0, NEG)
        m_old = m_ref[...]
        m_new = jnp.maximum(m_old, jnp.max(sc, axis=1, keepdims=True))
        pr = jnp.exp(sc - m_new)
        alpha = jnp.exp(m_old - m_new)
        l_ref[...] = alpha * l_ref[...] + jnp.sum(pr, axis=1, keepdims=True)
        acc_ref[...] = alpha * acc_ref[...] + _dot(pltpu.roll(pr, kvh, axis=1), xb)
        m_ref[...] = m_new

    @pl.when(j == n_sel - 1)
    def _():
        o_sel = acc_ref[...] / l_ref[...]
        xw = wb_ref[...].astype(BF16)
        wrow = lax.broadcasted_iota(jnp.int32, (1, xw.shape[0]), 1) % (2 * kvh)
        sw = _dot_nt(q, xw) + jnp.where(wrow == h, 0.0, NEG)
        sn = jnp.sum(q.astype(F32) * kwn_ref[...].astype(BF16).astype(F32), axis=1, keepdims=True)
        mw = jnp.maximum(jnp.max(sw, axis=1, keepdims=True), sn)
        ew = jnp.exp(sw - mw)
        en = jnp.exp(sn - mw)
        den = jnp.sum(ew, axis=1, keepdims=True) + en
        pn = (en / den).astype(BF16).astype(F32)
        o_win = _dot(pltpu.roll(ew / den, kvh, axis=1), xw) + pn * vwn_ref[...].astype(BF16).astype(F32)
        gs = _sigmoid(gt_ref[...])
        o_cmp = oc_ref[...]
        for gi in range(g):
            acc = None
            for ci, ob in enumerate((o_cmp, o_sel, o_win)):
                c0 = gi * 3 + ci
                c1 = (g + gi) * 3 + ci
                gate = jnp.where(h == 0, gs[:, c0:c0 + 1], gs[:, c1:c1 + 1])
                term = ob[gi:gi + 1] * gate
                acc = term if acc is None else acc + term
            o_ref[:, gi * hd:(gi + 1) * hd] = acc


def _nsa_sample(qr, kv4, kwv, cache_rows, win_rows, layer, n_pool, page, lwin, page_table, idx, o_cmp, p, dm, n_sel):
    b = qr.shape[0]
    g, hd = dm.g, NSA_HEAD
    bpp = page // NSA_BLOCK
    brows = NSA_BLOCK * 4 * NSA_KV_HEADS
    wrows = lwin * 2 * NSA_KV_HEADS
    kern = functools.partial(_nsa_sample_kernel, g=g, n_sel=n_sel)

    def sel_map(bi, h, j, idx_ref, pt_ref):
        blk = idx_ref[(bi * NSA_KV_HEADS + h) * LANES + j]
        n_past = pt_ref.shape[1] * bpp
        blk = jnp.minimum(blk, n_past - 1)
        return ((layer * n_pool + pt_ref[bi, blk // bpp]) * bpp + blk % bpp, 0)

    newspec = lambda c: pl.BlockSpec((None, 1, hd), lambda bi, h, j, ir, pr, c=c: (bi, 0, 2 * c + h))
    return pl.pallas_call(
        kern,
        out_shape=jax.ShapeDtypeStruct((b, 1, dm.nw), F32),
        grid_spec=pltpu.PrefetchScalarGridSpec(
            num_scalar_prefetch=2,
            grid=(b, NSA_KV_HEADS, n_sel),
            in_specs=[pl.BlockSpec((None, 1, g * hd), lambda bi, h, j, ir, pr: (bi, 0, h)),
                      pl.BlockSpec((brows, hd), sel_map),
                      newspec(2), newspec(3),
                      pl.BlockSpec((wrows, hd), lambda bi, h, j, ir, pr: (layer * b + bi, 0)),
                      newspec(0), newspec(1),
                      pl.BlockSpec((None, None, g, hd), lambda bi, h, j, ir, pr: (bi, h, 0, 0)),
                      pl.BlockSpec((None, 1, LANES), lambda bi, h, j, ir, pr: (bi, 0, _blk(dm.c_ng, LANES)))],
            out_specs=pl.BlockSpec((None, 1, g * hd), lambda bi, h, j, ir, pr: (bi, 0, h)),
            scratch_shapes=[pltpu.VMEM((SUBLANES, 1), F32), pltpu.VMEM((SUBLANES, 1), F32),
                            pltpu.VMEM((SUBLANES, hd), F32)],
        ),
        compiler_params=_cparams(("parallel", "parallel", "arbitrary")),
        name="nsa_sample",
    )(idx, page_table, qr, cache_rows, kv4, kv4, win_rows, kwv, kwv, o_cmp, p)


def _rope_tables(pos):
    half = NSA_HEAD // 2
    inv = jnp.exp(-math.log(ROPE_THETA) * jnp.arange(half, dtype=F32) / half)
    ang = pos.astype(F32)[:, None] * inv[None, :]
    cos, sin = jnp.cos(ang), jnp.sin(ang)
    return jnp.concatenate([cos, cos], axis=1), jnp.concatenate([-sin, sin], axis=1)


def _shift_cols(p_last, dm):
    return jnp.concatenate([p_last[..., dm.c_r:dm.c_r + 3 * dm.rw], p_last[..., dm.c_lora:dm.c_lora + RWKV_LORA]],
                           axis=-1)


def _row(x):
    return x.reshape(1, -1)


def kernel(x_prompt, x_sample, cache_nsa_kv, page_table, state_nsa_window, state_wkv, state_rwkv_shift,
           state_pool, state_ffn_conv, norm_mix, w_in, pool_w, pool_scale, rwkv_mu, rwkv_w0, rwkv_w2,
           rwkv_a0, rwkv_a2, rwkv_g2, rwkv_k_k, rwkv_k_a, rwkv_r_k, rwkv_ln_w, rwkv_ln_b, w_branch, w_out,
           norm_ffn, ffn_up, ffn_conv, ffn_conv_b, ffn_down, norm_final):
    bp, sp, d = x_prompt.shape
    bs, ts, _ = x_sample.shape
    assert ts == 1
    depth = w_in.shape[0]
    dm = Dims(d)
    ff = dm.ff
    hd = NSA_HEAD
    page = cache_nsa_kv.shape[2]
    n_pages = page_table.shape[1]
    past = n_pages * page
    kvw = 4 * NSA_KV_HEADS * hd

    w_in_b = w_in.astype(BF16)
    w_gate_b = w_in_b[:, :, dm.o3:]
    pool_w_b = pool_w.astype(BF16)
    wb_b = w_branch.astype(BF16)
    wo_b = w_out.astype(BF16)
    up_b = ffn_up.astype(BF16)
    down_b = ffn_down.astype(BF16)
    g2_b = rwkv_g2.astype(BF16)
    kk_r = rwkv_k_k.reshape(depth, 1, dm.rw)
    ka_r = rwkv_k_a.reshape(depth, 1, dm.rw)
    rk_r = rwkv_r_k.reshape(depth, 1, dm.rw)

    mp = bp * sp
    tm_p = min(512, sp)
    tm_in = min(1024, sp)
    wide = lambda n: 1024 if n % 1024 == 0 else 512
    tn_ff = 512 if ff % 512 == 0 else 256
    tt_pool = min(512, sp)
    tt_rwkv = min(256, sp)
    tt_nsa = min(512, sp)
    tm_ff = min(1024, sp)
    tps = sp // tm_ff

    def rwkv_params(l):
        return (_row(rwkv_mu[l]), _row(rwkv_w0[l]), rwkv_w2[l], _row(rwkv_a0[l]), rwkv_a2[l], g2_b[l],
                kk_r[l], ka_r[l], rk_r[l])

    cos_p, sin_p = _rope_tables(jnp.arange(sp, dtype=jnp.int32))
    x = x_prompt.reshape(mp, d)
    zeros_hist16 = jnp.zeros((bp, POOL_HIST + 1, dm.pw), F32)
    zeros_shift = jnp.zeros((bp, 1, dm.rcols), F32)
    zeros_wkv = jnp.zeros((bp, dm.rh, RWKV_HEAD, RWKV_HEAD), F32)
    zeros_conv = jnp.zeros((bp, CONV_W - 1, 2 * ff), F32)
    wl = min(NSA_WINDOW, sp)
    p_kv, p_win, p_wkv, p_shift, p_pool, p_conv = [], [], [], [], [], []
    for l in range(depth):
        p = _norm_matmul(x, _row(norm_mix[l]), w_in_b, l, dm.np, tm_in, wide(dm.np)).reshape(bp, sp, dm.np)
        pg = _norm_matmul(x, _row(norm_mix[l]), w_gate_b, l, N_BRANCH * d, tm_in, wide(N_BRANCH * d))
        ya = _pool_prompt(p, zeros_hist16, pool_w_b[l], _row(pool_scale[l]), dm, tt_pool)
        prep = _rwkv_prep(p, zeros_shift, *rwkv_params(l), dm, tt_rwkv)
        yb, s_t = _rwkv_scan(prep, _row(rwkv_ln_w[l]), _row(rwkv_ln_b[l]), zeros_wkv, dm)
        qr, kv4, kwv, kcvc = _nsa_prep(p, cos_p, sin_p, dm, tt_nsa, True)
        yc = _nsa_prompt(qr, kv4, kwv, kcvc, p, dm)
        mrg = _branch_merge(ya.reshape(mp, -1), yb.reshape(mp, -1), yc.reshape(mp, -1), pg, wb_b, l, dm,
                            tm_p, min(1024, d))
        x = _out_proj(x, mrg, wo_b, l,tm_p, min(1024, d))
        x, za, zv = _ffn(x, _row(norm_ffn[l]), up_b, ffn_conv[l], _row(ffn_conv_b[l]), down_b, l,
                         zeros_conv, ff, tm_ff, tn_ff, tps, False)
        p_kv.append(kv4.reshape(bp, sp, 4, NSA_KV_HEADS, hd))
        p_win.append(kwv[:, sp - wl:].reshape(bp, wl, 2, NSA_KV_HEADS, hd))
        p_wkv.append(jnp.swapaxes(s_t, -1, -2))
        p_shift.append(_shift_cols(p[:, sp - 1:, :], dm))
        p_pool.append(p[:, sp - POOL_HIST:, dm.c_pool:dm.c_pool + dm.pw])
        zl = jnp.concatenate([za, zv], axis=-1).reshape(bp, tps, 2, 2 * ff)
        p_conv.append(zl[:, tps - 1])
    y_prompt = _final_norm(x, _row(norm_final), tm_p).reshape(bp, sp, d)

    pos_s = past
    cos_s, sin_s = _rope_tables(jnp.full((1,), pos_s, dtype=jnp.int32))
    xs = x_sample.reshape(bs, d)
    nbp = past // NSA_BLOCK
    n_sel = min(NSA_TOPK, nbp + 1)
    n_pool = cache_nsa_kv.shape[1]
    lwin = state_nsa_window.shape[2]
    cache_pairs = cache_nsa_kv.reshape(depth, n_pool, page, 2, 2 * NSA_KV_HEADS, hd)
    cache_rows = cache_nsa_kv.reshape(-1, hd)
    win_rows = state_nsa_window.reshape(-1, hd)
    s_kv, s_win, s_wkv, s_shift, s_pool, s_conv = [], [], [], [], [], []
    for l in range(depth):
        p = _norm_matmul(xs, _row(norm_mix[l]), w_in_b, l, dm.np, bs, wide(dm.np)).reshape(bs, 1, dm.np)
        pg = _norm_matmul(xs, _row(norm_mix[l]), w_gate_b, l, N_BRANCH * d, bs, wide(N_BRANCH * d))
        e16 = jnp.concatenate([state_pool[l], p[:, :, dm.c_pool:dm.c_pool + dm.pw]], axis=1)
        ya = _pool_sample(e16, pool_w_b[l], _row(pool_scale[l]), dm, pos_s)
        yb, s_new = _rwkv_step(p, state_rwkv_shift[l], state_wkv[l], *rwkv_params(l),
                               _row(rwkv_ln_w[l]), _row(rwkv_ln_b[l]), dm)
        qr, kv4, kwv = _nsa_prep(p, cos_s, sin_s, dm, 1, False)
        means = _page_means(cache_pairs, l, page_table)
        o_cmp, idx = _nsa_choose(qr, means, dm, n_sel)
        yc = _nsa_sample(qr, kv4, kwv, cache_rows, win_rows, l, n_pool, page, lwin, page_table, idx.reshape(-1),
                         o_cmp, p, dm, n_sel)
        mrg = _branch_merge(ya, yb.reshape(bs, -1), yc.reshape(bs, -1), pg, wb_b, l, dm, bs, min(1024, d))
        xs = _out_proj(xs, mrg, wo_b, l,bs, min(1024, d))
        xs, za, zv = _ffn(xs, _row(norm_ffn[l]), up_b, ffn_conv[l], _row(ffn_conv_b[l]), down_b, l,
                          state_ffn_conv[l], ff, bs, tn_ff, 1, True)
        s_kv.append(kv4.reshape(bs, 1, 4, NSA_KV_HEADS, hd))
        wk = jnp.concatenate([state_nsa_window[l], kwv.reshape(bs, 1, 2, NSA_KV_HEADS, hd)], axis=1)
        s_win.append(wk[:, -NSA_WINDOW:])
        s_wkv.append(s_new)
        s_shift.append(_shift_cols(p, dm))
        s_pool.append(e16[:, 1:])
        z_new = jnp.concatenate([za, zv], axis=-1)[:, None, :]
        s_conv.append(jnp.concatenate([state_ffn_conv[l][:, 1:], z_new], axis=1))
    y_sample = _final_norm(xs, _row(norm_final), bs).reshape(bs, 1, d)

    st = lambda xs_: jnp.stack(xs_)
    return (y_prompt, y_sample, st(p_kv), st(p_win), st(p_wkv), st(p_shift), st(p_pool), st(p_conv),
            st(s_kv), st(s_win), st(s_wkv), st(s_shift), st(s_pool), st(s_conv))
```

```python
import functools
import math

import jax
import jax.numpy as jnp
from jax import lax
from jax.experimental import pallas as pl
from jax.experimental.pallas import tpu as pltpu

F32 = jnp.float32
BF16 = jnp.bfloat16
HIGHEST = lax.Precision.HIGHEST

POOL_GROUPS = 4
POOL_WINDOWS = (2, 4, 8, 16)
POOL_HIST = 15
RWKV_HEAD = 64
RWKV_W_RANK = 64
RWKV_A_RANK = 64
RWKV_G_RANK = 128
RWKV_LORA = RWKV_W_RANK + RWKV_A_RANK + RWKV_G_RANK
RWKV_DECAY_SCALE = 0.6065306597126334
GN_EPS = 64e-5
NSA_HEAD = 128
NSA_KV_HEADS = 2
NSA_BLOCK = 64
NSA_TOPK = 16
NSA_WINDOW = 512
NSA_QTILE = 128
N_BRANCH = 3
CONV_W = 3
ROPE_THETA = 10000.0
RMS_EPS = 1e-6
NEG = -1e30
M_INIT = -1e29

LANES = 128
SUBLANES = 8
VMEM_LIMIT = 56 * 1024 * 1024
RWKV_CHUNK = 64
FFN_SUB_ROWS = 128


class Dims:
    def __init__(self, d_model):
        d = d_model
        self.d = d
        self.pw = d // 2
        self.cg = self.pw // POOL_GROUPS
        self.rw = d // 2
        self.rh = self.rw // RWKV_HEAD
        self.rcols = 3 * self.rw + RWKV_LORA
        self.nw = d // 2
        self.nh = self.nw // NSA_HEAD
        self.g = self.nh // NSA_KV_HEADS
        self.kvc = 6 * NSA_KV_HEADS * NSA_HEAD
        self.ngate = 3 * self.nh
        self.ff = 256 * ((8 * d // 3 + 255) // 256)
        self.o1 = self.pw
        self.o2 = self.o1 + self.rcols
        self.o3 = self.o2 + self.nw + self.kvc + self.ngate
        self.c_pool = 0
        self.c_r = self.o1
        self.c_lora = self.o1 + 3 * self.rw
        self.c_q = self.o2
        self.c_kv = self.o2 + self.nw
        self.c_ng = self.c_kv + self.kvc
        self.tn_in = 512
        self.np = -(-self.o3 // self.tn_in) * self.tn_in
        self.part = NSA_KV_HEADS * NSA_HEAD


def _blk(offset, width):
    assert offset % width == 0, (offset, width)
    return offset // width


def _cparams(sem):
    return pltpu.CompilerParams(dimension_semantics=sem, vmem_limit_bytes=VMEM_LIMIT)


def _dot(a, b):
    return jnp.dot(a.astype(BF16), b.astype(BF16), preferred_element_type=F32)


def _dot_nt(a, b):
    return lax.dot_general(a.astype(BF16), b.astype(BF16), (((1,), (1,)), ((), ())),
                           preferred_element_type=F32)


def _dot_hi(a, b):
    return jnp.dot(a, b, precision=HIGHEST, preferred_element_type=F32)


_NN = (((1,), (0,)), ((), ()))
_NT = (((1,), (1,)), ((), ()))
_TN = (((0,), (0,)), ((), ()))


def _split2(x):
    hi = x.astype(BF16)
    return hi, (x - hi.astype(F32)).astype(BF16)


def _mm3(a, b, dims):
    d = lambda p, q: lax.dot_general(p, q, dims, preferred_element_type=F32)
    return d(a[0], b[0]) + d(a[0], b[1]) + d(a[1], b[0])


def _split3(x):
    hi = x.astype(BF16)
    r1 = x - hi.astype(F32)
    mid = r1.astype(BF16)
    lo = (r1 - mid.astype(F32)).astype(BF16)
    return hi, mid, lo


def _dot_exact_lhs(m_bf16, x):
    hi, mid, lo = _split3(x)
    d = lambda p: jnp.dot(m_bf16, p, preferred_element_type=F32)
    return d(hi) + d(mid) + d(lo)


def _dot_exact_rhs(x, m_bf16):
    hi, mid, lo = _split3(x)
    d = lambda p: jnp.dot(p, m_bf16, preferred_element_type=F32)
    return d(hi) + d(mid) + d(lo)


def _head_ones():
    i = lax.broadcasted_iota(jnp.int32, (LANES, LANES), 0) // RWKV_HEAD
    j = lax.broadcasted_iota(jnp.int32, (LANES, LANES), 1) // RWKV_HEAD
    return (i == j).astype(BF16)


def _head_sum(x, ones):
    parts = [_dot_exact_rhs(x[:, c:c + LANES], ones) for c in range(0, x.shape[1], LANES)]
    return parts[0] if len(parts) == 1 else jnp.concatenate(parts, axis=1)


def _rmsnorm_val(x, g):
    ms = jnp.mean(x * x, axis=-1, keepdims=True)
    return x * lax.rsqrt(ms + RMS_EPS) * g


def _sigmoid(x):
    return 1.0 / (1.0 + jnp.exp(-x))


def _norm_matmul_kernel(x_ref, g_ref, w_ref, o_ref, h_ref):
    @pl.when(pl.program_id(1) == 0)
    def _():
        h_ref[...] = _rmsnorm_val(x_ref[...], g_ref[...]).astype(BF16)

    o_ref[...] = jnp.dot(h_ref[...], w_ref[...], preferred_element_type=F32)


def _norm_matmul(x, g, w, layer, n, tm, tn):
    m, d = x.shape
    assert n % tn == 0 and n <= w.shape[2]
    return pl.pallas_call(
        _norm_matmul_kernel,
        out_shape=jax.ShapeDtypeStruct((m, n), F32),
        grid=(m // tm, n // tn),
        in_specs=[pl.BlockSpec((tm, d), lambda i, j: (i, 0)),
                  pl.BlockSpec((1, d), lambda i, j: (0, 0)),
                  pl.BlockSpec((None, d, tn), lambda i, j: (layer, 0, j))],
        out_specs=pl.BlockSpec((tm, tn), lambda i, j: (i, j)),
        scratch_shapes=[pltpu.VMEM((tm, d), BF16)],
        compiler_params=_cparams(("parallel", "arbitrary")),
        name="norm_matmul",
    )(x, g, w)


def _branch_merge_kernel(ya_ref, yb_ref, yc_ref, ga_ref, gb_ref, gc_ref, w_ref, o_ref):
    acc = _dot(ya_ref[...], w_ref[0]) * _sigmoid(ga_ref[...])
    acc = acc + _dot(yb_ref[...], w_ref[1]) * _sigmoid(gb_ref[...])
    acc = acc + _dot(yc_ref[...], w_ref[2]) * _sigmoid(gc_ref[...])
    o_ref[...] = acc.astype(BF16)


def _branch_merge(ya, yb, yc, p, wb, layer, dm, tm, tn):
    m = ya.shape[0]
    d, hw = dm.d, dm.d // 2
    nb = d // tn
    yspec = pl.BlockSpec((tm, hw), lambda j, i: (i, 0))
    gspec = lambda k: pl.BlockSpec((tm, tn), lambda j, i, k=k: (i, k * nb + j))
    return pl.pallas_call(
        _branch_merge_kernel,
        out_shape=jax.ShapeDtypeStruct((m, d), BF16),
        grid=(nb, m // tm),
        in_specs=[yspec, yspec, yspec, gspec(0), gspec(1), gspec(2),
                  pl.BlockSpec((None, N_BRANCH, hw, tn), lambda j, i: (layer, 0, 0, j))],
        out_specs=pl.BlockSpec((tm, tn), lambda j, i: (i, j)),
        compiler_params=_cparams(("parallel", "parallel")),
        name="branch_merge",
    )(ya, yb, yc, p, p, p, wb)


def _out_proj_kernel(x_ref, m_ref, w_ref, o_ref):
    o_ref[...] = x_ref[...] + jnp.dot(m_ref[...], w_ref[...], preferred_element_type=F32)


def _out_proj(x, mrg, wo, layer, tm, tn):
    m, d = x.shape
    return pl.pallas_call(
        _out_proj_kernel,
        out_shape=jax.ShapeDtypeStruct((m, d), F32),
        grid=(m // tm, d // tn),
        in_specs=[pl.BlockSpec((tm, tn), lambda i, j: (i, j)),
                  pl.BlockSpec((tm, d), lambda i, j: (i, 0)),
                  pl.BlockSpec((None, d, tn), lambda i, j: (layer, 0, j))],
        out_specs=pl.BlockSpec((tm, tn), lambda i, j: (i, j)),
        compiler_params=_cparams(("parallel", "parallel")),
        name="out_proj",
    )(x, mrg, wo)


def _ffn_kernel(x_ref, g_ref, wa_ref, wv_ref, cwa_ref, cwv_ref, cba_ref, cbv_ref, wd_ref,
                ha_ref, hv_ref, o_ref, za_ref, zv_ref, h_ref, ca_ref, cv_ref, *, tm, sub, tps, rows_are_seqs):
    i = pl.program_id(0)
    j = pl.program_id(1)

    @pl.when(j == 0)
    def _():
        x = x_ref[...]
        h_ref[...] = _rmsnorm_val(x, g_ref[...]).astype(BF16)
        o_ref[...] = x

    def conv_mix(z, zs1, zs2, cw_ref, cb_ref):
        return zs2 * cw_ref[0:1] + zs1 * cw_ref[1:2] + z * cw_ref[2:3] + cb_ref[...]

    if rows_are_seqs:
        h = h_ref[...]
        za = jnp.dot(h, wa_ref[...], preferred_element_type=F32)
        zv = jnp.dot(h, wv_ref[...], preferred_element_type=F32)
        za_ref[...] = za
        zv_ref[...] = zv
        ca = conv_mix(za, ha_ref[:, 1, :], ha_ref[:, 0, :], cwa_ref, cba_ref)
        cv = conv_mix(zv, hv_ref[:, 1, :], hv_ref[:, 0, :], cwv_ref, cbv_ref)
        o_ref[...] += _dot(ca * _sigmoid(ca) * cv, wd_ref[...])
        return

    @pl.when((i % tps) == 0)
    def _():
        ca_ref[j] = ha_ref[...]
        cv_ref[j] = hv_ref[...]

    rs = [slice(s * sub, (s + 1) * sub) for s in range(tm // sub)]
    zas = [jnp.dot(h_ref[r], wa_ref[...], preferred_element_type=F32) for r in rs]
    zvs = [jnp.dot(h_ref[r], wv_ref[...], preferred_element_type=F32) for r in rs]
    row = lax.broadcasted_iota(jnp.int32, (sub, zas[0].shape[1]), 0)

    def conv(zs, s, carry_ref, cw_ref, cb_ref):
        z = zs[s]
        prev = carry_ref[j] if s == 0 else zs[s - 1][sub - 2:sub]
        zs1 = jnp.where(row == 0, prev[1:2], pltpu.roll(z, 1, axis=0))
        zs2 = jnp.where(row == 0, prev[0:1], jnp.where(row == 1, prev[1:2], pltpu.roll(z, 2, axis=0)))
        return conv_mix(z, zs1, zs2, cw_ref, cb_ref)

    for s, r in enumerate(rs):
        ca = conv(zas, s, ca_ref, cwa_ref, cba_ref)
        cv = conv(zvs, s, cv_ref, cwv_ref, cbv_ref)
        o_ref[r] += _dot(ca * _sigmoid(ca) * cv, wd_ref[...])
    za_last = zas[-1][sub - 2:sub]
    zv_last = zvs[-1][sub - 2:sub]
    ca_ref[j] = za_last
    cv_ref[j] = zv_last
    za_ref[...] = za_last
    zv_ref[...] = zv_last


def _ffn(x, g, w_up, cw, cb, w_down, layer, hist, ff, tm, tn, tps, rows_are_seqs):
    m, d = x.shape
    nj = ff // tn
    nm = m // tm
    if rows_are_seqs:
        assert nm == 1
        hspec_a = pl.BlockSpec((m, 2, tn), lambda i, j: (0, 0, j))
        hspec_v = pl.BlockSpec((m, 2, tn), lambda i, j: (0, 0, nj + j))
        zshape = jax.ShapeDtypeStruct((m, ff), F32)
        zspec = pl.BlockSpec((m, tn), lambda i, j: (0, j))
    else:
        hspec_a = pl.BlockSpec((None, 2, tn), lambda i, j: (i // tps, 0, j))
        hspec_v = pl.BlockSpec((None, 2, tn), lambda i, j: (i // tps, 0, nj + j))
        zshape = jax.ShapeDtypeStruct((nm, 2, ff), F32)
        zspec = pl.BlockSpec((None, 2, tn), lambda i, j: (i, 0, j))
    sub = FFN_SUB_ROWS if tm % FFN_SUB_ROWS == 0 else tm
    kern = functools.partial(_ffn_kernel, tm=tm, sub=sub, tps=tps, rows_are_seqs=rows_are_seqs)
    return pl.pallas_call(
        kern,
        out_shape=(jax.ShapeDtypeStruct((m, d), F32), zshape, zshape),
        grid=(nm, nj),
        in_specs=[pl.BlockSpec((tm, d), lambda i, j: (i, 0)),
                  pl.BlockSpec((1, d), lambda i, j: (0, 0)),
                  pl.BlockSpec((None, d, tn), lambda i, j: (layer, 0, j)),
                  pl.BlockSpec((None, d, tn), lambda i, j: (layer, 0, nj + j)),
                  pl.BlockSpec((CONV_W, tn), lambda i, j: (0, j)),
                  pl.BlockSpec((CONV_W, tn), lambda i, j: (0, nj + j)),
                  pl.BlockSpec((1, tn), lambda i, j: (0, j)),
                  pl.BlockSpec((1, tn), lambda i, j: (0, nj + j)),
                  pl.BlockSpec((None, tn, d), lambda i, j: (layer, j, 0)),
                  hspec_a, hspec_v],
        out_specs=(pl.BlockSpec((tm, d), lambda i, j: (i, 0)), zspec, zspec),
        scratch_shapes=[pltpu.VMEM((tm, d), BF16),
                        pltpu.VMEM((nj, 2, tn), F32),
                        pltpu.VMEM((nj, 2, tn), F32)],
        compiler_params=_cparams(("arbitrary", "arbitrary")),
        name="conv_ffn",
    )(x, g, w_up, w_up, cw, cw, cb, cb, w_down, hist, hist)


def _final_norm_kernel(x_ref, g_ref, o_ref):
    o_ref[...] = _rmsnorm_val(x_ref[...], g_ref[...])


def _final_norm(x, g, tm):
    m, d = x.shape
    return pl.pallas_call(
        _final_norm_kernel,
        out_shape=jax.ShapeDtypeStruct((m, d), F32),
        grid=(m // tm,),
        in_specs=[pl.BlockSpec((tm, d), lambda i: (i, 0)), pl.BlockSpec((1, d), lambda i: (0, 0))],
        out_specs=pl.BlockSpec((tm, d), lambda i: (i, 0)),
        compiler_params=_cparams(("parallel",)),
        name="final_norm",
    )(x, g)


def _pool_kernel(u_ref, hist_ref, w_ref, sc_ref, o_ref, e_ref, *, tt, cg):
    i = pl.program_id(1)
    hrows = POOL_HIST + 1

    @pl.when(i == 0)
    def _():
        e_ref[0:hrows] = hist_ref[...]

    @pl.when(i > 0)
    def _():
        e_ref[0:hrows] = e_ref[tt:tt + hrows]

    u = u_ref[...]
    e_ref[hrows:hrows + tt] = u
    pos = i * tt + lax.broadcasted_iota(jnp.int32, (tt, 1), 0)
    for gi, w in enumerate(POOL_WINDOWS):
        cs = slice(gi * cg, (gi + 1) * cg)
        s = e_ref[:, cs]
        sh = 1
        while sh < w:
            s = s + pltpu.roll(s, sh, axis=0)
            sh *= 2
        cnt = jnp.minimum(w, pos + 1).astype(F32)
        dlt = s[hrows:] / cnt - u[:, cs]
        o_ref[:, cs] = _dot(dlt, w_ref[gi]) * sc_ref[:, cs]


def _pool_prompt(p, hist16, w, scale, dm, tt):
    b, t, _ = p.shape
    kern = functools.partial(_pool_kernel, tt=tt, cg=dm.cg)
    return pl.pallas_call(
        kern,
        out_shape=jax.ShapeDtypeStruct((b, t, dm.pw), F32),
        grid=(b, t // tt),
        in_specs=[pl.BlockSpec((None, tt, dm.pw), lambda bi, i: (bi, i, _blk(dm.c_pool, dm.pw))),
                  pl.BlockSpec((None, POOL_HIST + 1, dm.pw), lambda bi, i: (bi, 0, 0)),
                  pl.BlockSpec((POOL_GROUPS, dm.cg, dm.cg), lambda bi, i: (0, 0, 0)),
                  pl.BlockSpec((1, dm.pw), lambda bi, i: (0, 0))],
        out_specs=pl.BlockSpec((None, tt, dm.pw), lambda bi, i: (bi, i, 0)),
        scratch_shapes=[pltpu.VMEM((tt + POOL_HIST + 1, dm.pw), F32)],
        compiler_params=_cparams(("parallel", "arbitrary")),
        name="pool_mixer",
    )(p, hist16, w, scale)


def _pool_sample_kernel(e_ref, w_ref, sc_ref, o_ref, *, cg, pos):
    e = e_ref[...]
    hrows = POOL_HIST + 1
    row = lax.broadcasted_iota(jnp.int32, e.shape, 1)
    u = e[:, hrows - 1, :]
    for gi, w in enumerate(POOL_WINDOWS):
        cs = slice(gi * cg, (gi + 1) * cg)
        win = jnp.sum(jnp.where(row >= hrows - w, e, 0.0)[:, :, cs], axis=1)
        dlt = win / float(min(w, pos + 1)) - u[:, cs]
        o_ref[:, cs] = _dot(dlt, w_ref[gi]) * sc_ref[:, cs]


def _pool_sample(e16, w, scale, dm, pos):
    b = e16.shape[0]
    kern = functools.partial(_pool_sample_kernel, cg=dm.cg, pos=pos)
    return pl.pallas_call(
        kern,
        out_shape=jax.ShapeDtypeStruct((b, dm.pw), F32),
        compiler_params=pltpu.CompilerParams(vmem_limit_bytes=VMEM_LIMIT),
        name="pool_mixer_step",
    )(e16, w, scale)


def _rwkv_mix_cols(x, xprev, mu):
    return x + (xprev - x) * mu


def _rwkv_lora(pl_, w0, w2, a0, a2, g2):
    w_in = pl_[:, 0:RWKV_W_RANK]
    a_in = pl_[:, RWKV_W_RANK:RWKV_W_RANK + RWKV_A_RANK]
    g_in = pl_[:, RWKV_W_RANK + RWKV_A_RANK:]
    lw = -RWKV_DECAY_SCALE * _sigmoid(w0 + _dot_hi(jnp.tanh(w_in), w2))
    a = _sigmoid(a0 + _dot_hi(a_in, a2))
    g = _dot(_sigmoid(g_in), g2)
    return lw, a, g


def _rwkv_prep_kernel(r_ref, k_ref, v_ref, l_ref, rp_ref, kp_ref, vp_ref, lp_ref, sh_ref, mu_ref,
                      w0_ref, w2_ref, a0_ref, a2_ref, g2_ref, kk_ref, ka_ref, rk_ref,
                      rt_ref, at_ref, kh_ref, bh_ref, kg_ref, bg_ref, vo_ref, bo_ref, go_ref, gc_ref,
                      *, tt, rw, chunk):
    i = pl.program_id(1)
    first = i == 0
    row1 = lax.broadcasted_iota(jnp.int32, (tt, 1), 0)

    def mixed(x_ref, xp_ref, c0, c1):
        x = x_ref[...]
        carry = jnp.where(first, sh_ref[:, c0:c1], xp_ref[SUBLANES - 1:SUBLANES, :])
        xprev = jnp.where(row1 == 0, carry, pltpu.roll(x, 1, axis=0))
        return _rwkv_mix_cols(x, xprev, mu_ref[:, c0:c1])

    r = mixed(r_ref, rp_ref, 0, rw)
    k = mixed(k_ref, kp_ref, rw, 2 * rw)
    v = mixed(v_ref, vp_ref, 2 * rw, 3 * rw)
    lo = mixed(l_ref, lp_ref, 3 * rw, 3 * rw + RWKV_LORA)
    lw, a, g = _rwkv_lora(lo, w0_ref[...], w2_ref[...], a0_ref[...], a2_ref[...], g2_ref[...])

    ones = _head_ones()
    kk = k * kk_ref[...]
    kk = kk * lax.rsqrt(jnp.maximum(_head_sum(kk * kk, ones), 1e-12))
    k2 = k * (1.0 + (a - 1.0) * ka_ref[...])
    bonus = _head_sum(r * k2 * rk_ref[...], ones) * v
    bvec = kk * a

    ti = lax.broadcasted_iota(jnp.int32, (tt, tt), 0)
    si = lax.broadcasted_iota(jnp.int32, (tt, tt), 1)
    same = (ti // chunk) == (si // chunk)
    tri = (same & (si <= ti)).astype(BF16)
    blk = same.astype(BF16)
    cum = _dot_exact_lhs(tri, lw)
    tot = _dot_exact_lhs(blk, lw)
    e_in = jnp.exp(cum)
    e_out = jnp.exp(-cum)
    e_rest = jnp.exp(tot - cum)
    rt_ref[...] = r * e_in
    at_ref[...] = -kk * jnp.exp(cum - lw)
    kh_ref[...] = k2 * e_out
    bh_ref[...] = bvec * e_out
    kg_ref[...] = k2 * e_rest
    bg_ref[...] = bvec * e_rest
    vo_ref[...] = v
    bo_ref[...] = bonus
    go_ref[...] = g
    etot = jnp.exp(tot)
    for c in range(tt // chunk):
        gc_ref[c] = etot[c * chunk:c * chunk + 1, :]


def _rwkv_prep(p, shift, mu, w0, w2, a0, a2, g2, k_k, k_a, r_k, dm, tt):
    b, t, _ = p.shape
    rw = dm.rw
    chunk = RWKV_CHUNK
    kern = functools.partial(_rwkv_prep_kernel, tt=tt, rw=rw, chunk=chunk)
    cur = lambda c0, w: pl.BlockSpec((None, tt, w), lambda bi, i: (bi, i, _blk(c0, w)))
    prv = lambda c0, w: pl.BlockSpec(
        (None, SUBLANES, w), lambda bi, i: (bi, jnp.maximum(i * (tt // SUBLANES) - 1, 0), _blk(c0, w)))
    full = lambda shp: pl.BlockSpec(shp, lambda bi, i: (0,) * len(shp))
    tok = jax.ShapeDtypeStruct((b, t, rw), F32)
    tspec = pl.BlockSpec((None, tt, rw), lambda bi, i: (bi, i, 0))
    return pl.pallas_call(
        kern,
        out_shape=(tok,) * 9 + (jax.ShapeDtypeStruct((b, t // chunk, 1, rw), F32),),
        grid=(b, t // tt),
        in_specs=[cur(dm.c_r, rw), cur(dm.c_r + rw, rw), cur(dm.c_r + 2 * rw, rw), cur(dm.c_lora, RWKV_LORA),
                  prv(dm.c_r, rw), prv(dm.c_r + rw, rw), prv(dm.c_r + 2 * rw, rw), prv(dm.c_lora, RWKV_LORA),
                  pl.BlockSpec((None, 1, dm.rcols), lambda bi, i: (bi, 0, 0)),
                  full((1, dm.rcols)), full((1, rw)), full((RWKV_W_RANK, rw)), full((1, rw)),
                  full((RWKV_A_RANK, rw)), full((RWKV_G_RANK, rw)), full((1, rw)), full((1, rw)), full((1, rw))],
        out_specs=(tspec,) * 9 + (pl.BlockSpec((None, tt // chunk, 1, rw), lambda bi, i: (bi, i, 0, 0)),),
        compiler_params=_cparams(("parallel", "parallel")),
        name="rwkv_prep",
    )(p, p, p, p, p, p, p, p, shift, mu, w0, w2, a0, a2, g2, k_k, k_a, r_k)


def _group_norm_out(y, bonus, g, lnw, lnb, ones):
    ym = _head_sum(y, ones) * (1.0 / RWKV_HEAD)
    d = y - ym
    yv = _head_sum(d * d, ones) * (1.0 / RWKV_HEAD)
    return (d * lax.rsqrt(yv + GN_EPS) * lnw + lnb + bonus) * g


def _rwkv_solve_kernel(at_ref, kh_ref, bh_ref, rt_ref, v_ref, kg_ref, bg_ref,
                       wt_ref, p1_ref, y1_ref, arb_ref, kbt_ref, *, chunk, pairs, group):
    n = RWKV_HEAD
    assert chunk == n and 2 * n == LANES
    row = lax.broadcasted_iota(jnp.int32, (chunk, LANES), 0)
    lane = lax.broadcasted_iota(jnp.int32, (chunk, LANES), 1)
    la = lane < n
    src = lane % n
    strict_a, strict_b = (src < row) & la, (src < row) & ~la
    incl_a, incl_b = (src <= row) & la, (src <= row) & ~la
    la2 = jnp.concatenate([la, la], axis=0)
    n_apply = int(math.log2(chunk))
    assert 2 ** n_apply == chunk
    w = lambda m, x: jnp.where(m, x, 0.0)
    stack = lambda a, b: jnp.concatenate([a, b], axis=0)

    for p0 in range(0, pairs, group):
        sls = [slice(p * LANES, (p + 1) * LANES) for p in range(p0, min(p0 + group, pairs))]
        ng = len(sls)
        at = [at_ref[:, sl] for sl in sls]
        v = [v_ref[:, sl] for sl in sls]
        a2 = [_split2(stack(at[i], rt_ref[:, sl])) for i, sl in enumerate(sls)]
        kh = [_split2(kh_ref[:, sl]) for sl in sls]
        bh = [_split2(bh_ref[:, sl]) for sl in sls]
        zero = jnp.zeros((2 * chunk, LANES), BF16)
        pa = [_mm3(tuple(jnp.where(la2, t, zero) for t in a2[i]),
                   tuple(stack(b, k) for b, k in zip(bh[i], kh[i])), _NT) for i in range(ng)]
        pb = [_mm3(tuple(jnp.where(la2, zero, t) for t in a2[i]),
                   tuple(stack(k, b) for b, k in zip(bh[i], kh[i])), _NT) for i in range(ng)]
        pw = [stack(w(strict_a, pa[i][:chunk]), w(strict_b, pb[i][:chunk])) for i in range(ng)]
        mak = [stack(w(strict_b, pa[i][:chunk]), w(strict_a, pb[i][:chunk])) for i in range(ng)]
        ark = [stack(w(incl_b, pa[i][chunk:]), w(incl_a, pb[i][chunk:])) for i in range(ng)]
        vsw = [pltpu.roll(v[i], n, axis=1) for i in range(ng)]
        vx = [_split2(stack(w(la, vsw[i]), w(~la, vsw[i]))) for i in range(ng)]
        kvy = [_mm3(_split2(stack(mak[i], ark[i])), vx[i], _NN) for i in range(ng)]
        for i, sl in enumerate(sls):
            arb_ref[:, sl] = stack(w(incl_a, pa[i][chunk:]), w(incl_b, pb[i][chunk:]))
            y1_ref[:, sl] = pltpu.roll(kvy[i][2 * chunk:3 * chunk] + kvy[i][3 * chunk:], n, axis=1)
            kbt_ref[:, sl] = stack(kg_ref[:, sl], bg_ref[:, sl]).T
        x = [stack(w(la, at[i]), w(~la, at[i])) + kvy[i][:2 * chunk] for i in range(ng)]
        for lvl in range(n_apply):
            if lvl + 1 < n_apply:
                r = [_mm3(_split2(pw[i]), _split2(jnp.concatenate([x[i], pw[i]], axis=1)), _NN) for i in range(ng)]
                x = [x[i] + r[i][:, :LANES] for i in range(ng)]
                pw = [r[i][:, LANES:] for i in range(ng)]
            else:
                x = [x[i] + _mm3(_split2(pw[i]), _split2(x[i]), _NN) for i in range(ng)]
        for i, sl in enumerate(sls):
            wt_ref[:, sl] = w(la, x[i][:chunk]) + w(~la, x[i][chunk:])
            p1_ref[:, sl] = pltpu.roll(w(~la, x[i][:chunk]) + w(la, x[i][chunk:]), n, axis=1)


def _rwkv_solve(at, kh, bh, rt, v, kg, bg, dm):
    b, t, rw = rt.shape
    chunk = RWKV_CHUNK
    pairs = dm.rh // 2
    kern = functools.partial(_rwkv_solve_kernel, chunk=chunk, pairs=pairs, group=min(4, pairs))
    tspec = pl.BlockSpec((None, chunk, rw), lambda bi, c: (bi, c, 0))
    sspec = pl.BlockSpec((None, 2 * chunk, rw), lambda bi, c: (bi, c, 0))
    tok = jax.ShapeDtypeStruct((b, t, rw), F32)
    stk = jax.ShapeDtypeStruct((b, 2 * t, rw), F32)
    return pl.pallas_call(
        kern,
        out_shape=(tok, tok, tok, stk, stk),
        grid=(b, t // chunk),
        in_specs=[tspec] * 7,
        out_specs=(tspec, tspec, tspec, sspec, sspec),
        compiler_params=_cparams(("parallel", "parallel")),
        name="rwkv_solve",
    )(at, kh, bh, rt, v, kg, bg)


def _rwkv_scan_kernel(wt_ref, rt_ref, p1_ref, y1_ref, arb_ref, kbt_ref, v_ref, bo_ref, g_ref, gc_ref,
                      lnw_ref, lnb_ref, s0_ref, y_ref, so_ref, st_ref, yacc_ref, *, chunk, pairs):
    c = pl.program_id(1)
    n = RWKV_HEAD
    row = lax.broadcasted_iota(jnp.int32, (LANES, LANES), 0)
    lane = lax.broadcasted_iota(jnp.int32, (LANES, LANES), 1)
    eye = row == lane
    same_head = (row < n) == (lane < n)
    la = lax.broadcasted_iota(jnp.int32, (chunk, LANES), 1) < n

    @pl.when(c == 0)
    def _():
        z = jnp.zeros((n, n), F32)
        for p in range(pairs):
            st_ref[p] = jnp.concatenate([jnp.concatenate([s0_ref[2 * p], z], axis=1),
                                         jnp.concatenate([z, s0_ref[2 * p + 1]], axis=1)], axis=0)

    sls = [slice(p * LANES, (p + 1) * LANES) for p in range(pairs)]
    s0 = [st_ref[p] for p in range(pairs)]
    ws = [_mm3(_split2(jnp.concatenate([wt_ref[:, sl], rt_ref[:, sl]], axis=0)), _split2(s0[p]), _NN)
          for p, sl in enumerate(sls)]
    u = [p1_ref[:, sl] + ws[p][:chunk] for p, sl in enumerate(sls)]
    for p, sl in enumerate(sls):
        ust = jnp.concatenate([jnp.where(la, u[p], 0.0), jnp.where(la, 0.0, u[p])], axis=0)
        yb = _mm3(_split2(arb_ref[:, sl]), _split2(ust), _NN)
        yacc_ref[:, sl] = ws[p][chunk:] + y1_ref[:, sl] + (yb[:chunk] + yb[chunk:])
    for p, sl in enumerate(sls):
        dg = jnp.where(eye, jnp.broadcast_to(gc_ref[:, sl], (LANES, LANES)), 0.0)
        lhs = jnp.concatenate([kbt_ref[:, sl], dg], axis=1)
        rhs = jnp.concatenate([v_ref[:, sl], u[p], s0[p]], axis=0)
        st_ref[p] = jnp.where(same_head, _mm3(_split2(lhs), _split2(rhs), _NN), 0.0)

    y_ref[...] = _group_norm_out(yacc_ref[...], bo_ref[...], g_ref[...], lnw_ref[...], lnb_ref[...], _head_ones())

    @pl.when(c == pl.num_programs(1) - 1)
    def _():
        for p in range(pairs):
            s = st_ref[p]
            so_ref[2 * p] = s[:n, :n]
            so_ref[2 * p + 1] = s[n:, n:]


def _rwkv_scan(prep, lnw, lnb, s0t, dm):
    rt, at, kh, bh, kg, bg, v, bonus, g, gc = prep
    wt, p1, y1, arb, kbt = _rwkv_solve(at, kh, bh, rt, v, kg, bg, dm)
    b, t, rw = rt.shape
    chunk = RWKV_CHUNK
    heads = dm.rh
    pairs = heads // 2
    kern = functools.partial(_rwkv_scan_kernel, chunk=chunk, pairs=pairs)
    tspec = pl.BlockSpec((None, chunk, rw), lambda bi, c: (bi, c, 0))
    kspec = pl.BlockSpec((None, 2 * chunk, rw), lambda bi, c: (bi, c, 0))
    sspec = pl.BlockSpec((None, heads, RWKV_HEAD, RWKV_HEAD), lambda bi, c: (bi, 0, 0, 0))
    row = pl.BlockSpec((1, rw), lambda bi, c: (0, 0))
    return pl.pallas_call(
        kern,
        out_shape=(jax.ShapeDtypeStruct((b, t, rw), F32),
                   jax.ShapeDtypeStruct((b, heads, RWKV_HEAD, RWKV_HEAD), F32)),
        grid=(b, t // chunk),
        in_specs=[tspec, tspec, tspec, tspec, kspec, kspec, tspec, tspec, tspec,
                  pl.BlockSpec((None, None, 1, rw), lambda bi, c: (bi, c, 0, 0)), row, row, sspec],
        out_specs=(tspec, sspec),
        scratch_shapes=[pltpu.VMEM((pairs, LANES, LANES), F32), pltpu.VMEM((chunk, rw), F32)],
        compiler_params=_cparams(("parallel", "arbitrary")),
        name="rwkv_scan",
    )(wt, rt, p1, y1, arb, kbt, v, bonus, g, gc, lnw, lnb, s0t)


def _rwkv_step_kernel(r_ref, k_ref, v_ref, l_ref, sh_ref, mu_ref, w0_ref, w2_ref, a0_ref, a2_ref, g2_ref,
                      kk_ref, ka_ref, rk_ref, lnw_ref, lnb_ref, s_ref, y_ref, so_ref, *, rw, heads):
    n = RWKV_HEAD
    rows = SUBLANES

    def mixed(x_ref, c0, c1):
        x = jnp.broadcast_to(x_ref[...], (rows, c1 - c0))
        return _rwkv_mix_cols(x, sh_ref[:, c0:c1], mu_ref[:, c0:c1])

    r = mixed(r_ref, 0, rw)
    k = mixed(k_ref, rw, 2 * rw)
    v = mixed(v_ref, 2 * rw, 3 * rw)
    lo = mixed(l_ref, 3 * rw, 3 * rw + RWKV_LORA)
    lw, a, g = _rwkv_lora(lo, w0_ref[...], w2_ref[...], a0_ref[...], a2_ref[...], g2_ref[...])
    w = jnp.exp(lw)
    kk = k * kk_ref[...]
    k2 = k * (1.0 + (a - 1.0) * ka_ref[...])
    eye = lax.broadcasted_iota(jnp.int32, (n, n), 0) == lax.broadcasted_iota(jnp.int32, (n, n), 1)

    def col(rowvec):
        return jnp.sum(jnp.where(eye, jnp.broadcast_to(rowvec, (n, n)), 0.0), axis=1, keepdims=True)

    outs = []
    for h in range(heads):
        sl = slice(h * n, (h + 1) * n)
        kkh = kk[0:1, sl]
        kkh = kkh * lax.rsqrt(jnp.maximum(jnp.sum(kkh * kkh, axis=1, keepdims=True), 1e-12))
        ah, wh, k2h, rh, vh = a[0:1, sl], w[0:1, sl], k2[0:1, sl], r[0:1, sl], v[0:1, sl]
        s = s_ref[h]
        sa = -jnp.sum(s * kkh, axis=1, keepdims=True)
        s = s * wh + sa * (kkh * ah) + col(vh) * k2h
        so_ref[h] = s
        ycol = jnp.sum(s * rh, axis=1, keepdims=True)
        yrow = jnp.sum(jnp.where(eye, jnp.broadcast_to(ycol, (n, n)), 0.0), axis=0, keepdims=True)
        ym = jnp.mean(yrow, axis=1, keepdims=True)
        d = yrow - ym
        yv = jnp.mean(d * d, axis=1, keepdims=True)
        yn = d * lax.rsqrt(yv + GN_EPS) * lnw_ref[:, sl] + lnb_ref[:, sl]
        bonus = jnp.sum(rh * k2h * rk_ref[:, sl], axis=1, keepdims=True) * vh
        outs.append((yn + bonus) * g[0:1, sl])
    y_ref[...] = jnp.concatenate(outs, axis=1)


def _rwkv_step(p, shift, s0, mu, w0, w2, a0, a2, g2, k_k, k_a, r_k, lnw, lnb, dm):
    b = p.shape[0]
    rw, heads = dm.rw, dm.rh
    kern = functools.partial(_rwkv_step_kernel, rw=rw, heads=heads)
    cur = lambda c0, w: pl.BlockSpec((None, 1, w), lambda bi: (bi, 0, _blk(c0, w)))
    full = lambda shp: pl.BlockSpec(shp, lambda bi: (0,) * len(shp))
    sspec = pl.BlockSpec((None, heads, RWKV_HEAD, RWKV_HEAD), lambda bi: (bi, 0, 0, 0))
    return pl.pallas_call(
        kern,
        out_shape=(jax.ShapeDtypeStruct((b, 1, rw), F32),
                   jax.ShapeDtypeStruct((b, heads, RWKV_HEAD, RWKV_HEAD), F32)),
        grid=(b,),
        in_specs=[cur(dm.c_r, rw), cur(dm.c_r + rw, rw), cur(dm.c_r + 2 * rw, rw), cur(dm.c_lora, RWKV_LORA),
                  pl.BlockSpec((None, 1, dm.rcols), lambda bi: (bi, 0, 0)),
                  full((1, dm.rcols)), full((1, rw)), full((RWKV_W_RANK, rw)), full((1, rw)),
                  full((RWKV_A_RANK, rw)), full((RWKV_G_RANK, rw)), full((1, rw)), full((1, rw)), full((1, rw)),
                  full((1, rw)), full((1, rw)), sspec],
        out_specs=(pl.BlockSpec((None, 1, rw), lambda bi: (bi, 0, 0)), sspec),
        compiler_params=_cparams(("parallel",)),
        name="rwkv_step",
    )(p, p, p, p, shift, mu, w0, w2, a0, a2, g2, k_k, k_a, r_k, lnw, lnb, s0)


def _rope_blocks(x, cos2, sin2):
    outs = []
    for c in range(0, x.shape[1], NSA_HEAD):
        xb = x[:, c:c + NSA_HEAD]
        outs.append(xb * cos2 + pltpu.roll(xb, NSA_HEAD // 2, axis=1) * sin2)
    return outs[0] if len(outs) == 1 else jnp.concatenate(outs, axis=1)


def _nsa_prep_kernel(*refs, tt, nq, with_means):
    q_refs, kv_refs = refs[:nq], refs[nq:nq + 6]
    cos_ref, sin_ref, qo_ref, kv_ref, kw_ref = refs[nq + 6:nq + 11]
    cos2, sin2 = cos_ref[...], sin_ref[...]
    pw = NSA_KV_HEADS * NSA_HEAD
    for idx, ref in enumerate(q_refs):
        qo_ref[:, idx * pw:(idx + 1) * pw] = (_rope_blocks(ref[...], cos2, sin2) * (NSA_HEAD ** -0.5)).astype(BF16)
    for idx, ref in enumerate(kv_refs):
        x = ref[...]
        if idx % 2 == 0:
            x = _rope_blocks(x, cos2, sin2)
        if idx < 4:
            kv_ref[:, idx * pw:(idx + 1) * pw] = x
        else:
            kw_ref[:, (idx - 4) * pw:(idx - 3) * pw] = x
        if idx < 2 and with_means:
            m_ref = refs[nq + 11]
            m_ref[:, idx * pw:(idx + 1) * pw] = (
                jnp.sum(x.reshape(tt // NSA_BLOCK, NSA_BLOCK, pw), axis=1) * (1.0 / NSA_BLOCK))


def _nsa_prep(p, cos2, sin2, dm, tt, with_means):
    b, t, _ = p.shape
    pw = dm.part
    nq = dm.nw // pw
    kern = functools.partial(_nsa_prep_kernel, tt=tt, nq=nq, with_means=with_means)
    cur = lambda c0: pl.BlockSpec((None, tt, pw), lambda bi, i: (bi, i, _blk(c0, pw)))
    outs = [jax.ShapeDtypeStruct((b, t, dm.nw), BF16), jax.ShapeDtypeStruct((b, t, 4 * pw), F32),
            jax.ShapeDtypeStruct((b, t, 2 * pw), F32)]
    ospecs = [pl.BlockSpec((None, tt, dm.nw), lambda bi, i: (bi, i, 0)),
              pl.BlockSpec((None, tt, 4 * pw), lambda bi, i: (bi, i, 0)),
              pl.BlockSpec((None, tt, 2 * pw), lambda bi, i: (bi, i, 0))]
    if with_means:
        outs.append(jax.ShapeDtypeStruct((b, t // NSA_BLOCK, 2 * pw), F32))
        ospecs.append(pl.BlockSpec((None, tt // NSA_BLOCK, 2 * pw), lambda bi, i: (bi, i, 0)))
    in_specs = [cur(dm.c_q + k * pw) for k in range(nq)] + [cur(dm.c_kv + k * pw) for k in range(6)]
    in_specs += [pl.BlockSpec((tt, NSA_HEAD), lambda bi, i: (i, 0))] * 2
    return pl.pallas_call(
        kern,
        out_shape=tuple(outs),
        grid=(b, t // tt),
        in_specs=in_specs,
        out_specs=tuple(ospecs),
        compiler_params=_cparams(("parallel", "parallel")),
        name="nsa_prep",
    )(*([p] * (nq + 6)), cos2, sin2)


def _nsa_prompt_kernel(q_ref, kc_ref, vc_ref, ks_ref, vs_ref, kw_ref, vw_ref, gt_ref, o_ref,
                       m_ref, l_ref, acc_ref, *, g, nb, n_sel, tk, lw):
    h = pl.program_id(1)
    i = pl.program_id(2)
    qb = NSA_QTILE
    rows = g * qb
    q = jnp.concatenate([q_ref[:, gi * NSA_HEAD:(gi + 1) * NSA_HEAD] for gi in range(g)], axis=0)
    gsl = [slice(gi * qb, (gi + 1) * qb) for gi in range(g)]
    qpos_c = i * qb + lax.broadcasted_iota(jnp.int32, (qb, 1), 0)
    qpos_r = i * qb + lax.broadcasted_iota(jnp.int32, (1, rows), 1) % qb

    blk = lax.broadcasted_iota(jnp.int32, (nb, 1), 0)
    st = _dot_nt(kc_ref[...], q)
    ok = ((blk + 1) * NSA_BLOCK - 1) <= qpos_r
    mc = jnp.max(jnp.where(ok, st, NEG), axis=0, keepdims=True)
    ec = jnp.where(ok, jnp.exp(st - mc), 0.0)
    den = jnp.sum(ec, axis=0, keepdims=True)
    p_c = ec / jnp.where(den > 0.0, den, 1.0)
    o_cmp = lax.dot_general(p_c.astype(BF16), vc_ref[...].astype(BF16), _TN, preferred_element_type=F32)

    imp = p_c[:, gsl[0]]
    for gi in range(1, g):
        imp = imp + p_c[:, gsl[gi]]
    qp1 = qpos_r[:, 0:qb]
    imp = jnp.where(blk == qp1 // NSA_BLOCK, g + 1.0, jnp.where(blk * NSA_BLOCK <= qp1, imp, -1.0))
    cnt = jnp.zeros((nb, qb), F32)
    for bi in range(nb):
        ci = imp[bi:bi + 1, :]
        beats = (ci > imp) | ((ci == imp) & (blk > bi))
        cnt = cnt + jnp.where(beats, 1.0, 0.0)
    sel_t = jnp.where(cnt < n_sel, 1.0, 0.0).astype(BF16)

    m_ref[...] = jnp.full((rows, 1), M_INIT, F32)
    l_ref[...] = jnp.zeros((rows, 1), F32)
    acc_ref[...] = jnp.zeros((rows, NSA_HEAD), F32)
    bpt = tk // NSA_BLOCK
    brow = lax.broadcasted_iota(jnp.int32, (nb, tk), 0)
    bcol = lax.broadcasted_iota(jnp.int32, (nb, tk), 1) // NSA_BLOCK
    kcol = lax.broadcasted_iota(jnp.int32, (1, tk), 1)
    n_kt = ((i + 1) * qb + tk - 1) // tk

    def scores(kt):
        k0 = pl.multiple_of(kt * tk, tk)
        sc = _dot_nt(q, ks_ref[pl.ds(k0, tk), :].astype(BF16))
        expand = (brow == bcol + kt * bpt).astype(BF16)
        picked = lax.dot_general(sel_t, expand, _TN, preferred_element_type=F32)
        return sc, jnp.where((picked > 0.5) & ((k0 + kcol) <= qpos_c), 0.0, NEG)

    def accumulate(kt, sc, bias):
        k0 = pl.multiple_of(kt * tk, tk)
        vb = vs_ref[pl.ds(k0, tk), :].astype(BF16)
        s_g = [sc[sl] + bias for sl in gsl]
        m_old = [m_ref[sl] for sl in gsl]
        m_new = [jnp.maximum(m_old[gi], jnp.max(s_g[gi], axis=1, keepdims=True)) for gi in range(g)]
        pr = [jnp.exp(s_g[gi] - m_new[gi]) for gi in range(g)]
        pv = [_dot(pr[gi], vb) for gi in range(g)]
        for gi, sl in enumerate(gsl):
            alpha = jnp.exp(m_old[gi] - m_new[gi])
            l_ref[sl] = alpha * l_ref[sl] + jnp.sum(pr[gi], axis=1, keepdims=True)
            acc_ref[sl] = alpha * acc_ref[sl] + pv[gi]
            m_ref[sl] = m_new[gi]

    def body(kt, carry):
        nxt = scores(kt + 1)
        accumulate(kt, *carry)
        return nxt

    accumulate(n_kt - 1, *lax.fori_loop(0, n_kt - 1, body, scores(0)))
    o_sel = acc_ref[...] / l_ref[...]

    w0 = pl.multiple_of(jnp.maximum(i * qb - NSA_WINDOW, 0), qb)
    kwb = kw_ref[pl.ds(w0, lw), :].astype(BF16)
    vwb = vw_ref[pl.ds(w0, lw), :].astype(BF16)
    dp = qpos_c - (w0 + lax.broadcasted_iota(jnp.int32, (1, lw), 1))
    bias_w = jnp.where((dp >= 0) & (dp <= NSA_WINDOW), 0.0, NEG)
    sw = _dot_nt(q, kwb)
    s_w = [sw[sl] + bias_w for sl in gsl]
    e_w = [jnp.exp(s_w[gi] - jnp.max(s_w[gi], axis=1, keepdims=True)) for gi in range(g)]
    o_win = jnp.concatenate([_dot(e_w[gi], vwb) / jnp.sum(e_w[gi], axis=1, keepdims=True) for gi in range(g)],
                            axis=0)

    gs = _sigmoid(gt_ref[...])
    for gi in range(g):
        acc = None
        for ci, ob in enumerate((o_cmp, o_sel, o_win)):
            c0 = gi * 3 + ci
            c1 = (g + gi) * 3 + ci
            gate = jnp.where(h == 0, gs[:, c0:c0 + 1], gs[:, c1:c1 + 1])
            term = ob[gi * qb:(gi + 1) * qb] * gate
            acc = term if acc is None else acc + term
        o_ref[:, gi * NSA_HEAD:(gi + 1) * NSA_HEAD] = acc


def _nsa_prompt(qr, kv4, kwv, kcvc, p, dm):
    b, t, _ = qr.shape
    assert NSA_KV_HEADS == 2
    g = dm.g
    nb = t // NSA_BLOCK
    n_sel = min(NSA_TOPK, nb)
    tk = min(512, t)
    lw = NSA_WINDOW + NSA_QTILE
    assert t % tk == 0 and t >= lw
    hd = NSA_HEAD
    kern = functools.partial(_nsa_prompt_kernel, g=g, nb=nb, n_sel=n_sel, tk=tk, lw=lw)
    kvspec = lambda c: pl.BlockSpec((None, t, hd), lambda bi, h, i, c=c: (bi, 0, 2 * c + h))
    rows = g * NSA_QTILE
    return pl.pallas_call(
        kern,
        out_shape=jax.ShapeDtypeStruct((b, t, dm.nw), F32),
        grid=(b, NSA_KV_HEADS, t // NSA_QTILE),
        in_specs=[pl.BlockSpec((None, NSA_QTILE, g * hd), lambda bi, h, i: (bi, i, h)),
                  pl.BlockSpec((None, nb, hd), lambda bi, h, i: (bi, 0, h)),
                  pl.BlockSpec((None, nb, hd), lambda bi, h, i: (bi, 0, 2 + h)),
                  kvspec(2), kvspec(3), kvspec(0), kvspec(1),
                  pl.BlockSpec((None, NSA_QTILE, LANES), lambda bi, h, i: (bi, i, _blk(dm.c_ng, LANES)))],
        out_specs=pl.BlockSpec((None, NSA_QTILE, g * hd), lambda bi, h, i: (bi, i, h)),
        scratch_shapes=[pltpu.VMEM((rows, 1), F32), pltpu.VMEM((rows, 1), F32), pltpu.VMEM((rows, hd), F32)],
        compiler_params=_cparams(("parallel", "parallel", "arbitrary")),
        name="nsa_prompt",
    )(qr, kcvc, kcvc, kv4, kv4, kwv, kwv, p)


def _page_means_kernel(pt_ref, *refs, ppb):
    o_ref = refs[ppb]
    for k in range(ppb):
        x = refs[k][...]
        nbp = x.shape[0] // NSA_BLOCK
        o_ref[k] = jnp.sum(x.reshape(nbp, NSA_BLOCK, x.shape[1], x.shape[2]), axis=1) * (1.0 / NSA_BLOCK)


def _page_means(cache, layer, page_table):
    _, _, page, _, prow, hd = cache.shape
    b, n_pages = page_table.shape
    nbp = page // NSA_BLOCK
    ppb = math.gcd(n_pages, 16)
    kern = functools.partial(_page_means_kernel, ppb=ppb)
    pspec = lambda k: pl.BlockSpec((None, None, page, None, prow, hd),
                                   lambda bi, j, pt, k=k: (layer, pt[bi, j * ppb + k], 0, 0, 0, 0))
    out = pl.pallas_call(
        kern,
        out_shape=jax.ShapeDtypeStruct((b, n_pages, nbp, prow, hd), F32),
        grid_spec=pltpu.PrefetchScalarGridSpec(
            num_scalar_prefetch=1,
            grid=(b, n_pages // ppb),
            in_specs=[pspec(k) for k in range(ppb)],
            out_specs=pl.BlockSpec((None, ppb, nbp, prow, hd), lambda bi, j, pt: (bi, j, 0, 0, 0)),
        ),
        compiler_params=_cparams(("parallel", "parallel")),
        name="nsa_page_means",
    )(page_table, *([cache] * ppb))
    return out.reshape(b, n_pages * nbp, prow * hd)


def _nsa_choose_kernel(q_ref, m_ref, oc_ref, idx_ref, *, g, nbp, n_sel):
    hd = NSA_HEAD
    blk = lax.broadcasted_iota(jnp.int32, (1, nbp), 1)
    lane = lax.broadcasted_iota(jnp.int32, (1, LANES), 1)
    for h in range(NSA_KV_HEADS):
        q = jnp.concatenate([q_ref[:, (h * g + gi) * hd:(h * g + gi + 1) * hd] for gi in range(g)], axis=0)
        q = jnp.concatenate([q, jnp.zeros((SUBLANES - g, hd), q.dtype)], axis=0) if g < SUBLANES else q
        kc = m_ref[:, h * hd:(h + 1) * hd]
        vc = m_ref[:, (NSA_KV_HEADS + h) * hd:(NSA_KV_HEADS + h + 1) * hd]
        s = _dot_nt(q, kc)
        m = jnp.max(s, axis=1, keepdims=True)
        e = jnp.exp(s - m)
        p_c = e / jnp.sum(e, axis=1, keepdims=True)
        oc_ref[h] = _dot(p_c, vc)[0:g]
        imp = jnp.sum(p_c[0:g], axis=0, keepdims=True)
        idx = jnp.where(lane == 0, nbp, 0)
        for it in range(1, n_sel):
            best = jnp.max(imp, axis=1, keepdims=True)
            j = jnp.min(jnp.where(imp == best, blk, nbp), axis=1, keepdims=True)
            idx = jnp.where(lane == it, j, idx)
            imp = jnp.where(blk == j, -2.0, imp)
        idx_ref[h] = idx


def _nsa_choose(qr, means, dm, n_sel):
    b = qr.shape[0]
    nbp = means.shape[1]
    g = dm.g
    kern = functools.partial(_nsa_choose_kernel, g=g, nbp=nbp, n_sel=n_sel)
    return pl.pallas_call(
        kern,
        out_shape=(jax.ShapeDtypeStruct((b, NSA_KV_HEADS, g, NSA_HEAD), F32),
                   jax.ShapeDtypeStruct((b, NSA_KV_HEADS, 1, LANES), jnp.int32)),
        grid=(b,),
        in_specs=[pl.BlockSpec((None, 1, dm.nw), lambda bi: (bi, 0, 0)),
                  pl.BlockSpec((None, nbp, means.shape[2]), lambda bi: (bi, 0, 0))],
        out_specs=(pl.BlockSpec((None, NSA_KV_HEADS, g, NSA_HEAD), lambda bi: (bi, 0, 0, 0)),
                   pl.BlockSpec((None, NSA_KV_HEADS, 1, LANES), lambda bi: (bi, 0, 0, 0))),
        compiler_params=_cparams(("parallel",)),
        name="nsa_choose",
    )(qr, means)


def _nsa_sample_kernel(idx_ref, pt_ref, q_ref, *refs, g, n_sel):
    cb_refs = refs[:n_sel - 1]
    kn_ref, vn_ref, wb_ref, kwn_ref, vwn_ref, oc_ref, gt_ref, o_ref = refs[n_sel - 1:]
    h = pl.program_id(1)
    hd = NSA_HEAD
    rows = SUBLANES
    kvh = NSA_KV_HEADS
    q = jnp.concatenate([q_ref[:, gi * hd:(gi + 1) * hd] for gi in range(g)], axis=0)
    if g < rows:
        q = jnp.concatenate([q, jnp.zeros((rows - g, hd), q.dtype)], axis=0)
    qf = q.astype(F32)

    m = jnp.sum(qf * kn_ref[...].astype(BF16).astype(F32), axis=1, keepdims=True)
    l = jnp.ones((rows, 1), F32)
    acc = jnp.broadcast_to(vn_ref[...].astype(BF16).astype(F32), (rows, hd))
    for cb_ref in cb_refs:
        xb = cb_ref[...].astype(BF16)
        srow = lax.broadcasted_iota(jnp.int32, (1, xb.shape[0]), 1) % (4 * kvh)
        sc = _dot_nt(q, xb) + jnp.where(srow == 2 * kvh + h, 0.0, NEG)
        m_new = jnp.maximum(m, jnp.max(sc, axis=1, keepdims=True))
        pr = jnp.exp(sc - m_new)
        alpha = jnp.exp(m - m_new)
        l = alpha * l + jnp.sum(pr, axis=1, keepdims=True)
        acc = alpha * acc + _dot(pltpu.roll(pr, kvh, axis=1), xb)
        m = m_new
    o_sel = acc / l

    xw = wb_ref[...].astype(BF16)
    wrow = lax.broadcasted_iota(jnp.int32, (1, xw.shape[0]), 1) % (2 * kvh)
    sw = _dot_nt(q, xw) + jnp.where(wrow == h, 0.0, NEG)
    sn = jnp.sum(qf * kwn_ref[...].astype(BF16).astype(F32), axis=1, keepdims=True)
    mw = jnp.maximum(jnp.max(sw, axis=1, keepdims=True), sn)
    ew = jnp.exp(sw - mw)
    en = jnp.exp(sn - mw)
    den = jnp.sum(ew, axis=1, keepdims=True) + en
    pn = (en / den).astype(BF16).astype(F32)
    o_win = _dot(pltpu.roll(ew / den, kvh, axis=1), xw) + pn * vwn_ref[...].astype(BF16).astype(F32)

    gs = _sigmoid(gt_ref[...])
    o_cmp = oc_ref[...]
    for gi in range(g):
        tot = None
        for ci, ob in enumerate((o_cmp, o_sel, o_win)):
            c0 = gi * 3 + ci
            c1 = (g + gi) * 3 + ci
            gate = jnp.where(h == 0, gs[:, c0:c0 + 1], gs[:, c1:c1 + 1])
            term = ob[gi:gi + 1] * gate
            tot = term if tot is None else tot + term
        o_ref[:, gi * hd:(gi + 1) * hd] = tot


def _nsa_sample(qr, kv4, kwv, cache_rows, win_rows, layer, n_pool, page, lwin, page_table, idx, o_cmp, p, dm, n_sel):
    b = qr.shape[0]
    g, hd = dm.g, NSA_HEAD
    bpp = page // NSA_BLOCK
    brows = NSA_BLOCK * 4 * NSA_KV_HEADS
    wrows = lwin * 2 * NSA_KV_HEADS
    kern = functools.partial(_nsa_sample_kernel, g=g, n_sel=n_sel)

    def sel_spec(slot):
        def index_map(bi, h, idx_ref, pt_ref):
            blk = idx_ref[(bi * NSA_KV_HEADS + h) * LANES + slot]
            return ((layer * n_pool + pt_ref[bi, blk // bpp]) * bpp + blk % bpp, 0)
        return pl.BlockSpec((brows, hd), index_map)

    newspec = lambda c: pl.BlockSpec((None, 1, hd), lambda bi, h, ir, pr, c=c: (bi, 0, 2 * c + h))
    in_specs = [pl.BlockSpec((None, 1, g * hd), lambda bi, h, ir, pr: (bi, 0, h))]
    in_specs += [sel_spec(slot) for slot in range(1, n_sel)]
    in_specs += [newspec(2), newspec(3),
                 pl.BlockSpec((wrows, hd), lambda bi, h, ir, pr: (layer * b + bi, 0)),
                 newspec(0), newspec(1),
                 pl.BlockSpec((None, None, g, hd), lambda bi, h, ir, pr: (bi, h, 0, 0)),
                 pl.BlockSpec((None, 1, LANES), lambda bi, h, ir, pr: (bi, 0, _blk(dm.c_ng, LANES)))]
    return pl.pallas_call(
        kern,
        out_shape=jax.ShapeDtypeStruct((b, 1, dm.nw), F32),
        grid_spec=pltpu.PrefetchScalarGridSpec(
            num_scalar_prefetch=2,
            grid=(b, NSA_KV_HEADS),
            in_specs=in_specs,
            out_specs=pl.BlockSpec((None, 1, g * hd), lambda bi, h, ir, pr: (bi, 0, h)),
        ),
        compiler_params=_cparams(("parallel", "parallel")),
        name="nsa_sample",
    )(idx, page_table, qr, *([cache_rows] * (n_sel - 1)), kv4, kv4, win_rows, kwv, kwv, o_cmp, p)


def _rope_tables(pos):
    half = NSA_HEAD // 2
    inv = jnp.exp(-math.log(ROPE_THETA) * jnp.arange(half, dtype=F32) / half)
    ang = pos.astype(F32)[:, None] * inv[None, :]
    cos, sin = jnp.cos(ang), jnp.sin(ang)
    return jnp.concatenate([cos, cos], axis=1), jnp.concatenate([-sin, sin], axis=1)


def _shift_cols(p_last, dm):
    return jnp.concatenate([p_last[..., dm.c_r:dm.c_r + 3 * dm.rw], p_last[..., dm.c_lora:dm.c_lora + RWKV_LORA]],
                           axis=-1)


def _row(x):
    return x.reshape(1, -1)


def kernel(x_prompt, x_sample, cache_nsa_kv, page_table, state_nsa_window, state_wkv, state_rwkv_shift,
           state_pool, state_ffn_conv, norm_mix, w_in, pool_w, pool_scale, rwkv_mu, rwkv_w0, rwkv_w2,
           rwkv_a0, rwkv_a2, rwkv_g2, rwkv_k_k, rwkv_k_a, rwkv_r_k, rwkv_ln_w, rwkv_ln_b, w_branch, w_out,
           norm_ffn, ffn_up, ffn_conv, ffn_conv_b, ffn_down, norm_final):
    bp, sp, d = x_prompt.shape
    bs, ts, _ = x_sample.shape
    assert ts == 1
    depth = w_in.shape[0]
    dm = Dims(d)
    ff = dm.ff
    hd = NSA_HEAD
    page = cache_nsa_kv.shape[2]
    n_pages = page_table.shape[1]
    past = n_pages * page
    kvw = 4 * NSA_KV_HEADS * hd

    w_in_b = w_in.astype(BF16)
    w_gate_b = w_in_b[:, :, dm.o3:]
    pool_w_b = pool_w.astype(BF16)
    wb_b = w_branch.astype(BF16)
    wo_b = w_out.astype(BF16)
    up_b = ffn_up.astype(BF16)
    down_b = ffn_down.astype(BF16)
    g2_b = rwkv_g2.astype(BF16)
    kk_r = rwkv_k_k.reshape(depth, 1, dm.rw)
    ka_r = rwkv_k_a.reshape(depth, 1, dm.rw)
    rk_r = rwkv_r_k.reshape(depth, 1, dm.rw)

    mp = bp * sp
    tm_p = min(512, sp)
    tm_in = min(1024, sp)
    wide = lambda n: 1024 if n % 1024 == 0 else 512
    tn_ff = 512 if ff % 512 == 0 else 256
    tt_pool = min(512, sp)
    tt_rwkv = min(256, sp)
    tt_nsa = min(512, sp)
    tm_ff = min(1024, sp)
    tps = sp // tm_ff

    def rwkv_params(l):
        return (_row(rwkv_mu[l]), _row(rwkv_w0[l]), rwkv_w2[l], _row(rwkv_a0[l]), rwkv_a2[l], g2_b[l],
                kk_r[l], ka_r[l], rk_r[l])

    cos_p, sin_p = _rope_tables(jnp.arange(sp, dtype=jnp.int32))
    x = x_prompt.reshape(mp, d)
    zeros_hist16 = jnp.zeros((bp, POOL_HIST + 1, dm.pw), F32)
    zeros_shift = jnp.zeros((bp, 1, dm.rcols), F32)
    zeros_wkv = jnp.zeros((bp, dm.rh, RWKV_HEAD, RWKV_HEAD), F32)
    zeros_conv = jnp.zeros((bp, CONV_W - 1, 2 * ff), F32)
    wl = min(NSA_WINDOW, sp)
    p_kv, p_win, p_wkv, p_shift, p_pool, p_conv = [], [], [], [], [], []
    for l in range(depth):
        p = _norm_matmul(x, _row(norm_mix[l]), w_in_b, l, dm.np, tm_in, wide(dm.np)).reshape(bp, sp, dm.np)
        pg = _norm_matmul(x, _row(norm_mix[l]), w_gate_b, l, N_BRANCH * d, tm_in, wide(N_BRANCH * d))
        ya = _pool_prompt(p, zeros_hist16, pool_w_b[l], _row(pool_scale[l]), dm, tt_pool)
        prep = _rwkv_prep(p, zeros_shift, *rwkv_params(l), dm, tt_rwkv)
        yb, s_t = _rwkv_scan(prep, _row(rwkv_ln_w[l]), _row(rwkv_ln_b[l]), zeros_wkv, dm)
        qr, kv4, kwv, kcvc = _nsa_prep(p, cos_p, sin_p, dm, tt_nsa, True)
        yc = _nsa_prompt(qr, kv4, kwv, kcvc, p, dm)
        mrg = _branch_merge(ya.reshape(mp, -1), yb.reshape(mp, -1), yc.reshape(mp, -1), pg, wb_b, l, dm,
                            tm_p, min(1024, d))
        x = _out_proj(x, mrg, wo_b, l,tm_p, min(1024, d))
        x, za, zv = _ffn(x, _row(norm_ffn[l]), up_b, ffn_conv[l], _row(ffn_conv_b[l]), down_b, l,
                         zeros_conv, ff, tm_ff, tn_ff, tps, False)
        p_kv.append(kv4.reshape(bp, sp, 4, NSA_KV_HEADS, hd))
        p_win.append(kwv[:, sp - wl:].reshape(bp, wl, 2, NSA_KV_HEADS, hd))
        p_wkv.append(jnp.swapaxes(s_t, -1, -2))
        p_shift.append(_shift_cols(p[:, sp - 1:, :], dm))
        p_pool.append(p[:, sp - POOL_HIST:, dm.c_pool:dm.c_pool + dm.pw])
        zl = jnp.concatenate([za, zv], axis=-1).reshape(bp, tps, 2, 2 * ff)
        p_conv.append(zl[:, tps - 1])
    y_prompt = _final_norm(x, _row(norm_final), tm_p).reshape(bp, sp, d)

    pos_s = past
    cos_s, sin_s = _rope_tables(jnp.full((1,), pos_s, dtype=jnp.int32))
    xs = x_sample.reshape(bs, d)
    nbp = past // NSA_BLOCK
    n_sel = min(NSA_TOPK, nbp + 1)
    n_pool = cache_nsa_kv.shape[1]
    lwin = state_nsa_window.shape[2]
    cache_pairs = cache_nsa_kv.reshape(depth, n_pool, page, 2, 2 * NSA_KV_HEADS, hd)
    cache_rows = cache_nsa_kv.reshape(-1, hd)
    win_rows = state_nsa_window.reshape(-1, hd)
    s_kv, s_win, s_wkv, s_shift, s_pool, s_conv = [], [], [], [], [], []
    for l in range(depth):
        p = _norm_matmul(xs, _row(norm_mix[l]), w_in_b, l, dm.np, bs, wide(dm.np)).reshape(bs, 1, dm.np)
        pg = _norm_matmul(xs, _row(norm_mix[l]), w_gate_b, l, N_BRANCH * d, bs, wide(N_BRANCH * d))
        e16 = jnp.concatenate([state_pool[l], p[:, :, dm.c_pool:dm.c_pool + dm.pw]], axis=1)
        ya = _pool_sample(e16, pool_w_b[l], _row(pool_scale[l]), dm, pos_s)
        yb, s_new = _rwkv_step(p, state_rwkv_shift[l], state_wkv[l], *rwkv_params(l),
                               _row(rwkv_ln_w[l]), _row(rwkv_ln_b[l]), dm)
        qr, kv4, kwv = _nsa_prep(p, cos_s, sin_s, dm, 1, False)
        means = _page_means(cache_pairs, l, page_table)
        o_cmp, idx = _nsa_choose(qr, means, dm, n_sel)
        yc = _nsa_sample(qr, kv4, kwv, cache_rows, win_rows, l, n_pool, page, lwin, page_table, idx.reshape(-1),
                         o_cmp, p, dm, n_sel)
        mrg = _branch_merge(ya, yb.reshape(bs, -1), yc.reshape(bs, -1), pg, wb_b, l, dm, bs, min(1024, d))
        xs = _out_proj(xs, mrg, wo_b, l,bs, min(1024, d))
        xs, za, zv = _ffn(xs, _row(norm_ffn[l]), up_b, ffn_conv[l], _row(ffn_conv_b[l]), down_b, l,
                          state_ffn_conv[l], ff, bs, tn_ff, 1, True)
        s_kv.append(kv4.reshape(bs, 1, 4, NSA_KV_HEADS, hd))
        wk = jnp.concatenate([state_nsa_window[l], kwv.reshape(bs, 1, 2, NSA_KV_HEADS, hd)], axis=1)
        s_win.append(wk[:, -NSA_WINDOW:])
        s_wkv.append(s_new)
        s_shift.append(_shift_cols(p, dm))
        s_pool.append(e16[:, 1:])
        z_new = jnp.concatenate([za, zv], axis=-1)[:, None, :]
        s_conv.append(jnp.concatenate([state_ffn_conv[l][:, 1:], z_new], axis=1))
    y_sample = _final_norm(xs, _row(norm_final), bs).reshape(bs, 1, d)

    st = lambda xs_: jnp.stack(xs_)
    return (y_prompt, y_sample, st(p_kv), st(p_win), st(p_wkv), st(p_shift), st(p_pool), st(p_conv),
            st(s_kv), st(s_win), st(s_wkv), st(s_shift), st(s_pool), st(s_conv))
```

```python
import functools
import math

import jax
import jax.numpy as jnp
from jax import lax
from jax.experimental import pallas as pl
from jax.experimental.pallas import tpu as pltpu

F32 = jnp.float32
BF16 = jnp.bfloat16
HIGHEST = lax.Precision.HIGHEST

POOL_GROUPS = 4
POOL_WINDOWS = (2, 4, 8, 16)
POOL_HIST = 15
RWKV_HEAD = 64
RWKV_W_RANK = 64
RWKV_A_RANK = 64
RWKV_G_RANK = 128
RWKV_LORA = RWKV_W_RANK + RWKV_A_RANK + RWKV_G_RANK
RWKV_DECAY_SCALE = 0.6065306597126334
GN_EPS = 64e-5
NSA_HEAD = 128
NSA_KV_HEADS = 2
NSA_BLOCK = 64
NSA_TOPK = 16
NSA_WINDOW = 512
NSA_QTILE = 128
N_BRANCH = 3
CONV_W = 3
ROPE_THETA = 10000.0
RMS_EPS = 1e-6
NEG = -1e30
M_INIT = -1e29

LANES = 128
SUBLANES = 8
VMEM_LIMIT = 56 * 1024 * 1024
RWKV_CHUNK = 64
FFN_SUB_ROWS = 128


class Dims:
    def __init__(self, d_model):
        d = d_model
        self.d = d
        self.pw = d // 2
        self.cg = self.pw // POOL_GROUPS
        self.rw = d // 2
        self.rh = self.rw // RWKV_HEAD
        self.rcols = 3 * self.rw + RWKV_LORA
        self.nw = d // 2
        self.nh = self.nw // NSA_HEAD
        self.g = self.nh // NSA_KV_HEADS
        self.kvc = 6 * NSA_KV_HEADS * NSA_HEAD
        self.ngate = 3 * self.nh
        self.ff = 256 * ((8 * d // 3 + 255) // 256)
        self.o1 = self.pw
        self.o2 = self.o1 + self.rcols
        self.o3 = self.o2 + self.nw + self.kvc + self.ngate
        self.c_pool = 0
        self.c_r = self.o1
        self.c_lora = self.o1 + 3 * self.rw
        self.c_q = self.o2
        self.c_kv = self.o2 + self.nw
        self.c_ng = self.c_kv + self.kvc
        self.tn_in = 512
        self.np = -(-self.o3 // self.tn_in) * self.tn_in
        self.part = NSA_KV_HEADS * NSA_HEAD


def _blk(offset, width):
    assert offset % width == 0, (offset, width)
    return offset // width


def _cparams(sem):
    return pltpu.CompilerParams(dimension_semantics=sem, vmem_limit_bytes=VMEM_LIMIT)


def _dot(a, b):
    return jnp.dot(a.astype(BF16), b.astype(BF16), preferred_element_type=F32)


def _dot_nt(a, b):
    return lax.dot_general(a.astype(BF16), b.astype(BF16), (((1,), (1,)), ((), ())),
                           preferred_element_type=F32)


def _dot_hi(a, b):
    return jnp.dot(a, b, precision=HIGHEST, preferred_element_type=F32)


_NN = (((1,), (0,)), ((), ()))
_NT = (((1,), (1,)), ((), ()))
_TN = (((0,), (0,)), ((), ()))


def _split2(x):
    hi = x.astype(BF16)
    return hi, (x - hi.astype(F32)).astype(BF16)


def _mm3(a, b, dims):
    d = lambda p, q: lax.dot_general(p, q, dims, preferred_element_type=F32)
    return d(a[0], b[0]) + d(a[0], b[1]) + d(a[1], b[0])


def _split3(x):
    hi = x.astype(BF16)
    r1 = x - hi.astype(F32)
    mid = r1.astype(BF16)
    lo = (r1 - mid.astype(F32)).astype(BF16)
    return hi, mid, lo


def _dot_exact_lhs(m_bf16, x):
    hi, mid, lo = _split3(x)
    d = lambda p: jnp.dot(m_bf16, p, preferred_element_type=F32)
    return d(hi) + d(mid) + d(lo)


def _dot_exact_rhs(x, m_bf16):
    hi, mid, lo = _split3(x)
    d = lambda p: jnp.dot(p, m_bf16, preferred_element_type=F32)
    return d(hi) + d(mid) + d(lo)


def _head_ones():
    i = lax.broadcasted_iota(jnp.int32, (LANES, LANES), 0) // RWKV_HEAD
    j = lax.broadcasted_iota(jnp.int32, (LANES, LANES), 1) // RWKV_HEAD
    return (i == j).astype(BF16)


def _head_sum(x, ones):
    parts = [_dot_exact_rhs(x[:, c:c + LANES], ones) for c in range(0, x.shape[1], LANES)]
    return parts[0] if len(parts) == 1 else jnp.concatenate(parts, axis=1)


def _rmsnorm_val(x, g):
    ms = jnp.mean(x * x, axis=-1, keepdims=True)
    return x * lax.rsqrt(ms + RMS_EPS) * g


def _sigmoid(x):
    return 1.0 / (1.0 + jnp.exp(-x))


def _norm_matmul_kernel(x_ref, g_ref, w_ref, o_ref, h_ref):
    @pl.when(pl.program_id(1) == 0)
    def _():
        h_ref[...] = _rmsnorm_val(x_ref[...], g_ref[...]).astype(BF16)

    o_ref[...] = jnp.dot(h_ref[...], w_ref[...], preferred_element_type=F32)


def _norm_matmul(x, g, w, layer, n, tm, tn):
    m, d = x.shape
    assert n % tn == 0 and n <= w.shape[2]
    return pl.pallas_call(
        _norm_matmul_kernel,
        out_shape=jax.ShapeDtypeStruct((m, n), F32),
        grid=(m // tm, n // tn),
        in_specs=[pl.BlockSpec((tm, d), lambda i, j: (i, 0)),
                  pl.BlockSpec((1, d), lambda i, j: (0, 0)),
                  pl.BlockSpec((None, d, tn), lambda i, j: (layer, 0, j))],
        out_specs=pl.BlockSpec((tm, tn), lambda i, j: (i, j)),
        scratch_shapes=[pltpu.VMEM((tm, d), BF16)],
        compiler_params=_cparams(("parallel", "arbitrary")),
        name="norm_matmul",
    )(x, g, w)


def _branch_merge_kernel(ya_ref, yb_ref, yc_ref, ga_ref, gb_ref, gc_ref, w_ref, o_ref):
    acc = _dot(ya_ref[...], w_ref[0]) * _sigmoid(ga_ref[...])
    acc = acc + _dot(yb_ref[...], w_ref[1]) * _sigmoid(gb_ref[...])
    acc = acc + _dot(yc_ref[...], w_ref[2]) * _sigmoid(gc_ref[...])
    o_ref[...] = acc.astype(BF16)


def _branch_merge(ya, yb, yc, p, wb, layer, dm, tm, tn):
    m = ya.shape[0]
    d, hw = dm.d, dm.d // 2
    nb = d // tn
    yspec = pl.BlockSpec((tm, hw), lambda j, i: (i, 0))
    gspec = lambda k: pl.BlockSpec((tm, tn), lambda j, i, k=k: (i, k * nb + j))
    return pl.pallas_call(
        _branch_merge_kernel,
        out_shape=jax.ShapeDtypeStruct((m, d), BF16),
        grid=(nb, m // tm),
        in_specs=[yspec, yspec, yspec, gspec(0), gspec(1), gspec(2),
                  pl.BlockSpec((None, N_BRANCH, hw, tn), lambda j, i: (layer, 0, 0, j))],
        out_specs=pl.BlockSpec((tm, tn), lambda j, i: (i, j)),
        compiler_params=_cparams(("parallel", "parallel")),
        name="branch_merge",
    )(ya, yb, yc, p, p, p, wb)


def _out_proj_kernel(x_ref, m_ref, w_ref, o_ref):
    o_ref[...] = x_ref[...] + jnp.dot(m_ref[...], w_ref[...], preferred_element_type=F32)


def _out_proj(x, mrg, wo, layer, tm, tn):
    m, d = x.shape
    return pl.pallas_call(
        _out_proj_kernel,
        out_shape=jax.ShapeDtypeStruct((m, d), F32),
        grid=(m // tm, d // tn),
        in_specs=[pl.BlockSpec((tm, tn), lambda i, j: (i, j)),
                  pl.BlockSpec((tm, d), lambda i, j: (i, 0)),
                  pl.BlockSpec((None, d, tn), lambda i, j: (layer, 0, j))],
        out_specs=pl.BlockSpec((tm, tn), lambda i, j: (i, j)),
        compiler_params=_cparams(("parallel", "parallel")),
        name="out_proj",
    )(x, mrg, wo)


def _ffn_kernel(x_ref, g_ref, wa_ref, wv_ref, cwa_ref, cwv_ref, cba_ref, cbv_ref, wd_ref,
                ha_ref, hv_ref, o_ref, za_ref, zv_ref, h_ref, ca_ref, cv_ref, *, tm, sub, tps, rows_are_seqs):
    i = pl.program_id(0)
    j = pl.program_id(1)

    @pl.when(j == 0)
    def _():
        x = x_ref[...]
        h_ref[...] = _rmsnorm_val(x, g_ref[...]).astype(BF16)
        o_ref[...] = x

    def conv_mix(z, zs1, zs2, cw_ref, cb_ref):
        return zs2 * cw_ref[0:1] + zs1 * cw_ref[1:2] + z * cw_ref[2:3] + cb_ref[...]

    if rows_are_seqs:
        h = h_ref[...]
        za = jnp.dot(h, wa_ref[...], preferred_element_type=F32)
        zv = jnp.dot(h, wv_ref[...], preferred_element_type=F32)
        za_ref[...] = za
        zv_ref[...] = zv
        ca = conv_mix(za, ha_ref[:, 1, :], ha_ref[:, 0, :], cwa_ref, cba_ref)
        cv = conv_mix(zv, hv_ref[:, 1, :], hv_ref[:, 0, :], cwv_ref, cbv_ref)
        o_ref[...] += _dot(ca * _sigmoid(ca) * cv, wd_ref[...])
        return

    @pl.when((i % tps) == 0)
    def _():
        ca_ref[j] = ha_ref[...]
        cv_ref[j] = hv_ref[...]

    rs = [slice(s * sub, (s + 1) * sub) for s in range(tm // sub)]
    zas = [jnp.dot(h_ref[r], wa_ref[...], preferred_element_type=F32) for r in rs]
    zvs = [jnp.dot(h_ref[r], wv_ref[...], preferred_element_type=F32) for r in rs]
    row = lax.broadcasted_iota(jnp.int32, (sub, zas[0].shape[1]), 0)

    def conv(zs, s, carry_ref, cw_ref, cb_ref):
        z = zs[s]
        prev = carry_ref[j] if s == 0 else zs[s - 1][sub - 2:sub]
        zs1 = jnp.where(row == 0, prev[1:2], pltpu.roll(z, 1, axis=0))
        zs2 = jnp.where(row == 0, prev[0:1], jnp.where(row == 1, prev[1:2], pltpu.roll(z, 2, axis=0)))
        return conv_mix(z, zs1, zs2, cw_ref, cb_ref)

    for s, r in enumerate(rs):
        ca = conv(zas, s, ca_ref, cwa_ref, cba_ref)
        cv = conv(zvs, s, cv_ref, cwv_ref, cbv_ref)
        o_ref[r] += _dot(ca * _sigmoid(ca) * cv, wd_ref[...])
    za_last = zas[-1][sub - 2:sub]
    zv_last = zvs[-1][sub - 2:sub]
    ca_ref[j] = za_last
    cv_ref[j] = zv_last
    za_ref[...] = za_last
    zv_ref[...] = zv_last


def _ffn(x, g, w_up, cw, cb, w_down, layer, hist, ff, tm, tn, tps, rows_are_seqs):
    m, d = x.shape
    nj = ff // tn
    nm = m // tm
    if rows_are_seqs:
        assert nm == 1
        hspec_a = pl.BlockSpec((m, 2, tn), lambda i, j: (0, 0, j))
        hspec_v = pl.BlockSpec((m, 2, tn), lambda i, j: (0, 0, nj + j))
        zshape = jax.ShapeDtypeStruct((m, ff), F32)
        zspec = pl.BlockSpec((m, tn), lambda i, j: (0, j))
    else:
        hspec_a = pl.BlockSpec((None, 2, tn), lambda i, j: (i // tps, 0, j))
        hspec_v = pl.BlockSpec((None, 2, tn), lambda i, j: (i // tps, 0, nj + j))
        zshape = jax.ShapeDtypeStruct((nm, 2, ff), F32)
        zspec = pl.BlockSpec((None, 2, tn), lambda i, j: (i, 0, j))
    sub = FFN_SUB_ROWS if tm % FFN_SUB_ROWS == 0 else tm
    kern = functools.partial(_ffn_kernel, tm=tm, sub=sub, tps=tps, rows_are_seqs=rows_are_seqs)
    return pl.pallas_call(
        kern,
        out_shape=(jax.ShapeDtypeStruct((m, d), F32), zshape, zshape),
        grid=(nm, nj),
        in_specs=[pl.BlockSpec((tm, d), lambda i, j: (i, 0)),
                  pl.BlockSpec((1, d), lambda i, j: (0, 0)),
                  pl.BlockSpec((None, d, tn), lambda i, j: (layer, 0, j)),
                  pl.BlockSpec((None, d, tn), lambda i, j: (layer, 0, nj + j)),
                  pl.BlockSpec((CONV_W, tn), lambda i, j: (0, j)),
                  pl.BlockSpec((CONV_W, tn), lambda i, j: (0, nj + j)),
                  pl.BlockSpec((1, tn), lambda i, j: (0, j)),
                  pl.BlockSpec((1, tn), lambda i, j: (0, nj + j)),
                  pl.BlockSpec((None, tn, d), lambda i, j: (layer, j, 0)),
                  hspec_a, hspec_v],
        out_specs=(pl.BlockSpec((tm, d), lambda i, j: (i, 0)), zspec, zspec),
        scratch_shapes=[pltpu.VMEM((tm, d), BF16),
                        pltpu.VMEM((nj, 2, tn), F32),
                        pltpu.VMEM((nj, 2, tn), F32)],
        compiler_params=_cparams(("arbitrary", "arbitrary")),
        name="conv_ffn",
    )(x, g, w_up, w_up, cw, cw, cb, cb, w_down, hist, hist)


def _final_norm_kernel(x_ref, g_ref, o_ref):
    o_ref[...] = _rmsnorm_val(x_ref[...], g_ref[...])


def _final_norm(x, g, tm):
    m, d = x.shape
    return pl.pallas_call(
        _final_norm_kernel,
        out_shape=jax.ShapeDtypeStruct((m, d), F32),
        grid=(m // tm,),
        in_specs=[pl.BlockSpec((tm, d), lambda i: (i, 0)), pl.BlockSpec((1, d), lambda i: (0, 0))],
        out_specs=pl.BlockSpec((tm, d), lambda i: (i, 0)),
        compiler_params=_cparams(("parallel",)),
        name="final_norm",
    )(x, g)


def _pool_kernel(u_ref, hist_ref, w_ref, sc_ref, o_ref, e_ref, *, tt, cg):
    i = pl.program_id(1)
    hrows = POOL_HIST + 1

    @pl.when(i == 0)
    def _():
        e_ref[0:hrows] = hist_ref[...]

    @pl.when(i > 0)
    def _():
        e_ref[0:hrows] = e_ref[tt:tt + hrows]

    u = u_ref[...]
    e_ref[hrows:hrows + tt] = u
    pos = i * tt + lax.broadcasted_iota(jnp.int32, (tt, 1), 0)
    for gi, w in enumerate(POOL_WINDOWS):
        cs = slice(gi * cg, (gi + 1) * cg)
        s = e_ref[:, cs]
        sh = 1
        while sh < w:
            s = s + pltpu.roll(s, sh, axis=0)
            sh *= 2
        cnt = jnp.minimum(w, pos + 1).astype(F32)
        dlt = s[hrows:] / cnt - u[:, cs]
        o_ref[:, cs] = _dot(dlt, w_ref[gi]) * sc_ref[:, cs]


def _pool_prompt(p, hist16, w, scale, dm, tt):
    b, t, _ = p.shape
    kern = functools.partial(_pool_kernel, tt=tt, cg=dm.cg)
    return pl.pallas_call(
        kern,
        out_shape=jax.ShapeDtypeStruct((b, t, dm.pw), F32),
        grid=(b, t // tt),
        in_specs=[pl.BlockSpec((None, tt, dm.pw), lambda bi, i: (bi, i, _blk(dm.c_pool, dm.pw))),
                  pl.BlockSpec((None, POOL_HIST + 1, dm.pw), lambda bi, i: (bi, 0, 0)),
                  pl.BlockSpec((POOL_GROUPS, dm.cg, dm.cg), lambda bi, i: (0, 0, 0)),
                  pl.BlockSpec((1, dm.pw), lambda bi, i: (0, 0))],
        out_specs=pl.BlockSpec((None, tt, dm.pw), lambda bi, i: (bi, i, 0)),
        scratch_shapes=[pltpu.VMEM((tt + POOL_HIST + 1, dm.pw), F32)],
        compiler_params=_cparams(("parallel", "arbitrary")),
        name="pool_mixer",
    )(p, hist16, w, scale)


def _pool_sample_kernel(e_ref, w_ref, sc_ref, o_ref, *, cg, pos):
    e = e_ref[...]
    hrows = POOL_HIST + 1
    row = lax.broadcasted_iota(jnp.int32, e.shape, 1)
    u = e[:, hrows - 1, :]
    for gi, w in enumerate(POOL_WINDOWS):
        cs = slice(gi * cg, (gi + 1) * cg)
        win = jnp.sum(jnp.where(row >= hrows - w, e, 0.0)[:, :, cs], axis=1)
        dlt = win / float(min(w, pos + 1)) - u[:, cs]
        o_ref[:, cs] = _dot(dlt, w_ref[gi]) * sc_ref[:, cs]


def _pool_sample(e16, w, scale, dm, pos):
    b = e16.shape[0]
    kern = functools.partial(_pool_sample_kernel, cg=dm.cg, pos=pos)
    return pl.pallas_call(
        kern,
        out_shape=jax.ShapeDtypeStruct((b, dm.pw), F32),
        compiler_params=pltpu.CompilerParams(vmem_limit_bytes=VMEM_LIMIT),
        name="pool_mixer_step",
    )(e16, w, scale)


def _rwkv_mix_cols(x, xprev, mu):
    return x + (xprev - x) * mu


def _rwkv_lora(pl_, w0, w2, a0, a2, g2):
    w_in = pl_[:, 0:RWKV_W_RANK]
    a_in = pl_[:, RWKV_W_RANK:RWKV_W_RANK + RWKV_A_RANK]
    g_in = pl_[:, RWKV_W_RANK + RWKV_A_RANK:]
    lw = -RWKV_DECAY_SCALE * _sigmoid(w0 + _dot_hi(jnp.tanh(w_in), w2))
    a = _sigmoid(a0 + _dot_hi(a_in, a2))
    g = _dot(_sigmoid(g_in), g2)
    return lw, a, g


def _rwkv_prep_kernel(r_ref, k_ref, v_ref, l_ref, rp_ref, kp_ref, vp_ref, lp_ref, sh_ref, mu_ref,
                      w0_ref, w2_ref, a0_ref, a2_ref, g2_ref, kk_ref, ka_ref, rk_ref,
                      rt_ref, at_ref, kh_ref, bh_ref, kg_ref, bg_ref, vo_ref, bo_ref, go_ref, gc_ref,
                      *, tt, rw, chunk):
    i = pl.program_id(1)
    first = i == 0
    row1 = lax.broadcasted_iota(jnp.int32, (tt, 1), 0)

    def mixed(x_ref, xp_ref, c0, c1):
        x = x_ref[...]
        carry = jnp.where(first, sh_ref[:, c0:c1], xp_ref[SUBLANES - 1:SUBLANES, :])
        xprev = jnp.where(row1 == 0, carry, pltpu.roll(x, 1, axis=0))
        return _rwkv_mix_cols(x, xprev, mu_ref[:, c0:c1])

    r = mixed(r_ref, rp_ref, 0, rw)
    k = mixed(k_ref, kp_ref, rw, 2 * rw)
    v = mixed(v_ref, vp_ref, 2 * rw, 3 * rw)
    lo = mixed(l_ref, lp_ref, 3 * rw, 3 * rw + RWKV_LORA)
    lw, a, g = _rwkv_lora(lo, w0_ref[...], w2_ref[...], a0_ref[...], a2_ref[...], g2_ref[...])

    ones = _head_ones()
    kk = k * kk_ref[...]
    kk = kk * lax.rsqrt(jnp.maximum(_head_sum(kk * kk, ones), 1e-12))
    k2 = k * (1.0 + (a - 1.0) * ka_ref[...])
    bonus = _head_sum(r * k2 * rk_ref[...], ones) * v
    bvec = kk * a

    ti = lax.broadcasted_iota(jnp.int32, (tt, tt), 0)
    si = lax.broadcasted_iota(jnp.int32, (tt, tt), 1)
    same = (ti // chunk) == (si // chunk)
    tri = (same & (si <= ti)).astype(BF16)
    blk = same.astype(BF16)
    cum = _dot_exact_lhs(tri, lw)
    tot = _dot_exact_lhs(blk, lw)
    e_in = jnp.exp(cum)
    e_out = jnp.exp(-cum)
    e_rest = jnp.exp(tot - cum)
    rt_ref[...] = r * e_in
    at_ref[...] = -kk * jnp.exp(cum - lw)
    kh_ref[...] = k2 * e_out
    bh_ref[...] = bvec * e_out
    kg_ref[...] = k2 * e_rest
    bg_ref[...] = bvec * e_rest
    vo_ref[...] = v
    bo_ref[...] = bonus
    go_ref[...] = g
    etot = jnp.exp(tot)
    for c in range(tt // chunk):
        gc_ref[c] = etot[c * chunk:c * chunk + 1, :]


def _rwkv_prep(p, shift, mu, w0, w2, a0, a2, g2, k_k, k_a, r_k, dm, tt):
    b, t, _ = p.shape
    rw = dm.rw
    chunk = RWKV_CHUNK
    kern = functools.partial(_rwkv_prep_kernel, tt=tt, rw=rw, chunk=chunk)
    cur = lambda c0, w: pl.BlockSpec((None, tt, w), lambda bi, i: (bi, i, _blk(c0, w)))
    prv = lambda c0, w: pl.BlockSpec(
        (None, SUBLANES, w), lambda bi, i: (bi, jnp.maximum(i * (tt // SUBLANES) - 1, 0), _blk(c0, w)))
    full = lambda shp: pl.BlockSpec(shp, lambda bi, i: (0,) * len(shp))
    tok = jax.ShapeDtypeStruct((b, t, rw), F32)
    tspec = pl.BlockSpec((None, tt, rw), lambda bi, i: (bi, i, 0))
    return pl.pallas_call(
        kern,
        out_shape=(tok,) * 9 + (jax.ShapeDtypeStruct((b, t // chunk, 1, rw), F32),),
        grid=(b, t // tt),
        in_specs=[cur(dm.c_r, rw), cur(dm.c_r + rw, rw), cur(dm.c_r + 2 * rw, rw), cur(dm.c_lora, RWKV_LORA),
                  prv(dm.c_r, rw), prv(dm.c_r + rw, rw), prv(dm.c_r + 2 * rw, rw), prv(dm.c_lora, RWKV_LORA),
                  pl.BlockSpec((None, 1, dm.rcols), lambda bi, i: (bi, 0, 0)),
                  full((1, dm.rcols)), full((1, rw)), full((RWKV_W_RANK, rw)), full((1, rw)),
                  full((RWKV_A_RANK, rw)), full((RWKV_G_RANK, rw)), full((1, rw)), full((1, rw)), full((1, rw))],
        out_specs=(tspec,) * 9 + (pl.BlockSpec((None, tt // chunk, 1, rw), lambda bi, i: (bi, i, 0, 0)),),
        compiler_params=_cparams(("parallel", "parallel")),
        name="rwkv_prep",
    )(p, p, p, p, p, p, p, p, shift, mu, w0, w2, a0, a2, g2, k_k, k_a, r_k)


def _group_norm_out(y, bonus, g, lnw, lnb, ones):
    ym = _head_sum(y, ones) * (1.0 / RWKV_HEAD)
    d = y - ym
    yv = _head_sum(d * d, ones) * (1.0 / RWKV_HEAD)
    return (d * lax.rsqrt(yv + GN_EPS) * lnw + lnb + bonus) * g


def _rwkv_solve_kernel(at_ref, kh_ref, bh_ref, rt_ref, v_ref, kg_ref, bg_ref,
                       wt_ref, p1_ref, y1_ref, arb_ref, kbt_ref, *, chunk, pairs, group):
    n = RWKV_HEAD
    assert chunk == n and 2 * n == LANES
    row = lax.broadcasted_iota(jnp.int32, (chunk, LANES), 0)
    lane = lax.broadcasted_iota(jnp.int32, (chunk, LANES), 1)
    la = lane < n
    src = lane % n
    strict_a, strict_b = (src < row) & la, (src < row) & ~la
    incl_a, incl_b = (src <= row) & la, (src <= row) & ~la
    la2 = jnp.concatenate([la, la], axis=0)
    n_apply = int(math.log2(chunk))
    assert 2 ** n_apply == chunk
    w = lambda m, x: jnp.where(m, x, 0.0)
    stack = lambda a, b: jnp.concatenate([a, b], axis=0)

    for p0 in range(0, pairs, group):
        sls = [slice(p * LANES, (p + 1) * LANES) for p in range(p0, min(p0 + group, pairs))]
        ng = len(sls)
        at = [at_ref[:, sl] for sl in sls]
        v = [v_ref[:, sl] for sl in sls]
        a2 = [_split2(stack(at[i], rt_ref[:, sl])) for i, sl in enumerate(sls)]
        kh = [_split2(kh_ref[:, sl]) for sl in sls]
        bh = [_split2(bh_ref[:, sl]) for sl in sls]
        zero = jnp.zeros((2 * chunk, LANES), BF16)
        pa = [_mm3(tuple(jnp.where(la2, t, zero) for t in a2[i]),
                   tuple(stack(b, k) for b, k in zip(bh[i], kh[i])), _NT) for i in range(ng)]
        pb = [_mm3(tuple(jnp.where(la2, zero, t) for t in a2[i]),
                   tuple(stack(k, b) for b, k in zip(bh[i], kh[i])), _NT) for i in range(ng)]
        pw = [stack(w(strict_a, pa[i][:chunk]), w(strict_b, pb[i][:chunk])) for i in range(ng)]
        mak = [stack(w(strict_b, pa[i][:chunk]), w(strict_a, pb[i][:chunk])) for i in range(ng)]
        ark = [stack(w(incl_b, pa[i][chunk:]), w(incl_a, pb[i][chunk:])) for i in range(ng)]
        vsw = [pltpu.roll(v[i], n, axis=1) for i in range(ng)]
        vx = [_split2(stack(w(la, vsw[i]), w(~la, vsw[i]))) for i in range(ng)]
        kvy = [_mm3(_split2(stack(mak[i], ark[i])), vx[i], _NN) for i in range(ng)]
        for i, sl in enumerate(sls):
            arb_ref[:, sl] = stack(w(incl_a, pa[i][chunk:]), w(incl_b, pb[i][chunk:]))
            y1_ref[:, sl] = pltpu.roll(kvy[i][2 * chunk:3 * chunk] + kvy[i][3 * chunk:], n, axis=1)
            kbt_ref[:, sl] = stack(kg_ref[:, sl], bg_ref[:, sl]).T
        x = [stack(w(la, at[i]), w(~la, at[i])) + kvy[i][:2 * chunk] for i in range(ng)]
        for lvl in range(n_apply):
            if lvl + 1 < n_apply:
                r = [_mm3(_split2(pw[i]), _split2(jnp.concatenate([x[i], pw[i]], axis=1)), _NN) for i in range(ng)]
                x = [x[i] + r[i][:, :LANES] for i in range(ng)]
                pw = [r[i][:, LANES:] for i in range(ng)]
            else:
                x = [x[i] + _mm3(_split2(pw[i]), _split2(x[i]), _NN) for i in range(ng)]
        for i, sl in enumerate(sls):
            wt_ref[:, sl] = w(la, x[i][:chunk]) + w(~la, x[i][chunk:])
            p1_ref[:, sl] = pltpu.roll(w(~la, x[i][:chunk]) + w(la, x[i][chunk:]), n, axis=1)


def _rwkv_solve(at, kh, bh, rt, v, kg, bg, dm):
    b, t, rw = rt.shape
    chunk = RWKV_CHUNK
    pairs = dm.rh // 2
    kern = functools.partial(_rwkv_solve_kernel, chunk=chunk, pairs=pairs, group=min(4, pairs))
    tspec = pl.BlockSpec((None, chunk, rw), lambda bi, c: (bi, c, 0))
    sspec = pl.BlockSpec((None, 2 * chunk, rw), lambda bi, c: (bi, c, 0))
    tok = jax.ShapeDtypeStruct((b, t, rw), F32)
    stk = jax.ShapeDtypeStruct((b, 2 * t, rw), F32)
    return pl.pallas_call(
        kern,
        out_shape=(tok, tok, tok, stk, stk),
        grid=(b, t // chunk),
        in_specs=[tspec] * 7,
        out_specs=(tspec, tspec, tspec, sspec, sspec),
        compiler_params=_cparams(("parallel", "parallel")),
        name="rwkv_solve",
    )(at, kh, bh, rt, v, kg, bg)


def _rwkv_scan_kernel(wt_ref, rt_ref, p1_ref, y1_ref, arb_ref, kbt_ref, v_ref, bo_ref, g_ref, gc_ref,
                      lnw_ref, lnb_ref, s0_ref, y_ref, so_ref, st_ref, yacc_ref, *, chunk, pairs):
    c = pl.program_id(1)
    n = RWKV_HEAD
    row = lax.broadcasted_iota(jnp.int32, (LANES, LANES), 0)
    lane = lax.broadcasted_iota(jnp.int32, (LANES, LANES), 1)
    eye = row == lane
    same_head = (row < n) == (lane < n)
    la = lax.broadcasted_iota(jnp.int32, (chunk, LANES), 1) < n

    @pl.when(c == 0)
    def _():
        z = jnp.zeros((n, n), F32)
        for p in range(pairs):
            st_ref[p] = jnp.concatenate([jnp.concatenate([s0_ref[2 * p], z], axis=1),
                                         jnp.concatenate([z, s0_ref[2 * p + 1]], axis=1)], axis=0)

    sls = [slice(p * LANES, (p + 1) * LANES) for p in range(pairs)]
    s0 = [st_ref[p] for p in range(pairs)]
    ws = [_mm3(_split2(jnp.concatenate([wt_ref[:, sl], rt_ref[:, sl]], axis=0)), _split2(s0[p]), _NN)
          for p, sl in enumerate(sls)]
    u = [p1_ref[:, sl] + ws[p][:chunk] for p, sl in enumerate(sls)]
    for p, sl in enumerate(sls):
        ust = jnp.concatenate([jnp.where(la, u[p], 0.0), jnp.where(la, 0.0, u[p])], axis=0)
        yb = _mm3(_split2(arb_ref[:, sl]), _split2(ust), _NN)
        yacc_ref[:, sl] = ws[p][chunk:] + y1_ref[:, sl] + (yb[:chunk] + yb[chunk:])
    for p, sl in enumerate(sls):
        dg = jnp.where(eye, jnp.broadcast_to(gc_ref[:, sl], (LANES, LANES)), 0.0)
        lhs = jnp.concatenate([kbt_ref[:, sl], dg], axis=1)
        rhs = jnp.concatenate([v_ref[:, sl], u[p], s0[p]], axis=0)
        st_ref[p] = jnp.where(same_head, _mm3(_split2(lhs), _split2(rhs), _NN), 0.0)

    y_ref[...] = _group_norm_out(yacc_ref[...], bo_ref[...], g_ref[...], lnw_ref[...], lnb_ref[...], _head_ones())

    @pl.when(c == pl.num_programs(1) - 1)
    def _():
        for p in range(pairs):
            s = st_ref[p]
            so_ref[2 * p] = s[:n, :n]
            so_ref[2 * p + 1] = s[n:, n:]


def _rwkv_scan(prep, lnw, lnb, s0t, dm):
    rt, at, kh, bh, kg, bg, v, bonus, g, gc = prep
    wt, p1, y1, arb, kbt = _rwkv_solve(at, kh, bh, rt, v, kg, bg, dm)
    b, t, rw = rt.shape
    chunk = RWKV_CHUNK
    heads = dm.rh
    pairs = heads // 2
    kern = functools.partial(_rwkv_scan_kernel, chunk=chunk, pairs=pairs)
    tspec = pl.BlockSpec((None, chunk, rw), lambda bi, c: (bi, c, 0))
    kspec = pl.BlockSpec((None, 2 * chunk, rw), lambda bi, c: (bi, c, 0))
    sspec = pl.BlockSpec((None, heads, RWKV_HEAD, RWKV_HEAD), lambda bi, c: (bi, 0, 0, 0))
    row = pl.BlockSpec((1, rw), lambda bi, c: (0, 0))
    return pl.pallas_call(
        kern,
        out_shape=(jax.ShapeDtypeStruct((b, t, rw), F32),
                   jax.ShapeDtypeStruct((b, heads, RWKV_HEAD, RWKV_HEAD), F32)),
        grid=(b, t // chunk),
        in_specs=[tspec, tspec, tspec, tspec, kspec, kspec, tspec, tspec, tspec,
                  pl.BlockSpec((None, None, 1, rw), lambda bi, c: (bi, c, 0, 0)), row, row, sspec],
        out_specs=(tspec, sspec),
        scratch_shapes=[pltpu.VMEM((pairs, LANES, LANES), F32), pltpu.VMEM((chunk, rw), F32)],
        compiler_params=_cparams(("parallel", "arbitrary")),
        name="rwkv_scan",
    )(wt, rt, p1, y1, arb, kbt, v, bonus, g, gc, lnw, lnb, s0t)


def _rwkv_step_kernel(r_ref, k_ref, v_ref, l_ref, sh_ref, mu_ref, w0_ref, w2_ref, a0_ref, a2_ref, g2_ref,
                      kk_ref, ka_ref, rk_ref, lnw_ref, lnb_ref, s_ref, y_ref, so_ref, *, rw, heads):
    n = RWKV_HEAD
    rows = SUBLANES

    def mixed(x_ref, c0, c1):
        x = jnp.broadcast_to(x_ref[...], (rows, c1 - c0))
        return _rwkv_mix_cols(x, sh_ref[:, c0:c1], mu_ref[:, c0:c1])

    r = mixed(r_ref, 0, rw)
    k = mixed(k_ref, rw, 2 * rw)
    v = mixed(v_ref, 2 * rw, 3 * rw)
    lo = mixed(l_ref, 3 * rw, 3 * rw + RWKV_LORA)
    lw, a, g = _rwkv_lora(lo, w0_ref[...], w2_ref[...], a0_ref[...], a2_ref[...], g2_ref[...])
    w = jnp.exp(lw)
    kk = k * kk_ref[...]
    k2 = k * (1.0 + (a - 1.0) * ka_ref[...])
    eye = lax.broadcasted_iota(jnp.int32, (n, n), 0) == lax.broadcasted_iota(jnp.int32, (n, n), 1)

    def col(rowvec):
        return jnp.sum(jnp.where(eye, jnp.broadcast_to(rowvec, (n, n)), 0.0), axis=1, keepdims=True)

    outs = []
    for h in range(heads):
        sl = slice(h * n, (h + 1) * n)
        kkh = kk[0:1, sl]
        kkh = kkh * lax.rsqrt(jnp.maximum(jnp.sum(kkh * kkh, axis=1, keepdims=True), 1e-12))
        ah, wh, k2h, rh, vh = a[0:1, sl], w[0:1, sl], k2[0:1, sl], r[0:1, sl], v[0:1, sl]
        s = s_ref[h]
        sa = -jnp.sum(s * kkh, axis=1, keepdims=True)
        s = s * wh + sa * (kkh * ah) + col(vh) * k2h
        so_ref[h] = s
        ycol = jnp.sum(s * rh, axis=1, keepdims=True)
        yrow = jnp.sum(jnp.where(eye, jnp.broadcast_to(ycol, (n, n)), 0.0), axis=0, keepdims=True)
        ym = jnp.mean(yrow, axis=1, keepdims=True)
        d = yrow - ym
        yv = jnp.mean(d * d, axis=1, keepdims=True)
        yn = d * lax.rsqrt(yv + GN_EPS) * lnw_ref[:, sl] + lnb_ref[:, sl]
        bonus = jnp.sum(rh * k2h * rk_ref[:, sl], axis=1, keepdims=True) * vh
        outs.append((yn + bonus) * g[0:1, sl])
    y_ref[...] = jnp.concatenate(outs, axis=1)


def _rwkv_step(p, shift, s0, mu, w0, w2, a0, a2, g2, k_k, k_a, r_k, lnw, lnb, dm):
    b = p.shape[0]
    rw, heads = dm.rw, dm.rh
    kern = functools.partial(_rwkv_step_kernel, rw=rw, heads=heads)
    cur = lambda c0, w: pl.BlockSpec((None, 1, w), lambda bi: (bi, 0, _blk(c0, w)))
    full = lambda shp: pl.BlockSpec(shp, lambda bi: (0,) * len(shp))
    sspec = pl.BlockSpec((None, heads, RWKV_HEAD, RWKV_HEAD), lambda bi: (bi, 0, 0, 0))
    return pl.pallas_call(
        kern,
        out_shape=(jax.ShapeDtypeStruct((b, 1, rw), F32),
                   jax.ShapeDtypeStruct((b, heads, RWKV_HEAD, RWKV_HEAD), F32)),
        grid=(b,),
        in_specs=[cur(dm.c_r, rw), cur(dm.c_r + rw, rw), cur(dm.c_r + 2 * rw, rw), cur(dm.c_lora, RWKV_LORA),
                  pl.BlockSpec((None, 1, dm.rcols), lambda bi: (bi, 0, 0)),
                  full((1, dm.rcols)), full((1, rw)), full((RWKV_W_RANK, rw)), full((1, rw)),
                  full((RWKV_A_RANK, rw)), full((RWKV_G_RANK, rw)), full((1, rw)), full((1, rw)), full((1, rw)),
                  full((1, rw)), full((1, rw)), sspec],
        out_specs=(pl.BlockSpec((None, 1, rw), lambda bi: (bi, 0, 0)), sspec),
        compiler_params=_cparams(("parallel",)),
        name="rwkv_step",
    )(p, p, p, p, shift, mu, w0, w2, a0, a2, g2, k_k, k_a, r_k, lnw, lnb, s0)


def _rope_blocks(x, cos2, sin2):
    outs = []
    for c in range(0, x.shape[1], NSA_HEAD):
        xb = x[:, c:c + NSA_HEAD]
        outs.append(xb * cos2 + pltpu.roll(xb, NSA_HEAD // 2, axis=1) * sin2)
    return outs[0] if len(outs) == 1 else jnp.concatenate(outs, axis=1)


def _nsa_prep_kernel(*refs, tt, nq, with_means):
    q_refs, kv_refs = refs[:nq], refs[nq:nq + 6]
    cos_ref, sin_ref, qo_ref, kv_ref, kw_ref = refs[nq + 6:nq + 11]
    cos2, sin2 = cos_ref[...], sin_ref[...]
    pw = NSA_KV_HEADS * NSA_HEAD
    for idx, ref in enumerate(q_refs):
        qo_ref[:, idx * pw:(idx + 1) * pw] = (_rope_blocks(ref[...], cos2, sin2) * (NSA_HEAD ** -0.5)).astype(BF16)
    for idx, ref in enumerate(kv_refs):
        x = ref[...]
        if idx % 2 == 0:
            x = _rope_blocks(x, cos2, sin2)
        if idx < 4:
            kv_ref[:, idx * pw:(idx + 1) * pw] = x
        else:
            kw_ref[:, (idx - 4) * pw:(idx - 3) * pw] = x
        if idx < 2 and with_means:
            m_ref = refs[nq + 11]
            m_ref[:, idx * pw:(idx + 1) * pw] = (
                jnp.sum(x.reshape(tt // NSA_BLOCK, NSA_BLOCK, pw), axis=1) * (1.0 / NSA_BLOCK))


def _nsa_prep(p, cos2, sin2, dm, tt, with_means):
    b, t, _ = p.shape
    pw = dm.part
    nq = dm.nw // pw
    kern = functools.partial(_nsa_prep_kernel, tt=tt, nq=nq, with_means=with_means)
    cur = lambda c0: pl.BlockSpec((None, tt, pw), lambda bi, i: (bi, i, _blk(c0, pw)))
    outs = [jax.ShapeDtypeStruct((b, t, dm.nw), BF16), jax.ShapeDtypeStruct((b, t, 4 * pw), F32),
            jax.ShapeDtypeStruct((b, t, 2 * pw), F32)]
    ospecs = [pl.BlockSpec((None, tt, dm.nw), lambda bi, i: (bi, i, 0)),
              pl.BlockSpec((None, tt, 4 * pw), lambda bi, i: (bi, i, 0)),
              pl.BlockSpec((None, tt, 2 * pw), lambda bi, i: (bi, i, 0))]
    if with_means:
        outs.append(jax.ShapeDtypeStruct((b, t // NSA_BLOCK, 2 * pw), F32))
        ospecs.append(pl.BlockSpec((None, tt // NSA_BLOCK, 2 * pw), lambda bi, i: (bi, i, 0)))
    in_specs = [cur(dm.c_q + k * pw) for k in range(nq)] + [cur(dm.c_kv + k * pw) for k in range(6)]
    in_specs += [pl.BlockSpec((tt, NSA_HEAD), lambda bi, i: (i, 0))] * 2
    return pl.pallas_call(
        kern,
        out_shape=tuple(outs),
        grid=(b, t // tt),
        in_specs=in_specs,
        out_specs=tuple(ospecs),
        compiler_params=_cparams(("parallel", "parallel")),
        name="nsa_prep",
    )(*([p] * (nq + 6)), cos2, sin2)


def _nsa_prompt_kernel(q_ref, kc_ref, vc_ref, ks_ref, vs_ref, kw_ref, vw_ref, gt_ref, o_ref,
                       m_ref, l_ref, acc_ref, *, g, nb, n_sel, tk, lw):
    h = pl.program_id(1)
    i = pl.program_id(2)
    qb = NSA_QTILE
    rows = g * qb
    q = jnp.concatenate([q_ref[:, gi * NSA_HEAD:(gi + 1) * NSA_HEAD] for gi in range(g)], axis=0)
    gsl = [slice(gi * qb, (gi + 1) * qb) for gi in range(g)]
    qpos_c = i * qb + lax.broadcasted_iota(jnp.int32, (qb, 1), 0)
    qpos_r = i * qb + lax.broadcasted_iota(jnp.int32, (1, rows), 1) % qb

    blk = lax.broadcasted_iota(jnp.int32, (nb, 1), 0)
    st = _dot_nt(kc_ref[...], q)
    ok = ((blk + 1) * NSA_BLOCK - 1) <= qpos_r
    mc = jnp.max(jnp.where(ok, st, NEG), axis=0, keepdims=True)
    ec = jnp.where(ok, jnp.exp(st - mc), 0.0)
    den = jnp.sum(ec, axis=0, keepdims=True)
    p_c = ec / jnp.where(den > 0.0, den, 1.0)
    o_cmp = lax.dot_general(p_c.astype(BF16), vc_ref[...].astype(BF16), _TN, preferred_element_type=F32)

    imp = p_c[:, gsl[0]]
    for gi in range(1, g):
        imp = imp + p_c[:, gsl[gi]]
    qp1 = qpos_r[:, 0:qb]
    imp = jnp.where(blk == qp1 // NSA_BLOCK, g + 1.0, jnp.where(blk * NSA_BLOCK <= qp1, imp, -1.0))
    cnt = jnp.zeros((nb, qb), F32)
    for bi in range(nb):
        ci = imp[bi:bi + 1, :]
        beats = (ci > imp) | ((ci == imp) & (blk > bi))
        cnt = cnt + jnp.where(beats, 1.0, 0.0)
    assert nb <= LANES
    drop_t = jnp.where(cnt < n_sel, 0.0, NEG)
    if nb < LANES:
        drop_t = jnp.concatenate([drop_t, jnp.zeros((LANES - nb, qb), F32)], axis=0)
    drop = drop_t.T.astype(BF16)
    q_aug = jnp.concatenate([q, jnp.concatenate([drop] * g, axis=0)], axis=1)

    m_ref[...] = jnp.full((rows, 1), M_INIT, F32)
    l_ref[...] = jnp.zeros((rows, 1), F32)
    acc_ref[...] = jnp.zeros((rows, NSA_HEAD), F32)
    bpt = tk // NSA_BLOCK
    key_blk = lax.broadcasted_iota(jnp.int32, (tk, LANES), 0) // NSA_BLOCK
    blk_lane = lax.broadcasted_iota(jnp.int32, (tk, LANES), 1)
    kcol = lax.broadcasted_iota(jnp.int32, (1, tk), 1)
    n_kt = ((i + 1) * qb + tk - 1) // tk

    def scores(kt):
        k0 = pl.multiple_of(kt * tk, tk)
        one_hot = jnp.where(blk_lane == key_blk + kt * bpt, 1.0, 0.0).astype(BF16)
        k_aug = jnp.concatenate([ks_ref[pl.ds(k0, tk), :].astype(BF16), one_hot], axis=1)
        return _dot_nt(q_aug, k_aug)

    def accumulate(kt, sc, causal):
        k0 = pl.multiple_of(kt * tk, tk)
        vb = vs_ref[pl.ds(k0, tk), :].astype(BF16)
        s_g = [sc[sl] for sl in gsl]
        if causal:
            s_g = [jnp.where((k0 + kcol) <= qpos_c, s, NEG) for s in s_g]
        m_old = [m_ref[sl] for sl in gsl]
        m_new = [jnp.maximum(m_old[gi], jnp.max(s_g[gi], axis=1, keepdims=True)) for gi in range(g)]
        pr = [jnp.exp(s_g[gi] - m_new[gi]) for gi in range(g)]
        pv = _dot(jnp.concatenate([x.astype(BF16) for x in pr], axis=0), vb)
        for gi, sl in enumerate(gsl):
            alpha = jnp.exp(m_old[gi] - m_new[gi])
            l_ref[sl] = alpha * l_ref[sl] + jnp.sum(pr[gi], axis=1, keepdims=True)
            acc_ref[sl] = alpha * acc_ref[sl] + pv[sl]
            m_ref[sl] = m_new[gi]

    def body(kt, sc):
        nxt = scores(kt + 1)
        accumulate(kt, sc, False)
        return nxt

    accumulate(n_kt - 1, lax.fori_loop(0, n_kt - 1, body, scores(0)), True)
    o_sel = acc_ref[...] / l_ref[...]

    w0 = pl.multiple_of(jnp.maximum(i * qb - NSA_WINDOW, 0), qb)
    kwb = kw_ref[pl.ds(w0, lw), :].astype(BF16)
    vwb = vw_ref[pl.ds(w0, lw), :].astype(BF16)
    dp = qpos_c - (w0 + lax.broadcasted_iota(jnp.int32, (1, lw), 1))
    bias_w = jnp.where((dp >= 0) & (dp <= NSA_WINDOW), 0.0, NEG)
    sw = _dot_nt(q, kwb)
    s_w = [sw[sl] + bias_w for sl in gsl]
    e_w = [jnp.exp(s_w[gi] - jnp.max(s_w[gi], axis=1, keepdims=True)) for gi in range(g)]
    den_w = jnp.concatenate([jnp.sum(e, axis=1, keepdims=True) for e in e_w], axis=0)
    o_win = _dot(jnp.concatenate([e.astype(BF16) for e in e_w], axis=0), vwb) / den_w

    gs = _sigmoid(gt_ref[...])
    for gi in range(g):
        acc = None
        for ci, ob in enumerate((o_cmp, o_sel, o_win)):
            c0 = gi * 3 + ci
            c1 = (g + gi) * 3 + ci
            gate = jnp.where(h == 0, gs[:, c0:c0 + 1], gs[:, c1:c1 + 1])
            term = ob[gi * qb:(gi + 1) * qb] * gate
            acc = term if acc is None else acc + term
        o_ref[:, gi * NSA_HEAD:(gi + 1) * NSA_HEAD] = acc


def _nsa_prompt(qr, kv4, kwv, kcvc, p, dm):
    b, t, _ = qr.shape
    assert NSA_KV_HEADS == 2
    g = dm.g
    nb = t // NSA_BLOCK
    n_sel = min(NSA_TOPK, nb)
    tk = min(512, t)
    lw = NSA_WINDOW + NSA_QTILE
    assert t % tk == 0 and t >= lw
    hd = NSA_HEAD
    kern = functools.partial(_nsa_prompt_kernel, g=g, nb=nb, n_sel=n_sel, tk=tk, lw=lw)
    kvspec = lambda c: pl.BlockSpec((None, t, hd), lambda bi, h, i, c=c: (bi, 0, 2 * c + h))
    rows = g * NSA_QTILE
    return pl.pallas_call(
        kern,
        out_shape=jax.ShapeDtypeStruct((b, t, dm.nw), F32),
        grid=(b, NSA_KV_HEADS, t // NSA_QTILE),
        in_specs=[pl.BlockSpec((None, NSA_QTILE, g * hd), lambda bi, h, i: (bi, i, h)),
                  pl.BlockSpec((None, nb, hd), lambda bi, h, i: (bi, 0, h)),
                  pl.BlockSpec((None, nb, hd), lambda bi, h, i: (bi, 0, 2 + h)),
                  kvspec(2), kvspec(3), kvspec(0), kvspec(1),
                  pl.BlockSpec((None, NSA_QTILE, LANES), lambda bi, h, i: (bi, i, _blk(dm.c_ng, LANES)))],
        out_specs=pl.BlockSpec((None, NSA_QTILE, g * hd), lambda bi, h, i: (bi, i, h)),
        scratch_shapes=[pltpu.VMEM((rows, 1), F32), pltpu.VMEM((rows, 1), F32), pltpu.VMEM((rows, hd), F32)],
        compiler_params=_cparams(("parallel", "parallel", "arbitrary")),
        name="nsa_prompt",
    )(qr, kcvc, kcvc, kv4, kv4, kwv, kwv, p)


def _page_means_kernel(pt_ref, *refs, ppb):
    o_ref = refs[ppb]
    for k in range(ppb):
        x = refs[k][...]
        nbp = x.shape[0] // NSA_BLOCK
        o_ref[k] = jnp.sum(x.reshape(nbp, NSA_BLOCK, x.shape[1], x.shape[2]), axis=1) * (1.0 / NSA_BLOCK)


def _page_means(cache, layer, page_table):
    _, _, page, _, prow, hd = cache.shape
    b, n_pages = page_table.shape
    nbp = page // NSA_BLOCK
    ppb = math.gcd(n_pages, 32)
    kern = functools.partial(_page_means_kernel, ppb=ppb)
    pspec = lambda k: pl.BlockSpec((None, None, page, None, prow, hd),
                                   lambda bi, j, pt, k=k: (layer, pt[bi, j * ppb + k], 0, 0, 0, 0))
    out = pl.pallas_call(
        kern,
        out_shape=jax.ShapeDtypeStruct((b, n_pages, nbp, prow, hd), F32),
        grid_spec=pltpu.PrefetchScalarGridSpec(
            num_scalar_prefetch=1,
            grid=(b, n_pages // ppb),
            in_specs=[pspec(k) for k in range(ppb)],
            out_specs=pl.BlockSpec((None, ppb, nbp, prow, hd), lambda bi, j, pt: (bi, j, 0, 0, 0)),
        ),
        compiler_params=_cparams(("parallel", "parallel")),
        name="nsa_page_means",
    )(page_table, *([cache] * ppb))
    return out.reshape(b, n_pages * nbp, prow * hd)


def _nsa_choose_kernel(q_ref, m_ref, oc_ref, idx_ref, *, g, nbp, n_sel):
    hd = NSA_HEAD
    blk = lax.broadcasted_iota(jnp.int32, (1, nbp), 1)
    lane = lax.broadcasted_iota(jnp.int32, (1, LANES), 1)
    for h in range(NSA_KV_HEADS):
        q = jnp.concatenate([q_ref[:, (h * g + gi) * hd:(h * g + gi + 1) * hd] for gi in range(g)], axis=0)
        q = jnp.concatenate([q, jnp.zeros((SUBLANES - g, hd), q.dtype)], axis=0) if g < SUBLANES else q
        kc = m_ref[:, h * hd:(h + 1) * hd]
        vc = m_ref[:, (NSA_KV_HEADS + h) * hd:(NSA_KV_HEADS + h + 1) * hd]
        s = _dot_nt(q, kc)
        m = jnp.max(s, axis=1, keepdims=True)
        e = jnp.exp(s - m)
        p_c = e / jnp.sum(e, axis=1, keepdims=True)
        oc_ref[h] = _dot(p_c, vc)[0:g]
        imp = jnp.sum(p_c[0:g], axis=0, keepdims=True)
        idx = jnp.where(lane == 0, nbp, 0)
        for it in range(1, n_sel):
            best = jnp.max(imp, axis=1, keepdims=True)
            j = jnp.min(jnp.where(imp == best, blk, nbp), axis=1, keepdims=True)
            idx = jnp.where(lane == it, j, idx)
            imp = jnp.where(blk == j, -2.0, imp)
        idx_ref[h] = idx


def _nsa_choose(qr, means, dm, n_sel):
    b = qr.shape[0]
    nbp = means.shape[1]
    g = dm.g
    kern = functools.partial(_nsa_choose_kernel, g=g, nbp=nbp, n_sel=n_sel)
    return pl.pallas_call(
        kern,
        out_shape=(jax.ShapeDtypeStruct((b, NSA_KV_HEADS, g, NSA_HEAD), F32),
                   jax.ShapeDtypeStruct((b, NSA_KV_HEADS, 1, LANES), jnp.int32)),
        grid=(b,),
        in_specs=[pl.BlockSpec((None, 1, dm.nw), lambda bi: (bi, 0, 0)),
                  pl.BlockSpec((None, nbp, means.shape[2]), lambda bi: (bi, 0, 0))],
        out_specs=(pl.BlockSpec((None, NSA_KV_HEADS, g, NSA_HEAD), lambda bi: (bi, 0, 0, 0)),
                   pl.BlockSpec((None, NSA_KV_HEADS, 1, LANES), lambda bi: (bi, 0, 0, 0))),
        compiler_params=_cparams(("parallel",)),
        name="nsa_choose",
    )(qr, means)


def _nsa_sample_kernel(idx_ref, pt_ref, q_ref, *refs, g, n_sel):
    cb_refs = refs[:n_sel - 1]
    kn_ref, vn_ref, wb_ref, kwn_ref, vwn_ref, oc_ref, gt_ref, o_ref = refs[n_sel - 1:]
    h = pl.program_id(1)
    hd = NSA_HEAD
    rows = SUBLANES
    kvh = NSA_KV_HEADS
    q = jnp.concatenate([q_ref[:, gi * hd:(gi + 1) * hd] for gi in range(g)], axis=0)
    if g < rows:
        q = jnp.concatenate([q, jnp.zeros((rows - g, hd), q.dtype)], axis=0)
    qf = q.astype(F32)

    m = jnp.sum(qf * kn_ref[...].astype(BF16).astype(F32), axis=1, keepdims=True)
    l = jnp.ones((rows, 1), F32)
    acc = jnp.broadcast_to(vn_ref[...].astype(BF16).astype(F32), (rows, hd))
    for cb_ref in cb_refs:
        xb = cb_ref[...].astype(BF16)
        srow = lax.broadcasted_iota(jnp.int32, (1, xb.shape[0]), 1) % (4 * kvh)
        sc = _dot_nt(q, xb) + jnp.where(srow == 2 * kvh + h, 0.0, NEG)
        m_new = jnp.maximum(m, jnp.max(sc, axis=1, keepdims=True))
        pr = jnp.exp(sc - m_new)
        alpha = jnp.exp(m - m_new)
        l = alpha * l + jnp.sum(pr, axis=1, keepdims=True)
        acc = alpha * acc + _dot(pltpu.roll(pr, kvh, axis=1), xb)
        m = m_new
    o_sel = acc / l

    xw = wb_ref[...].astype(BF16)
    wrow = lax.broadcasted_iota(jnp.int32, (1, xw.shape[0]), 1) % (2 * kvh)
    sw = _dot_nt(q, xw) + jnp.where(wrow == h, 0.0, NEG)
    sn = jnp.sum(qf * kwn_ref[...].astype(BF16).astype(F32), axis=1, keepdims=True)
    mw = jnp.maximum(jnp.max(sw, axis=1, keepdims=True), sn)
    ew = jnp.exp(sw - mw)
    en = jnp.exp(sn - mw)
    den = jnp.sum(ew, axis=1, keepdims=True) + en
    pn = (en / den).astype(BF16).astype(F32)
    o_win = _dot(pltpu.roll(ew / den, kvh, axis=1), xw) + pn * vwn_ref[...].astype(BF16).astype(F32)

    gs = _sigmoid(gt_ref[...])
    o_cmp = oc_ref[...]
    for gi in range(g):
        tot = None
        for ci, ob in enumerate((o_cmp, o_sel, o_win)):
            c0 = gi * 3 + ci
            c1 = (g + gi) * 3 + ci
            gate = jnp.where(h == 0, gs[:, c0:c0 + 1], gs[:, c1:c1 + 1])
            term = ob[gi:gi + 1] * gate
            tot = term if tot is None else tot + term
        o_ref[:, gi * hd:(gi + 1) * hd] = tot


def _nsa_sample(qr, kv4, kwv, cache_rows, win_rows, layer, n_pool, page, lwin, page_table, idx, o_cmp, p, dm, n_sel):
    b = qr.shape[0]
    g, hd = dm.g, NSA_HEAD
    bpp = page // NSA_BLOCK
    brows = NSA_BLOCK * 4 * NSA_KV_HEADS
    wrows = lwin * 2 * NSA_KV_HEADS
    kern = functools.partial(_nsa_sample_kernel, g=g, n_sel=n_sel)

    def sel_spec(slot):
        def index_map(bi, h, idx_ref, pt_ref):
            blk = idx_ref[(bi * NSA_KV_HEADS + h) * LANES + slot]
            return ((layer * n_pool + pt_ref[bi, blk // bpp]) * bpp + blk % bpp, 0)
        return pl.BlockSpec((brows, hd), index_map)

    newspec = lambda c: pl.BlockSpec((None, 1, hd), lambda bi, h, ir, pr, c=c: (bi, 0, 2 * c + h))
    in_specs = [pl.BlockSpec((None, 1, g * hd), lambda bi, h, ir, pr: (bi, 0, h))]
    in_specs += [sel_spec(slot) for slot in range(1, n_sel)]
    in_specs += [newspec(2), newspec(3),
                 pl.BlockSpec((wrows, hd), lambda bi, h, ir, pr: (layer * b + bi, 0)),
                 newspec(0), newspec(1),
                 pl.BlockSpec((None, None, g, hd), lambda bi, h, ir, pr: (bi, h, 0, 0)),
                 pl.BlockSpec((None, 1, LANES), lambda bi, h, ir, pr: (bi, 0, _blk(dm.c_ng, LANES)))]
    return pl.pallas_call(
        kern,
        out_shape=jax.ShapeDtypeStruct((b, 1, dm.nw), F32),
        grid_spec=pltpu.PrefetchScalarGridSpec(
            num_scalar_prefetch=2,
            grid=(b, NSA_KV_HEADS),
            in_specs=in_specs,
            out_specs=pl.BlockSpec((None, 1, g * hd), lambda bi, h, ir, pr: (bi, 0, h)),
        ),
        compiler_params=_cparams(("parallel", "parallel")),
        name="nsa_sample",
    )(idx, page_table, qr, *([cache_rows] * (n_sel - 1)), kv4, kv4, win_rows, kwv, kwv, o_cmp, p)


def _rope_tables(pos):
    half = NSA_HEAD // 2
    inv = jnp.exp(-math.log(ROPE_THETA) * jnp.arange(half, dtype=F32) / half)
    ang = pos.astype(F32)[:, None] * inv[None, :]
    cos, sin = jnp.cos(ang), jnp.sin(ang)
    return jnp.concatenate([cos, cos], axis=1), jnp.concatenate([-sin, sin], axis=1)


def _shift_cols(p_last, dm):
    return jnp.concatenate([p_last[..., dm.c_r:dm.c_r + 3 * dm.rw], p_last[..., dm.c_lora:dm.c_lora + RWKV_LORA]],
                           axis=-1)


def _row(x):
    return x.reshape(1, -1)


def kernel(x_prompt, x_sample, cache_nsa_kv, page_table, state_nsa_window, state_wkv, state_rwkv_shift,
           state_pool, state_ffn_conv, norm_mix, w_in, pool_w, pool_scale, rwkv_mu, rwkv_w0, rwkv_w2,
           rwkv_a0, rwkv_a2, rwkv_g2, rwkv_k_k, rwkv_k_a, rwkv_r_k, rwkv_ln_w, rwkv_ln_b, w_branch, w_out,
           norm_ffn, ffn_up, ffn_conv, ffn_conv_b, ffn_down, norm_final):
    bp, sp, d = x_prompt.shape
    bs, ts, _ = x_sample.shape
    assert ts == 1
    depth = w_in.shape[0]
    dm = Dims(d)
    ff = dm.ff
    hd = NSA_HEAD
    page = cache_nsa_kv.shape[2]
    n_pages = page_table.shape[1]
    past = n_pages * page
    kvw = 4 * NSA_KV_HEADS * hd

    w_in_b = w_in.astype(BF16)
    w_gate_b = w_in_b[:, :, dm.o3:]
    pool_w_b = pool_w.astype(BF16)
    wb_b = w_branch.astype(BF16)
    wo_b = w_out.astype(BF16)
    up_b = ffn_up.astype(BF16)
    down_b = ffn_down.astype(BF16)
    g2_b = rwkv_g2.astype(BF16)
    kk_r = rwkv_k_k.reshape(depth, 1, dm.rw)
    ka_r = rwkv_k_a.reshape(depth, 1, dm.rw)
    rk_r = rwkv_r_k.reshape(depth, 1, dm.rw)

    mp = bp * sp
    tm_p = min(512, sp)
    tm_in = min(1024, sp)
    wide = lambda n: 1024 if n % 1024 == 0 else 512
    tn_ff = 512 if ff % 512 == 0 else 256
    tt_pool = min(512, sp)
    tt_rwkv = min(256, sp)
    tt_nsa = min(512, sp)
    tm_ff = min(1024, sp)
    tps = sp // tm_ff

    def rwkv_params(l):
        return (_row(rwkv_mu[l]), _row(rwkv_w0[l]), rwkv_w2[l], _row(rwkv_a0[l]), rwkv_a2[l], g2_b[l],
                kk_r[l], ka_r[l], rk_r[l])

    cos_p, sin_p = _rope_tables(jnp.arange(sp, dtype=jnp.int32))
    x = x_prompt.reshape(mp, d)
    zeros_hist16 = jnp.zeros((bp, POOL_HIST + 1, dm.pw), F32)
    zeros_shift = jnp.zeros((bp, 1, dm.rcols), F32)
    zeros_wkv = jnp.zeros((bp, dm.rh, RWKV_HEAD, RWKV_HEAD), F32)
    zeros_conv = jnp.zeros((bp, CONV_W - 1, 2 * ff), F32)
    wl = min(NSA_WINDOW, sp)
    p_kv, p_win, p_wkv, p_shift, p_pool, p_conv = [], [], [], [], [], []
    for l in range(depth):
        p = _norm_matmul(x, _row(norm_mix[l]), w_in_b, l, dm.np, tm_in, wide(dm.np)).reshape(bp, sp, dm.np)
        pg = _norm_matmul(x, _row(norm_mix[l]), w_gate_b, l, N_BRANCH * d, tm_in, wide(N_BRANCH * d))
        ya = _pool_prompt(p, zeros_hist16, pool_w_b[l], _row(pool_scale[l]), dm, tt_pool)
        prep = _rwkv_prep(p, zeros_shift, *rwkv_params(l), dm, tt_rwkv)
        yb, s_t = _rwkv_scan(prep, _row(rwkv_ln_w[l]), _row(rwkv_ln_b[l]), zeros_wkv, dm)
        qr, kv4, kwv, kcvc = _nsa_prep(p, cos_p, sin_p, dm, tt_nsa, True)
        yc = _nsa_prompt(qr, kv4, kwv, kcvc, p, dm)
        mrg = _branch_merge(ya.reshape(mp, -1), yb.reshape(mp, -1), yc.reshape(mp, -1), pg, wb_b, l, dm,
                            tm_p, min(1024, d))
        x = _out_proj(x, mrg, wo_b, l,tm_p, min(1024, d))
        x, za, zv = _ffn(x, _row(norm_ffn[l]), up_b, ffn_conv[l], _row(ffn_conv_b[l]), down_b, l,
                         zeros_conv, ff, tm_ff, tn_ff, tps, False)
        p_kv.append(kv4.reshape(bp, sp, 4, NSA_KV_HEADS, hd))
        p_win.append(kwv[:, sp - wl:].reshape(bp, wl, 2, NSA_KV_HEADS, hd))
        p_wkv.append(jnp.swapaxes(s_t, -1, -2))
        p_shift.append(_shift_cols(p[:, sp - 1:, :], dm))
        p_pool.append(p[:, sp - POOL_HIST:, dm.c_pool:dm.c_pool + dm.pw])
        zl = jnp.concatenate([za, zv], axis=-1).reshape(bp, tps, 2, 2 * ff)
        p_conv.append(zl[:, tps - 1])
    y_prompt = _final_norm(x, _row(norm_final), tm_p).reshape(bp, sp, d)

    pos_s = past
    cos_s, sin_s = _rope_tables(jnp.full((1,), pos_s, dtype=jnp.int32))
    xs = x_sample.reshape(bs, d)
    nbp = past // NSA_BLOCK
    n_sel = min(NSA_TOPK, nbp + 1)
    n_pool = cache_nsa_kv.shape[1]
    lwin = state_nsa_window.shape[2]
    cache_pairs = cache_nsa_kv.reshape(depth, n_pool, page, 2, 2 * NSA_KV_HEADS, hd)
    cache_rows = cache_nsa_kv.reshape(-1, hd)
    win_rows = state_nsa_window.reshape(-1, hd)
    s_kv, s_win, s_wkv, s_shift, s_pool, s_conv = [], [], [], [], [], []
    for l in range(depth):
        p = _norm_matmul(xs, _row(norm_mix[l]), w_in_b, l, dm.np, bs, wide(dm.np)).reshape(bs, 1, dm.np)
        pg = _norm_matmul(xs, _row(norm_mix[l]), w_gate_b, l, N_BRANCH * d, bs, wide(N_BRANCH * d))
        e16 = jnp.concatenate([state_pool[l], p[:, :, dm.c_pool:dm.c_pool + dm.pw]], axis=1)
        ya = _pool_sample(e16, pool_w_b[l], _row(pool_scale[l]), dm, pos_s)
        yb, s_new = _rwkv_step(p, state_rwkv_shift[l], state_wkv[l], *rwkv_params(l),
                               _row(rwkv_ln_w[l]), _row(rwkv_ln_b[l]), dm)
        qr, kv4, kwv = _nsa_prep(p, cos_s, sin_s, dm, 1, False)
        means = _page_means(cache_pairs, l, page_table)
        o_cmp, idx = _nsa_choose(qr, means, dm, n_sel)
        yc = _nsa_sample(qr, kv4, kwv, cache_rows, win_rows, l, n_pool, page, lwin, page_table, idx.reshape(-1),
                         o_cmp, p, dm, n_sel)
        mrg = _branch_merge(ya, yb.reshape(bs, -1), yc.reshape(bs, -1), pg, wb_b, l, dm, bs, min(1024, d))
        xs = _out_proj(xs, mrg, wo_b, l,bs, min(1024, d))
        xs, za, zv = _ffn(xs, _row(norm_ffn[l]), up_b, ffn_conv[l], _row(ffn_conv_b[l]), down_b, l,
                          state_ffn_conv[l], ff, bs, tn_ff, 1, True)
        s_kv.append(kv4.reshape(bs, 1, 4, NSA_KV_HEADS, hd))
        wk = jnp.concatenate([state_nsa_window[l], kwv.reshape(bs, 1, 2, NSA_KV_HEADS, hd)], axis=1)
        s_win.append(wk[:, -NSA_WINDOW:])
        s_wkv.append(s_new)
        s_shift.append(_shift_cols(p, dm))
        s_pool.append(e16[:, 1:])
        z_new = jnp.concatenate([za, zv], axis=-1)[:, None, :]
        s_conv.append(jnp.concatenate([state_ffn_conv[l][:, 1:], z_new], axis=1))
    y_sample = _final_norm(xs, _row(norm_final), bs).reshape(bs, 1, d)

    st = lambda xs_: jnp.stack(xs_)
    return (y_prompt, y_sample, st(p_kv), st(p_win), st(p_wkv), st(p_shift), st(p_pool), st(p_conv),
            st(s_kv), st(s_win), st(s_wkv), st(s_shift), st(s_pool), st(s_conv))
```

```python
import functools
import math

import jax
import jax.numpy as jnp
from jax import lax
from jax.experimental import pallas as pl
from jax.experimental.pallas import tpu as pltpu

F32 = jnp.float32
BF16 = jnp.bfloat16
HIGHEST = lax.Precision.HIGHEST

POOL_GROUPS = 4
POOL_WINDOWS = (2, 4, 8, 16)
POOL_HIST = 15
RWKV_HEAD = 64
RWKV_W_RANK = 64
RWKV_A_RANK = 64
RWKV_G_RANK = 128
RWKV_LORA = RWKV_W_RANK + RWKV_A_RANK + RWKV_G_RANK
RWKV_DECAY_SCALE = 0.6065306597126334
GN_EPS = 64e-5
NSA_HEAD = 128
NSA_KV_HEADS = 2
NSA_BLOCK = 64
NSA_TOPK = 16
NSA_WINDOW = 512
NSA_QTILE = 128
N_BRANCH = 3
CONV_W = 3
ROPE_THETA = 10000.0
RMS_EPS = 1e-6
NEG = -1e30
M_INIT = -1e29

LANES = 128
SUBLANES = 8
VMEM_LIMIT = 56 * 1024 * 1024
RWKV_CHUNK = 64
FFN_SUB_ROWS = 128


class Dims:
    def __init__(self, d_model):
        d = d_model
        self.d = d
        self.pw = d // 2
        self.cg = self.pw // POOL_GROUPS
        self.rw = d // 2
        self.rh = self.rw // RWKV_HEAD
        self.rcols = 3 * self.rw + RWKV_LORA
        self.nw = d // 2
        self.nh = self.nw // NSA_HEAD
        self.g = self.nh // NSA_KV_HEADS
        self.kvc = 6 * NSA_KV_HEADS * NSA_HEAD
        self.ngate = 3 * self.nh
        self.ff = 256 * ((8 * d // 3 + 255) // 256)
        self.o1 = self.pw
        self.o2 = self.o1 + self.rcols
        self.o3 = self.o2 + self.nw + self.kvc + self.ngate
        self.c_pool = 0
        self.c_r = self.o1
        self.c_lora = self.o1 + 3 * self.rw
        self.c_q = self.o2
        self.c_kv = self.o2 + self.nw
        self.c_ng = self.c_kv + self.kvc
        self.tn_in = 512
        self.np = -(-self.o3 // self.tn_in) * self.tn_in
        self.part = NSA_KV_HEADS * NSA_HEAD


def _blk(offset, width):
    assert offset % width == 0, (offset, width)
    return offset // width


def _cparams(sem):
    return pltpu.CompilerParams(dimension_semantics=sem, vmem_limit_bytes=VMEM_LIMIT)


def _dot(a, b):
    return jnp.dot(a.astype(BF16), b.astype(BF16), preferred_element_type=F32)


def _dot_nt(a, b):
    return lax.dot_general(a.astype(BF16), b.astype(BF16), (((1,), (1,)), ((), ())),
                           preferred_element_type=F32)


def _dot_hi(a, b):
    return jnp.dot(a, b, precision=HIGHEST, preferred_element_type=F32)


_NN = (((1,), (0,)), ((), ()))
_NT = (((1,), (1,)), ((), ()))
_TN = (((0,), (0,)), ((), ()))


def _split2(x):
    hi = x.astype(BF16)
    return hi, (x - hi.astype(F32)).astype(BF16)


def _mm3(a, b, dims):
    d = lambda p, q: lax.dot_general(p, q, dims, preferred_element_type=F32)
    return d(a[0], b[0]) + d(a[0], b[1]) + d(a[1], b[0])


def _split3(x):
    hi = x.astype(BF16)
    r1 = x - hi.astype(F32)
    mid = r1.astype(BF16)
    lo = (r1 - mid.astype(F32)).astype(BF16)
    return hi, mid, lo


def _dot_exact_lhs(m_bf16, x):
    hi, mid, lo = _split3(x)
    d = lambda p: jnp.dot(m_bf16, p, preferred_element_type=F32)
    return d(hi) + d(mid) + d(lo)


def _dot_exact_rhs(x, m_bf16):
    hi, mid, lo = _split3(x)
    d = lambda p: jnp.dot(p, m_bf16, preferred_element_type=F32)
    return d(hi) + d(mid) + d(lo)


def _head_ones():
    i = lax.broadcasted_iota(jnp.int32, (LANES, LANES), 0) // RWKV_HEAD
    j = lax.broadcasted_iota(jnp.int32, (LANES, LANES), 1) // RWKV_HEAD
    return (i == j).astype(BF16)


def _head_sum(x, ones):
    parts = [_dot_exact_rhs(x[:, c:c + LANES], ones) for c in range(0, x.shape[1], LANES)]
    return parts[0] if len(parts) == 1 else jnp.concatenate(parts, axis=1)


def _rmsnorm_val(x, g):
    ms = jnp.mean(x * x, axis=-1, keepdims=True)
    return x * lax.rsqrt(ms + RMS_EPS) * g


def _sigmoid(x):
    return 1.0 / (1.0 + jnp.exp(-x))


def _norm_matmul_kernel(x_ref, g_ref, w_ref, o_ref, h_ref):
    @pl.when(pl.program_id(1) == 0)
    def _():
        h_ref[...] = _rmsnorm_val(x_ref[...], g_ref[...]).astype(BF16)

    o_ref[...] = jnp.dot(h_ref[...], w_ref[...], preferred_element_type=F32)


def _norm_matmul(x, g, w, layer, n, tm, tn):
    m, d = x.shape
    assert n % tn == 0 and n <= w.shape[2]
    return pl.pallas_call(
        _norm_matmul_kernel,
        out_shape=jax.ShapeDtypeStruct((m, n), F32),
        grid=(m // tm, n // tn),
        in_specs=[pl.BlockSpec((tm, d), lambda i, j: (i, 0)),
                  pl.BlockSpec((1, d), lambda i, j: (0, 0)),
                  pl.BlockSpec((None, d, tn), lambda i, j: (layer, 0, j))],
        out_specs=pl.BlockSpec((tm, tn), lambda i, j: (i, j)),
        scratch_shapes=[pltpu.VMEM((tm, d), BF16)],
        compiler_params=_cparams(("parallel", "arbitrary")),
        name="norm_matmul",
    )(x, g, w)


def _branch_merge_kernel(ya_ref, yb_ref, yc_ref, ga_ref, gb_ref, gc_ref, w_ref, o_ref):
    acc = _dot(ya_ref[...], w_ref[0]) * _sigmoid(ga_ref[...])
    acc = acc + _dot(yb_ref[...], w_ref[1]) * _sigmoid(gb_ref[...])
    acc = acc + _dot(yc_ref[...], w_ref[2]) * _sigmoid(gc_ref[...])
    o_ref[...] = acc.astype(BF16)


def _branch_merge(ya, yb, yc, p, wb, layer, dm, tm, tn):
    m = ya.shape[0]
    d, hw = dm.d, dm.d // 2
    nb = d // tn
    yspec = pl.BlockSpec((tm, hw), lambda j, i: (i, 0))
    gspec = lambda k: pl.BlockSpec((tm, tn), lambda j, i, k=k: (i, k * nb + j))
    return pl.pallas_call(
        _branch_merge_kernel,
        out_shape=jax.ShapeDtypeStruct((m, d), BF16),
        grid=(nb, m // tm),
        in_specs=[yspec, yspec, yspec, gspec(0), gspec(1), gspec(2),
                  pl.BlockSpec((None, N_BRANCH, hw, tn), lambda j, i: (layer, 0, 0, j))],
        out_specs=pl.BlockSpec((tm, tn), lambda j, i: (i, j)),
        compiler_params=_cparams(("parallel", "parallel")),
        name="branch_merge",
    )(ya, yb, yc, p, p, p, wb)


def _out_proj_kernel(x_ref, m_ref, w_ref, o_ref):
    o_ref[...] = x_ref[...] + jnp.dot(m_ref[...], w_ref[...], preferred_element_type=F32)


def _out_proj(x, mrg, wo, layer, tm, tn):
    m, d = x.shape
    return pl.pallas_call(
        _out_proj_kernel,
        out_shape=jax.ShapeDtypeStruct((m, d), F32),
        grid=(m // tm, d // tn),
        in_specs=[pl.BlockSpec((tm, tn), lambda i, j: (i, j)),
                  pl.BlockSpec((tm, d), lambda i, j: (i, 0)),
                  pl.BlockSpec((None, d, tn), lambda i, j: (layer, 0, j))],
        out_specs=pl.BlockSpec((tm, tn), lambda i, j: (i, j)),
        compiler_params=_cparams(("parallel", "parallel")),
        name="out_proj",
    )(x, mrg, wo)


def _ffn_kernel(x_ref, g_ref, wa_ref, wv_ref, cwa_ref, cwv_ref, cba_ref, cbv_ref, wd_ref,
                ha_ref, hv_ref, o_ref, za_ref, zv_ref, h_ref, ca_ref, cv_ref, *, tm, sub, tps, rows_are_seqs):
    i = pl.program_id(0)
    j = pl.program_id(1)

    @pl.when(j == 0)
    def _():
        x = x_ref[...]
        h_ref[...] = _rmsnorm_val(x, g_ref[...]).astype(BF16)
        o_ref[...] = x

    def conv_mix(z, zs1, zs2, cw_ref, cb_ref):
        return zs2 * cw_ref[0:1] + zs1 * cw_ref[1:2] + z * cw_ref[2:3] + cb_ref[...]

    if rows_are_seqs:
        h = h_ref[...]
        za = jnp.dot(h, wa_ref[...], preferred_element_type=F32)
        zv = jnp.dot(h, wv_ref[...], preferred_element_type=F32)
        za_ref[...] = za
        zv_ref[...] = zv
        ca = conv_mix(za, ha_ref[:, 1, :], ha_ref[:, 0, :], cwa_ref, cba_ref)
        cv = conv_mix(zv, hv_ref[:, 1, :], hv_ref[:, 0, :], cwv_ref, cbv_ref)
        o_ref[...] += _dot(ca * _sigmoid(ca) * cv, wd_ref[...])
        return

    @pl.when((i % tps) == 0)
    def _():
        ca_ref[j] = ha_ref[...]
        cv_ref[j] = hv_ref[...]

    rs = [slice(s * sub, (s + 1) * sub) for s in range(tm // sub)]
    zas = [jnp.dot(h_ref[r], wa_ref[...], preferred_element_type=F32) for r in rs]
    zvs = [jnp.dot(h_ref[r], wv_ref[...], preferred_element_type=F32) for r in rs]
    row = lax.broadcasted_iota(jnp.int32, (sub, zas[0].shape[1]), 0)

    def conv(zs, s, carry_ref, cw_ref, cb_ref):
        z = zs[s]
        prev = carry_ref[j] if s == 0 else zs[s - 1][sub - 2:sub]
        zs1 = jnp.where(row == 0, prev[1:2], pltpu.roll(z, 1, axis=0))
        zs2 = jnp.where(row == 0, prev[0:1], jnp.where(row == 1, prev[1:2], pltpu.roll(z, 2, axis=0)))
        return conv_mix(z, zs1, zs2, cw_ref, cb_ref)

    for s, r in enumerate(rs):
        ca = conv(zas, s, ca_ref, cwa_ref, cba_ref)
        cv = conv(zvs, s, cv_ref, cwv_ref, cbv_ref)
        o_ref[r] += _dot(ca * _sigmoid(ca) * cv, wd_ref[...])
    za_last = zas[-1][sub - 2:sub]
    zv_last = zvs[-1][sub - 2:sub]
    ca_ref[j] = za_last
    cv_ref[j] = zv_last
    za_ref[...] = za_last
    zv_ref[...] = zv_last


def _ffn(x, g, w_up, cw, cb, w_down, layer, hist, ff, tm, tn, tps, rows_are_seqs):
    m, d = x.shape
    nj = ff // tn
    nm = m // tm
    if rows_are_seqs:
        assert nm == 1
        hspec_a = pl.BlockSpec((m, 2, tn), lambda i, j: (0, 0, j))
        hspec_v = pl.BlockSpec((m, 2, tn), lambda i, j: (0, 0, nj + j))
        zshape = jax.ShapeDtypeStruct((m, ff), F32)
        zspec = pl.BlockSpec((m, tn), lambda i, j: (0, j))
    else:
        hspec_a = pl.BlockSpec((None, 2, tn), lambda i, j: (i // tps, 0, j))
        hspec_v = pl.BlockSpec((None, 2, tn), lambda i, j: (i // tps, 0, nj + j))
        zshape = jax.ShapeDtypeStruct((nm, 2, ff), F32)
        zspec = pl.BlockSpec((None, 2, tn), lambda i, j: (i, 0, j))
    sub = FFN_SUB_ROWS if tm % FFN_SUB_ROWS == 0 else tm
    kern = functools.partial(_ffn_kernel, tm=tm, sub=sub, tps=tps, rows_are_seqs=rows_are_seqs)
    return pl.pallas_call(
        kern,
        out_shape=(jax.ShapeDtypeStruct((m, d), F32), zshape, zshape),
        grid=(nm, nj),
        in_specs=[pl.BlockSpec((tm, d), lambda i, j: (i, 0)),
                  pl.BlockSpec((1, d), lambda i, j: (0, 0)),
                  pl.BlockSpec((None, d, tn), lambda i, j: (layer, 0, j)),
                  pl.BlockSpec((None, d, tn), lambda i, j: (layer, 0, nj + j)),
                  pl.BlockSpec((CONV_W, tn), lambda i, j: (0, j)),
                  pl.BlockSpec((CONV_W, tn), lambda i, j: (0, nj + j)),
                  pl.BlockSpec((1, tn), lambda i, j: (0, j)),
                  pl.BlockSpec((1, tn), lambda i, j: (0, nj + j)),
                  pl.BlockSpec((None, tn, d), lambda i, j: (layer, j, 0)),
                  hspec_a, hspec_v],
        out_specs=(pl.BlockSpec((tm, d), lambda i, j: (i, 0)), zspec, zspec),
        scratch_shapes=[pltpu.VMEM((tm, d), BF16),
                        pltpu.VMEM((nj, 2, tn), F32),
                        pltpu.VMEM((nj, 2, tn), F32)],
        compiler_params=_cparams(("arbitrary", "arbitrary")),
        name="conv_ffn",
    )(x, g, w_up, w_up, cw, cw, cb, cb, w_down, hist, hist)


def _final_norm_kernel(x_ref, g_ref, o_ref):
    o_ref[...] = _rmsnorm_val(x_ref[...], g_ref[...])


def _final_norm(x, g, tm):
    m, d = x.shape
    return pl.pallas_call(
        _final_norm_kernel,
        out_shape=jax.ShapeDtypeStruct((m, d), F32),
        grid=(m // tm,),
        in_specs=[pl.BlockSpec((tm, d), lambda i: (i, 0)), pl.BlockSpec((1, d), lambda i: (0, 0))],
        out_specs=pl.BlockSpec((tm, d), lambda i: (i, 0)),
        compiler_params=_cparams(("parallel",)),
        name="final_norm",
    )(x, g)


def _pool_kernel(u_ref, hist_ref, w_ref, sc_ref, o_ref, e_ref, *, tt, cg):
    i = pl.program_id(1)
    hrows = POOL_HIST + 1

    @pl.when(i == 0)
    def _():
        e_ref[0:hrows] = hist_ref[...]

    @pl.when(i > 0)
    def _():
        e_ref[0:hrows] = e_ref[tt:tt + hrows]

    u = u_ref[...]
    e_ref[hrows:hrows + tt] = u
    pos = i * tt + lax.broadcasted_iota(jnp.int32, (tt, 1), 0)
    for gi, w in enumerate(POOL_WINDOWS):
        cs = slice(gi * cg, (gi + 1) * cg)
        s = e_ref[:, cs]
        sh = 1
        while sh < w:
            s = s + pltpu.roll(s, sh, axis=0)
            sh *= 2
        cnt = jnp.minimum(w, pos + 1).astype(F32)
        dlt = s[hrows:] / cnt - u[:, cs]
        o_ref[:, cs] = _dot(dlt, w_ref[gi]) * sc_ref[:, cs]


def _pool_prompt(p, hist16, w, scale, dm, tt):
    b, t, _ = p.shape
    kern = functools.partial(_pool_kernel, tt=tt, cg=dm.cg)
    return pl.pallas_call(
        kern,
        out_shape=jax.ShapeDtypeStruct((b, t, dm.pw), F32),
        grid=(b, t // tt),
        in_specs=[pl.BlockSpec((None, tt, dm.pw), lambda bi, i: (bi, i, _blk(dm.c_pool, dm.pw))),
                  pl.BlockSpec((None, POOL_HIST + 1, dm.pw), lambda bi, i: (bi, 0, 0)),
                  pl.BlockSpec((POOL_GROUPS, dm.cg, dm.cg), lambda bi, i: (0, 0, 0)),
                  pl.BlockSpec((1, dm.pw), lambda bi, i: (0, 0))],
        out_specs=pl.BlockSpec((None, tt, dm.pw), lambda bi, i: (bi, i, 0)),
        scratch_shapes=[pltpu.VMEM((tt + POOL_HIST + 1, dm.pw), F32)],
        compiler_params=_cparams(("parallel", "arbitrary")),
        name="pool_mixer",
    )(p, hist16, w, scale)


def _pool_sample_kernel(e_ref, w_ref, sc_ref, o_ref, *, cg, pos):
    e = e_ref[...]
    hrows = POOL_HIST + 1
    row = lax.broadcasted_iota(jnp.int32, e.shape, 1)
    u = e[:, hrows - 1, :]
    for gi, w in enumerate(POOL_WINDOWS):
        cs = slice(gi * cg, (gi + 1) * cg)
        win = jnp.sum(jnp.where(row >= hrows - w, e, 0.0)[:, :, cs], axis=1)
        dlt = win / float(min(w, pos + 1)) - u[:, cs]
        o_ref[:, cs] = _dot(dlt, w_ref[gi]) * sc_ref[:, cs]


def _pool_sample(e16, w, scale, dm, pos):
    b = e16.shape[0]
    kern = functools.partial(_pool_sample_kernel, cg=dm.cg, pos=pos)
    return pl.pallas_call(
        kern,
        out_shape=jax.ShapeDtypeStruct((b, dm.pw), F32),
        compiler_params=pltpu.CompilerParams(vmem_limit_bytes=VMEM_LIMIT),
        name="pool_mixer_step",
    )(e16, w, scale)


def _rwkv_mix_cols(x, xprev, mu):
    return x + (xprev - x) * mu


def _rwkv_lora(pl_, w0, w2, a0, a2, g2):
    w_in = pl_[:, 0:RWKV_W_RANK]
    a_in = pl_[:, RWKV_W_RANK:RWKV_W_RANK + RWKV_A_RANK]
    g_in = pl_[:, RWKV_W_RANK + RWKV_A_RANK:]
    lw = -RWKV_DECAY_SCALE * _sigmoid(w0 + _dot_hi(jnp.tanh(w_in), w2))
    a = _sigmoid(a0 + _dot_hi(a_in, a2))
    g = _dot(_sigmoid(g_in), g2)
    return lw, a, g


def _rwkv_prep_kernel(r_ref, k_ref, v_ref, l_ref, rp_ref, kp_ref, vp_ref, lp_ref, sh_ref, mu_ref,
                      w0_ref, w2_ref, a0_ref, a2_ref, g2_ref, kk_ref, ka_ref, rk_ref,
                      rt_ref, at_ref, kh_ref, bh_ref, kg_ref, bg_ref, vo_ref, bo_ref, go_ref, gc_ref,
                      *, tt, rw, chunk):
    i = pl.program_id(1)
    first = i == 0
    row1 = lax.broadcasted_iota(jnp.int32, (tt, 1), 0)

    def mixed(x_ref, xp_ref, c0, c1):
        x = x_ref[...]
        carry = jnp.where(first, sh_ref[:, c0:c1], xp_ref[SUBLANES - 1:SUBLANES, :])
        xprev = jnp.where(row1 == 0, carry, pltpu.roll(x, 1, axis=0))
        return _rwkv_mix_cols(x, xprev, mu_ref[:, c0:c1])

    r = mixed(r_ref, rp_ref, 0, rw)
    k = mixed(k_ref, kp_ref, rw, 2 * rw)
    v = mixed(v_ref, vp_ref, 2 * rw, 3 * rw)
    lo = mixed(l_ref, lp_ref, 3 * rw, 3 * rw + RWKV_LORA)
    lw, a, g = _rwkv_lora(lo, w0_ref[...], w2_ref[...], a0_ref[...], a2_ref[...], g2_ref[...])

    ones = _head_ones()
    kk = k * kk_ref[...]
    kk = kk * lax.rsqrt(jnp.maximum(_head_sum(kk * kk, ones), 1e-12))
    k2 = k * (1.0 + (a - 1.0) * ka_ref[...])
    bonus = _head_sum(r * k2 * rk_ref[...], ones) * v
    bvec = kk * a

    ti = lax.broadcasted_iota(jnp.int32, (tt, tt), 0)
    si = lax.broadcasted_iota(jnp.int32, (tt, tt), 1)
    same = (ti // chunk) == (si // chunk)
    tri = (same & (si <= ti)).astype(BF16)
    blk = same.astype(BF16)
    cum = _dot_exact_lhs(tri, lw)
    tot = _dot_exact_lhs(blk, lw)
    e_in = jnp.exp(cum)
    e_out = jnp.exp(-cum)
    e_rest = jnp.exp(tot - cum)
    rt_ref[...] = r * e_in
    at_ref[...] = -kk * jnp.exp(cum - lw)
    kh_ref[...] = k2 * e_out
    bh_ref[...] = bvec * e_out
    kg_ref[...] = k2 * e_rest
    bg_ref[...] = bvec * e_rest
    vo_ref[...] = v
    bo_ref[...] = bonus
    go_ref[...] = g
    etot = jnp.exp(tot)
    for c in range(tt // chunk):
        gc_ref[c] = etot[c * chunk:c * chunk + 1, :]


def _rwkv_prep(p, shift, mu, w0, w2, a0, a2, g2, k_k, k_a, r_k, dm, tt):
    b, t, _ = p.shape
    rw = dm.rw
    chunk = RWKV_CHUNK
    kern = functools.partial(_rwkv_prep_kernel, tt=tt, rw=rw, chunk=chunk)
    cur = lambda c0, w: pl.BlockSpec((None, tt, w), lambda bi, i: (bi, i, _blk(c0, w)))
    prv = lambda c0, w: pl.BlockSpec(
        (None, SUBLANES, w), lambda bi, i: (bi, jnp.maximum(i * (tt // SUBLANES) - 1, 0), _blk(c0, w)))
    full = lambda shp: pl.BlockSpec(shp, lambda bi, i: (0,) * len(shp))
    tok = jax.ShapeDtypeStruct((b, t, rw), F32)
    tspec = pl.BlockSpec((None, tt, rw), lambda bi, i: (bi, i, 0))
    return pl.pallas_call(
        kern,
        out_shape=(tok,) * 9 + (jax.ShapeDtypeStruct((b, t // chunk, 1, rw), F32),),
        grid=(b, t // tt),
        in_specs=[cur(dm.c_r, rw), cur(dm.c_r + rw, rw), cur(dm.c_r + 2 * rw, rw), cur(dm.c_lora, RWKV_LORA),
                  prv(dm.c_r, rw), prv(dm.c_r + rw, rw), prv(dm.c_r + 2 * rw, rw), prv(dm.c_lora, RWKV_LORA),
                  pl.BlockSpec((None, 1, dm.rcols), lambda bi, i: (bi, 0, 0)),
                  full((1, dm.rcols)), full((1, rw)), full((RWKV_W_RANK, rw)), full((1, rw)),
                  full((RWKV_A_RANK, rw)), full((RWKV_G_RANK, rw)), full((1, rw)), full((1, rw)), full((1, rw))],
        out_specs=(tspec,) * 9 + (pl.BlockSpec((None, tt // chunk, 1, rw), lambda bi, i: (bi, i, 0, 0)),),
        compiler_params=_cparams(("parallel", "parallel")),
        name="rwkv_prep",
    )(p, p, p, p, p, p, p, p, shift, mu, w0, w2, a0, a2, g2, k_k, k_a, r_k)


def _group_norm_out(y, bonus, g, lnw, lnb, ones):
    ym = _head_sum(y, ones) * (1.0 / RWKV_HEAD)
    d = y - ym
    yv = _head_sum(d * d, ones) * (1.0 / RWKV_HEAD)
    return (d * lax.rsqrt(yv + GN_EPS) * lnw + lnb + bonus) * g


def _rwkv_solve_kernel(at_ref, kh_ref, bh_ref, rt_ref, v_ref, kg_ref, bg_ref,
                       wt_ref, p1_ref, y1_ref, arb_ref, kbt_ref, *, chunk, pairs, group):
    n = RWKV_HEAD
    assert chunk == n and 2 * n == LANES
    row = lax.broadcasted_iota(jnp.int32, (chunk, LANES), 0)
    lane = lax.broadcasted_iota(jnp.int32, (chunk, LANES), 1)
    la = lane < n
    src = lane % n
    strict_a, strict_b = (src < row) & la, (src < row) & ~la
    incl_a, incl_b = (src <= row) & la, (src <= row) & ~la
    la2 = jnp.concatenate([la, la], axis=0)
    n_apply = int(math.log2(chunk))
    assert 2 ** n_apply == chunk
    w = lambda m, x: jnp.where(m, x, 0.0)
    stack = lambda a, b: jnp.concatenate([a, b], axis=0)

    for p0 in range(0, pairs, group):
        sls = [slice(p * LANES, (p + 1) * LANES) for p in range(p0, min(p0 + group, pairs))]
        ng = len(sls)
        at = [at_ref[:, sl] for sl in sls]
        v = [v_ref[:, sl] for sl in sls]
        a2 = [_split2(stack(at[i], rt_ref[:, sl])) for i, sl in enumerate(sls)]
        kh = [_split2(kh_ref[:, sl]) for sl in sls]
        bh = [_split2(bh_ref[:, sl]) for sl in sls]
        zero = jnp.zeros((2 * chunk, LANES), BF16)
        pa = [_mm3(tuple(jnp.where(la2, t, zero) for t in a2[i]),
                   tuple(stack(b, k) for b, k in zip(bh[i], kh[i])), _NT) for i in range(ng)]
        pb = [_mm3(tuple(jnp.where(la2, zero, t) for t in a2[i]),
                   tuple(stack(k, b) for b, k in zip(bh[i], kh[i])), _NT) for i in range(ng)]
        pw = [stack(w(strict_a, pa[i][:chunk]), w(strict_b, pb[i][:chunk])) for i in range(ng)]
        mak = [stack(w(strict_b, pa[i][:chunk]), w(strict_a, pb[i][:chunk])) for i in range(ng)]
        ark = [stack(w(incl_b, pa[i][chunk:]), w(incl_a, pb[i][chunk:])) for i in range(ng)]
        vsw = [pltpu.roll(v[i], n, axis=1) for i in range(ng)]
        vx = [_split2(stack(w(la, vsw[i]), w(~la, vsw[i]))) for i in range(ng)]
        kvy = [_mm3(_split2(stack(mak[i], ark[i])), vx[i], _NN) for i in range(ng)]
        for i, sl in enumerate(sls):
            arb_ref[:, sl] = stack(w(incl_a, pa[i][chunk:]), w(incl_b, pb[i][chunk:]))
            y1_ref[:, sl] = pltpu.roll(kvy[i][2 * chunk:3 * chunk] + kvy[i][3 * chunk:], n, axis=1)
            kbt_ref[:, sl] = stack(kg_ref[:, sl], bg_ref[:, sl]).T
        x = [stack(w(la, at[i]), w(~la, at[i])) + kvy[i][:2 * chunk] for i in range(ng)]
        for lvl in range(n_apply):
            if lvl + 1 < n_apply:
                r = [_mm3(_split2(pw[i]), _split2(jnp.concatenate([x[i], pw[i]], axis=1)), _NN) for i in range(ng)]
                x = [x[i] + r[i][:, :LANES] for i in range(ng)]
                pw = [r[i][:, LANES:] for i in range(ng)]
            else:
                x = [x[i] + _mm3(_split2(pw[i]), _split2(x[i]), _NN) for i in range(ng)]
        for i, sl in enumerate(sls):
            wt_ref[:, sl] = w(la, x[i][:chunk]) + w(~la, x[i][chunk:])
            p1_ref[:, sl] = pltpu.roll(w(~la, x[i][:chunk]) + w(la, x[i][chunk:]), n, axis=1)


def _rwkv_solve(at, kh, bh, rt, v, kg, bg, dm):
    b, t, rw = rt.shape
    chunk = RWKV_CHUNK
    pairs = dm.rh // 2
    kern = functools.partial(_rwkv_solve_kernel, chunk=chunk, pairs=pairs, group=min(8, pairs))
    tspec = pl.BlockSpec((None, chunk, rw), lambda bi, c: (bi, c, 0))
    sspec = pl.BlockSpec((None, 2 * chunk, rw), lambda bi, c: (bi, c, 0))
    tok = jax.ShapeDtypeStruct((b, t, rw), F32)
    stk = jax.ShapeDtypeStruct((b, 2 * t, rw), F32)
    return pl.pallas_call(
        kern,
        out_shape=(tok, tok, tok, stk, stk),
        grid=(b, t // chunk),
        in_specs=[tspec] * 7,
        out_specs=(tspec, tspec, tspec, sspec, sspec),
        compiler_params=_cparams(("parallel", "parallel")),
        name="rwkv_solve",
    )(at, kh, bh, rt, v, kg, bg)


def _rwkv_scan_kernel(wt_ref, rt_ref, p1_ref, y1_ref, arb_ref, kbt_ref, v_ref, bo_ref, g_ref, gc_ref,
                      lnw_ref, lnb_ref, s0_ref, y_ref, so_ref, st_ref, yacc_ref, *, chunk, pairs):
    c = pl.program_id(1)
    n = RWKV_HEAD
    row = lax.broadcasted_iota(jnp.int32, (LANES, LANES), 0)
    lane = lax.broadcasted_iota(jnp.int32, (LANES, LANES), 1)
    eye = row == lane
    same_head = (row < n) == (lane < n)
    la = lax.broadcasted_iota(jnp.int32, (chunk, LANES), 1) < n

    @pl.when(c == 0)
    def _():
        z = jnp.zeros((n, n), F32)
        for p in range(pairs):
            st_ref[p] = jnp.concatenate([jnp.concatenate([s0_ref[2 * p], z], axis=1),
                                         jnp.concatenate([z, s0_ref[2 * p + 1]], axis=1)], axis=0)

    sls = [slice(p * LANES, (p + 1) * LANES) for p in range(pairs)]
    s0 = [st_ref[p] for p in range(pairs)]
    ws = [_mm3(_split2(jnp.concatenate([wt_ref[:, sl], rt_ref[:, sl]], axis=0)), _split2(s0[p]), _NN)
          for p, sl in enumerate(sls)]
    u = [p1_ref[:, sl] + ws[p][:chunk] for p, sl in enumerate(sls)]
    for p, sl in enumerate(sls):
        ust = jnp.concatenate([jnp.where(la, u[p], 0.0), jnp.where(la, 0.0, u[p])], axis=0)
        yb = _mm3(_split2(arb_ref[:, sl]), _split2(ust), _NN)
        yacc_ref[:, sl] = ws[p][chunk:] + y1_ref[:, sl] + (yb[:chunk] + yb[chunk:])
    for p, sl in enumerate(sls):
        dg = jnp.where(eye, jnp.broadcast_to(gc_ref[:, sl], (LANES, LANES)), 0.0)
        lhs = jnp.concatenate([kbt_ref[:, sl], dg], axis=1)
        rhs = jnp.concatenate([v_ref[:, sl], u[p], s0[p]], axis=0)
        st_ref[p] = jnp.where(same_head, _mm3(_split2(lhs), _split2(rhs), _NN), 0.0)

    y_ref[...] = _group_norm_out(yacc_ref[...], bo_ref[...], g_ref[...], lnw_ref[...], lnb_ref[...], _head_ones())

    @pl.when(c == pl.num_programs(1) - 1)
    def _():
        for p in range(pairs):
            s = st_ref[p]
            so_ref[2 * p] = s[:n, :n]
            so_ref[2 * p + 1] = s[n:, n:]


def _rwkv_scan(prep, lnw, lnb, s0t, dm):
    rt, at, kh, bh, kg, bg, v, bonus, g, gc = prep
    wt, p1, y1, arb, kbt = _rwkv_solve(at, kh, bh, rt, v, kg, bg, dm)
    b, t, rw = rt.shape
    chunk = RWKV_CHUNK
    heads = dm.rh
    pairs = heads // 2
    kern = functools.partial(_rwkv_scan_kernel, chunk=chunk, pairs=pairs)
    tspec = pl.BlockSpec((None, chunk, rw), lambda bi, c: (bi, c, 0))
    kspec = pl.BlockSpec((None, 2 * chunk, rw), lambda bi, c: (bi, c, 0))
    sspec = pl.BlockSpec((None, heads, RWKV_HEAD, RWKV_HEAD), lambda bi, c: (bi, 0, 0, 0))
    row = pl.BlockSpec((1, rw), lambda bi, c: (0, 0))
    return pl.pallas_call(
        kern,
        out_shape=(jax.ShapeDtypeStruct((b, t, rw), F32),
                   jax.ShapeDtypeStruct((b, heads, RWKV_HEAD, RWKV_HEAD), F32)),
        grid=(b, t // chunk),
        in_specs=[tspec, tspec, tspec, tspec, kspec, kspec, tspec, tspec, tspec,
                  pl.BlockSpec((None, None, 1, rw), lambda bi, c: (bi, c, 0, 0)), row, row, sspec],
        out_specs=(tspec, sspec),
        scratch_shapes=[pltpu.VMEM((pairs, LANES, LANES), F32), pltpu.VMEM((chunk, rw), F32)],
        compiler_params=_cparams(("parallel", "arbitrary")),
        name="rwkv_scan",
    )(wt, rt, p1, y1, arb, kbt, v, bonus, g, gc, lnw, lnb, s0t)


def _rwkv_step_kernel(r_ref, k_ref, v_ref, l_ref, sh_ref, mu_ref, w0_ref, w2_ref, a0_ref, a2_ref, g2_ref,
                      kk_ref, ka_ref, rk_ref, lnw_ref, lnb_ref, s_ref, y_ref, so_ref, *, rw, heads):
    n = RWKV_HEAD
    rows = SUBLANES

    def mixed(x_ref, c0, c1):
        x = jnp.broadcast_to(x_ref[...], (rows, c1 - c0))
        return _rwkv_mix_cols(x, sh_ref[:, c0:c1], mu_ref[:, c0:c1])

    r = mixed(r_ref, 0, rw)
    k = mixed(k_ref, rw, 2 * rw)
    v = mixed(v_ref, 2 * rw, 3 * rw)
    lo = mixed(l_ref, 3 * rw, 3 * rw + RWKV_LORA)
    lw, a, g = _rwkv_lora(lo, w0_ref[...], w2_ref[...], a0_ref[...], a2_ref[...], g2_ref[...])
    w = jnp.exp(lw)
    kk = k * kk_ref[...]
    k2 = k * (1.0 + (a - 1.0) * ka_ref[...])
    eye = lax.broadcasted_iota(jnp.int32, (n, n), 0) == lax.broadcasted_iota(jnp.int32, (n, n), 1)

    def col(rowvec):
        return jnp.sum(jnp.where(eye, jnp.broadcast_to(rowvec, (n, n)), 0.0), axis=1, keepdims=True)

    outs = []
    for h in range(heads):
        sl = slice(h * n, (h + 1) * n)
        kkh = kk[0:1, sl]
        kkh = kkh * lax.rsqrt(jnp.maximum(jnp.sum(kkh * kkh, axis=1, keepdims=True), 1e-12))
        ah, wh, k2h, rh, vh = a[0:1, sl], w[0:1, sl], k2[0:1, sl], r[0:1, sl], v[0:1, sl]
        s = s_ref[h]
        sa = -jnp.sum(s * kkh, axis=1, keepdims=True)
        s = s * wh + sa * (kkh * ah) + col(vh) * k2h
        so_ref[h] = s
        ycol = jnp.sum(s * rh, axis=1, keepdims=True)
        yrow = jnp.sum(jnp.where(eye, jnp.broadcast_to(ycol, (n, n)), 0.0), axis=0, keepdims=True)
        ym = jnp.mean(yrow, axis=1, keepdims=True)
        d = yrow - ym
        yv = jnp.mean(d * d, axis=1, keepdims=True)
        yn = d * lax.rsqrt(yv + GN_EPS) * lnw_ref[:, sl] + lnb_ref[:, sl]
        bonus = jnp.sum(rh * k2h * rk_ref[:, sl], axis=1, keepdims=True) * vh
        outs.append((yn + bonus) * g[0:1, sl])
    y_ref[...] = jnp.concatenate(outs, axis=1)


def _rwkv_step(p, shift, s0, mu, w0, w2, a0, a2, g2, k_k, k_a, r_k, lnw, lnb, dm):
    b = p.shape[0]
    rw, heads = dm.rw, dm.rh
    kern = functools.partial(_rwkv_step_kernel, rw=rw, heads=heads)
    cur = lambda c0, w: pl.BlockSpec((None, 1, w), lambda bi: (bi, 0, _blk(c0, w)))
    full = lambda shp: pl.BlockSpec(shp, lambda bi: (0,) * len(shp))
    sspec = pl.BlockSpec((None, heads, RWKV_HEAD, RWKV_HEAD), lambda bi: (bi, 0, 0, 0))
    return pl.pallas_call(
        kern,
        out_shape=(jax.ShapeDtypeStruct((b, 1, rw), F32),
                   jax.ShapeDtypeStruct((b, heads, RWKV_HEAD, RWKV_HEAD), F32)),
        grid=(b,),
        in_specs=[cur(dm.c_r, rw), cur(dm.c_r + rw, rw), cur(dm.c_r + 2 * rw, rw), cur(dm.c_lora, RWKV_LORA),
                  pl.BlockSpec((None, 1, dm.rcols), lambda bi: (bi, 0, 0)),
                  full((1, dm.rcols)), full((1, rw)), full((RWKV_W_RANK, rw)), full((1, rw)),
                  full((RWKV_A_RANK, rw)), full((RWKV_G_RANK, rw)), full((1, rw)), full((1, rw)), full((1, rw)),
                  full((1, rw)), full((1, rw)), sspec],
        out_specs=(pl.BlockSpec((None, 1, rw), lambda bi: (bi, 0, 0)), sspec),
        compiler_params=_cparams(("parallel",)),
        name="rwkv_step",
    )(p, p, p, p, shift, mu, w0, w2, a0, a2, g2, k_k, k_a, r_k, lnw, lnb, s0)


def _rope_blocks(x, cos2, sin2):
    outs = []
    for c in range(0, x.shape[1], NSA_HEAD):
        xb = x[:, c:c + NSA_HEAD]
        outs.append(xb * cos2 + pltpu.roll(xb, NSA_HEAD // 2, axis=1) * sin2)
    return outs[0] if len(outs) == 1 else jnp.concatenate(outs, axis=1)


def _nsa_prep_kernel(*refs, tt, nq, with_means):
    q_refs, kv_refs = refs[:nq], refs[nq:nq + 6]
    cos_ref, sin_ref, qo_ref, kv_ref, kw_ref = refs[nq + 6:nq + 11]
    cos2, sin2 = cos_ref[...], sin_ref[...]
    pw = NSA_KV_HEADS * NSA_HEAD
    for idx, ref in enumerate(q_refs):
        qo_ref[:, idx * pw:(idx + 1) * pw] = (_rope_blocks(ref[...], cos2, sin2) * (NSA_HEAD ** -0.5)).astype(BF16)
    for idx, ref in enumerate(kv_refs):
        x = ref[...]
        if idx % 2 == 0:
            x = _rope_blocks(x, cos2, sin2)
        if idx < 4:
            kv_ref[:, idx * pw:(idx + 1) * pw] = x
        else:
            kw_ref[:, (idx - 4) * pw:(idx - 3) * pw] = x
        if idx < 2 and with_means:
            m_ref = refs[nq + 11]
            m_ref[:, idx * pw:(idx + 1) * pw] = (
                jnp.sum(x.reshape(tt // NSA_BLOCK, NSA_BLOCK, pw), axis=1) * (1.0 / NSA_BLOCK))


def _nsa_prep(p, cos2, sin2, dm, tt, with_means):
    b, t, _ = p.shape
    pw = dm.part
    nq = dm.nw // pw
    kern = functools.partial(_nsa_prep_kernel, tt=tt, nq=nq, with_means=with_means)
    cur = lambda c0: pl.BlockSpec((None, tt, pw), lambda bi, i: (bi, i, _blk(c0, pw)))
    outs = [jax.ShapeDtypeStruct((b, t, dm.nw), BF16), jax.ShapeDtypeStruct((b, t, 4 * pw), F32),
            jax.ShapeDtypeStruct((b, t, 2 * pw), F32)]
    ospecs = [pl.BlockSpec((None, tt, dm.nw), lambda bi, i: (bi, i, 0)),
              pl.BlockSpec((None, tt, 4 * pw), lambda bi, i: (bi, i, 0)),
              pl.BlockSpec((None, tt, 2 * pw), lambda bi, i: (bi, i, 0))]
    if with_means:
        outs.append(jax.ShapeDtypeStruct((b, t // NSA_BLOCK, 2 * pw), F32))
        ospecs.append(pl.BlockSpec((None, tt // NSA_BLOCK, 2 * pw), lambda bi, i: (bi, i, 0)))
    in_specs = [cur(dm.c_q + k * pw) for k in range(nq)] + [cur(dm.c_kv + k * pw) for k in range(6)]
    in_specs += [pl.BlockSpec((tt, NSA_HEAD), lambda bi, i: (i, 0))] * 2
    return pl.pallas_call(
        kern,
        out_shape=tuple(outs),
        grid=(b, t // tt),
        in_specs=in_specs,
        out_specs=tuple(ospecs),
        compiler_params=_cparams(("parallel", "parallel")),
        name="nsa_prep",
    )(*([p] * (nq + 6)), cos2, sin2)


def _nsa_prompt_kernel(q_ref, kc_ref, vc_ref, ks_ref, vs_ref, kw_ref, vw_ref, gt_ref, o_ref,
                       m_ref, l_ref, acc_ref, *, g, nb, n_sel, tk, lw):
    h = pl.program_id(1)
    i = pl.program_id(2)
    qb = NSA_QTILE
    rows = g * qb
    q = jnp.concatenate([q_ref[:, gi * NSA_HEAD:(gi + 1) * NSA_HEAD] for gi in range(g)], axis=0)
    gsl = [slice(gi * qb, (gi + 1) * qb) for gi in range(g)]
    qpos_c = i * qb + lax.broadcasted_iota(jnp.int32, (qb, 1), 0)
    qpos_r = i * qb + lax.broadcasted_iota(jnp.int32, (1, rows), 1) % qb

    blk = lax.broadcasted_iota(jnp.int32, (nb, 1), 0)
    st = _dot_nt(kc_ref[...], q)
    ok = ((blk + 1) * NSA_BLOCK - 1) <= qpos_r
    mc = jnp.max(jnp.where(ok, st, NEG), axis=0, keepdims=True)
    ec = jnp.where(ok, jnp.exp(st - mc), 0.0)
    den = jnp.sum(ec, axis=0, keepdims=True)
    p_c = ec / jnp.where(den > 0.0, den, 1.0)
    o_cmp = lax.dot_general(p_c.astype(BF16), vc_ref[...].astype(BF16), _TN, preferred_element_type=F32)

    imp = p_c[:, gsl[0]]
    for gi in range(1, g):
        imp = imp + p_c[:, gsl[gi]]
    qp1 = qpos_r[:, 0:qb]
    imp = jnp.where(blk == qp1 // NSA_BLOCK, g + 1.0, jnp.where(blk * NSA_BLOCK <= qp1, imp, -1.0))
    cnt = jnp.zeros((nb, qb), F32)
    for bi in range(nb):
        ci = imp[bi:bi + 1, :]
        beats = (ci > imp) | ((ci == imp) & (blk > bi))
        cnt = cnt + jnp.where(beats, 1.0, 0.0)
    assert nb <= LANES
    drop_t = jnp.where(cnt < n_sel, 0.0, NEG)
    if nb < LANES:
        drop_t = jnp.concatenate([drop_t, jnp.zeros((LANES - nb, qb), F32)], axis=0)
    drop = drop_t.T.astype(BF16)
    q_aug = jnp.concatenate([q, jnp.concatenate([drop] * g, axis=0)], axis=1)

    m_ref[...] = jnp.full((rows, 1), M_INIT, F32)
    l_ref[...] = jnp.zeros((rows, 1), F32)
    acc_ref[...] = jnp.zeros((rows, NSA_HEAD), F32)
    bpt = tk // NSA_BLOCK
    key_blk = lax.broadcasted_iota(jnp.int32, (tk, LANES), 0) // NSA_BLOCK
    blk_lane = lax.broadcasted_iota(jnp.int32, (tk, LANES), 1)
    kcol = lax.broadcasted_iota(jnp.int32, (1, tk), 1)
    n_kt = ((i + 1) * qb + tk - 1) // tk

    def scores(kt):
        k0 = pl.multiple_of(kt * tk, tk)
        one_hot = jnp.where(blk_lane == key_blk + kt * bpt, 1.0, 0.0).astype(BF16)
        k_aug = jnp.concatenate([ks_ref[pl.ds(k0, tk), :].astype(BF16), one_hot], axis=1)
        return _dot_nt(q_aug, k_aug)

    def accumulate(kt, sc, causal):
        k0 = pl.multiple_of(kt * tk, tk)
        vb = vs_ref[pl.ds(k0, tk), :].astype(BF16)
        s_g = [sc[sl] for sl in gsl]
        if causal:
            s_g = [jnp.where((k0 + kcol) <= qpos_c, s, NEG) for s in s_g]
        m_old = [m_ref[sl] for sl in gsl]
        m_new = [jnp.maximum(m_old[gi], jnp.max(s_g[gi], axis=1, keepdims=True)) for gi in range(g)]
        pr = [jnp.exp(s_g[gi] - m_new[gi]) for gi in range(g)]
        pv = _dot(jnp.concatenate([x.astype(BF16) for x in pr], axis=0), vb)
        for gi, sl in enumerate(gsl):
            alpha = jnp.exp(m_old[gi] - m_new[gi])
            l_ref[sl] = alpha * l_ref[sl] + jnp.sum(pr[gi], axis=1, keepdims=True)
            acc_ref[sl] = alpha * acc_ref[sl] + pv[sl]
            m_ref[sl] = m_new[gi]

    def body(kt, sc):
        nxt = scores(kt + 1)
        accumulate(kt, sc, False)
        return nxt

    accumulate(n_kt - 1, lax.fori_loop(0, n_kt - 1, body, scores(0)), True)
    o_sel = acc_ref[...] / l_ref[...]

    w0 = pl.multiple_of(jnp.maximum(i * qb - NSA_WINDOW, 0), qb)
    kwb = kw_ref[pl.ds(w0, lw), :].astype(BF16)
    vwb = vw_ref[pl.ds(w0, lw), :].astype(BF16)
    dp = qpos_c - (w0 + lax.broadcasted_iota(jnp.int32, (1, lw), 1))
    bias_w = jnp.where((dp >= 0) & (dp <= NSA_WINDOW), 0.0, NEG)
    sw = _dot_nt(q, kwb)
    s_w = [sw[sl] + bias_w for sl in gsl]
    e_w = [jnp.exp(s_w[gi] - jnp.max(s_w[gi], axis=1, keepdims=True)) for gi in range(g)]
    den_w = jnp.concatenate([jnp.sum(e, axis=1, keepdims=True) for e in e_w], axis=0)
    o_win = _dot(jnp.concatenate([e.astype(BF16) for e in e_w], axis=0), vwb) / den_w

    gs = _sigmoid(gt_ref[...])
    for gi in range(g):
        acc = None
        for ci, ob in enumerate((o_cmp, o_sel, o_win)):
            c0 = gi * 3 + ci
            c1 = (g + gi) * 3 + ci
            gate = jnp.where(h == 0, gs[:, c0:c0 + 1], gs[:, c1:c1 + 1])
            term = ob[gi * qb:(gi + 1) * qb] * gate
            acc = term if acc is None else acc + term
        o_ref[:, gi * NSA_HEAD:(gi + 1) * NSA_HEAD] = acc


def _nsa_prompt(qr, kv4, kwv, kcvc, p, dm):
    b, t, _ = qr.shape
    assert NSA_KV_HEADS == 2
    g = dm.g
    nb = t // NSA_BLOCK
    n_sel = min(NSA_TOPK, nb)
    tk = min(512, t)
    lw = NSA_WINDOW + NSA_QTILE
    assert t % tk == 0 and t >= lw
    hd = NSA_HEAD
    kern = functools.partial(_nsa_prompt_kernel, g=g, nb=nb, n_sel=n_sel, tk=tk, lw=lw)
    kvspec = lambda c: pl.BlockSpec((None, t, hd), lambda bi, h, i, c=c: (bi, 0, 2 * c + h))
    rows = g * NSA_QTILE
    return pl.pallas_call(
        kern,
        out_shape=jax.ShapeDtypeStruct((b, t, dm.nw), F32),
        grid=(b, NSA_KV_HEADS, t // NSA_QTILE),
        in_specs=[pl.BlockSpec((None, NSA_QTILE, g * hd), lambda bi, h, i: (bi, i, h)),
                  pl.BlockSpec((None, nb, hd), lambda bi, h, i: (bi, 0, h)),
                  pl.BlockSpec((None, nb, hd), lambda bi, h, i: (bi, 0, 2 + h)),
                  kvspec(2), kvspec(3), kvspec(0), kvspec(1),
                  pl.BlockSpec((None, NSA_QTILE, LANES), lambda bi, h, i: (bi, i, _blk(dm.c_ng, LANES)))],
        out_specs=pl.BlockSpec((None, NSA_QTILE, g * hd), lambda bi, h, i: (bi, i, h)),
        scratch_shapes=[pltpu.VMEM((rows, 1), F32), pltpu.VMEM((rows, 1), F32), pltpu.VMEM((rows, hd), F32)],
        compiler_params=_cparams(("parallel", "parallel", "arbitrary")),
        name="nsa_prompt",
    )(qr, kcvc, kcvc, kv4, kv4, kwv, kwv, p)


def _page_means_kernel(pt_ref, *refs, ppb):
    o_ref = refs[ppb]
    for k in range(ppb):
        x = refs[k][...]
        nbp = x.shape[0] // NSA_BLOCK
        o_ref[k] = jnp.sum(x.reshape(nbp, NSA_BLOCK, x.shape[1], x.shape[2]), axis=1) * (1.0 / NSA_BLOCK)


def _page_means(cache, layer, page_table):
    _, _, page, _, prow, hd = cache.shape
    b, n_pages = page_table.shape
    nbp = page // NSA_BLOCK
    ppb = math.gcd(n_pages, 32)
    kern = functools.partial(_page_means_kernel, ppb=ppb)
    pspec = lambda k: pl.BlockSpec((None, None, page, None, prow, hd),
                                   lambda bi, j, pt, k=k: (layer, pt[bi, j * ppb + k], 0, 0, 0, 0))
    out = pl.pallas_call(
        kern,
        out_shape=jax.ShapeDtypeStruct((b, n_pages, nbp, prow, hd), F32),
        grid_spec=pltpu.PrefetchScalarGridSpec(
            num_scalar_prefetch=1,
            grid=(b, n_pages // ppb),
            in_specs=[pspec(k) for k in range(ppb)],
            out_specs=pl.BlockSpec((None, ppb, nbp, prow, hd), lambda bi, j, pt: (bi, j, 0, 0, 0)),
        ),
        compiler_params=_cparams(("parallel", "parallel")),
        name="nsa_page_means",
    )(page_table, *([cache] * ppb))
    return out.reshape(b, n_pages * nbp, prow * hd)


def _nsa_choose_kernel(q_ref, m_ref, oc_ref, idx_ref, *, g, nbp, n_sel):
    hd = NSA_HEAD
    blk = lax.broadcasted_iota(jnp.int32, (1, nbp), 1)
    lane = lax.broadcasted_iota(jnp.int32, (1, LANES), 1)
    for h in range(NSA_KV_HEADS):
        q = jnp.concatenate([q_ref[:, (h * g + gi) * hd:(h * g + gi + 1) * hd] for gi in range(g)], axis=0)
        q = jnp.concatenate([q, jnp.zeros((SUBLANES - g, hd), q.dtype)], axis=0) if g < SUBLANES else q
        kc = m_ref[:, h * hd:(h + 1) * hd]
        vc = m_ref[:, (NSA_KV_HEADS + h) * hd:(NSA_KV_HEADS + h + 1) * hd]
        s = _dot_nt(q, kc)
        m = jnp.max(s, axis=1, keepdims=True)
        e = jnp.exp(s - m)
        p_c = e / jnp.sum(e, axis=1, keepdims=True)
        oc_ref[h] = _dot(p_c, vc)[0:g]
        imp = jnp.sum(p_c[0:g], axis=0, keepdims=True)
        idx = jnp.where(lane == 0, nbp, 0)
        for it in range(1, n_sel):
            best = jnp.max(imp, axis=1, keepdims=True)
            j = jnp.min(jnp.where(imp == best, blk, nbp), axis=1, keepdims=True)
            idx = jnp.where(lane == it, j, idx)
            imp = jnp.where(blk == j, -2.0, imp)
        idx_ref[h] = idx


def _nsa_choose(qr, means, dm, n_sel):
    b = qr.shape[0]
    nbp = means.shape[1]
    g = dm.g
    kern = functools.partial(_nsa_choose_kernel, g=g, nbp=nbp, n_sel=n_sel)
    return pl.pallas_call(
        kern,
        out_shape=(jax.ShapeDtypeStruct((b, NSA_KV_HEADS, g, NSA_HEAD), F32),
                   jax.ShapeDtypeStruct((b, NSA_KV_HEADS, 1, LANES), jnp.int32)),
        grid=(b,),
        in_specs=[pl.BlockSpec((None, 1, dm.nw), lambda bi: (bi, 0, 0)),
                  pl.BlockSpec((None, nbp, means.shape[2]), lambda bi: (bi, 0, 0))],
        out_specs=(pl.BlockSpec((None, NSA_KV_HEADS, g, NSA_HEAD), lambda bi: (bi, 0, 0, 0)),
                   pl.BlockSpec((None, NSA_KV_HEADS, 1, LANES), lambda bi: (bi, 0, 0, 0))),
        compiler_params=_cparams(("parallel",)),
        name="nsa_choose",
    )(qr, means)


def _nsa_sample_kernel(idx_ref, pt_ref, q_ref, *refs, g, n_sel):
    cb_refs = refs[:n_sel - 1]
    kn_ref, vn_ref, wb_ref, kwn_ref, vwn_ref, oc_ref, gt_ref, o_ref = refs[n_sel - 1:]
    h = pl.program_id(1)
    hd = NSA_HEAD
    rows = SUBLANES
    kvh = NSA_KV_HEADS
    q = jnp.concatenate([q_ref[:, gi * hd:(gi + 1) * hd] for gi in range(g)], axis=0)
    if g < rows:
        q = jnp.concatenate([q, jnp.zeros((rows - g, hd), q.dtype)], axis=0)
    qf = q.astype(F32)

    xbs = [cb_ref[...].astype(BF16) for cb_ref in cb_refs]
    srow = lax.broadcasted_iota(jnp.int32, (1, xbs[0].shape[0]), 1) % (4 * kvh)
    keep = jnp.where(srow == 2 * kvh + h, 0.0, NEG)
    s_new = jnp.sum(qf * kn_ref[...].astype(BF16).astype(F32), axis=1, keepdims=True)
    scs = [_dot_nt(q, xb) + keep for xb in xbs]
    m = s_new
    for sc in scs:
        m = jnp.maximum(m, jnp.max(sc, axis=1, keepdims=True))
    prs = [jnp.exp(sc - m) for sc in scs]
    p_new = jnp.exp(s_new - m)
    l = p_new
    acc = p_new.astype(BF16).astype(F32) * vn_ref[...].astype(BF16).astype(F32)
    for pr, xb in zip(prs, xbs):
        l = l + jnp.sum(pr, axis=1, keepdims=True)
        acc = acc + _dot(pltpu.roll(pr, kvh, axis=1), xb)
    o_sel = acc / l

    xw = wb_ref[...].astype(BF16)
    wrow = lax.broadcasted_iota(jnp.int32, (1, xw.shape[0]), 1) % (2 * kvh)
    sw = _dot_nt(q, xw) + jnp.where(wrow == h, 0.0, NEG)
    sn = jnp.sum(qf * kwn_ref[...].astype(BF16).astype(F32), axis=1, keepdims=True)
    mw = jnp.maximum(jnp.max(sw, axis=1, keepdims=True), sn)
    ew = jnp.exp(sw - mw)
    en = jnp.exp(sn - mw)
    den = jnp.sum(ew, axis=1, keepdims=True) + en
    pn = (en / den).astype(BF16).astype(F32)
    o_win = _dot(pltpu.roll(ew / den, kvh, axis=1), xw) + pn * vwn_ref[...].astype(BF16).astype(F32)

    gs = _sigmoid(gt_ref[...])
    o_cmp = oc_ref[...]
    for gi in range(g):
        tot = None
        for ci, ob in enumerate((o_cmp, o_sel, o_win)):
            c0 = gi * 3 + ci
            c1 = (g + gi) * 3 + ci
            gate = jnp.where(h == 0, gs[:, c0:c0 + 1], gs[:, c1:c1 + 1])
            term = ob[gi:gi + 1] * gate
            tot = term if tot is None else tot + term
        o_ref[:, gi * hd:(gi + 1) * hd] = tot


def _nsa_sample(qr, kv4, kwv, cache_rows, win_rows, layer, n_pool, page, lwin, page_table, idx, o_cmp, p, dm, n_sel):
    b = qr.shape[0]
    g, hd = dm.g, NSA_HEAD
    bpp = page // NSA_BLOCK
    brows = NSA_BLOCK * 4 * NSA_KV_HEADS
    wrows = lwin * 2 * NSA_KV_HEADS
    kern = functools.partial(_nsa_sample_kernel, g=g, n_sel=n_sel)

    def sel_spec(slot):
        def index_map(bi, h, idx_ref, pt_ref):
            blk = idx_ref[(bi * NSA_KV_HEADS + h) * LANES + slot]
            return ((layer * n_pool + pt_ref[bi, blk // bpp]) * bpp + blk % bpp, 0)
        return pl.BlockSpec((brows, hd), index_map)

    newspec = lambda c: pl.BlockSpec((None, 1, hd), lambda bi, h, ir, pr, c=c: (bi, 0, 2 * c + h))
    in_specs = [pl.BlockSpec((None, 1, g * hd), lambda bi, h, ir, pr: (bi, 0, h))]
    in_specs += [sel_spec(slot) for slot in range(1, n_sel)]
    in_specs += [newspec(2), newspec(3),
                 pl.BlockSpec((wrows, hd), lambda bi, h, ir, pr: (layer * b + bi, 0)),
                 newspec(0), newspec(1),
                 pl.BlockSpec((None, None, g, hd), lambda bi, h, ir, pr: (bi, h, 0, 0)),
                 pl.BlockSpec((None, 1, LANES), lambda bi, h, ir, pr: (bi, 0, _blk(dm.c_ng, LANES)))]
    return pl.pallas_call(
        kern,
        out_shape=jax.ShapeDtypeStruct((b, 1, dm.nw), F32),
        grid_spec=pltpu.PrefetchScalarGridSpec(
            num_scalar_prefetch=2,
            grid=(b, NSA_KV_HEADS),
            in_specs=in_specs,
            out_specs=pl.BlockSpec((None, 1, g * hd), lambda bi, h, ir, pr: (bi, 0, h)),
        ),
        compiler_params=_cparams(("parallel", "parallel")),
        name="nsa_sample",
    )(idx, page_table, qr, *([cache_rows] * (n_sel - 1)), kv4, kv4, win_rows, kwv, kwv, o_cmp, p)


def _rope_tables(pos):
    half = NSA_HEAD // 2
    inv = jnp.exp(-math.log(ROPE_THETA) * jnp.arange(half, dtype=F32) / half)
    ang = pos.astype(F32)[:, None] * inv[None, :]
    cos, sin = jnp.cos(ang), jnp.sin(ang)
    return jnp.concatenate([cos, cos], axis=1), jnp.concatenate([-sin, sin], axis=1)


def _shift_cols(p_last, dm):
    return jnp.concatenate([p_last[..., dm.c_r:dm.c_r + 3 * dm.rw], p_last[..., dm.c_lora:dm.c_lora + RWKV_LORA]],
                           axis=-1)


def _row(x):
    return x.reshape(1, -1)


def kernel(x_prompt, x_sample, cache_nsa_kv, page_table, state_nsa_window, state_wkv, state_rwkv_shift,
           state_pool, state_ffn_conv, norm_mix, w_in, pool_w, pool_scale, rwkv_mu, rwkv_w0, rwkv_w2,
           rwkv_a0, rwkv_a2, rwkv_g2, rwkv_k_k, rwkv_k_a, rwkv_r_k, rwkv_ln_w, rwkv_ln_b, w_branch, w_out,
           norm_ffn, ffn_up, ffn_conv, ffn_conv_b, ffn_down, norm_final):
    bp, sp, d = x_prompt.shape
    bs, ts, _ = x_sample.shape
    assert ts == 1
    depth = w_in.shape[0]
    dm = Dims(d)
    ff = dm.ff
    hd = NSA_HEAD
    page = cache_nsa_kv.shape[2]
    n_pages = page_table.shape[1]
    past = n_pages * page
    kvw = 4 * NSA_KV_HEADS * hd

    w_in_b = w_in.astype(BF16)
    w_gate_b = w_in_b[:, :, dm.o3:]
    pool_w_b = pool_w.astype(BF16)
    wb_b = w_branch.astype(BF16)
    wo_b = w_out.astype(BF16)
    up_b = ffn_up.astype(BF16)
    down_b = ffn_down.astype(BF16)
    g2_b = rwkv_g2.astype(BF16)
    kk_r = rwkv_k_k.reshape(depth, 1, dm.rw)
    ka_r = rwkv_k_a.reshape(depth, 1, dm.rw)
    rk_r = rwkv_r_k.reshape(depth, 1, dm.rw)

    mp = bp * sp
    tm_p = min(512, sp)
    tm_in = min(1024, sp)
    wide = lambda n: 1024 if n % 1024 == 0 else 512
    tn_ff = 512 if ff % 512 == 0 else 256
    tt_pool = min(512, sp)
    tt_rwkv = min(256, sp)
    tt_nsa = min(512, sp)
    tm_ff = min(1024, sp)
    tps = sp // tm_ff

    def rwkv_params(l):
        return (_row(rwkv_mu[l]), _row(rwkv_w0[l]), rwkv_w2[l], _row(rwkv_a0[l]), rwkv_a2[l], g2_b[l],
                kk_r[l], ka_r[l], rk_r[l])

    cos_p, sin_p = _rope_tables(jnp.arange(sp, dtype=jnp.int32))
    x = x_prompt.reshape(mp, d)
    zeros_hist16 = jnp.zeros((bp, POOL_HIST + 1, dm.pw), F32)
    zeros_shift = jnp.zeros((bp, 1, dm.rcols), F32)
    zeros_wkv = jnp.zeros((bp, dm.rh, RWKV_HEAD, RWKV_HEAD), F32)
    zeros_conv = jnp.zeros((bp, CONV_W - 1, 2 * ff), F32)
    wl = min(NSA_WINDOW, sp)
    p_kv, p_win, p_wkv, p_shift, p_pool, p_conv = [], [], [], [], [], []
    for l in range(depth):
        p = _norm_matmul(x, _row(norm_mix[l]), w_in_b, l, dm.np, tm_in, wide(dm.np)).reshape(bp, sp, dm.np)
        pg = _norm_matmul(x, _row(norm_mix[l]), w_gate_b, l, N_BRANCH * d, tm_in, wide(N_BRANCH * d))
        ya = _pool_prompt(p, zeros_hist16, pool_w_b[l], _row(pool_scale[l]), dm, tt_pool)
        prep = _rwkv_prep(p, zeros_shift, *rwkv_params(l), dm, tt_rwkv)
        yb, s_t = _rwkv_scan(prep, _row(rwkv_ln_w[l]), _row(rwkv_ln_b[l]), zeros_wkv, dm)
        qr, kv4, kwv, kcvc = _nsa_prep(p, cos_p, sin_p, dm, tt_nsa, True)
        yc = _nsa_prompt(qr, kv4, kwv, kcvc, p, dm)
        mrg = _branch_merge(ya.reshape(mp, -1), yb.reshape(mp, -1), yc.reshape(mp, -1), pg, wb_b, l, dm,
                            tm_p, min(1024, d))
        x = _out_proj(x, mrg, wo_b, l,tm_p, min(1024, d))
        x, za, zv = _ffn(x, _row(norm_ffn[l]), up_b, ffn_conv[l], _row(ffn_conv_b[l]), down_b, l,
                         zeros_conv, ff, tm_ff, tn_ff, tps, False)
        p_kv.append(kv4.reshape(bp, sp, 4, NSA_KV_HEADS, hd))
        p_win.append(kwv[:, sp - wl:].reshape(bp, wl, 2, NSA_KV_HEADS, hd))
        p_wkv.append(jnp.swapaxes(s_t, -1, -2))
        p_shift.append(_shift_cols(p[:, sp - 1:, :], dm))
        p_pool.append(p[:, sp - POOL_HIST:, dm.c_pool:dm.c_pool + dm.pw])
        zl = jnp.concatenate([za, zv], axis=-1).reshape(bp, tps, 2, 2 * ff)
        p_conv.append(zl[:, tps - 1])
    y_prompt = _final_norm(x, _row(norm_final), tm_p).reshape(bp, sp, d)

    pos_s = past
    cos_s, sin_s = _rope_tables(jnp.full((1,), pos_s, dtype=jnp.int32))
    xs = x_sample.reshape(bs, d)
    nbp = past // NSA_BLOCK
    n_sel = min(NSA_TOPK, nbp + 1)
    n_pool = cache_nsa_kv.shape[1]
    lwin = state_nsa_window.shape[2]
    cache_pairs = cache_nsa_kv.reshape(depth, n_pool, page, 2, 2 * NSA_KV_HEADS, hd)
    cache_rows = cache_nsa_kv.reshape(-1, hd)
    win_rows = state_nsa_window.reshape(-1, hd)
    s_kv, s_win, s_wkv, s_shift, s_pool, s_conv = [], [], [], [], [], []
    for l in range(depth):
        p = _norm_matmul(xs, _row(norm_mix[l]), w_in_b, l, dm.np, bs, wide(dm.np)).reshape(bs, 1, dm.np)
        pg = _norm_matmul(xs, _row(norm_mix[l]), w_gate_b, l, N_BRANCH * d, bs, wide(N_BRANCH * d))
        e16 = jnp.concatenate([state_pool[l], p[:, :, dm.c_pool:dm.c_pool + dm.pw]], axis=1)
        ya = _pool_sample(e16, pool_w_b[l], _row(pool_scale[l]), dm, pos_s)
        yb, s_new = _rwkv_step(p, state_rwkv_shift[l], state_wkv[l], *rwkv_params(l),
                               _row(rwkv_ln_w[l]), _row(rwkv_ln_b[l]), dm)
        qr, kv4, kwv = _nsa_prep(p, cos_s, sin_s, dm, 1, False)
        means = _page_means(cache_pairs, l, page_table)
        o_cmp, idx = _nsa_choose(qr, means, dm, n_sel)
        yc = _nsa_sample(qr, kv4, kwv, cache_rows, win_rows, l, n_pool, page, lwin, page_table, idx.reshape(-1),
                         o_cmp, p, dm, n_sel)
        mrg = _branch_merge(ya, yb.reshape(bs, -1), yc.reshape(bs, -1), pg, wb_b, l, dm, bs, min(1024, d))
        xs = _out_proj(xs, mrg, wo_b, l,bs, min(1024, d))
        xs, za, zv = _ffn(xs, _row(norm_ffn[l]), up_b, ffn_conv[l], _row(ffn_conv_b[l]), down_b, l,
                          state_ffn_conv[l], ff, bs, tn_ff, 1, True)
        s_kv.append(kv4.reshape(bs, 1, 4, NSA_KV_HEADS, hd))
        wk = jnp.concatenate([state_nsa_window[l], kwv.reshape(bs, 1, 2, NSA_KV_HEADS, hd)], axis=1)
        s_win.append(wk[:, -NSA_WINDOW:])
        s_wkv.append(s_new)
        s_shift.append(_shift_cols(p, dm))
        s_pool.append(e16[:, 1:])
        z_new = jnp.concatenate([za, zv], axis=-1)[:, None, :]
        s_conv.append(jnp.concatenate([state_ffn_conv[l][:, 1:], z_new], axis=1))
    y_sample = _final_norm(xs, _row(norm_final), bs).reshape(bs, 1, d)

    st = lambda xs_: jnp.stack(xs_)
    return (y_prompt, y_sample, st(p_kv), st(p_win), st(p_wkv), st(p_shift), st(p_pool), st(p_conv),
            st(s_kv), st(s_win), st(s_wkv), st(s_shift), st(s_pool), st(s_conv))
```

```python
import functools
import math

import jax
import jax.numpy as jnp
from jax import lax
from jax.experimental import pallas as pl
from jax.experimental.pallas import tpu as pltpu

F32 = jnp.float32
BF16 = jnp.bfloat16
HIGHEST = lax.Precision.HIGHEST

POOL_GROUPS = 4
POOL_WINDOWS = (2, 4, 8, 16)
POOL_HIST = 15
RWKV_HEAD = 64
RWKV_W_RANK = 64
RWKV_A_RANK = 64
RWKV_G_RANK = 128
RWKV_LORA = RWKV_W_RANK + RWKV_A_RANK + RWKV_G_RANK
RWKV_DECAY_SCALE = 0.6065306597126334
GN_EPS = 64e-5
NSA_HEAD = 128
NSA_KV_HEADS = 2
NSA_BLOCK = 64
NSA_TOPK = 16
NSA_WINDOW = 512
NSA_QTILE = 128
N_BRANCH = 3
CONV_W = 3
ROPE_THETA = 10000.0
RMS_EPS = 1e-6
NEG = -1e30
M_INIT = -1e29

LANES = 128
SUBLANES = 8
VMEM_LIMIT = 56 * 1024 * 1024
RWKV_CHUNK = 64
FFN_SUB_ROWS = 128


class Dims:
    def __init__(self, d_model):
        d = d_model
        self.d = d
        self.pw = d // 2
        self.cg = self.pw // POOL_GROUPS
        self.rw = d // 2
        self.rh = self.rw // RWKV_HEAD
        self.rcols = 3 * self.rw + RWKV_LORA
        self.nw = d // 2
        self.nh = self.nw // NSA_HEAD
        self.g = self.nh // NSA_KV_HEADS
        self.kvc = 6 * NSA_KV_HEADS * NSA_HEAD
        self.ngate = 3 * self.nh
        self.ff = 256 * ((8 * d // 3 + 255) // 256)
        self.o1 = self.pw
        self.o2 = self.o1 + self.rcols
        self.o3 = self.o2 + self.nw + self.kvc + self.ngate
        self.c_pool = 0
        self.c_r = self.o1
        self.c_lora = self.o1 + 3 * self.rw
        self.c_q = self.o2
        self.c_kv = self.o2 + self.nw
        self.c_ng = self.c_kv + self.kvc
        self.tn_in = 512
        self.np = -(-self.o3 // self.tn_in) * self.tn_in
        self.part = NSA_KV_HEADS * NSA_HEAD


def _blk(offset, width):
    assert offset % width == 0, (offset, width)
    return offset // width


def _cparams(sem):
    return pltpu.CompilerParams(dimension_semantics=sem, vmem_limit_bytes=VMEM_LIMIT)


def _dot(a, b):
    return jnp.dot(a.astype(BF16), b.astype(BF16), preferred_element_type=F32)


def _dot_nt(a, b):
    return lax.dot_general(a.astype(BF16), b.astype(BF16), (((1,), (1,)), ((), ())),
                           preferred_element_type=F32)


def _dot_hi(a, b):
    return jnp.dot(a, b, precision=HIGHEST, preferred_element_type=F32)


_NN = (((1,), (0,)), ((), ()))
_NT = (((1,), (1,)), ((), ()))
_TN = (((0,), (0,)), ((), ()))


def _split2(x):
    hi = x.astype(BF16)
    return hi, (x - hi.astype(F32)).astype(BF16)


def _mm3(a, b, dims):
    d = lambda p, q: lax.dot_general(p, q, dims, preferred_element_type=F32)
    return d(a[0], b[0]) + d(a[0], b[1]) + d(a[1], b[0])


def _split3(x):
    hi = x.astype(BF16)
    r1 = x - hi.astype(F32)
    mid = r1.astype(BF16)
    lo = (r1 - mid.astype(F32)).astype(BF16)
    return hi, mid, lo


def _dot_exact_lhs(m_bf16, x):
    hi, mid, lo = _split3(x)
    d = lambda p: jnp.dot(m_bf16, p, preferred_element_type=F32)
    return d(hi) + d(mid) + d(lo)


def _dot_exact_rhs(x, m_bf16):
    hi, mid, lo = _split3(x)
    d = lambda p: jnp.dot(p, m_bf16, preferred_element_type=F32)
    return d(hi) + d(mid) + d(lo)


def _head_ones():
    i = lax.broadcasted_iota(jnp.int32, (LANES, LANES), 0) // RWKV_HEAD
    j = lax.broadcasted_iota(jnp.int32, (LANES, LANES), 1) // RWKV_HEAD
    return (i == j).astype(BF16)


def _head_sum(x, ones):
    parts = [_dot_exact_rhs(x[:, c:c + LANES], ones) for c in range(0, x.shape[1], LANES)]
    return parts[0] if len(parts) == 1 else jnp.concatenate(parts, axis=1)


def _rmsnorm_val(x, g):
    ms = jnp.mean(x * x, axis=-1, keepdims=True)
    return x * lax.rsqrt(ms + RMS_EPS) * g


def _sigmoid(x):
    return 1.0 / (1.0 + jnp.exp(-x))


def _norm_matmul_kernel(x_ref, g_ref, w_ref, o_ref, h_ref):
    @pl.when(pl.program_id(1) == 0)
    def _():
        h_ref[...] = _rmsnorm_val(x_ref[...], g_ref[...]).astype(BF16)

    o_ref[...] = jnp.dot(h_ref[...], w_ref[...], preferred_element_type=F32)


def _norm_matmul(x, g, w, layer, n, tm, tn):
    m, d = x.shape
    assert n % tn == 0 and n <= w.shape[2]
    return pl.pallas_call(
        _norm_matmul_kernel,
        out_shape=jax.ShapeDtypeStruct((m, n), F32),
        grid=(m // tm, n // tn),
        in_specs=[pl.BlockSpec((tm, d), lambda i, j: (i, 0)),
                  pl.BlockSpec((1, d), lambda i, j: (0, 0)),
                  pl.BlockSpec((None, d, tn), lambda i, j: (layer, 0, j))],
        out_specs=pl.BlockSpec((tm, tn), lambda i, j: (i, j)),
        scratch_shapes=[pltpu.VMEM((tm, d), BF16)],
        compiler_params=_cparams(("parallel", "arbitrary")),
        name="norm_matmul",
    )(x, g, w)


def _branch_merge_kernel(ya_ref, yb_ref, yc_ref, ga_ref, gb_ref, gc_ref, w_ref, o_ref):
    acc = _dot(ya_ref[...], w_ref[0]) * _sigmoid(ga_ref[...])
    acc = acc + _dot(yb_ref[...], w_ref[1]) * _sigmoid(gb_ref[...])
    acc = acc + _dot(yc_ref[...], w_ref[2]) * _sigmoid(gc_ref[...])
    o_ref[...] = acc.astype(BF16)


def _branch_merge(ya, yb, yc, p, wb, layer, dm, tm, tn):
    m = ya.shape[0]
    d, hw = dm.d, dm.d // 2
    nb = d // tn
    yspec = pl.BlockSpec((tm, hw), lambda j, i: (i, 0))
    gspec = lambda k: pl.BlockSpec((tm, tn), lambda j, i, k=k: (i, k * nb + j))
    return pl.pallas_call(
        _branch_merge_kernel,
        out_shape=jax.ShapeDtypeStruct((m, d), BF16),
        grid=(nb, m // tm),
        in_specs=[yspec, yspec, yspec, gspec(0), gspec(1), gspec(2),
                  pl.BlockSpec((None, N_BRANCH, hw, tn), lambda j, i: (layer, 0, 0, j))],
        out_specs=pl.BlockSpec((tm, tn), lambda j, i: (i, j)),
        compiler_params=_cparams(("parallel", "parallel")),
        name="branch_merge",
    )(ya, yb, yc, p, p, p, wb)


def _out_proj_kernel(x_ref, m_ref, w_ref, o_ref):
    o_ref[...] = x_ref[...] + jnp.dot(m_ref[...], w_ref[...], preferred_element_type=F32)


def _out_proj(x, mrg, wo, layer, tm, tn):
    m, d = x.shape
    return pl.pallas_call(
        _out_proj_kernel,
        out_shape=jax.ShapeDtypeStruct((m, d), F32),
        grid=(m // tm, d // tn),
        in_specs=[pl.BlockSpec((tm, tn), lambda i, j: (i, j)),
                  pl.BlockSpec((tm, d), lambda i, j: (i, 0)),
                  pl.BlockSpec((None, d, tn), lambda i, j: (layer, 0, j))],
        out_specs=pl.BlockSpec((tm, tn), lambda i, j: (i, j)),
        compiler_params=_cparams(("parallel", "parallel")),
        name="out_proj",
    )(x, mrg, wo)


def _ffn_kernel(x_ref, g_ref, wa_ref, wv_ref, cwa_ref, cwv_ref, cba_ref, cbv_ref, wd_ref,
                ha_ref, hv_ref, o_ref, za_ref, zv_ref, h_ref, ca_ref, cv_ref, *, tm, sub, tps, rows_are_seqs):
    i = pl.program_id(0)
    j = pl.program_id(1)

    @pl.when(j == 0)
    def _():
        x = x_ref[...]
        h_ref[...] = _rmsnorm_val(x, g_ref[...]).astype(BF16)
        o_ref[...] = x

    def conv_mix(z, zs1, zs2, cw_ref, cb_ref):
        return zs2 * cw_ref[0:1] + zs1 * cw_ref[1:2] + z * cw_ref[2:3] + cb_ref[...]

    if rows_are_seqs:
        h = h_ref[...]
        za = jnp.dot(h, wa_ref[...], preferred_element_type=F32)
        zv = jnp.dot(h, wv_ref[...], preferred_element_type=F32)
        za_ref[...] = za
        zv_ref[...] = zv
        ca = conv_mix(za, ha_ref[:, 1, :], ha_ref[:, 0, :], cwa_ref, cba_ref)
        cv = conv_mix(zv, hv_ref[:, 1, :], hv_ref[:, 0, :], cwv_ref, cbv_ref)
        o_ref[...] += _dot(ca * _sigmoid(ca) * cv, wd_ref[...])
        return

    @pl.when((i % tps) == 0)
    def _():
        ca_ref[j] = ha_ref[...]
        cv_ref[j] = hv_ref[...]

    rs = [slice(s * sub, (s + 1) * sub) for s in range(tm // sub)]
    zas = [jnp.dot(h_ref[r], wa_ref[...], preferred_element_type=F32) for r in rs]
    zvs = [jnp.dot(h_ref[r], wv_ref[...], preferred_element_type=F32) for r in rs]
    row = lax.broadcasted_iota(jnp.int32, (sub, zas[0].shape[1]), 0)

    def conv(zs, s, carry_ref, cw_ref, cb_ref):
        z = zs[s]
        prev = carry_ref[j] if s == 0 else zs[s - 1][sub - 2:sub]
        zs1 = jnp.where(row == 0, prev[1:2], pltpu.roll(z, 1, axis=0))
        zs2 = jnp.where(row == 0, prev[0:1], jnp.where(row == 1, prev[1:2], pltpu.roll(z, 2, axis=0)))
        return conv_mix(z, zs1, zs2, cw_ref, cb_ref)

    for s, r in enumerate(rs):
        ca = conv(zas, s, ca_ref, cwa_ref, cba_ref)
        cv = conv(zvs, s, cv_ref, cwv_ref, cbv_ref)
        o_ref[r] += _dot(ca * _sigmoid(ca) * cv, wd_ref[...])
    za_last = zas[-1][sub - 2:sub]
    zv_last = zvs[-1][sub - 2:sub]
    ca_ref[j] = za_last
    cv_ref[j] = zv_last
    za_ref[...] = za_last
    zv_ref[...] = zv_last


def _ffn(x, g, w_up, cw, cb, w_down, layer, hist, ff, tm, tn, tps, rows_are_seqs):
    m, d = x.shape
    nj = ff // tn
    nm = m // tm
    if rows_are_seqs:
        assert nm == 1
        hspec_a = pl.BlockSpec((m, 2, tn), lambda i, j: (0, 0, j))
        hspec_v = pl.BlockSpec((m, 2, tn), lambda i, j: (0, 0, nj + j))
        zshape = jax.ShapeDtypeStruct((m, ff), F32)
        zspec = pl.BlockSpec((m, tn), lambda i, j: (0, j))
    else:
        hspec_a = pl.BlockSpec((None, 2, tn), lambda i, j: (i // tps, 0, j))
        hspec_v = pl.BlockSpec((None, 2, tn), lambda i, j: (i // tps, 0, nj + j))
        zshape = jax.ShapeDtypeStruct((nm, 2, ff), F32)
        zspec = pl.BlockSpec((None, 2, tn), lambda i, j: (i, 0, j))
    sub = FFN_SUB_ROWS if tm % FFN_SUB_ROWS == 0 else tm
    kern = functools.partial(_ffn_kernel, tm=tm, sub=sub, tps=tps, rows_are_seqs=rows_are_seqs)
    return pl.pallas_call(
        kern,
        out_shape=(jax.ShapeDtypeStruct((m, d), F32), zshape, zshape),
        grid=(nm, nj),
        in_specs=[pl.BlockSpec((tm, d), lambda i, j: (i, 0)),
                  pl.BlockSpec((1, d), lambda i, j: (0, 0)),
                  pl.BlockSpec((None, d, tn), lambda i, j: (layer, 0, j)),
                  pl.BlockSpec((None, d, tn), lambda i, j: (layer, 0, nj + j)),
                  pl.BlockSpec((CONV_W, tn), lambda i, j: (0, j)),
                  pl.BlockSpec((CONV_W, tn), lambda i, j: (0, nj + j)),
                  pl.BlockSpec((1, tn), lambda i, j: (0, j)),
                  pl.BlockSpec((1, tn), lambda i, j: (0, nj + j)),
                  pl.BlockSpec((None, tn, d), lambda i, j: (layer, j, 0)),
                  hspec_a, hspec_v],
        out_specs=(pl.BlockSpec((tm, d), lambda i, j: (i, 0)), zspec, zspec),
        scratch_shapes=[pltpu.VMEM((tm, d), BF16),
                        pltpu.VMEM((nj, 2, tn), F32),
                        pltpu.VMEM((nj, 2, tn), F32)],
        compiler_params=_cparams(("arbitrary", "arbitrary")),
        name="conv_ffn",
    )(x, g, w_up, w_up, cw, cw, cb, cb, w_down, hist, hist)


def _final_norm_kernel(x_ref, g_ref, o_ref):
    o_ref[...] = _rmsnorm_val(x_ref[...], g_ref[...])


def _final_norm(x, g, tm):
    m, d = x.shape
    return pl.pallas_call(
        _final_norm_kernel,
        out_shape=jax.ShapeDtypeStruct((m, d), F32),
        grid=(m // tm,),
        in_specs=[pl.BlockSpec((tm, d), lambda i: (i, 0)), pl.BlockSpec((1, d), lambda i: (0, 0))],
        out_specs=pl.BlockSpec((tm, d), lambda i: (i, 0)),
        compiler_params=_cparams(("parallel",)),
        name="final_norm",
    )(x, g)


def _pool_kernel(u_ref, hist_ref, w_ref, sc_ref, o_ref, e_ref, *, tt, cg):
    i = pl.program_id(1)
    hrows = POOL_HIST + 1

    @pl.when(i == 0)
    def _():
        e_ref[0:hrows] = hist_ref[...]

    @pl.when(i > 0)
    def _():
        e_ref[0:hrows] = e_ref[tt:tt + hrows]

    u = u_ref[...]
    e_ref[hrows:hrows + tt] = u
    pos = i * tt + lax.broadcasted_iota(jnp.int32, (tt, 1), 0)
    for gi, w in enumerate(POOL_WINDOWS):
        cs = slice(gi * cg, (gi + 1) * cg)
        s = e_ref[:, cs]
        sh = 1
        while sh < w:
            s = s + pltpu.roll(s, sh, axis=0)
            sh *= 2
        cnt = jnp.minimum(w, pos + 1).astype(F32)
        dlt = s[hrows:] / cnt - u[:, cs]
        o_ref[:, cs] = _dot(dlt, w_ref[gi]) * sc_ref[:, cs]


def _pool_prompt(p, hist16, w, scale, dm, tt):
    b, t, _ = p.shape
    kern = functools.partial(_pool_kernel, tt=tt, cg=dm.cg)
    return pl.pallas_call(
        kern,
        out_shape=jax.ShapeDtypeStruct((b, t, dm.pw), F32),
        grid=(b, t // tt),
        in_specs=[pl.BlockSpec((None, tt, dm.pw), lambda bi, i: (bi, i, _blk(dm.c_pool, dm.pw))),
                  pl.BlockSpec((None, POOL_HIST + 1, dm.pw), lambda bi, i: (bi, 0, 0)),
                  pl.BlockSpec((POOL_GROUPS, dm.cg, dm.cg), lambda bi, i: (0, 0, 0)),
                  pl.BlockSpec((1, dm.pw), lambda bi, i: (0, 0))],
        out_specs=pl.BlockSpec((None, tt, dm.pw), lambda bi, i: (bi, i, 0)),
        scratch_shapes=[pltpu.VMEM((tt + POOL_HIST + 1, dm.pw), F32)],
        compiler_params=_cparams(("parallel", "arbitrary")),
        name="pool_mixer",
    )(p, hist16, w, scale)


def _pool_sample_kernel(e_ref, w_ref, sc_ref, o_ref, *, cg, pos):
    e = e_ref[...]
    hrows = POOL_HIST + 1
    row = lax.broadcasted_iota(jnp.int32, e.shape, 1)
    u = e[:, hrows - 1, :]
    for gi, w in enumerate(POOL_WINDOWS):
        cs = slice(gi * cg, (gi + 1) * cg)
        win = jnp.sum(jnp.where(row >= hrows - w, e, 0.0)[:, :, cs], axis=1)
        dlt = win / float(min(w, pos + 1)) - u[:, cs]
        o_ref[:, cs] = _dot(dlt, w_ref[gi]) * sc_ref[:, cs]


def _pool_sample(e16, w, scale, dm, pos):
    b = e16.shape[0]
    kern = functools.partial(_pool_sample_kernel, cg=dm.cg, pos=pos)
    return pl.pallas_call(
        kern,
        out_shape=jax.ShapeDtypeStruct((b, dm.pw), F32),
        compiler_params=pltpu.CompilerParams(vmem_limit_bytes=VMEM_LIMIT),
        name="pool_mixer_step",
    )(e16, w, scale)


def _rwkv_mix_cols(x, xprev, mu):
    return x + (xprev - x) * mu


def _rwkv_lora(pl_, w0, w2, a0, a2, g2):
    w_in = pl_[:, 0:RWKV_W_RANK]
    a_in = pl_[:, RWKV_W_RANK:RWKV_W_RANK + RWKV_A_RANK]
    g_in = pl_[:, RWKV_W_RANK + RWKV_A_RANK:]
    lw = -RWKV_DECAY_SCALE * _sigmoid(w0 + _dot_hi(jnp.tanh(w_in), w2))
    a = _sigmoid(a0 + _dot_hi(a_in, a2))
    g = _dot(_sigmoid(g_in), g2)
    return lw, a, g


def _rwkv_prep_kernel(r_ref, k_ref, v_ref, l_ref, rp_ref, kp_ref, vp_ref, lp_ref, sh_ref, mu_ref,
                      w0_ref, w2_ref, a0_ref, a2_ref, g2_ref, kk_ref, ka_ref, rk_ref,
                      rt_ref, at_ref, kh_ref, bh_ref, kg_ref, bg_ref, vo_ref, bo_ref, go_ref, gc_ref,
                      *, tt, rw, chunk):
    i = pl.program_id(1)
    first = i == 0
    row1 = lax.broadcasted_iota(jnp.int32, (tt, 1), 0)

    def mixed(x_ref, xp_ref, c0, c1):
        x = x_ref[...]
        carry = jnp.where(first, sh_ref[:, c0:c1], xp_ref[SUBLANES - 1:SUBLANES, :])
        xprev = jnp.where(row1 == 0, carry, pltpu.roll(x, 1, axis=0))
        return _rwkv_mix_cols(x, xprev, mu_ref[:, c0:c1])

    r = mixed(r_ref, rp_ref, 0, rw)
    k = mixed(k_ref, kp_ref, rw, 2 * rw)
    v = mixed(v_ref, vp_ref, 2 * rw, 3 * rw)
    lo = mixed(l_ref, lp_ref, 3 * rw, 3 * rw + RWKV_LORA)
    lw, a, g = _rwkv_lora(lo, w0_ref[...], w2_ref[...], a0_ref[...], a2_ref[...], g2_ref[...])

    ones = _head_ones()
    kk = k * kk_ref[...]
    kk = kk * lax.rsqrt(jnp.maximum(_head_sum(kk * kk, ones), 1e-12))
    k2 = k * (1.0 + (a - 1.0) * ka_ref[...])
    bonus = _head_sum(r * k2 * rk_ref[...], ones) * v
    bvec = kk * a

    ti = lax.broadcasted_iota(jnp.int32, (tt, tt), 0)
    si = lax.broadcasted_iota(jnp.int32, (tt, tt), 1)
    same = (ti // chunk) == (si // chunk)
    tri = (same & (si <= ti)).astype(BF16)
    blk = same.astype(BF16)
    cum = _dot_exact_lhs(tri, lw)
    tot = _dot_exact_lhs(blk, lw)
    e_in = jnp.exp(cum)
    e_out = jnp.exp(-cum)
    e_rest = jnp.exp(tot - cum)
    rt_ref[...] = r * e_in
    at_ref[...] = -kk * jnp.exp(cum - lw)
    kh_ref[...] = k2 * e_out
    bh_ref[...] = bvec * e_out
    kg_ref[...] = k2 * e_rest
    bg_ref[...] = bvec * e_rest
    vo_ref[...] = v
    bo_ref[...] = bonus
    go_ref[...] = g
    etot = jnp.exp(tot)
    for c in range(tt // chunk):
        gc_ref[c] = etot[c * chunk:c * chunk + 1, :]


def _rwkv_prep(p, shift, mu, w0, w2, a0, a2, g2, k_k, k_a, r_k, dm, tt):
    b, t, _ = p.shape
    rw = dm.rw
    chunk = RWKV_CHUNK
    kern = functools.partial(_rwkv_prep_kernel, tt=tt, rw=rw, chunk=chunk)
    cur = lambda c0, w: pl.BlockSpec((None, tt, w), lambda bi, i: (bi, i, _blk(c0, w)))
    prv = lambda c0, w: pl.BlockSpec(
        (None, SUBLANES, w), lambda bi, i: (bi, jnp.maximum(i * (tt // SUBLANES) - 1, 0), _blk(c0, w)))
    full = lambda shp: pl.BlockSpec(shp, lambda bi, i: (0,) * len(shp))
    tok = jax.ShapeDtypeStruct((b, t, rw), F32)
    tspec = pl.BlockSpec((None, tt, rw), lambda bi, i: (bi, i, 0))
    return pl.pallas_call(
        kern,
        out_shape=(tok,) * 9 + (jax.ShapeDtypeStruct((b, t // chunk, 1, rw), F32),),
        grid=(b, t // tt),
        in_specs=[cur(dm.c_r, rw), cur(dm.c_r + rw, rw), cur(dm.c_r + 2 * rw, rw), cur(dm.c_lora, RWKV_LORA),
                  prv(dm.c_r, rw), prv(dm.c_r + rw, rw), prv(dm.c_r + 2 * rw, rw), prv(dm.c_lora, RWKV_LORA),
                  pl.BlockSpec((None, 1, dm.rcols), lambda bi, i: (bi, 0, 0)),
                  full((1, dm.rcols)), full((1, rw)), full((RWKV_W_RANK, rw)), full((1, rw)),
                  full((RWKV_A_RANK, rw)), full((RWKV_G_RANK, rw)), full((1, rw)), full((1, rw)), full((1, rw))],
        out_specs=(tspec,) * 9 + (pl.BlockSpec((None, tt // chunk, 1, rw), lambda bi, i: (bi, i, 0, 0)),),
        compiler_params=_cparams(("parallel", "parallel")),
        name="rwkv_prep",
    )(p, p, p, p, p, p, p, p, shift, mu, w0, w2, a0, a2, g2, k_k, k_a, r_k)


def _group_norm_out(y, bonus, g, lnw, lnb, ones):
    ym = _head_sum(y, ones) * (1.0 / RWKV_HEAD)
    d = y - ym
    yv = _head_sum(d * d, ones) * (1.0 / RWKV_HEAD)
    return (d * lax.rsqrt(yv + GN_EPS) * lnw + lnb + bonus) * g


def _rwkv_solve_kernel(at_ref, kh_ref, bh_ref, rt_ref, v_ref, kg_ref, bg_ref,
                       wt_ref, p1_ref, y1_ref, arb_ref, kbt_ref, *, chunk, pairs, group):
    n = RWKV_HEAD
    assert chunk == n and 2 * n == LANES
    row = lax.broadcasted_iota(jnp.int32, (chunk, LANES), 0)
    lane = lax.broadcasted_iota(jnp.int32, (chunk, LANES), 1)
    la = lane < n
    src = lane % n
    strict_a, strict_b = (src < row) & la, (src < row) & ~la
    incl_a, incl_b = (src <= row) & la, (src <= row) & ~la
    la2 = jnp.concatenate([la, la], axis=0)
    n_apply = int(math.log2(chunk))
    assert 2 ** n_apply == chunk
    w = lambda m, x: jnp.where(m, x, 0.0)
    stack = lambda a, b: jnp.concatenate([a, b], axis=0)

    for p0 in range(0, pairs, group):
        sls = [slice(p * LANES, (p + 1) * LANES) for p in range(p0, min(p0 + group, pairs))]
        ng = len(sls)
        at = [at_ref[:, sl] for sl in sls]
        v = [v_ref[:, sl] for sl in sls]
        a2 = [_split2(stack(at[i], rt_ref[:, sl])) for i, sl in enumerate(sls)]
        kh = [_split2(kh_ref[:, sl]) for sl in sls]
        bh = [_split2(bh_ref[:, sl]) for sl in sls]
        zero = jnp.zeros((2 * chunk, LANES), BF16)
        pa = [_mm3(tuple(jnp.where(la2, t, zero) for t in a2[i]),
                   tuple(stack(b, k) for b, k in zip(bh[i], kh[i])), _NT) for i in range(ng)]
        pb = [_mm3(tuple(jnp.where(la2, zero, t) for t in a2[i]),
                   tuple(stack(k, b) for b, k in zip(bh[i], kh[i])), _NT) for i in range(ng)]
        pw = [stack(w(strict_a, pa[i][:chunk]), w(strict_b, pb[i][:chunk])) for i in range(ng)]
        mak = [stack(w(strict_b, pa[i][:chunk]), w(strict_a, pb[i][:chunk])) for i in range(ng)]
        ark = [stack(w(incl_b, pa[i][chunk:]), w(incl_a, pb[i][chunk:])) for i in range(ng)]
        vsw = [pltpu.roll(v[i], n, axis=1) for i in range(ng)]
        vx = [_split2(stack(w(la, vsw[i]), w(~la, vsw[i]))) for i in range(ng)]
        kvy = [_mm3(_split2(stack(mak[i], ark[i])), vx[i], _NN) for i in range(ng)]
        for i, sl in enumerate(sls):
            arb_ref[:, sl] = stack(w(incl_a, pa[i][chunk:]), w(incl_b, pb[i][chunk:]))
            y1_ref[:, sl] = pltpu.roll(kvy[i][2 * chunk:3 * chunk] + kvy[i][3 * chunk:], n, axis=1)
            kbt_ref[:, sl] = stack(kg_ref[:, sl], bg_ref[:, sl]).T
        x = [stack(w(la, at[i]), w(~la, at[i])) + kvy[i][:2 * chunk] for i in range(ng)]
        for lvl in range(n_apply):
            if lvl + 1 < n_apply:
                r = [_mm3(_split2(pw[i]), _split2(jnp.concatenate([x[i], pw[i]], axis=1)), _NN) for i in range(ng)]
                x = [x[i] + r[i][:, :LANES] for i in range(ng)]
                pw = [r[i][:, LANES:] for i in range(ng)]
            else:
                x = [x[i] + _mm3(_split2(pw[i]), _split2(x[i]), _NN) for i in range(ng)]
        for i, sl in enumerate(sls):
            wt_ref[:, sl] = w(la, x[i][:chunk]) + w(~la, x[i][chunk:])
            p1_ref[:, sl] = pltpu.roll(w(~la, x[i][:chunk]) + w(la, x[i][chunk:]), n, axis=1)


def _rwkv_solve(at, kh, bh, rt, v, kg, bg, dm):
    b, t, rw = rt.shape
    chunk = RWKV_CHUNK
    pairs = dm.rh // 2
    kern = functools.partial(_rwkv_solve_kernel, chunk=chunk, pairs=pairs, group=min(8, pairs))
    tspec = pl.BlockSpec((None, chunk, rw), lambda bi, c: (bi, c, 0))
    sspec = pl.BlockSpec((None, 2 * chunk, rw), lambda bi, c: (bi, c, 0))
    tok = jax.ShapeDtypeStruct((b, t, rw), F32)
    stk = jax.ShapeDtypeStruct((b, 2 * t, rw), F32)
    return pl.pallas_call(
        kern,
        out_shape=(tok, tok, tok, stk, stk),
        grid=(b, t // chunk),
        in_specs=[tspec] * 7,
        out_specs=(tspec, tspec, tspec, sspec, sspec),
        compiler_params=_cparams(("parallel", "parallel")),
        name="rwkv_solve",
    )(at, kh, bh, rt, v, kg, bg)


def _rwkv_scan_kernel(wt_ref, rt_ref, p1_ref, y1_ref, arb_ref, kbt_ref, v_ref, bo_ref, g_ref, gc_ref,
                      lnw_ref, lnb_ref, s0_ref, y_ref, so_ref, st_ref, yacc_ref, *, chunk, pairs):
    c = pl.program_id(1)
    n = RWKV_HEAD
    row = lax.broadcasted_iota(jnp.int32, (LANES, LANES), 0)
    lane = lax.broadcasted_iota(jnp.int32, (LANES, LANES), 1)
    eye = row == lane
    same_head = (row < n) == (lane < n)
    la = lax.broadcasted_iota(jnp.int32, (chunk, LANES), 1) < n

    @pl.when(c == 0)
    def _():
        z = jnp.zeros((n, n), F32)
        for p in range(pairs):
            st_ref[p] = jnp.concatenate([jnp.concatenate([s0_ref[2 * p], z], axis=1),
                                         jnp.concatenate([z, s0_ref[2 * p + 1]], axis=1)], axis=0)

    sls = [slice(p * LANES, (p + 1) * LANES) for p in range(pairs)]
    s0 = [st_ref[p] for p in range(pairs)]
    ws = [_mm3(_split2(jnp.concatenate([wt_ref[:, sl], rt_ref[:, sl]], axis=0)), _split2(s0[p]), _NN)
          for p, sl in enumerate(sls)]
    u = [p1_ref[:, sl] + ws[p][:chunk] for p, sl in enumerate(sls)]
    for p, sl in enumerate(sls):
        ust = jnp.concatenate([jnp.where(la, u[p], 0.0), jnp.where(la, 0.0, u[p])], axis=0)
        yb = _mm3(_split2(arb_ref[:, sl]), _split2(ust), _NN)
        yacc_ref[:, sl] = ws[p][chunk:] + y1_ref[:, sl] + (yb[:chunk] + yb[chunk:])
    for p, sl in enumerate(sls):
        dg = jnp.where(eye, jnp.broadcast_to(gc_ref[:, sl], (LANES, LANES)), 0.0)
        lhs = jnp.concatenate([kbt_ref[:, sl], dg], axis=1)
        rhs = jnp.concatenate([v_ref[:, sl], u[p], s0[p]], axis=0)
        st_ref[p] = jnp.where(same_head, _mm3(_split2(lhs), _split2(rhs), _NN), 0.0)

    y_ref[...] = _group_norm_out(yacc_ref[...], bo_ref[...], g_ref[...], lnw_ref[...], lnb_ref[...], _head_ones())

    @pl.when(c == pl.num_programs(1) - 1)
    def _():
        for p in range(pairs):
            s = st_ref[p]
            so_ref[2 * p] = s[:n, :n]
            so_ref[2 * p + 1] = s[n:, n:]


def _rwkv_scan(prep, lnw, lnb, s0t, dm):
    rt, at, kh, bh, kg, bg, v, bonus, g, gc = prep
    wt, p1, y1, arb, kbt = _rwkv_solve(at, kh, bh, rt, v, kg, bg, dm)
    b, t, rw = rt.shape
    chunk = RWKV_CHUNK
    heads = dm.rh
    pairs = heads // 2
    kern = functools.partial(_rwkv_scan_kernel, chunk=chunk, pairs=pairs)
    tspec = pl.BlockSpec((None, chunk, rw), lambda bi, c: (bi, c, 0))
    kspec = pl.BlockSpec((None, 2 * chunk, rw), lambda bi, c: (bi, c, 0))
    sspec = pl.BlockSpec((None, heads, RWKV_HEAD, RWKV_HEAD), lambda bi, c: (bi, 0, 0, 0))
    row = pl.BlockSpec((1, rw), lambda bi, c: (0, 0))
    return pl.pallas_call(
        kern,
        out_shape=(jax.ShapeDtypeStruct((b, t, rw), F32),
                   jax.ShapeDtypeStruct((b, heads, RWKV_HEAD, RWKV_HEAD), F32)),
        grid=(b, t // chunk),
        in_specs=[tspec, tspec, tspec, tspec, kspec, kspec, tspec, tspec, tspec,
                  pl.BlockSpec((None, None, 1, rw), lambda bi, c: (bi, c, 0, 0)), row, row, sspec],
        out_specs=(tspec, sspec),
        scratch_shapes=[pltpu.VMEM((pairs, LANES, LANES), F32), pltpu.VMEM((chunk, rw), F32)],
        compiler_params=_cparams(("parallel", "arbitrary")),
        name="rwkv_scan",
    )(wt, rt, p1, y1, arb, kbt, v, bonus, g, gc, lnw, lnb, s0t)


def _rwkv_step_kernel(r_ref, k_ref, v_ref, l_ref, sh_ref, mu_ref, w0_ref, w2_ref, a0_ref, a2_ref, g2_ref,
                      kk_ref, ka_ref, rk_ref, lnw_ref, lnb_ref, s_ref, y_ref, so_ref, *, rw, heads):
    n = RWKV_HEAD
    rows = SUBLANES

    def mixed(x_ref, c0, c1):
        x = jnp.broadcast_to(x_ref[...], (rows, c1 - c0))
        return _rwkv_mix_cols(x, sh_ref[:, c0:c1], mu_ref[:, c0:c1])

    r = mixed(r_ref, 0, rw)
    k = mixed(k_ref, rw, 2 * rw)
    v = mixed(v_ref, 2 * rw, 3 * rw)
    lo = mixed(l_ref, 3 * rw, 3 * rw + RWKV_LORA)
    lw, a, g = _rwkv_lora(lo, w0_ref[...], w2_ref[...], a0_ref[...], a2_ref[...], g2_ref[...])
    w = jnp.exp(lw)
    kk = k * kk_ref[...]
    k2 = k * (1.0 + (a - 1.0) * ka_ref[...])
    eye = lax.broadcasted_iota(jnp.int32, (n, n), 0) == lax.broadcasted_iota(jnp.int32, (n, n), 1)

    def col(rowvec):
        return jnp.sum(jnp.where(eye, jnp.broadcast_to(rowvec, (n, n)), 0.0), axis=1, keepdims=True)

    hs = range(heads)
    sls = [slice(h * n, (h + 1) * n) for h in hs]
    kkh = [kk[0:1, sl] for sl in sls]
    kkh = [x * lax.rsqrt(jnp.maximum(jnp.sum(x * x, axis=1, keepdims=True), 1e-12)) for x in kkh]
    rh = [r[0:1, sl] for sl in sls]
    vh = [v[0:1, sl] for sl in sls]
    k2h = [k2[0:1, sl] for sl in sls]
    st = [s_ref[h] for h in hs]
    sa = [-jnp.sum(st[h] * kkh[h], axis=1, keepdims=True) for h in hs]
    vcol = [col(vh[h]) for h in hs]
    st = [st[h] * w[0:1, sls[h]] + sa[h] * (kkh[h] * a[0:1, sls[h]]) + vcol[h] * k2h[h] for h in hs]
    for h in hs:
        so_ref[h] = st[h]
    ycol = [jnp.sum(st[h] * rh[h], axis=1, keepdims=True) for h in hs]
    yrow = [jnp.sum(jnp.where(eye, jnp.broadcast_to(ycol[h], (n, n)), 0.0), axis=0, keepdims=True) for h in hs]
    ym = [jnp.mean(x, axis=1, keepdims=True) for x in yrow]
    d = [yrow[h] - ym[h] for h in hs]
    yv = [jnp.mean(x * x, axis=1, keepdims=True) for x in d]
    bonus = [jnp.sum(rh[h] * k2h[h] * rk_ref[:, sls[h]], axis=1, keepdims=True) * vh[h] for h in hs]
    outs = [(d[h] * lax.rsqrt(yv[h] + GN_EPS) * lnw_ref[:, sls[h]] + lnb_ref[:, sls[h]] + bonus[h]) * g[0:1, sls[h]]
            for h in hs]
    y_ref[...] = jnp.concatenate(outs, axis=1)


def _rwkv_step(p, shift, s0, mu, w0, w2, a0, a2, g2, k_k, k_a, r_k, lnw, lnb, dm):
    b = p.shape[0]
    rw, heads = dm.rw, dm.rh
    kern = functools.partial(_rwkv_step_kernel, rw=rw, heads=heads)
    cur = lambda c0, w: pl.BlockSpec((None, 1, w), lambda bi: (bi, 0, _blk(c0, w)))
    full = lambda shp: pl.BlockSpec(shp, lambda bi: (0,) * len(shp))
    sspec = pl.BlockSpec((None, heads, RWKV_HEAD, RWKV_HEAD), lambda bi: (bi, 0, 0, 0))
    return pl.pallas_call(
        kern,
        out_shape=(jax.ShapeDtypeStruct((b, 1, rw), F32),
                   jax.ShapeDtypeStruct((b, heads, RWKV_HEAD, RWKV_HEAD), F32)),
        grid=(b,),
        in_specs=[cur(dm.c_r, rw), cur(dm.c_r + rw, rw), cur(dm.c_r + 2 * rw, rw), cur(dm.c_lora, RWKV_LORA),
                  pl.BlockSpec((None, 1, dm.rcols), lambda bi: (bi, 0, 0)),
                  full((1, dm.rcols)), full((1, rw)), full((RWKV_W_RANK, rw)), full((1, rw)),
                  full((RWKV_A_RANK, rw)), full((RWKV_G_RANK, rw)), full((1, rw)), full((1, rw)), full((1, rw)),
                  full((1, rw)), full((1, rw)), sspec],
        out_specs=(pl.BlockSpec((None, 1, rw), lambda bi: (bi, 0, 0)), sspec),
        compiler_params=_cparams(("parallel",)),
        name="rwkv_step",
    )(p, p, p, p, shift, mu, w0, w2, a0, a2, g2, k_k, k_a, r_k, lnw, lnb, s0)


def _rope_blocks(x, cos2, sin2):
    outs = []
    for c in range(0, x.shape[1], NSA_HEAD):
        xb = x[:, c:c + NSA_HEAD]
        outs.append(xb * cos2 + pltpu.roll(xb, NSA_HEAD // 2, axis=1) * sin2)
    return outs[0] if len(outs) == 1 else jnp.concatenate(outs, axis=1)


def _nsa_prep_kernel(*refs, tt, nq, with_means):
    q_refs, kv_refs = refs[:nq], refs[nq:nq + 6]
    cos_ref, sin_ref, qo_ref, kv_ref, kw_ref = refs[nq + 6:nq + 11]
    cos2, sin2 = cos_ref[...], sin_ref[...]
    pw = NSA_KV_HEADS * NSA_HEAD
    for idx, ref in enumerate(q_refs):
        qo_ref[:, idx * pw:(idx + 1) * pw] = (_rope_blocks(ref[...], cos2, sin2) * (NSA_HEAD ** -0.5)).astype(BF16)
    for idx, ref in enumerate(kv_refs):
        x = ref[...]
        if idx % 2 == 0:
            x = _rope_blocks(x, cos2, sin2)
        if idx < 4:
            kv_ref[:, idx * pw:(idx + 1) * pw] = x
        else:
            kw_ref[:, (idx - 4) * pw:(idx - 3) * pw] = x
        if idx < 2 and with_means:
            m_ref = refs[nq + 11]
            m_ref[:, idx * pw:(idx + 1) * pw] = (
                jnp.sum(x.reshape(tt // NSA_BLOCK, NSA_BLOCK, pw), axis=1) * (1.0 / NSA_BLOCK))


def _nsa_prep(p, cos2, sin2, dm, tt, with_means):
    b, t, _ = p.shape
    pw = dm.part
    nq = dm.nw // pw
    kern = functools.partial(_nsa_prep_kernel, tt=tt, nq=nq, with_means=with_means)
    cur = lambda c0: pl.BlockSpec((None, tt, pw), lambda bi, i: (bi, i, _blk(c0, pw)))
    outs = [jax.ShapeDtypeStruct((b, t, dm.nw), BF16), jax.ShapeDtypeStruct((b, t, 4 * pw), F32),
            jax.ShapeDtypeStruct((b, t, 2 * pw), F32)]
    ospecs = [pl.BlockSpec((None, tt, dm.nw), lambda bi, i: (bi, i, 0)),
              pl.BlockSpec((None, tt, 4 * pw), lambda bi, i: (bi, i, 0)),
              pl.BlockSpec((None, tt, 2 * pw), lambda bi, i: (bi, i, 0))]
    if with_means:
        outs.append(jax.ShapeDtypeStruct((b, t // NSA_BLOCK, 2 * pw), F32))
        ospecs.append(pl.BlockSpec((None, tt // NSA_BLOCK, 2 * pw), lambda bi, i: (bi, i, 0)))
    in_specs = [cur(dm.c_q + k * pw) for k in range(nq)] + [cur(dm.c_kv + k * pw) for k in range(6)]
    in_specs += [pl.BlockSpec((tt, NSA_HEAD), lambda bi, i: (i, 0))] * 2
    return pl.pallas_call(
        kern,
        out_shape=tuple(outs),
        grid=(b, t // tt),
        in_specs=in_specs,
        out_specs=tuple(ospecs),
        compiler_params=_cparams(("parallel", "parallel")),
        name="nsa_prep",
    )(*([p] * (nq + 6)), cos2, sin2)


def _nsa_prompt_kernel(q_ref, kc_ref, vc_ref, ks_ref, vs_ref, kw_ref, vw_ref, gt_ref, o_ref,
                       m_ref, l_ref, acc_ref, *, g, nb, n_sel, tk, lw):
    h = pl.program_id(1)
    i = pl.program_id(2)
    qb = NSA_QTILE
    rows = g * qb
    q = jnp.concatenate([q_ref[:, gi * NSA_HEAD:(gi + 1) * NSA_HEAD] for gi in range(g)], axis=0)
    gsl = [slice(gi * qb, (gi + 1) * qb) for gi in range(g)]
    qpos_c = i * qb + lax.broadcasted_iota(jnp.int32, (qb, 1), 0)
    qpos_r = i * qb + lax.broadcasted_iota(jnp.int32, (1, rows), 1) % qb

    blk = lax.broadcasted_iota(jnp.int32, (nb, 1), 0)
    st = _dot_nt(kc_ref[...], q)
    ok = ((blk + 1) * NSA_BLOCK - 1) <= qpos_r
    mc = jnp.max(jnp.where(ok, st, NEG), axis=0, keepdims=True)
    ec = jnp.where(ok, jnp.exp(st - mc), 0.0)
    den = jnp.sum(ec, axis=0, keepdims=True)
    p_c = ec / jnp.where(den > 0.0, den, 1.0)
    o_cmp = lax.dot_general(p_c.astype(BF16), vc_ref[...].astype(BF16), _TN, preferred_element_type=F32)

    imp = p_c[:, gsl[0]]
    for gi in range(1, g):
        imp = imp + p_c[:, gsl[gi]]
    qp1 = qpos_r[:, 0:qb]
    imp = jnp.where(blk == qp1 // NSA_BLOCK, g + 1.0, jnp.where(blk * NSA_BLOCK <= qp1, imp, -1.0))
    cnt = jnp.zeros((nb, qb), F32)
    for bi in range(nb):
        ci = imp[bi:bi + 1, :]
        beats = (ci > imp) | ((ci == imp) & (blk > bi))
        cnt = cnt + jnp.where(beats, 1.0, 0.0)
    assert nb <= LANES
    drop_t = jnp.where(cnt < n_sel, 0.0, NEG)
    if nb < LANES:
        drop_t = jnp.concatenate([drop_t, jnp.zeros((LANES - nb, qb), F32)], axis=0)
    drop = drop_t.T.astype(BF16)
    q_aug = jnp.concatenate([q, jnp.concatenate([drop] * g, axis=0)], axis=1)

    m_ref[...] = jnp.full((rows, 1), M_INIT, F32)
    l_ref[...] = jnp.zeros((rows, 1), F32)
    acc_ref[...] = jnp.zeros((rows, NSA_HEAD), F32)
    bpt = tk // NSA_BLOCK
    key_blk = lax.broadcasted_iota(jnp.int32, (tk, LANES), 0) // NSA_BLOCK
    blk_lane = lax.broadcasted_iota(jnp.int32, (tk, LANES), 1)
    kcol = lax.broadcasted_iota(jnp.int32, (1, tk), 1)
    n_kt = ((i + 1) * qb + tk - 1) // tk

    def scores(kt):
        k0 = pl.multiple_of(kt * tk, tk)
        one_hot = jnp.where(blk_lane == key_blk + kt * bpt, 1.0, 0.0).astype(BF16)
        k_aug = jnp.concatenate([ks_ref[pl.ds(k0, tk), :].astype(BF16), one_hot], axis=1)
        return _dot_nt(q_aug, k_aug)

    def accumulate(kt, sc, causal):
        k0 = pl.multiple_of(kt * tk, tk)
        vb = vs_ref[pl.ds(k0, tk), :].astype(BF16)
        s_g = [sc[sl] for sl in gsl]
        if causal:
            s_g = [jnp.where((k0 + kcol) <= qpos_c, s, NEG) for s in s_g]
        m_old = [m_ref[sl] for sl in gsl]
        m_new = [jnp.maximum(m_old[gi], jnp.max(s_g[gi], axis=1, keepdims=True)) for gi in range(g)]
        pr = [jnp.exp(s_g[gi] - m_new[gi]) for gi in range(g)]
        pv = _dot(jnp.concatenate([x.astype(BF16) for x in pr], axis=0), vb)
        for gi, sl in enumerate(gsl):
            alpha = jnp.exp(m_old[gi] - m_new[gi])
            l_ref[sl] = alpha * l_ref[sl] + jnp.sum(pr[gi], axis=1, keepdims=True)
            acc_ref[sl] = alpha * acc_ref[sl] + pv[sl]
            m_ref[sl] = m_new[gi]

    def body(kt, sc):
        nxt = scores(kt + 1)
        accumulate(kt, sc, False)
        return nxt

    accumulate(n_kt - 1, lax.fori_loop(0, n_kt - 1, body, scores(0)), True)
    o_sel = acc_ref[...] / l_ref[...]

    w0 = pl.multiple_of(jnp.maximum(i * qb - NSA_WINDOW, 0), qb)
    kwb = kw_ref[pl.ds(w0, lw), :].astype(BF16)
    vwb = vw_ref[pl.ds(w0, lw), :].astype(BF16)
    dp = qpos_c - (w0 + lax.broadcasted_iota(jnp.int32, (1, lw), 1))
    bias_w = jnp.where((dp >= 0) & (dp <= NSA_WINDOW), 0.0, NEG)
    sw = _dot_nt(q, kwb)
    s_w = [sw[sl] + bias_w for sl in gsl]
    e_w = [jnp.exp(s_w[gi] - jnp.max(s_w[gi], axis=1, keepdims=True)) for gi in range(g)]
    den_w = jnp.concatenate([jnp.sum(e, axis=1, keepdims=True) for e in e_w], axis=0)
    o_win = _dot(jnp.concatenate([e.astype(BF16) for e in e_w], axis=0), vwb) / den_w

    gs = _sigmoid(gt_ref[...])
    for gi in range(g):
        acc = None
        for ci, ob in enumerate((o_cmp, o_sel, o_win)):
            c0 = gi * 3 + ci
            c1 = (g + gi) * 3 + ci
            gate = jnp.where(h == 0, gs[:, c0:c0 + 1], gs[:, c1:c1 + 1])
            term = ob[gi * qb:(gi + 1) * qb] * gate
            acc = term if acc is None else acc + term
        o_ref[:, gi * NSA_HEAD:(gi + 1) * NSA_HEAD] = acc


def _nsa_prompt(qr, kv4, kwv, kcvc, p, dm):
    b, t, _ = qr.shape
    assert NSA_KV_HEADS == 2
    g = dm.g
    nb = t // NSA_BLOCK
    n_sel = min(NSA_TOPK, nb)
    tk = min(512, t)
    lw = NSA_WINDOW + NSA_QTILE
    assert t % tk == 0 and t >= lw
    hd = NSA_HEAD
    kern = functools.partial(_nsa_prompt_kernel, g=g, nb=nb, n_sel=n_sel, tk=tk, lw=lw)
    kvspec = lambda c: pl.BlockSpec((None, t, hd), lambda bi, h, i, c=c: (bi, 0, 2 * c + h))
    rows = g * NSA_QTILE
    return pl.pallas_call(
        kern,
        out_shape=jax.ShapeDtypeStruct((b, t, dm.nw), F32),
        grid=(b, NSA_KV_HEADS, t // NSA_QTILE),
        in_specs=[pl.BlockSpec((None, NSA_QTILE, g * hd), lambda bi, h, i: (bi, i, h)),
                  pl.BlockSpec((None, nb, hd), lambda bi, h, i: (bi, 0, h)),
                  pl.BlockSpec((None, nb, hd), lambda bi, h, i: (bi, 0, 2 + h)),
                  kvspec(2), kvspec(3), kvspec(0), kvspec(1),
                  pl.BlockSpec((None, NSA_QTILE, LANES), lambda bi, h, i: (bi, i, _blk(dm.c_ng, LANES)))],
        out_specs=pl.BlockSpec((None, NSA_QTILE, g * hd), lambda bi, h, i: (bi, i, h)),
        scratch_shapes=[pltpu.VMEM((rows, 1), F32), pltpu.VMEM((rows, 1), F32), pltpu.VMEM((rows, hd), F32)],
        compiler_params=_cparams(("parallel", "parallel", "arbitrary")),
        name="nsa_prompt",
    )(qr, kcvc, kcvc, kv4, kv4, kwv, kwv, p)


def _page_means_kernel(pt_ref, *refs, ppb):
    o_ref = refs[ppb]
    for k in range(ppb):
        x = refs[k][...]
        nbp = x.shape[0] // NSA_BLOCK
        o_ref[k] = jnp.sum(x.reshape(nbp, NSA_BLOCK, x.shape[1], x.shape[2]), axis=1) * (1.0 / NSA_BLOCK)


def _page_means(cache, layer, page_table):
    _, _, page, _, prow, hd = cache.shape
    b, n_pages = page_table.shape
    nbp = page // NSA_BLOCK
    ppb = math.gcd(n_pages, 32)
    kern = functools.partial(_page_means_kernel, ppb=ppb)
    pspec = lambda k: pl.BlockSpec((None, None, page, None, prow, hd),
                                   lambda bi, j, pt, k=k: (layer, pt[bi, j * ppb + k], 0, 0, 0, 0))
    out = pl.pallas_call(
        kern,
        out_shape=jax.ShapeDtypeStruct((b, n_pages, nbp, prow, hd), F32),
        grid_spec=pltpu.PrefetchScalarGridSpec(
            num_scalar_prefetch=1,
            grid=(b, n_pages // ppb),
            in_specs=[pspec(k) for k in range(ppb)],
            out_specs=pl.BlockSpec((None, ppb, nbp, prow, hd), lambda bi, j, pt: (bi, j, 0, 0, 0)),
        ),
        compiler_params=_cparams(("parallel", "parallel")),
        name="nsa_page_means",
    )(page_table, *([cache] * ppb))
    return out.reshape(b, n_pages * nbp, prow * hd)


def _nsa_choose_kernel(q_ref, m_ref, oc_ref, idx_ref, *, g, nbp, n_sel):
    hd = NSA_HEAD
    blk = lax.broadcasted_iota(jnp.int32, (1, nbp), 1)
    lane = lax.broadcasted_iota(jnp.int32, (1, LANES), 1)
    for h in range(NSA_KV_HEADS):
        q = jnp.concatenate([q_ref[:, (h * g + gi) * hd:(h * g + gi + 1) * hd] for gi in range(g)], axis=0)
        q = jnp.concatenate([q, jnp.zeros((SUBLANES - g, hd), q.dtype)], axis=0) if g < SUBLANES else q
        kc = m_ref[:, h * hd:(h + 1) * hd]
        vc = m_ref[:, (NSA_KV_HEADS + h) * hd:(NSA_KV_HEADS + h + 1) * hd]
        s = _dot_nt(q, kc)
        m = jnp.max(s, axis=1, keepdims=True)
        e = jnp.exp(s - m)
        p_c = e / jnp.sum(e, axis=1, keepdims=True)
        oc_ref[h] = _dot(p_c, vc)[0:g]
        imp = jnp.sum(p_c[0:g], axis=0, keepdims=True)
        idx = jnp.where(lane == 0, nbp, 0)
        for it in range(1, n_sel):
            best = jnp.max(imp, axis=1, keepdims=True)
            j = jnp.min(jnp.where(imp == best, blk, nbp), axis=1, keepdims=True)
            idx = jnp.where(lane == it, j, idx)
            imp = jnp.where(blk == j, -2.0, imp)
        idx_ref[h] = idx


def _nsa_choose(qr, means, dm, n_sel):
    b = qr.shape[0]
    nbp = means.shape[1]
    g = dm.g
    kern = functools.partial(_nsa_choose_kernel, g=g, nbp=nbp, n_sel=n_sel)
    return pl.pallas_call(
        kern,
        out_shape=(jax.ShapeDtypeStruct((b, NSA_KV_HEADS, g, NSA_HEAD), F32),
                   jax.ShapeDtypeStruct((b, NSA_KV_HEADS, 1, LANES), jnp.int32)),
        grid=(b,),
        in_specs=[pl.BlockSpec((None, 1, dm.nw), lambda bi: (bi, 0, 0)),
                  pl.BlockSpec((None, nbp, means.shape[2]), lambda bi: (bi, 0, 0))],
        out_specs=(pl.BlockSpec((None, NSA_KV_HEADS, g, NSA_HEAD), lambda bi: (bi, 0, 0, 0)),
                   pl.BlockSpec((None, NSA_KV_HEADS, 1, LANES), lambda bi: (bi, 0, 0, 0))),
        compiler_params=_cparams(("parallel",)),
        name="nsa_choose",
    )(qr, means)


def _nsa_sample_kernel(idx_ref, pt_ref, q_ref, *refs, g, n_sel):
    cb_refs = refs[:n_sel - 1]
    kn_ref, vn_ref, wb_ref, kwn_ref, vwn_ref, oc_ref, gt_ref, o_ref = refs[n_sel - 1:]
    h = pl.program_id(1)
    hd = NSA_HEAD
    rows = SUBLANES
    kvh = NSA_KV_HEADS
    q = jnp.concatenate([q_ref[:, gi * hd:(gi + 1) * hd] for gi in range(g)], axis=0)
    if g < rows:
        q = jnp.concatenate([q, jnp.zeros((rows - g, hd), q.dtype)], axis=0)
    qf = q.astype(F32)

    xbs = [cb_ref[...].astype(BF16) for cb_ref in cb_refs]
    srow = lax.broadcasted_iota(jnp.int32, (1, xbs[0].shape[0]), 1) % (4 * kvh)
    keep = jnp.where(srow == 2 * kvh + h, 0.0, NEG)
    s_new = jnp.sum(qf * kn_ref[...].astype(BF16).astype(F32), axis=1, keepdims=True)
    scs = [_dot_nt(q, xb) + keep for xb in xbs]
    m = s_new
    for sc in scs:
        m = jnp.maximum(m, jnp.max(sc, axis=1, keepdims=True))
    prs = [jnp.exp(sc - m) for sc in scs]
    p_new = jnp.exp(s_new - m)
    l = p_new
    acc = p_new.astype(BF16).astype(F32) * vn_ref[...].astype(BF16).astype(F32)
    for pr, xb in zip(prs, xbs):
        l = l + jnp.sum(pr, axis=1, keepdims=True)
        acc = acc + _dot(pltpu.roll(pr, kvh, axis=1), xb)
    o_sel = acc / l

    xw = wb_ref[...].astype(BF16)
    wrow = lax.broadcasted_iota(jnp.int32, (1, xw.shape[0]), 1) % (2 * kvh)
    sw = _dot_nt(q, xw) + jnp.where(wrow == h, 0.0, NEG)
    sn = jnp.sum(qf * kwn_ref[...].astype(BF16).astype(F32), axis=1, keepdims=True)
    mw = jnp.maximum(jnp.max(sw, axis=1, keepdims=True), sn)
    ew = jnp.exp(sw - mw)
    en = jnp.exp(sn - mw)
    den = jnp.sum(ew, axis=1, keepdims=True) + en
    pn = (en / den).astype(BF16).astype(F32)
    o_win = _dot(pltpu.roll(ew / den, kvh, axis=1), xw) + pn * vwn_ref[...].astype(BF16).astype(F32)

    gs = _sigmoid(gt_ref[...])
    o_cmp = oc_ref[...]
    for gi in range(g):
        tot = None
        for ci, ob in enumerate((o_cmp, o_sel, o_win)):
            c0 = gi * 3 + ci
            c1 = (g + gi) * 3 + ci
            gate = jnp.where(h == 0, gs[:, c0:c0 + 1], gs[:, c1:c1 + 1])
            term = ob[gi:gi + 1] * gate
            tot = term if tot is None else tot + term
        o_ref[:, gi * hd:(gi + 1) * hd] = tot


def _nsa_sample(qr, kv4, kwv, cache_rows, win_rows, layer, n_pool, page, lwin, page_table, idx, o_cmp, p, dm, n_sel):
    b = qr.shape[0]
    g, hd = dm.g, NSA_HEAD
    bpp = page // NSA_BLOCK
    brows = NSA_BLOCK * 4 * NSA_KV_HEADS
    wrows = lwin * 2 * NSA_KV_HEADS
    kern = functools.partial(_nsa_sample_kernel, g=g, n_sel=n_sel)

    def sel_spec(slot):
        def index_map(bi, h, idx_ref, pt_ref):
            blk = idx_ref[(bi * NSA_KV_HEADS + h) * LANES + slot]
            return ((layer * n_pool + pt_ref[bi, blk // bpp]) * bpp + blk % bpp, 0)
        return pl.BlockSpec((brows, hd), index_map)

    newspec = lambda c: pl.BlockSpec((None, 1, hd), lambda bi, h, ir, pr, c=c: (bi, 0, 2 * c + h))
    in_specs = [pl.BlockSpec((None, 1, g * hd), lambda bi, h, ir, pr: (bi, 0, h))]
    in_specs += [sel_spec(slot) for slot in range(1, n_sel)]
    in_specs += [newspec(2), newspec(3),
                 pl.BlockSpec((wrows, hd), lambda bi, h, ir, pr: (layer * b + bi, 0)),
                 newspec(0), newspec(1),
                 pl.BlockSpec((None, None, g, hd), lambda bi, h, ir, pr: (bi, h, 0, 0)),
                 pl.BlockSpec((None, 1, LANES), lambda bi, h, ir, pr: (bi, 0, _blk(dm.c_ng, LANES)))]
    return pl.pallas_call(
        kern,
        out_shape=jax.ShapeDtypeStruct((b, 1, dm.nw), F32),
        grid_spec=pltpu.PrefetchScalarGridSpec(
            num_scalar_prefetch=2,
            grid=(b, NSA_KV_HEADS),
            in_specs=in_specs,
            out_specs=pl.BlockSpec((None, 1, g * hd), lambda bi, h, ir, pr: (bi, 0, h)),
        ),
        compiler_params=_cparams(("parallel", "parallel")),
        name="nsa_sample",
    )(idx, page_table, qr, *([cache_rows] * (n_sel - 1)), kv4, kv4, win_rows, kwv, kwv, o_cmp, p)


def _rope_tables(pos):
    half = NSA_HEAD // 2
    inv = jnp.exp(-math.log(ROPE_THETA) * jnp.arange(half, dtype=F32) / half)
    ang = pos.astype(F32)[:, None] * inv[None, :]
    cos, sin = jnp.cos(ang), jnp.sin(ang)
    return jnp.concatenate([cos, cos], axis=1), jnp.concatenate([-sin, sin], axis=1)


def _shift_cols(p_last, dm):
    return jnp.concatenate([p_last[..., dm.c_r:dm.c_r + 3 * dm.rw], p_last[..., dm.c_lora:dm.c_lora + RWKV_LORA]],
                           axis=-1)


def _row(x):
    return x.reshape(1, -1)


def kernel(x_prompt, x_sample, cache_nsa_kv, page_table, state_nsa_window, state_wkv, state_rwkv_shift,
           state_pool, state_ffn_conv, norm_mix, w_in, pool_w, pool_scale, rwkv_mu, rwkv_w0, rwkv_w2,
           rwkv_a0, rwkv_a2, rwkv_g2, rwkv_k_k, rwkv_k_a, rwkv_r_k, rwkv_ln_w, rwkv_ln_b, w_branch, w_out,
           norm_ffn, ffn_up, ffn_conv, ffn_conv_b, ffn_down, norm_final):
    bp, sp, d = x_prompt.shape
    bs, ts, _ = x_sample.shape
    assert ts == 1
    depth = w_in.shape[0]
    dm = Dims(d)
    ff = dm.ff
    hd = NSA_HEAD
    page = cache_nsa_kv.shape[2]
    n_pages = page_table.shape[1]
    past = n_pages * page
    kvw = 4 * NSA_KV_HEADS * hd

    w_in_b = w_in.astype(BF16)
    w_gate_b = w_in_b[:, :, dm.o3:]
    pool_w_b = pool_w.astype(BF16)
    wb_b = w_branch.astype(BF16)
    wo_b = w_out.astype(BF16)
    up_b = ffn_up.astype(BF16)
    down_b = ffn_down.astype(BF16)
    g2_b = rwkv_g2.astype(BF16)
    kk_r = rwkv_k_k.reshape(depth, 1, dm.rw)
    ka_r = rwkv_k_a.reshape(depth, 1, dm.rw)
    rk_r = rwkv_r_k.reshape(depth, 1, dm.rw)

    mp = bp * sp
    tm_p = min(512, sp)
    tm_in = min(1024, sp)
    wide = lambda n: 1024 if n % 1024 == 0 else 512
    tn_ff = 512 if ff % 512 == 0 else 256
    tt_pool = min(512, sp)
    tt_rwkv = min(256, sp)
    tt_nsa = min(512, sp)
    tm_ff = min(1024, sp)
    tps = sp // tm_ff

    def rwkv_params(l):
        return (_row(rwkv_mu[l]), _row(rwkv_w0[l]), rwkv_w2[l], _row(rwkv_a0[l]), rwkv_a2[l], g2_b[l],
                kk_r[l], ka_r[l], rk_r[l])

    cos_p, sin_p = _rope_tables(jnp.arange(sp, dtype=jnp.int32))
    x = x_prompt.reshape(mp, d)
    zeros_hist16 = jnp.zeros((bp, POOL_HIST + 1, dm.pw), F32)
    zeros_shift = jnp.zeros((bp, 1, dm.rcols), F32)
    zeros_wkv = jnp.zeros((bp, dm.rh, RWKV_HEAD, RWKV_HEAD), F32)
    zeros_conv = jnp.zeros((bp, CONV_W - 1, 2 * ff), F32)
    wl = min(NSA_WINDOW, sp)
    p_kv, p_win, p_wkv, p_shift, p_pool, p_conv = [], [], [], [], [], []
    for l in range(depth):
        p = _norm_matmul(x, _row(norm_mix[l]), w_in_b, l, dm.np, tm_in, wide(dm.np)).reshape(bp, sp, dm.np)
        pg = _norm_matmul(x, _row(norm_mix[l]), w_gate_b, l, N_BRANCH * d, tm_in, wide(N_BRANCH * d))
        ya = _pool_prompt(p, zeros_hist16, pool_w_b[l], _row(pool_scale[l]), dm, tt_pool)
        prep = _rwkv_prep(p, zeros_shift, *rwkv_params(l), dm, tt_rwkv)
        yb, s_t = _rwkv_scan(prep, _row(rwkv_ln_w[l]), _row(rwkv_ln_b[l]), zeros_wkv, dm)
        qr, kv4, kwv, kcvc = _nsa_prep(p, cos_p, sin_p, dm, tt_nsa, True)
        yc = _nsa_prompt(qr, kv4, kwv, kcvc, p, dm)
        mrg = _branch_merge(ya.reshape(mp, -1), yb.reshape(mp, -1), yc.reshape(mp, -1), pg, wb_b, l, dm,
                            tm_p, min(1024, d))
        x = _out_proj(x, mrg, wo_b, l,tm_p, min(1024, d))
        x, za, zv = _ffn(x, _row(norm_ffn[l]), up_b, ffn_conv[l], _row(ffn_conv_b[l]), down_b, l,
                         zeros_conv, ff, tm_ff, tn_ff, tps, False)
        p_kv.append(kv4.reshape(bp, sp, 4, NSA_KV_HEADS, hd))
        p_win.append(kwv[:, sp - wl:].reshape(bp, wl, 2, NSA_KV_HEADS, hd))
        p_wkv.append(jnp.swapaxes(s_t, -1, -2))
        p_shift.append(_shift_cols(p[:, sp - 1:, :], dm))
        p_pool.append(p[:, sp - POOL_HIST:, dm.c_pool:dm.c_pool + dm.pw])
        zl = jnp.concatenate([za, zv], axis=-1).reshape(bp, tps, 2, 2 * ff)
        p_conv.append(zl[:, tps - 1])
    y_prompt = _final_norm(x, _row(norm_final), tm_p).reshape(bp, sp, d)

    pos_s = past
    cos_s, sin_s = _rope_tables(jnp.full((1,), pos_s, dtype=jnp.int32))
    xs = x_sample.reshape(bs, d)
    nbp = past // NSA_BLOCK
    n_sel = min(NSA_TOPK, nbp + 1)
    n_pool = cache_nsa_kv.shape[1]
    lwin = state_nsa_window.shape[2]
    cache_pairs = cache_nsa_kv.reshape(depth, n_pool, page, 2, 2 * NSA_KV_HEADS, hd)
    cache_rows = cache_nsa_kv.reshape(-1, hd)
    win_rows = state_nsa_window.reshape(-1, hd)
    s_kv, s_win, s_wkv, s_shift, s_pool, s_conv = [], [], [], [], [], []
    for l in range(depth):
        p = _norm_matmul(xs, _row(norm_mix[l]), w_in_b, l, dm.np, bs, wide(dm.np)).reshape(bs, 1, dm.np)
        pg = _norm_matmul(xs, _row(norm_mix[l]), w_gate_b, l, N_BRANCH * d, bs, wide(N_BRANCH * d))
        e16 = jnp.concatenate([state_pool[l], p[:, :, dm.c_pool:dm.c_pool + dm.pw]], axis=1)
        ya = _pool_sample(e16, pool_w_b[l], _row(pool_scale[l]), dm, pos_s)
        yb, s_new = _rwkv_step(p, state_rwkv_shift[l], state_wkv[l], *rwkv_params(l),
                               _row(rwkv_ln_w[l]), _row(rwkv_ln_b[l]), dm)
        qr, kv4, kwv = _nsa_prep(p, cos_s, sin_s, dm, 1, False)
        means = _page_means(cache_pairs, l, page_table)
        o_cmp, idx = _nsa_choose(qr, means, dm, n_sel)
        yc = _nsa_sample(qr, kv4, kwv, cache_rows, win_rows, l, n_pool, page, lwin, page_table, idx.reshape(-1),
                         o_cmp, p, dm, n_sel)
        mrg = _branch_merge(ya, yb.reshape(bs, -1), yc.reshape(bs, -1), pg, wb_b, l, dm, bs, min(1024, d))
        xs = _out_proj(xs, mrg, wo_b, l,bs, min(1024, d))
        xs, za, zv = _ffn(xs, _row(norm_ffn[l]), up_b, ffn_conv[l], _row(ffn_conv_b[l]), down_b, l,
                          state_ffn_conv[l], ff, bs, tn_ff, 1, True)
        s_kv.append(kv4.reshape(bs, 1, 4, NSA_KV_HEADS, hd))
        wk = jnp.concatenate([state_nsa_window[l], kwv.reshape(bs, 1, 2, NSA_KV_HEADS, hd)], axis=1)
        s_win.append(wk[:, -NSA_WINDOW:])
        s_wkv.append(s_new)
        s_shift.append(_shift_cols(p, dm))
        s_pool.append(e16[:, 1:])
        z_new = jnp.concatenate([za, zv], axis=-1)[:, None, :]
        s_conv.append(jnp.concatenate([state_ffn_conv[l][:, 1:], z_new], axis=1))
    y_sample = _final_norm(xs, _row(norm_final), bs).reshape(bs, 1, d)

    st = lambda xs_: jnp.stack(xs_)
    return (y_prompt, y_sample, st(p_kv), st(p_win), st(p_wkv), st(p_shift), st(p_pool), st(p_conv),
            st(s_kv), st(s_win), st(s_wkv), st(s_shift), st(s_pool), st(s_conv))
```
